```python
import jax, jax.numpy as jnp
from jax import lax
import numpy as np

D_MODEL = 1024
BATCH = 32
SEQ = 2048
DEPTH = 4

N_MIXERS = 2
N_ATTN_LAYERS = (DEPTH + 1) // 2
N_RNN_LAYERS = DEPTH // 2
ATTN_HEADS = 16
ATTN_HEAD_DIM = D_MODEL // ATTN_HEADS
Q_BLOCK = 128
RNN_WIDTH = 1280
RNN_HEADS = 16
RNN_BLOCK = RNN_WIDTH // RNN_HEADS
RNN_CONV = 4
LRU_C = 8.0
FFN_DIM = 2816
FFN_CONV = 3
PLE_DIM = 256
EPS = 1e-6

kernel_name = "hybrid_stickbreak_rglru_convffn"


def rms_norm(x, g):
    xf = x.astype(jnp.float32)
    y = xf * lax.rsqrt(jnp.mean(xf * xf, axis=-1, keepdims=True) + EPS)
    return (y * g.astype(jnp.float32)).astype(x.dtype)


def causal_depthwise_conv(x, w, b):
    k = w.shape[0]
    c = x.shape[-1]
    y = lax.conv_general_dilated(
        x, w[:, None, :].astype(x.dtype), window_strides=(1,), padding=[(k - 1, 0)],
        dimension_numbers=("NWC", "WIO", "NWC"), feature_group_count=c)
    return y + b.astype(x.dtype)


def stick_breaking_attention(h, w_qkv, w_o):
    b, s, _ = h.shape
    q, k, v = jnp.split(h @ w_qkv, 3, axis=-1)

    def heads(t):
        return t.reshape(b, s, ATTN_HEADS, ATTN_HEAD_DIM).transpose(0, 2, 1, 3).astype(jnp.float32)

    q, k, v = heads(q), heads(k), heads(v)
    scale = ATTN_HEAD_DIM ** -0.5
    outs = []
    for start in range(0, s, Q_BLOCK):
        end = start + Q_BLOCK
        qb = q[:, :, start:end]
        kb = k[:, :, :end]
        vb = v[:, :, :end]
        z = jnp.einsum("bhqd,bhkd->bhqk", qb, kb) * scale
        t_idx = jnp.arange(start, end)[:, None]
        s_idx = jnp.arange(end)[None, :]
        causal = s_idx < t_idx
        log_keep = jnp.where(causal, jax.nn.log_sigmoid(-z), 0.0)
        rest = lax.cumsum(log_keep, axis=3, reverse=True) - log_keep
        weights = jnp.where(causal, jnp.exp(jax.nn.log_sigmoid(z) + rest), 0.0)
        outs.append(jnp.einsum("bhqk,bhkd->bhqd", weights, vb))
    o = jnp.concatenate(outs, axis=2)
    o = o.transpose(0, 2, 1, 3).reshape(b, s, D_MODEL).astype(h.dtype)
    return o @ w_o


def rglru_block(h, w_in, conv_w, conv_b, w_gate_a, b_gate_a, w_gate_x, b_gate_x, lru_param, w_out):
    gate_branch, rec_branch = jnp.split(h @ w_in, 2, axis=-1)
    xr = causal_depthwise_conv(rec_branch, conv_w, conv_b)
    b, s, _ = xr.shape
    xb = xr.reshape(b, s, RNN_HEADS, RNN_BLOCK)
    r = jax.nn.sigmoid(jnp.einsum("bshi,hij->bshj", xb, w_gate_a).reshape(b, s, RNN_WIDTH) + b_gate_a)
    i = jax.nn.sigmoid(jnp.einsum("bshi,hij->bshj", xb, w_gate_x).reshape(b, s, RNN_WIDTH) + b_gate_x)
    log_a = LRU_C * r.astype(jnp.float32) * jax.nn.log_sigmoid(lru_param.astype(jnp.float32))
    a = jnp.exp(log_a)
    mult = jnp.sqrt(-jnp.expm1(2.0 * log_a))
    u = mult * (i * xr).astype(jnp.float32)

    def combine(c1, c2):
        a1, b1 = c1
        a2, b2 = c2
        return a1 * a2, a2 * b1 + b2

    _, hseq = lax.associative_scan(combine, (a, u), axis=1)
    y = jax.nn.gelu(gate_branch) * hseq.astype(h.dtype)
    return y @ w_out


def conv_ffn(h, w_up, conv_w, conv_b, w_down):
    u = causal_depthwise_conv(h @ w_up, conv_w, conv_b)
    gate, val = jnp.split(u, 2, axis=-1)
    return (jax.nn.gelu(gate) * val) @ w_down


def per_layer_embedding(h, p_i, norm_g, w_gate, w_proj):
    g = jax.nn.sigmoid(rms_norm(h, norm_g) @ w_gate)
    return g * (p_i @ w_proj)


def _fwd_setup_inputs(seed: int = 0) -> dict:
    key = jax.random.key(seed)
    ks = jax.random.split(key, 32)
    f32 = jnp.float32

    def nrm(k, shape, fan_in):
        return jax.random.normal(k, shape, f32) * (fan_in ** -0.5)

    def gain(k, shape):
        return 1.0 + 0.02 * jax.random.normal(k, shape, f32)

    def bias(k, shape):
        return 0.01 * jax.random.normal(k, shape, f32)

    a_base = jax.random.uniform(ks[12], (N_RNN_LAYERS, RNN_WIDTH), f32, minval=0.9, maxval=0.999)
    lru_param = jnp.log(a_base) - jnp.log1p(-a_base)

    return {
        "x": jax.random.normal(ks[0], (BATCH, SEQ, D_MODEL), f32),
        "p": jax.random.normal(ks[1], (DEPTH, BATCH, SEQ, PLE_DIM), f32),
        "norm_mix": gain(ks[2], (DEPTH, D_MODEL)),
        "attn_w_qkv": nrm(ks[3], (N_ATTN_LAYERS, D_MODEL, 3 * D_MODEL), D_MODEL),
        "attn_w_o": nrm(ks[4], (N_ATTN_LAYERS, D_MODEL, D_MODEL), D_MODEL),
        "rnn_w_in": nrm(ks[5], (N_RNN_LAYERS, D_MODEL, 2 * RNN_WIDTH), D_MODEL),
        "rnn_conv_w": nrm(ks[6], (N_RNN_LAYERS, RNN_CONV, RNN_WIDTH), RNN_CONV),
        "rnn_conv_b": bias(ks[7], (N_RNN_LAYERS, RNN_WIDTH)),
        "rnn_w_gate_a": nrm(ks[8], (N_RNN_LAYERS, RNN_HEADS, RNN_BLOCK, RNN_BLOCK), RNN_BLOCK),
        "rnn_b_gate_a": bias(ks[9], (N_RNN_LAYERS, RNN_WIDTH)),
        "rnn_w_gate_x": nrm(ks[10], (N_RNN_LAYERS, RNN_HEADS, RNN_BLOCK, RNN_BLOCK), RNN_BLOCK),
        "rnn_b_gate_x": bias(ks[11], (N_RNN_LAYERS, RNN_WIDTH)),
        "rnn_lru_param": lru_param,
        "rnn_w_out": nrm(ks[13], (N_RNN_LAYERS, RNN_WIDTH, D_MODEL), RNN_WIDTH),
        "norm_ffn": gain(ks[14], (DEPTH, D_MODEL)),
        "ffn_w_up": nrm(ks[15], (DEPTH, D_MODEL, 2 * FFN_DIM), D_MODEL),
        "ffn_conv_w": nrm(ks[16], (DEPTH, FFN_CONV, 2 * FFN_DIM), FFN_CONV),
        "ffn_conv_b": bias(ks[17], (DEPTH, 2 * FFN_DIM)),
        "ffn_w_down": nrm(ks[18], (DEPTH, FFN_DIM, D_MODEL), FFN_DIM),
        "norm_ple": gain(ks[19], (DEPTH, D_MODEL)),
        "ple_w_gate": nrm(ks[20], (DEPTH, D_MODEL, D_MODEL), D_MODEL),
        "ple_w_proj": nrm(ks[21], (DEPTH, PLE_DIM, D_MODEL), PLE_DIM),
        "norm_final": gain(ks[22], (D_MODEL,)),
    }


def _fwd_reference(x, p, norm_mix, attn_w_qkv, attn_w_o, rnn_w_in, rnn_conv_w, rnn_conv_b,
              rnn_w_gate_a, rnn_b_gate_a, rnn_w_gate_x, rnn_b_gate_x, rnn_lru_param, rnn_w_out,
              norm_ffn, ffn_w_up, ffn_conv_w, ffn_conv_b, ffn_w_down,
              norm_ple, ple_w_gate, ple_w_proj, norm_final):
    for i in range(DEPTH):
        slot = i // N_MIXERS
        hn = rms_norm(x, norm_mix[i])
        if i % N_MIXERS == 0:
            x = x + stick_breaking_attention(hn, attn_w_qkv[slot], attn_w_o[slot])
        else:
            x = x + rglru_block(hn, rnn_w_in[slot], rnn_conv_w[slot], rnn_conv_b[slot],
                                rnn_w_gate_a[slot], rnn_b_gate_a[slot],
                                rnn_w_gate_x[slot], rnn_b_gate_x[slot],
                                rnn_lru_param[slot], rnn_w_out[slot])
        x = x + conv_ffn(rms_norm(x, norm_ffn[i]), ffn_w_up[i], ffn_conv_w[i], ffn_conv_b[i], ffn_w_down[i])
        x = x + per_layer_embedding(x, p[i], norm_ple[i], ple_w_gate[i], ple_w_proj[i])
    return rms_norm(x, norm_final)


import jax as _jax
import jax.numpy as _jnp

TWIN_FORMAT = 'train_step'
FWD_PARAMS = ['x', 'p', 'norm_mix', 'attn_w_qkv', 'attn_w_o', 'rnn_w_in', 'rnn_conv_w', 'rnn_conv_b', 'rnn_w_gate_a', 'rnn_b_gate_a', 'rnn_w_gate_x', 'rnn_b_gate_x', 'rnn_lru_param', 'rnn_w_out', 'norm_ffn', 'ffn_w_up', 'ffn_conv_w', 'ffn_conv_b', 'ffn_w_down', 'norm_ple', 'ple_w_gate', 'ple_w_proj', 'norm_final']
TWIN_WEIGHTS = ['norm_mix', 'attn_w_qkv', 'attn_w_o', 'rnn_w_in', 'rnn_conv_w', 'rnn_conv_b', 'rnn_w_gate_a', 'rnn_b_gate_a', 'rnn_w_gate_x', 'rnn_b_gate_x', 'rnn_lru_param', 'rnn_w_out', 'norm_ffn', 'ffn_w_up', 'ffn_conv_w', 'ffn_conv_b', 'ffn_w_down', 'norm_ple', 'ple_w_gate', 'ple_w_proj', 'norm_final']
TWIN_DIFF_INPUT = 'x'
TWIN_INPUTS = ['x', 'p', 'norm_mix', 'attn_w_qkv', 'attn_w_o', 'rnn_w_in', 'rnn_conv_w', 'rnn_conv_b', 'rnn_w_gate_a', 'rnn_b_gate_a', 'rnn_w_gate_x', 'rnn_b_gate_x', 'rnn_lru_param', 'rnn_w_out', 'norm_ffn', 'ffn_w_up', 'ffn_conv_w', 'ffn_conv_b', 'ffn_w_down', 'norm_ple', 'ple_w_gate', 'ple_w_proj', 'norm_final', 'loss_target', 'm_norm_mix', 'm_attn_w_qkv', 'm_attn_w_o', 'm_rnn_w_in', 'm_rnn_conv_w', 'm_rnn_conv_b', 'm_rnn_w_gate_a', 'm_rnn_b_gate_a', 'm_rnn_w_gate_x', 'm_rnn_b_gate_x', 'm_rnn_lru_param', 'm_rnn_w_out', 'm_norm_ffn', 'm_ffn_w_up', 'm_ffn_conv_w', 'm_ffn_conv_b', 'm_ffn_w_down', 'm_norm_ple', 'm_ple_w_gate', 'm_ple_w_proj', 'm_norm_final', 'v_norm_mix', 'v_attn_w_qkv', 'v_attn_w_o', 'v_rnn_w_in', 'v_rnn_conv_w', 'v_rnn_conv_b', 'v_rnn_w_gate_a', 'v_rnn_b_gate_a', 'v_rnn_w_gate_x', 'v_rnn_b_gate_x', 'v_rnn_lru_param', 'v_rnn_w_out', 'v_norm_ffn', 'v_ffn_w_up', 'v_ffn_conv_w', 'v_ffn_conv_b', 'v_ffn_w_down', 'v_norm_ple', 'v_ple_w_gate', 'v_ple_w_proj', 'v_norm_final']
TWIN_OUTPUTS = ['loss', 'grad_x', 'grad_norm_mix', 'grad_attn_w_qkv', 'grad_attn_w_o', 'grad_rnn_w_in', 'grad_rnn_conv_w', 'grad_rnn_conv_b', 'grad_rnn_w_gate_a', 'grad_rnn_b_gate_a', 'grad_rnn_w_gate_x', 'grad_rnn_b_gate_x', 'grad_rnn_lru_param', 'grad_rnn_w_out', 'grad_norm_ffn', 'grad_ffn_w_up', 'grad_ffn_conv_w', 'grad_ffn_conv_b', 'grad_ffn_w_down', 'grad_norm_ple', 'grad_ple_w_gate', 'grad_ple_w_proj', 'grad_norm_final', 'delta_norm_mix', 'delta_attn_w_qkv', 'delta_attn_w_o', 'delta_rnn_w_in', 'delta_rnn_conv_w', 'delta_rnn_conv_b', 'delta_rnn_w_gate_a', 'delta_rnn_b_gate_a', 'delta_rnn_w_gate_x', 'delta_rnn_b_gate_x', 'delta_rnn_lru_param', 'delta_rnn_w_out', 'delta_norm_ffn', 'delta_ffn_w_up', 'delta_ffn_conv_w', 'delta_ffn_conv_b', 'delta_ffn_w_down', 'delta_norm_ple', 'delta_ple_w_gate', 'delta_ple_w_proj', 'delta_norm_final', 'new_m_norm_mix', 'new_m_attn_w_qkv', 'new_m_attn_w_o', 'new_m_rnn_w_in', 'new_m_rnn_conv_w', 'new_m_rnn_conv_b', 'new_m_rnn_w_gate_a', 'new_m_rnn_b_gate_a', 'new_m_rnn_w_gate_x', 'new_m_rnn_b_gate_x', 'new_m_rnn_lru_param', 'new_m_rnn_w_out', 'new_m_norm_ffn', 'new_m_ffn_w_up', 'new_m_ffn_conv_w', 'new_m_ffn_conv_b', 'new_m_ffn_w_down', 'new_m_norm_ple', 'new_m_ple_w_gate', 'new_m_ple_w_proj', 'new_m_norm_final', 'new_v_norm_mix', 'new_v_attn_w_qkv', 'new_v_attn_w_o', 'new_v_rnn_w_in', 'new_v_rnn_conv_w', 'new_v_rnn_conv_b', 'new_v_rnn_w_gate_a', 'new_v_rnn_b_gate_a', 'new_v_rnn_w_gate_x', 'new_v_rnn_b_gate_x', 'new_v_rnn_lru_param', 'new_v_rnn_w_out', 'new_v_norm_ffn', 'new_v_ffn_w_up', 'new_v_ffn_conv_w', 'new_v_ffn_conv_b', 'new_v_ffn_w_down', 'new_v_norm_ple', 'new_v_ple_w_gate', 'new_v_ple_w_proj', 'new_v_norm_final']
TWIN_LEAF_KINDS = {'loss': 'loss', 'grad_x': 'grad_x', 'grad_norm_mix': 'grad_w', 'grad_attn_w_qkv': 'grad_w', 'grad_attn_w_o': 'grad_w', 'grad_rnn_w_in': 'grad_w', 'grad_rnn_conv_w': 'grad_w', 'grad_rnn_conv_b': 'grad_w', 'grad_rnn_w_gate_a': 'grad_w', 'grad_rnn_b_gate_a': 'grad_w', 'grad_rnn_w_gate_x': 'grad_w', 'grad_rnn_b_gate_x': 'grad_w', 'grad_rnn_lru_param': 'grad_w', 'grad_rnn_w_out': 'grad_w', 'grad_norm_ffn': 'grad_w', 'grad_ffn_w_up': 'grad_w', 'grad_ffn_conv_w': 'grad_w', 'grad_ffn_conv_b': 'grad_w', 'grad_ffn_w_down': 'grad_w', 'grad_norm_ple': 'grad_w', 'grad_ple_w_gate': 'grad_w', 'grad_ple_w_proj': 'grad_w', 'grad_norm_final': 'grad_w', 'delta_norm_mix': 'delta_w', 'delta_attn_w_qkv': 'delta_w', 'delta_attn_w_o': 'delta_w', 'delta_rnn_w_in': 'delta_w', 'delta_rnn_conv_w': 'delta_w', 'delta_rnn_conv_b': 'delta_w', 'delta_rnn_w_gate_a': 'delta_w', 'delta_rnn_b_gate_a': 'delta_w', 'delta_rnn_w_gate_x': 'delta_w', 'delta_rnn_b_gate_x': 'delta_w', 'delta_rnn_lru_param': 'delta_w', 'delta_rnn_w_out': 'delta_w', 'delta_norm_ffn': 'delta_w', 'delta_ffn_w_up': 'delta_w', 'delta_ffn_conv_w': 'delta_w', 'delta_ffn_conv_b': 'delta_w', 'delta_ffn_w_down': 'delta_w', 'delta_norm_ple': 'delta_w', 'delta_ple_w_gate': 'delta_w', 'delta_ple_w_proj': 'delta_w', 'delta_norm_final': 'delta_w', 'new_m_norm_mix': 'new_m', 'new_m_attn_w_qkv': 'new_m', 'new_m_attn_w_o': 'new_m', 'new_m_rnn_w_in': 'new_m', 'new_m_rnn_conv_w': 'new_m', 'new_m_rnn_conv_b': 'new_m', 'new_m_rnn_w_gate_a': 'new_m', 'new_m_rnn_b_gate_a': 'new_m', 'new_m_rnn_w_gate_x': 'new_m', 'new_m_rnn_b_gate_x': 'new_m', 'new_m_rnn_lru_param': 'new_m', 'new_m_rnn_w_out': 'new_m', 'new_m_norm_ffn': 'new_m', 'new_m_ffn_w_up': 'new_m', 'new_m_ffn_conv_w': 'new_m', 'new_m_ffn_conv_b': 'new_m', 'new_m_ffn_w_down': 'new_m', 'new_m_norm_ple': 'new_m', 'new_m_ple_w_gate': 'new_m', 'new_m_ple_w_proj': 'new_m', 'new_m_norm_final': 'new_m', 'new_v_norm_mix': 'new_v', 'new_v_attn_w_qkv': 'new_v', 'new_v_attn_w_o': 'new_v', 'new_v_rnn_w_in': 'new_v', 'new_v_rnn_conv_w': 'new_v', 'new_v_rnn_conv_b': 'new_v', 'new_v_rnn_w_gate_a': 'new_v', 'new_v_rnn_b_gate_a': 'new_v', 'new_v_rnn_w_gate_x': 'new_v', 'new_v_rnn_b_gate_x': 'new_v', 'new_v_rnn_lru_param': 'new_v', 'new_v_rnn_w_out': 'new_v', 'new_v_norm_ffn': 'new_v', 'new_v_ffn_w_up': 'new_v', 'new_v_ffn_conv_w': 'new_v', 'new_v_ffn_conv_b': 'new_v', 'new_v_ffn_w_down': 'new_v', 'new_v_norm_ple': 'new_v', 'new_v_ple_w_gate': 'new_v', 'new_v_ple_w_proj': 'new_v', 'new_v_norm_final': 'new_v'}


def _forward(args):
    return _fwd_reference(*[args[k] for k in FWD_PARAMS])


def _output_shape():
    out = _jax.eval_shape(lambda: _forward(_fwd_setup_inputs(0)))
    return out.shape, out.dtype

N_MICROBATCH = 1
ADAM_LR = 0.001
ADAM_B1 = 0.9
ADAM_B2 = 0.999
ADAM_EPS = 1e-08
ADAM_WD = 0.01
ADAM_STEP = 10
PER_EXAMPLE_BATCH_AXIS = {'x': 0, 'p': 1, 'loss_target': 0}
SHARED_INPUTS = []
_WEIGHT_DTYPES = {'norm_mix': _jnp.float32, 'attn_w_qkv': _jnp.float32, 'attn_w_o': _jnp.float32, 'rnn_w_in': _jnp.float32, 'rnn_conv_w': _jnp.float32, 'rnn_conv_b': _jnp.float32, 'rnn_w_gate_a': _jnp.float32, 'rnn_b_gate_a': _jnp.float32, 'rnn_w_gate_x': _jnp.float32, 'rnn_b_gate_x': _jnp.float32, 'rnn_lru_param': _jnp.float32, 'rnn_w_out': _jnp.float32, 'norm_ffn': _jnp.float32, 'ffn_w_up': _jnp.float32, 'ffn_conv_w': _jnp.float32, 'ffn_conv_b': _jnp.float32, 'ffn_w_down': _jnp.float32, 'norm_ple': _jnp.float32, 'ple_w_gate': _jnp.float32, 'ple_w_proj': _jnp.float32, 'norm_final': _jnp.float32}
MOMENT_SCALE = {'norm_mix': 1.404536e-01, 'attn_w_qkv': 9.581470e-02, 'attn_w_o': 1.412134e-01, 'rnn_w_in': 6.115112e-02, 'rnn_conv_w': 6.426972e-02, 'rnn_conv_b': 2.814417e-01, 'rnn_w_gate_a': 1.374250e-02, 'rnn_b_gate_a': 1.302920e-02, 'rnn_w_gate_x': 2.406004e-02, 'rnn_b_gate_x': 2.216444e-02, 'rnn_lru_param': 2.802104e-02, 'rnn_w_out': 6.712932e-02, 'norm_ffn': 1.466586e-01, 'ffn_w_up': 6.251516e-02, 'ffn_conv_w': 6.338712e-02, 'ffn_conv_b': 6.422228e-02, 'ffn_w_down': 1.020869e-01, 'norm_ple': 3.197257e-02, 'ple_w_gate': 3.236587e-02, 'ple_w_proj': 8.277813e-02, 'norm_final': 6.383162e+01}


def _to_microbatches(a, axis):
    t = _jnp.moveaxis(a, axis, 0)
    t = t.reshape((N_MICROBATCH, t.shape[0] // N_MICROBATCH) + t.shape[1:])
    return _jnp.moveaxis(t, 1, axis + 1)


def setup_inputs(seed: int = 0) -> dict:
    inp = _fwd_setup_inputs(seed)
    key = _jax.random.fold_in(_jax.random.key(seed), 7919)
    shape, _ = _output_shape()
    out = dict(inp)
    out["loss_target"] = _jax.random.normal(_jax.random.fold_in(key, 0), shape, _jnp.float32)
    for i, name in enumerate(TWIN_WEIGHTS):
        w = inp[name].astype(_jnp.float32)
        if MOMENT_SCALE is None:
            s = _jnp.sqrt(_jnp.mean(_jnp.square(w)) + 1e-30)
        else:
            s = MOMENT_SCALE[name]
        km, kv = _jax.random.split(_jax.random.fold_in(key, i + 1))
        out[name] = w
        out["m_" + name] = s * _jax.random.normal(km, w.shape, _jnp.float32)
        out["v_" + name] = (s * s) * _jax.random.uniform(kv, w.shape, _jnp.float32, 0.5, 1.5)
    if N_MICROBATCH > 1:
        for name, axis in PER_EXAMPLE_BATCH_AXIS.items():
            out[name] = _to_microbatches(out[name], axis)
    return {'x': out['x'], 'p': out['p'], 'norm_mix': out['norm_mix'], 'attn_w_qkv': out['attn_w_qkv'], 'attn_w_o': out['attn_w_o'], 'rnn_w_in': out['rnn_w_in'], 'rnn_conv_w': out['rnn_conv_w'], 'rnn_conv_b': out['rnn_conv_b'], 'rnn_w_gate_a': out['rnn_w_gate_a'], 'rnn_b_gate_a': out['rnn_b_gate_a'], 'rnn_w_gate_x': out['rnn_w_gate_x'], 'rnn_b_gate_x': out['rnn_b_gate_x'], 'rnn_lru_param': out['rnn_lru_param'], 'rnn_w_out': out['rnn_w_out'], 'norm_ffn': out['norm_ffn'], 'ffn_w_up': out['ffn_w_up'], 'ffn_conv_w': out['ffn_conv_w'], 'ffn_conv_b': out['ffn_conv_b'], 'ffn_w_down': out['ffn_w_down'], 'norm_ple': out['norm_ple'], 'ple_w_gate': out['ple_w_gate'], 'ple_w_proj': out['ple_w_proj'], 'norm_final': out['norm_final'], 'loss_target': out['loss_target'], 'm_norm_mix': out['m_norm_mix'], 'm_attn_w_qkv': out['m_attn_w_qkv'], 'm_attn_w_o': out['m_attn_w_o'], 'm_rnn_w_in': out['m_rnn_w_in'], 'm_rnn_conv_w': out['m_rnn_conv_w'], 'm_rnn_conv_b': out['m_rnn_conv_b'], 'm_rnn_w_gate_a': out['m_rnn_w_gate_a'], 'm_rnn_b_gate_a': out['m_rnn_b_gate_a'], 'm_rnn_w_gate_x': out['m_rnn_w_gate_x'], 'm_rnn_b_gate_x': out['m_rnn_b_gate_x'], 'm_rnn_lru_param': out['m_rnn_lru_param'], 'm_rnn_w_out': out['m_rnn_w_out'], 'm_norm_ffn': out['m_norm_ffn'], 'm_ffn_w_up': out['m_ffn_w_up'], 'm_ffn_conv_w': out['m_ffn_conv_w'], 'm_ffn_conv_b': out['m_ffn_conv_b'], 'm_ffn_w_down': out['m_ffn_w_down'], 'm_norm_ple': out['m_norm_ple'], 'm_ple_w_gate': out['m_ple_w_gate'], 'm_ple_w_proj': out['m_ple_w_proj'], 'm_norm_final': out['m_norm_final'], 'v_norm_mix': out['v_norm_mix'], 'v_attn_w_qkv': out['v_attn_w_qkv'], 'v_attn_w_o': out['v_attn_w_o'], 'v_rnn_w_in': out['v_rnn_w_in'], 'v_rnn_conv_w': out['v_rnn_conv_w'], 'v_rnn_conv_b': out['v_rnn_conv_b'], 'v_rnn_w_gate_a': out['v_rnn_w_gate_a'], 'v_rnn_b_gate_a': out['v_rnn_b_gate_a'], 'v_rnn_w_gate_x': out['v_rnn_w_gate_x'], 'v_rnn_b_gate_x': out['v_rnn_b_gate_x'], 'v_rnn_lru_param': out['v_rnn_lru_param'], 'v_rnn_w_out': out['v_rnn_w_out'], 'v_norm_ffn': out['v_norm_ffn'], 'v_ffn_w_up': out['v_ffn_w_up'], 'v_ffn_conv_w': out['v_ffn_conv_w'], 'v_ffn_conv_b': out['v_ffn_conv_b'], 'v_ffn_w_down': out['v_ffn_w_down'], 'v_norm_ple': out['v_norm_ple'], 'v_ple_w_gate': out['v_ple_w_gate'], 'v_ple_w_proj': out['v_ple_w_proj'], 'v_norm_final': out['v_norm_final']}


def _loss(weights, diff, rest, loss_target):
    with _jax.named_scope("forward"):
        args = {**rest, TWIN_DIFF_INPUT: diff, **{k: w.astype(_WEIGHT_DTYPES[k]) for k, w in weights.items()}}
        y = _forward(args)
    with _jax.named_scope("loss_head"):
        err = _jnp.square(y.astype(_jnp.float32) - loss_target)
        return 0.5 * _jnp.sum(_jnp.mean(err, axis=-1)) if err.ndim else 0.5 * err


def _adamw(w, g, m, v):
    m = ADAM_B1 * m + (1.0 - ADAM_B1) * g
    v = ADAM_B2 * v + (1.0 - ADAM_B2) * _jnp.square(g)
    m_hat = m / (1.0 - ADAM_B1 ** ADAM_STEP)
    v_hat = v / (1.0 - ADAM_B2 ** ADAM_STEP)
    delta = -ADAM_LR * (m_hat / (_jnp.sqrt(v_hat) + ADAM_EPS) + ADAM_WD * w)
    return delta, m, v


def reference(x, p, norm_mix, attn_w_qkv, attn_w_o, rnn_w_in, rnn_conv_w, rnn_conv_b, rnn_w_gate_a, rnn_b_gate_a, rnn_w_gate_x, rnn_b_gate_x, rnn_lru_param, rnn_w_out, norm_ffn, ffn_w_up, ffn_conv_w, ffn_conv_b, ffn_w_down, norm_ple, ple_w_gate, ple_w_proj, norm_final, loss_target, m_norm_mix, m_attn_w_qkv, m_attn_w_o, m_rnn_w_in, m_rnn_conv_w, m_rnn_conv_b, m_rnn_w_gate_a, m_rnn_b_gate_a, m_rnn_w_gate_x, m_rnn_b_gate_x, m_rnn_lru_param, m_rnn_w_out, m_norm_ffn, m_ffn_w_up, m_ffn_conv_w, m_ffn_conv_b, m_ffn_w_down, m_norm_ple, m_ple_w_gate, m_ple_w_proj, m_norm_final, v_norm_mix, v_attn_w_qkv, v_attn_w_o, v_rnn_w_in, v_rnn_conv_w, v_rnn_conv_b, v_rnn_w_gate_a, v_rnn_b_gate_a, v_rnn_w_gate_x, v_rnn_b_gate_x, v_rnn_lru_param, v_rnn_w_out, v_norm_ffn, v_ffn_w_up, v_ffn_conv_w, v_ffn_conv_b, v_ffn_w_down, v_norm_ple, v_ple_w_gate, v_ple_w_proj, v_norm_final):
    given = dict(x=x, p=p, norm_mix=norm_mix, attn_w_qkv=attn_w_qkv, attn_w_o=attn_w_o, rnn_w_in=rnn_w_in, rnn_conv_w=rnn_conv_w, rnn_conv_b=rnn_conv_b, rnn_w_gate_a=rnn_w_gate_a, rnn_b_gate_a=rnn_b_gate_a, rnn_w_gate_x=rnn_w_gate_x, rnn_b_gate_x=rnn_b_gate_x, rnn_lru_param=rnn_lru_param, rnn_w_out=rnn_w_out, norm_ffn=norm_ffn, ffn_w_up=ffn_w_up, ffn_conv_w=ffn_conv_w, ffn_conv_b=ffn_conv_b, ffn_w_down=ffn_w_down, norm_ple=norm_ple, ple_w_gate=ple_w_gate, ple_w_proj=ple_w_proj, norm_final=norm_final, loss_target=loss_target, m_norm_mix=m_norm_mix, m_attn_w_qkv=m_attn_w_qkv, m_attn_w_o=m_attn_w_o, m_rnn_w_in=m_rnn_w_in, m_rnn_conv_w=m_rnn_conv_w, m_rnn_conv_b=m_rnn_conv_b, m_rnn_w_gate_a=m_rnn_w_gate_a, m_rnn_b_gate_a=m_rnn_b_gate_a, m_rnn_w_gate_x=m_rnn_w_gate_x, m_rnn_b_gate_x=m_rnn_b_gate_x, m_rnn_lru_param=m_rnn_lru_param, m_rnn_w_out=m_rnn_w_out, m_norm_ffn=m_norm_ffn, m_ffn_w_up=m_ffn_w_up, m_ffn_conv_w=m_ffn_conv_w, m_ffn_conv_b=m_ffn_conv_b, m_ffn_w_down=m_ffn_w_down, m_norm_ple=m_norm_ple, m_ple_w_gate=m_ple_w_gate, m_ple_w_proj=m_ple_w_proj, m_norm_final=m_norm_final, v_norm_mix=v_norm_mix, v_attn_w_qkv=v_attn_w_qkv, v_attn_w_o=v_attn_w_o, v_rnn_w_in=v_rnn_w_in, v_rnn_conv_w=v_rnn_conv_w, v_rnn_conv_b=v_rnn_conv_b, v_rnn_w_gate_a=v_rnn_w_gate_a, v_rnn_b_gate_a=v_rnn_b_gate_a, v_rnn_w_gate_x=v_rnn_w_gate_x, v_rnn_b_gate_x=v_rnn_b_gate_x, v_rnn_lru_param=v_rnn_lru_param, v_rnn_w_out=v_rnn_w_out, v_norm_ffn=v_norm_ffn, v_ffn_w_up=v_ffn_w_up, v_ffn_conv_w=v_ffn_conv_w, v_ffn_conv_b=v_ffn_conv_b, v_ffn_w_down=v_ffn_w_down, v_norm_ple=v_norm_ple, v_ple_w_gate=v_ple_w_gate, v_ple_w_proj=v_ple_w_proj, v_norm_final=v_norm_final)
    weights = {n: given[n] for n in TWIN_WEIGHTS}
    shared = {n: given[n] for n in SHARED_INPUTS}
    per_example = {n: given[n] for n in ['x', 'p']}
    grad_fn = _jax.value_and_grad(_loss, argnums=(0, 1))

    def one_microbatch(ex, loss_target):
        ex = dict(ex)
        diff = ex.pop(TWIN_DIFF_INPUT)
        return grad_fn(weights, diff, {**shared, **ex}, loss_target)

    if N_MICROBATCH == 1:
        loss, (grad_w, grad_x) = one_microbatch(per_example, given["loss_target"])
    else:
        def body(carry, xs):
            loss_sum, grad_sum = carry
            l_k, (gw_k, gx_k) = one_microbatch(xs[0], xs[1])
            with _jax.named_scope("update"):
                return (loss_sum + l_k, _jax.tree.map(_jnp.add, grad_sum, gw_k)), gx_k

        init = (_jnp.zeros((), _jnp.float32), _jax.tree.map(_jnp.zeros_like, weights))
        (loss, grad_w), grad_x = _jax.lax.scan(body, init, (per_example, given["loss_target"]))
    with _jax.named_scope("update"):
        delta_w, new_m, new_v = {}, {}, {}
        for n in TWIN_WEIGHTS:
            delta_w[n], new_m[n], new_v[n] = _adamw(weights[n], grad_w[n], given["m_" + n], given["v_" + n])
    return (loss, grad_x, *[grad_w[n] for n in TWIN_WEIGHTS], *[delta_w[n] for n in TWIN_WEIGHTS],
            *[new_m[n] for n in TWIN_WEIGHTS], *[new_v[n] for n in TWIN_WEIGHTS])
```

```python
import functools
import math

import jax
import jax.numpy as jnp
from jax import lax
from jax.experimental import pallas as pl
from jax.experimental.pallas import tpu as pltpu

F32 = jnp.float32
BF16 = jnp.bfloat16

EPS = 1e-6
HEAD_DIM = 64
RNN_HEADS = 16
LRU_C = 8.0
ADAM_LR = 0.001
ADAM_B1 = 0.9
ADAM_B2 = 0.999
ADAM_EPS = 1e-08
ADAM_WD = 0.01
ADAM_STEP = 10

N_DEV = 8
LANES = 128
SUBLANES = 8
VMEM_LIMIT = 56 * 1024 * 1024
MESH = pl.DeviceIdType.MESH
GELU_C = math.sqrt(2.0 / math.pi)
GELU_A = 0.044715

WEIGHTS = ['norm_mix', 'attn_w_qkv', 'attn_w_o', 'rnn_w_in', 'rnn_conv_w', 'rnn_conv_b', 'rnn_w_gate_a',
           'rnn_b_gate_a', 'rnn_w_gate_x', 'rnn_b_gate_x', 'rnn_lru_param', 'rnn_w_out', 'norm_ffn', 'ffn_w_up',
           'ffn_conv_w', 'ffn_conv_b', 'ffn_w_down', 'norm_ple', 'ple_w_gate', 'ple_w_proj', 'norm_final']
SHARD_AXIS = {'attn_w_qkv': 2, 'attn_w_o': 1, 'rnn_w_in': 2, 'rnn_conv_w': 2, 'rnn_conv_b': 1, 'rnn_b_gate_a': 1,
              'rnn_b_gate_x': 1, 'rnn_lru_param': 1, 'rnn_w_out': 1, 'ffn_w_up': 2, 'ffn_conv_w': 2,
              'ffn_w_down': 1, 'ple_w_gate': 1, 'ple_w_proj': 2}
MATMUL_WEIGHTS = ['attn_w_qkv', 'attn_w_o', 'rnn_w_in', 'rnn_w_out', 'ffn_w_up', 'ffn_w_down', 'ple_w_gate',
                  'ple_w_proj']
CHANNEL_WEIGHTS = ['rnn_conv_w', 'rnn_conv_b', 'rnn_b_gate_a', 'rnn_b_gate_x', 'rnn_lru_param', 'ffn_conv_w']
REPLICATED = [n for n in WEIGHTS if n not in SHARD_AXIS]


def _params(*sem):
    return pltpu.CompilerParams(dimension_semantics=sem, vmem_limit_bytes=VMEM_LIMIT)


def _tile(dim, pref, align=LANES):
    if dim <= pref:
        return dim
    t = (pref + pref // 2) // align * align
    while t >= align:
        if dim % t == 0:
            return t
        t -= align
    return dim


def _gelu(x):
    return 0.5 * x * (1.0 + jnp.tanh(GELU_C * (x + GELU_A * x * x * x)))


def _gelu_and_grad(x):
    t = jnp.tanh(GELU_C * (x + GELU_A * x * x * x))
    g = 0.5 * x * (1.0 + t)
    dg = 0.5 * (1.0 + t) + 0.5 * x * (1.0 - t * t) * GELU_C * (1.0 + 3.0 * GELU_A * x * x)
    return g, dg


def _log_sigmoid(x):
    return jnp.minimum(x, 0.0) - jnp.log(1.0 + jnp.exp(-jnp.abs(x)))


def mm(a, b, *, name, ta=False, tb=False, out_dtype=F32, extras=(), epilogue=None, tm=512, tn=512, tk=1024):
    m, k = (a.shape[1], a.shape[0]) if ta else a.shape
    n = b.shape[0] if tb else b.shape[1]
    assert k == (b.shape[1] if tb else b.shape[0]), (a.shape, b.shape, ta, tb)
    tm, tn, tk = _tile(m, tm), _tile(n, tn), _tile(k, tk)
    nk = k // tk
    n_extra = len(extras)
    dims = (((0 if ta else 1,), (1 if tb else 0,)), ((), ()))

    def body(a_ref, b_ref, *rest):
        extra_refs, o_ref = rest[:n_extra], rest[n_extra]

        def finish(acc):
            if epilogue is not None:
                acc = epilogue(acc, *[e[...] for e in extra_refs])
            o_ref[...] = acc.astype(o_ref.dtype)

        part = lax.dot_general(a_ref[...].astype(BF16), b_ref[...].astype(BF16), dims,
                               preferred_element_type=F32)
        if nk == 1:
            finish(part)
        else:
            acc_ref = rest[n_extra + 1]
            kk = pl.program_id(2)

            @pl.when(kk == 0)
            def _():
                acc_ref[...] = part

            @pl.when(kk > 0)
            def _():
                acc_ref[...] += part

            @pl.when(kk == nk - 1)
            def _():
                finish(acc_ref[...])

    a_spec = pl.BlockSpec((tk, tm), lambda i, j, kk: (kk, i)) if ta else pl.BlockSpec((tm, tk), lambda i, j, kk: (i, kk))
    b_spec = pl.BlockSpec((tn, tk), lambda i, j, kk: (j, kk)) if tb else pl.BlockSpec((tk, tn), lambda i, j, kk: (kk, j))
    o_spec = pl.BlockSpec((tm, tn), lambda i, j, kk: (i, j))
    return pl.pallas_call(
        body, name=name, grid=(m // tm, n // tn, nk),
        in_specs=[a_spec, b_spec] + [o_spec] * n_extra, out_specs=o_spec,
        out_shape=jax.ShapeDtypeStruct((m, n), out_dtype),
        scratch_shapes=[pltpu.VMEM((tm, tn), F32)] if nk > 1 else [],
        compiler_params=_params("parallel", "parallel", "arbitrary"),
    )(a, b, *extras)


def _add(acc, res):
    return acc + res


def rms_fwd(x, g, *, name):
    t, d = x.shape
    tr = _tile(t, 512, SUBLANES)

    def body(x_ref, g_ref, o_ref):
        xv = x_ref[...]
        r = lax.rsqrt(jnp.mean(xv * xv, axis=-1, keepdims=True) + EPS)
        o_ref[...] = (xv * r * g_ref[...]).astype(o_ref.dtype)

    row = pl.BlockSpec((tr, d), lambda i: (i, 0))
    return pl.pallas_call(
        body, name=name, grid=(t // tr,), in_specs=[row, pl.BlockSpec((1, d), lambda i: (0, 0))], out_specs=row,
        out_shape=jax.ShapeDtypeStruct((t, d), BF16), compiler_params=_params("parallel"),
    )(x, g.reshape(1, d))


def rms_bwd(x, g, dh, dres, *, name):
    t, d = x.shape
    tr = _tile(t, 512, SUBLANES)

    def body(x_ref, g_ref, dh_ref, dres_ref, dx_ref, dg_ref):
        xv = x_ref[...]
        dhv = dh_ref[...].astype(F32)
        r = lax.rsqrt(jnp.mean(xv * xv, axis=-1, keepdims=True) + EPS)
        xh = xv * r
        u = dhv * g_ref[...]
        dx_ref[...] = dres_ref[...] + r * (u - xh * jnp.mean(u * xh, axis=-1, keepdims=True))
        part = jnp.sum(dhv * xh, axis=0, keepdims=True)

        @pl.when(pl.program_id(0) == 0)
        def _():
            dg_ref[...] = part

        @pl.when(pl.program_id(0) > 0)
        def _():
            dg_ref[...] += part

    row = pl.BlockSpec((tr, d), lambda i: (i, 0))
    vec = pl.BlockSpec((1, d), lambda i: (0, 0))
    dx, dg = pl.pallas_call(
        body, name=name, grid=(t // tr,), in_specs=[row, vec, row, row], out_specs=[row, vec],
        out_shape=[jax.ShapeDtypeStruct((t, d), F32), jax.ShapeDtypeStruct((1, d), F32)],
        compiler_params=_params("arbitrary"),
    )(x, g.reshape(1, d), dh, dres)
    return dx, dg.reshape(d)


def final_loss(x, g, target, *, name):
    t, d = x.shape
    tr = _tile(t, 512, SUBLANES)

    def body(x_ref, g_ref, t_ref, dx_ref, dg_ref, loss_ref):
        xv = x_ref[...]
        gv = g_ref[...]
        r = lax.rsqrt(jnp.mean(xv * xv, axis=-1, keepdims=True) + EPS)
        xh = xv * r
        err = xh * gv - t_ref[...]
        dy = err * (1.0 / d)
        u = dy * gv
        dx_ref[...] = r * (u - xh * jnp.mean(u * xh, axis=-1, keepdims=True))
        dg_part = jnp.sum(dy * xh, axis=0, keepdims=True)
        loss_part = jnp.zeros((1, LANES), F32) + (0.5 / d) * jnp.sum(err * err)

        @pl.when(pl.program_id(0) == 0)
        def _():
            dg_ref[...] = dg_part
            loss_ref[...] = loss_part

        @pl.when(pl.program_id(0) > 0)
        def _():
            dg_ref[...] += dg_part
            loss_ref[...] += loss_part

    row = pl.BlockSpec((tr, d), lambda i: (i, 0))
    vec = pl.BlockSpec((1, d), lambda i: (0, 0))
    dx, dg, loss = pl.pallas_call(
        body, name=name, grid=(t // tr,), in_specs=[row, vec, row],
        out_specs=[row, vec, pl.BlockSpec((1, LANES), lambda i: (0, 0))],
        out_shape=[jax.ShapeDtypeStruct((t, d), F32), jax.ShapeDtypeStruct((1, d), F32),
                   jax.ShapeDtypeStruct((1, LANES), F32)],
        compiler_params=_params("arbitrary"),
    )(x, g.reshape(1, d), target)
    return dx, dg.reshape(d), loss[0, 0]


def _split_dot(x, mat, left):
    hi = x.astype(BF16)
    lo = (x - hi.astype(F32)).astype(BF16)
    if left:
        return (jnp.dot(mat, hi, preferred_element_type=F32) + jnp.dot(mat, lo, preferred_element_type=F32))
    return (jnp.dot(hi, mat, preferred_element_type=F32) + jnp.dot(lo, mat, preferred_element_type=F32))


_NT = (((1,), (1,)), ((), ()))
_TN = (((0,), (0,)), ((), ()))


def attn_fwd(qkv, b, s, d, *, name):
    t = b * s
    tq = min(256, s)
    nq = s // tq
    pairs = d // LANES
    scale = HEAD_DIM ** -0.5

    def body(q_ref, k_ref, v_ref, o_ref, lt_ref):
        i = pl.program_id(2)
        row = lax.broadcasted_iota(jnp.int32, (tq, tq), 0)
        col = lax.broadcasted_iota(jnp.int32, (tq, tq), 1)
        later = (row > col).astype(BF16)
        causal = col < row
        ones = jnp.ones((SUBLANES, tq), BF16)
        for h in range(LANES // HEAD_DIM):
            sl = slice(HEAD_DIM * h, HEAD_DIM * (h + 1))
            qh = (q_ref[:, sl].astype(F32) * scale).astype(BF16)

            def block(j, carry, diag):
                run, acc, total = carry
                start = pl.multiple_of(j * tq, tq)
                kb = k_ref[pl.ds(start, tq), sl]
                vb = v_ref[pl.ds(start, tq), sl]
                z = lax.dot_general(qh, kb, _NT, preferred_element_type=F32)
                ls = _log_sigmoid(z)
                lk = ls - z
                if diag:
                    lk = jnp.where(causal, lk, 0.0)
                w = jnp.exp(ls + _split_dot(lk, later, left=False) + run)
                if diag:
                    w = jnp.where(causal, w, 0.0)
                acc = acc + jnp.dot(w.astype(BF16), vb, preferred_element_type=F32)
                run = run + jnp.sum(lk, axis=1, keepdims=True)
                hi = lk.astype(BF16)
                lo = (lk - hi.astype(F32)).astype(BF16)
                total = total + (lax.dot_general(ones, hi, _NT, preferred_element_type=F32)
                                 + lax.dot_general(ones, lo, _NT, preferred_element_type=F32))
                return run, acc, total

            carry = block(i, (jnp.zeros((tq, 1), F32), jnp.zeros((tq, HEAD_DIM), F32),
                              jnp.zeros((SUBLANES, tq), F32)), True)
            carry = lax.fori_loop(0, i, lambda jj, c: block(i - 1 - jj, c, False), carry)
            o_ref[:, sl] = carry[1].astype(o_ref.dtype)
            lt_ref[SUBLANES * h:SUBLANES * (h + 1), :] = carry[2]

    q_spec = pl.BlockSpec((tq, LANES), lambda bb, p, i: (bb * nq + i, p))
    k_spec = pl.BlockSpec((s, LANES), lambda bb, p, i: (bb, pairs + p))
    v_spec = pl.BlockSpec((s, LANES), lambda bb, p, i: (bb, 2 * pairs + p))
    heads_here = LANES // HEAD_DIM
    lt_spec = pl.BlockSpec((None, None, None, heads_here * SUBLANES, tq), lambda bb, p, i: (bb, p, i, 0, 0))
    return pl.pallas_call(
        body, name=name, grid=(b, pairs, nq), in_specs=[q_spec, k_spec, v_spec], out_specs=[q_spec, lt_spec],
        out_shape=[jax.ShapeDtypeStruct((t, d), BF16),
                   jax.ShapeDtypeStruct((b, pairs, nq, heads_here * SUBLANES, tq), F32)],
        compiler_params=_params("parallel", "parallel", "arbitrary"),
    )(qkv, qkv, qkv)


def attn_bwd(qkv, totals, do, b, s, d, *, name):
    t = b * s
    tq = min(256, s)
    nq = s // tq
    pairs = d // LANES
    scale = HEAD_DIM ** -0.5

    def body(q_ref, k_ref, v_ref, lt_ref, do_ref, dq_ref, dk_ref, dv_ref):
        i = pl.program_id(2)

        @pl.when(i == 0)
        def _():
            dk_ref[...] = jnp.zeros_like(dk_ref)
            dv_ref[...] = jnp.zeros_like(dv_ref)

        row = lax.broadcasted_iota(jnp.int32, (tq, tq), 0)
        col = lax.broadcasted_iota(jnp.int32, (tq, tq), 1)
        upto = (col <= row).astype(BF16)
        earlier = (col < row).astype(BF16)
        causal = row < col
        for h in range(LANES // HEAD_DIM):
            sl = slice(HEAD_DIM * h, HEAD_DIM * (h + 1))
            qh = (q_ref[:, sl].astype(F32) * scale).astype(BF16)
            doh = do_ref[:, sl].astype(BF16)
            total = lt_ref[SUBLANES * h:SUBLANES * h + 1, :]

            def block(j, carry, diag):
                run, grun, dq = carry
                start = pl.multiple_of(j * tq, tq)
                kb = k_ref[pl.ds(start, tq), sl]
                vb = v_ref[pl.ds(start, tq), sl]
                z = lax.dot_general(kb, qh, _NT, preferred_element_type=F32)
                ls = _log_sigmoid(z)
                lk = ls - z
                if diag:
                    lk = jnp.where(causal, lk, 0.0)
                w = jnp.exp(ls + ((total - run) - _split_dot(lk, upto, left=True)))
                if diag:
                    w = jnp.where(causal, w, 0.0)
                g = lax.dot_general(vb, doh, _NT, preferred_element_type=F32) * w
                before = grun + _split_dot(g, earlier, left=True)
                dz = g - jnp.exp(ls) * (g + before)
                if diag:
                    dz = jnp.where(causal, dz, 0.0)
                dzb = dz.astype(BF16)
                dv_ref[pl.ds(start, tq), sl] += jnp.dot(w.astype(BF16), doh, preferred_element_type=F32)
                dk_ref[pl.ds(start, tq), sl] += jnp.dot(dzb, qh, preferred_element_type=F32)
                dq = dq + lax.dot_general(dzb, kb, _TN, preferred_element_type=F32)
                run = run + jnp.sum(lk, axis=0, keepdims=True)
                grun = grun + jnp.sum(g, axis=0, keepdims=True)
                return run, grun, dq

            zero = jnp.zeros((1, tq), F32)
            carry = lax.fori_loop(0, i, lambda j, c: block(j, c, False), (zero, zero, jnp.zeros((tq, HEAD_DIM), F32)))
            carry = block(i, carry, True)
            dq_ref[:, sl] = carry[2] * scale

    q_spec = pl.BlockSpec((tq, LANES), lambda bb, p, i: (bb * nq + i, p))
    k_spec = pl.BlockSpec((s, LANES), lambda bb, p, i: (bb, pairs + p))
    v_spec = pl.BlockSpec((s, LANES), lambda bb, p, i: (bb, 2 * pairs + p))
    lt_spec = pl.BlockSpec((None, None, None) + totals.shape[3:], lambda bb, p, i: (bb, p, i, 0, 0))
    kv_out = pl.BlockSpec((s, LANES), lambda bb, p, i: (bb, p))
    out = jax.ShapeDtypeStruct((t, d), F32)
    return pl.pallas_call(
        body, name=name, grid=(b, pairs, nq), in_specs=[q_spec, k_spec, v_spec, lt_spec, q_spec],
        out_specs=[q_spec, kv_out, kv_out], out_shape=[out, out, out],
        compiler_params=_params("parallel", "parallel", "arbitrary"),
    )(qkv, qkv, qkv, totals, do)


def _shift_down(cur, prev8, dist):
    ext = jnp.concatenate([prev8, cur], axis=0)
    return pltpu.roll(ext, dist, 0)[SUBLANES:]


def _shift_up(cur, next8, dist):
    ext = jnp.concatenate([cur, next8], axis=0)
    return pltpu.roll(ext, ext.shape[0] - dist, 0)[:cur.shape[0]]


def _causal_conv(cur, prev8, w_ref, b_ref):
    taps = w_ref.shape[0]
    out = cur * w_ref[taps - 1:taps, :] + b_ref[...]
    for dist in range(1, taps):
        out = out + _shift_down(cur, prev8, dist) * w_ref[taps - 1 - dist:taps - dist, :]
    return out


def _conv_specs(t, s, rows, tc, col_of, time_axis):
    per8 = rows // SUBLANES
    last8 = t // SUBLANES - 1

    def grid_ids(*ids):
        return ids[time_axis], ids[1 - time_axis]

    def cur(*ids):
        i, j = grid_ids(*ids)
        return (i, col_of(j))

    def prev(*ids):
        i, j = grid_ids(*ids)
        return (jnp.maximum(i * per8 - 1, 0), col_of(j))

    def nxt(*ids):
        i, j = grid_ids(*ids)
        return (jnp.minimum((i + 1) * per8, last8), col_of(j))

    def chan(*ids):
        i, j = grid_ids(*ids)
        return (0, col_of(j))

    return (pl.BlockSpec((rows, tc), cur), pl.BlockSpec((SUBLANES, tc), prev), pl.BlockSpec((SUBLANES, tc), nxt),
            chan)


def _first_in_seq(i, rows, s):
    return (i % (s // rows)) == 0


def _last_in_seq(i, rows, s):
    return (i % (s // rows)) == (s // rows - 1)


def ffn_act_fwd(ug, uv, cwg, cwv, cbg, cbv, s, *, name):
    t, f = ug.shape
    rows, tc = _tile(s, 512, SUBLANES), _tile(f, 256)
    cur, prev, _, chan = _conv_specs(t, s, rows, tc, lambda j: j, 0)
    taps = cwg.shape[0]

    def body(ug_ref, ugp_ref, uv_ref, uvp_ref, cwg_ref, cwv_ref, cbg_ref, cbv_ref, a_ref):
        keep = jnp.where(_first_in_seq(pl.program_id(0), rows, s), 0.0, 1.0)
        gate = _causal_conv(ug_ref[...], ugp_ref[...] * keep, cwg_ref, cbg_ref)
        val = _causal_conv(uv_ref[...], uvp_ref[...] * keep, cwv_ref, cbv_ref)
        a_ref[...] = (_gelu(gate) * val).astype(a_ref.dtype)

    wspec = pl.BlockSpec((taps, tc), chan)
    bspec = pl.BlockSpec((1, tc), chan)
    return pl.pallas_call(
        body, name=name, grid=(t // rows, f // tc), in_specs=[cur, prev, cur, prev, wspec, wspec, bspec, bspec],
        out_specs=cur, out_shape=jax.ShapeDtypeStruct((t, f), BF16), compiler_params=_params("parallel", "parallel"),
    )(ug, ug, uv, uv, cwg, cwv, cbg.reshape(1, f), cbv.reshape(1, f))


def _accumulate_rows(first, ref, rows):
    for k, r in enumerate(rows):
        @pl.when(first)
        def _(k=k, r=r):
            ref[k:k + 1, :] = r

        @pl.when(jnp.logical_not(first))
        def _(k=k, r=r):
            ref[k:k + 1, :] += r


def _conv_weight_grads(dc, cur, prev8, taps):
    out = []
    for k in range(taps):
        dist = taps - 1 - k
        xs = cur if dist == 0 else _shift_down(cur, prev8, dist)
        out.append(jnp.sum(dc * xs, axis=0, keepdims=True))
    out.append(jnp.sum(dc, axis=0, keepdims=True))
    return out


def ffn_act_bwd(ug, uv, cwg, cwv, cbg, cbv, da, s, *, name):
    t, f = ug.shape
    rows, tc = _tile(s, 512, SUBLANES), _tile(f, 256)
    cur, prev, _, chan = _conv_specs(t, s, rows, tc, lambda j: j, 1)
    taps = cwg.shape[0]

    def body(ug_ref, ugp_ref, uv_ref, uvp_ref, cwg_ref, cwv_ref, cbg_ref, cbv_ref, da_ref,
             dcg_ref, dcv_ref, wg_ref, wv_ref):
        i = pl.program_id(1)
        keep = jnp.where(_first_in_seq(i, rows, s), 0.0, 1.0)
        ugc, ugp = ug_ref[...], ugp_ref[...] * keep
        uvc, uvp = uv_ref[...], uvp_ref[...] * keep
        gate = _causal_conv(ugc, ugp, cwg_ref, cbg_ref)
        val = _causal_conv(uvc, uvp, cwv_ref, cbv_ref)
        act, dact = _gelu_and_grad(gate)
        dav = da_ref[...].astype(F32)
        dgate = dav * val * dact
        dval = dav * act
        dcg_ref[...] = dgate.astype(dcg_ref.dtype)
        dcv_ref[...] = dval.astype(dcv_ref.dtype)
        _accumulate_rows(i == 0, wg_ref, _conv_weight_grads(dgate, ugc, ugp, taps))
        _accumulate_rows(i == 0, wv_ref, _conv_weight_grads(dval, uvc, uvp, taps))

    wspec = pl.BlockSpec((taps, tc), chan)
    bspec = pl.BlockSpec((1, tc), chan)
    gspec = pl.BlockSpec((taps + 1, tc), chan)
    act_shape = jax.ShapeDtypeStruct((t, f), BF16)
    stat_shape = jax.ShapeDtypeStruct((taps + 1, f), F32)
    return pl.pallas_call(
        body, name=name, grid=(f // tc, t // rows),
        in_specs=[cur, prev, cur, prev, wspec, wspec, bspec, bspec, cur],
        out_specs=[cur, cur, gspec, gspec], out_shape=[act_shape, act_shape, stat_shape, stat_shape],
        compiler_params=_params("parallel", "arbitrary"),
    )(ug, ug, uv, uv, cwg, cwv, cbg.reshape(1, f), cbv.reshape(1, f), da)


def conv_input_grad(dc, cw, s, *, name, out_dtype):
    t, f = dc.shape
    rows, tc = _tile(s, 512, SUBLANES), _tile(f, 256)
    cur, _, _, chan = _conv_specs(t, s, rows, tc, lambda j: j, 0)
    taps = cw.shape[0]
    sub = SUBLANES * (4 // jnp.dtype(dc.dtype).itemsize)
    nxt = pl.BlockSpec((sub, tc), lambda i, j: (jnp.minimum((i + 1) * (rows // sub), t // sub - 1), j))

    def body(dc_ref, dcn_ref, cw_ref, o_ref):
        keep = jnp.where(_last_in_seq(pl.program_id(0), rows, s), 0.0, 1.0)
        dcc = dc_ref[...].astype(F32)
        dcn = dcn_ref[...].astype(F32)[:SUBLANES] * keep
        out = dcc * cw_ref[taps - 1:taps, :]
        for dist in range(1, taps):
            out = out + _shift_up(dcc, dcn, dist) * cw_ref[taps - 1 - dist:taps - dist, :]
        o_ref[...] = out.astype(o_ref.dtype)

    return pl.pallas_call(
        body, name=name, grid=(t // rows, f // tc), in_specs=[cur, nxt, pl.BlockSpec((taps, tc), chan)],
        out_specs=cur, out_shape=jax.ShapeDtypeStruct((t, f), out_dtype),
        compiler_params=_params("parallel", "parallel"),
    )(dc, dc, cw)


def rnn_conv_fwd(yr, cw, cb, s, *, name):
    t, w = yr.shape
    rows, tc = _tile(s, 512, SUBLANES), _tile(w, 256)
    cur, prev, _, chan = _conv_specs(t, s, rows, tc, lambda j: j, 0)
    taps = cw.shape[0]

    def body(y_ref, yp_ref, cw_ref, cb_ref, o_ref):
        keep = jnp.where(_first_in_seq(pl.program_id(0), rows, s), 0.0, 1.0)
        o_ref[...] = _causal_conv(y_ref[...], yp_ref[...] * keep, cw_ref, cb_ref)

    return pl.pallas_call(
        body, name=name, grid=(t // rows, w // tc),
        in_specs=[cur, prev, pl.BlockSpec((taps, tc), chan), pl.BlockSpec((1, tc), chan)], out_specs=cur,
        out_shape=jax.ShapeDtypeStruct((t, w), F32), compiler_params=_params("parallel", "parallel"),
    )(yr, yr, cw, cb.reshape(1, w))


def rnn_conv_wgrad(dxr, yr, s, taps, *, name):
    t, w = yr.shape
    rows, tc = _tile(s, 512, SUBLANES), _tile(w, 256)
    cur, prev, _, chan = _conv_specs(t, s, rows, tc, lambda j: j, 1)

    def body(d_ref, y_ref, yp_ref, o_ref):
        i = pl.program_id(1)
        keep = jnp.where(_first_in_seq(i, rows, s), 0.0, 1.0)
        _accumulate_rows(i == 0, o_ref, _conv_weight_grads(d_ref[...], y_ref[...], yp_ref[...] * keep, taps))

    return pl.pallas_call(
        body, name=name, grid=(w // tc, t // rows), in_specs=[cur, cur, prev],
        out_specs=pl.BlockSpec((taps + 1, tc), chan), out_shape=jax.ShapeDtypeStruct((taps + 1, w), F32),
        compiler_params=_params("parallel", "arbitrary"),
    )(dxr, yr, yr)


def _one_minus_exp(x):
    series = -x * (1.0 + x * (0.5 + x * (1.0 / 6.0)))
    return jnp.where(x > -0.01, series, 1.0 - jnp.exp(x))


def _gates(ga, gi, ba, bx, log_lam):
    ra = jax.nn.sigmoid(ga + ba)
    ri = jax.nn.sigmoid(gi + bx)
    log_a = LRU_C * ra * log_lam
    a = jnp.exp(log_a)
    mult = jnp.sqrt(_one_minus_exp(2.0 * log_a))
    return ra, ri, a, mult


def rnn_scan_fwd(ga, gi, xr, yg, ba, bx, lam, b, s, *, name):
    t, w = xr.shape
    tc = _tile(w, 256)
    blocks = s // SUBLANES

    def body(ga_ref, gi_ref, xr_ref, yg_ref, ba_ref, bx_ref, lam_ref, h_ref, y_ref):
        log_lam = _log_sigmoid(lam_ref[...])
        ridx = lax.broadcasted_iota(jnp.int32, (SUBLANES, tc), 0)

        def step(n, carry):
            r0 = pl.multiple_of(n * SUBLANES, SUBLANES)
            rs = pl.ds(r0, SUBLANES)
            xrv = xr_ref[rs, :]
            _, ri, a, mult = _gates(ga_ref[rs, :], gi_ref[rs, :], ba_ref[...], bx_ref[...], log_lam)
            u = mult * (ri * xrv)
            for dist in (1, 2, 4):
                a_sh = jnp.where(ridx >= dist, pltpu.roll(a, dist, 0), 1.0)
                u_sh = jnp.where(ridx >= dist, pltpu.roll(u, dist, 0), 0.0)
                u = a * u_sh + u
                a = a * a_sh
            hb = u + a * carry
            h_ref[rs, :] = hb
            y_ref[rs, :] = (_gelu(yg_ref[rs, :]) * hb).astype(y_ref.dtype)
            return hb[SUBLANES - 1:SUBLANES, :]

        lax.fori_loop(0, blocks, step, jnp.zeros((1, tc), F32))

    seq = pl.BlockSpec((s, tc), lambda bb, j: (bb, j))
    vec = pl.BlockSpec((1, tc), lambda bb, j: (0, j))
    return pl.pallas_call(
        body, name=name, grid=(b, w // tc), in_specs=[seq, seq, seq, seq, vec, vec, vec], out_specs=[seq, seq],
        out_shape=[jax.ShapeDtypeStruct((t, w), F32), jax.ShapeDtypeStruct((t, w), BF16)],
        compiler_params=_params("parallel", "parallel"),
    )(ga, gi, xr, yg, ba.reshape(1, w), bx.reshape(1, w), lam.reshape(1, w))


def rnn_scan_bwd(dy, ga, gi, xr, yg, h, ba, bx, lam, b, s, *, name):
    t, w = xr.shape
    tc = _tile(w, 256)
    blocks = s // SUBLANES

    def body(dy_ref, ga_ref, gi_ref, xr_ref, yg_ref, h_ref, ba_ref, bx_ref, lam_ref,
             dyg_ref, dga_ref, dgi_ref, dxr_ref, stat_ref):
        lamv = lam_ref[...]
        log_lam = _log_sigmoid(lamv)
        dlog_lam = jax.nn.sigmoid(-lamv)
        ridx = lax.broadcasted_iota(jnp.int32, (SUBLANES, tc), 0)
        last = SUBLANES - 1

        def step(n, carry):
            lam_next, a_next, s_a, s_x, s_l = carry
            blk = blocks - 1 - n
            r0 = pl.multiple_of(blk * SUBLANES, SUBLANES)
            rs = pl.ds(r0, SUBLANES)
            rp = pl.ds(pl.multiple_of(jnp.maximum(blk - 1, 0) * SUBLANES, SUBLANES), SUBLANES)
            xrv = xr_ref[rs, :]
            hv = h_ref[rs, :]
            h_before = jnp.where(blk > 0, h_ref[rp, :][last:, :], 0.0)
            h_prev = jnp.where(ridx >= 1, pltpu.roll(hv, 1, 0), h_before)
            ra, ri, a, mult = _gates(ga_ref[rs, :], gi_ref[rs, :], ba_ref[...], bx_ref[...], log_lam)
            act, dact = _gelu_and_grad(yg_ref[rs, :])
            dyv = dy_ref[rs, :]
            dyg_ref[rs, :] = (dyv * hv * dact).astype(dyg_ref.dtype)
            v = dyv * act
            c = jnp.where(ridx < last, pltpu.roll(a, last, 0), a_next)
            for dist in (1, 2, 4):
                c_sh = jnp.where(ridx < SUBLANES - dist, pltpu.roll(c, SUBLANES - dist, 0), 1.0)
                v_sh = jnp.where(ridx < SUBLANES - dist, pltpu.roll(v, SUBLANES - dist, 0), 0.0)
                v = v + c * v_sh
                c = c * c_sh
            dh = v + c * lam_next
            du_ri_x = dh * xrv
            dmult = du_ri_x * ri
            dri = du_ri_x * mult
            dxr_ref[rs, :] = dh * mult * ri
            dlog_a = dh * h_prev * a - dmult * (a * a) / mult
            dra = dlog_a * (LRU_C * log_lam)
            dpa = dra * ra * (1.0 - ra)
            dpi = dri * ri * (1.0 - ri)
            dga_ref[rs, :] = dpa.astype(dga_ref.dtype)
            dgi_ref[rs, :] = dpi.astype(dgi_ref.dtype)
            s_a = s_a + jnp.sum(dpa, axis=0, keepdims=True)
            s_x = s_x + jnp.sum(dpi, axis=0, keepdims=True)
            s_l = s_l + jnp.sum(dlog_a * ra, axis=0, keepdims=True)
            return dh[0:1, :], a[0:1, :], s_a, s_x, s_l

        zero = jnp.zeros((1, tc), F32)
        _, _, s_a, s_x, s_l = lax.fori_loop(0, blocks, step, (zero, zero, zero, zero, zero))
        _accumulate_rows(pl.program_id(1) == 0, stat_ref, [s_a, s_x, s_l * (LRU_C * dlog_lam)])

    seq = pl.BlockSpec((s, tc), lambda j, bb: (bb, j))
    vec = pl.BlockSpec((1, tc), lambda j, bb: (0, j))
    half = jax.ShapeDtypeStruct((t, w), BF16)
    return pl.pallas_call(
        body, name=name, grid=(w // tc, b), in_specs=[seq, seq, seq, seq, seq, seq, vec, vec, vec],
        out_specs=[seq, seq, seq, seq, pl.BlockSpec((3, tc), lambda j, bb: (0, j))],
        out_shape=[half, half, half, jax.ShapeDtypeStruct((t, w), F32), jax.ShapeDtypeStruct((3, w), F32)],
        compiler_params=_params("parallel", "arbitrary"),
    )(dy, ga, gi, xr, yg, h, ba.reshape(1, w), bx.reshape(1, w), lam.reshape(1, w))


def ple_fwd(x, gate, emb, *, name):
    t, d = x.shape
    tr = _tile(t, 512, SUBLANES)

    def body(x_ref, g_ref, e_ref, o_ref):
        o_ref[...] = x_ref[...] + jax.nn.sigmoid(g_ref[...]) * e_ref[...]

    row = pl.BlockSpec((tr, d), lambda i: (i, 0))
    return pl.pallas_call(body, name=name, grid=(t // tr,), in_specs=[row, row, row], out_specs=row,
                          out_shape=jax.ShapeDtypeStruct((t, d), F32), compiler_params=_params("parallel"))(x, gate, emb)


def ple_bwd(dx, gate, emb, *, name):
    t, d = dx.shape
    tr = _tile(t, 512, SUBLANES)

    def body(dx_ref, g_ref, e_ref, dg_ref, de_ref):
        sg = jax.nn.sigmoid(g_ref[...])
        dxv = dx_ref[...]
        de_ref[...] = (dxv * sg).astype(de_ref.dtype)
        dg_ref[...] = (dxv * e_ref[...] * sg * (1.0 - sg)).astype(dg_ref.dtype)

    row = pl.BlockSpec((tr, d), lambda i: (i, 0))
    half = jax.ShapeDtypeStruct((t, d), BF16)
    return pl.pallas_call(body, name=name, grid=(t // tr,), in_specs=[row, row, row], out_specs=[row, row],
                          out_shape=[half, half], compiler_params=_params("parallel"))(dx, gate, emb)


def adamw(w, g, m, v, *, name):
    shape = w.shape
    cols = shape[-1]
    rows = w.size // cols
    tr = _tile(rows, 1024, SUBLANES)
    bc1 = 1.0 / (1.0 - ADAM_B1 ** ADAM_STEP)
    bc2 = 1.0 / (1.0 - ADAM_B2 ** ADAM_STEP)

    def body(w_ref, g_ref, m_ref, v_ref, d_ref, nm_ref, nv_ref):
        gv = g_ref[...]
        nm = ADAM_B1 * m_ref[...] + (1.0 - ADAM_B1) * gv
        nv = ADAM_B2 * v_ref[...] + (1.0 - ADAM_B2) * (gv * gv)
        d_ref[...] = -ADAM_LR * ((nm * bc1) / (jnp.sqrt(nv * bc2) + ADAM_EPS) + ADAM_WD * w_ref[...])
        nm_ref[...] = nm
        nv_ref[...] = nv

    blk = pl.BlockSpec((tr, cols), lambda i: (i, 0))
    out = jax.ShapeDtypeStruct((rows, cols), F32)
    res = pl.pallas_call(body, name=name, grid=(rows // tr,), in_specs=[blk] * 4, out_specs=[blk] * 3,
                         out_shape=[out] * 3, compiler_params=_params("parallel"),
                         )(*[a.reshape(rows, cols) for a in (w, g, m, v)])
    return [r.reshape(shape) for r in res]


ANY = pl.BlockSpec(memory_space=pl.ANY)


def _place():
    return lax.axis_index("x"), lax.axis_index("y"), lax.axis_index("c")


def all_gather(v, *, name):
    rows, cols = v.shape

    def body(v_ref, out_ref, send_sems, recv_sems, local_sem):
        x, y, c = _place()
        me, sibling = (x, y, c), (x, y, 1 - c)
        chips = [(1 - x, y), (x, 1 - y), (1 - x, 1 - y)]

        def slot(px, py, pc):
            return out_ref.at[4 * px + 2 * py + pc]

        def copy(k, block, to, src=None):
            return pltpu.make_async_remote_copy(
                src_ref=slot(*block) if src is None else src, dst_ref=slot(*block),
                send_sem=send_sems.at[k], recv_sem=recv_sems.at[k], device_id=to, device_id_type=MESH)

        mine = pltpu.make_async_copy(v_ref, slot(*me), local_sem)
        mine.start()
        first = [copy(0, me, sibling, src=v_ref)]
        first += [copy(1 + j, me, (*chip, c), src=v_ref) for j, chip in enumerate(chips)]
        for cp in first:
            cp.start()
        passed = [copy(4 + j, (*chip, c), sibling) for j, chip in enumerate(chips)]
        for j, chip in enumerate(chips):
            copy(1 + j, (*chip, c), me).wait_recv()
            passed[j].start()
        copy(0, sibling, me).wait_recv()
        for j, chip in enumerate(chips):
            copy(4 + j, (*chip, 1 - c), me).wait_recv()
        for cp in first + passed:
            cp.wait_send()
        mine.wait()

    return pl.pallas_call(
        body, name=name, out_shape=jax.ShapeDtypeStruct((N_DEV, rows, cols), v.dtype), in_specs=[ANY],
        out_specs=ANY,
        scratch_shapes=[pltpu.SemaphoreType.DMA((7,)), pltpu.SemaphoreType.DMA((7,)), pltpu.SemaphoreType.DMA(())],
    )(v)


def sibling_exchange(parts, *, name):
    _, quads, rows, cols = parts.shape

    def body(p_ref, got_ref, send_sem, recv_sem):
        x, y, c = _place()
        cp = pltpu.make_async_remote_copy(src_ref=p_ref.at[1 - c], dst_ref=got_ref, send_sem=send_sem,
                                          recv_sem=recv_sem, device_id=(x, y, 1 - c), device_id_type=MESH)
        cp.start()
        cp.wait()

    return pl.pallas_call(
        body, name=name, out_shape=jax.ShapeDtypeStruct((quads, rows, cols), parts.dtype), in_specs=[ANY],
        out_specs=ANY, scratch_shapes=[pltpu.SemaphoreType.DMA(()), pltpu.SemaphoreType.DMA(())],
    )(parts)


def chip_exchange(parts, *, name):
    _, rows, cols = parts.shape

    def body(p_ref, got_ref, send_sems, recv_sems):
        x, y, c = _place()
        chips = [(1 - x, y), (x, 1 - y), (1 - x, 1 - y)]
        copies = [pltpu.make_async_remote_copy(
            src_ref=p_ref.at[2 * cx + cy], dst_ref=got_ref.at[k], send_sem=send_sems.at[k],
            recv_sem=recv_sems.at[k], device_id=(cx, cy, c), device_id_type=MESH)
            for k, (cx, cy) in enumerate(chips)]
        for cp in copies:
            cp.start()
        for cp in copies:
            cp.wait()

    return pl.pallas_call(
        body, name=name, out_shape=jax.ShapeDtypeStruct((3, rows, cols), parts.dtype), in_specs=[ANY],
        out_specs=ANY, scratch_shapes=[pltpu.SemaphoreType.DMA((3,)), pltpu.SemaphoreType.DMA((3,))],
    )(parts)


def add_sibling(parts, got, *, name):
    _, quads, rows, cols = parts.shape
    tr = _tile(rows, 2048, SUBLANES)

    def body(c_ref, p_ref, g_ref, o_ref):
        o_ref[...] = p_ref[...] + g_ref[...]

    c = lax.axis_index("c").astype(jnp.int32).reshape(1)
    return pl.pallas_call(
        body, name=name,
        grid_spec=pltpu.PrefetchScalarGridSpec(
            num_scalar_prefetch=1, grid=(quads, rows // tr),
            in_specs=[pl.BlockSpec((None, None, tr, cols), lambda q, i, c_ref: (c_ref[0], q, i, 0)),
                      pl.BlockSpec((None, tr, cols), lambda q, i, c_ref: (q, i, 0))],
            out_specs=pl.BlockSpec((None, tr, cols), lambda q, i, c_ref: (q, i, 0))),
        out_shape=jax.ShapeDtypeStruct((quads, rows, cols), parts.dtype),
        compiler_params=_params("parallel", "parallel"),
    )(c, parts, got)


def add_chips(parts, got, *, name):
    _, rows, cols = parts.shape
    tr = _tile(rows, 2048, SUBLANES)

    def body(q_ref, p_ref, g_ref, o_ref):
        o_ref[...] = ((p_ref[...] + g_ref[0]) + g_ref[1]) + g_ref[2]

    q = (2 * lax.axis_index("x") + lax.axis_index("y")).astype(jnp.int32).reshape(1)
    return pl.pallas_call(
        body, name=name,
        grid_spec=pltpu.PrefetchScalarGridSpec(
            num_scalar_prefetch=1, grid=(rows // tr,),
            in_specs=[pl.BlockSpec((None, tr, cols), lambda i, q_ref: (q_ref[0], i, 0)),
                      pl.BlockSpec((3, tr, cols), lambda i, q_ref: (0, i, 0))],
            out_specs=pl.BlockSpec((tr, cols), lambda i, q_ref: (i, 0))),
        out_shape=jax.ShapeDtypeStruct((rows, cols), parts.dtype), compiler_params=_params("parallel"),
    )(q, parts, got)


def _pack(arrays, dtype, row_align):
    pieces, spans, at = [], [], 0
    for a in arrays:
        flat = a.reshape(-1).astype(dtype)
        rows = -(-flat.size // (LANES * row_align)) * row_align
        pieces.append(jnp.pad(flat, (0, rows * LANES - flat.size)).reshape(rows, LANES))
        spans.append((at, rows))
        at += rows
    return jnp.concatenate(pieces, axis=0), spans


def _unpack(buf, spans, shapes, lead):
    out = []
    for (at, rows), shape in zip(spans, shapes):
        size = math.prod(shape)
        piece = buf[..., at:at + rows, :].reshape(*lead, rows * LANES)[..., :size]
        out.append(piece.reshape(*lead, *shape))
    return out


def _whole(gathered, axis):
    moved = jnp.moveaxis(gathered, 0, axis)
    shape = moved.shape
    return moved.reshape(*shape[:axis], shape[axis] * shape[axis + 1], *shape[axis + 2:])


def _blocks(whole, axis):
    shape = whole.shape
    cut = whole.reshape(*shape[:axis], N_DEV, shape[axis] // N_DEV, *shape[axis + 1:])
    return jnp.moveaxis(cut, axis, 0)


def _block_diag(w):
    heads, n, _ = w.shape
    eye = jnp.eye(heads, dtype=w.dtype)
    return (w[:, :, None, :] * eye[:, None, :, None]).reshape(heads * n, heads * n)


def _diag_blocks(full, heads):
    n = full.shape[0] // heads
    return jnp.stack([full[h * n:(h + 1) * n, h * n:(h + 1) * n] for h in range(heads)])


def kernel(x, p, norm_mix, attn_w_qkv, attn_w_o, rnn_w_in, rnn_conv_w, rnn_conv_b, rnn_w_gate_a, rnn_b_gate_a, rnn_w_gate_x, rnn_b_gate_x, rnn_lru_param, rnn_w_out, norm_ffn, ffn_w_up, ffn_conv_w, ffn_conv_b, ffn_w_down, norm_ple, ple_w_gate, ple_w_proj, norm_final, loss_target, m_norm_mix, m_attn_w_qkv, m_attn_w_o, m_rnn_w_in, m_rnn_conv_w, m_rnn_conv_b, m_rnn_w_gate_a, m_rnn_b_gate_a, m_rnn_w_gate_x, m_rnn_b_gate_x, m_rnn_lru_param, m_rnn_w_out, m_norm_ffn, m_ffn_w_up, m_ffn_conv_w, m_ffn_conv_b, m_ffn_w_down, m_norm_ple, m_ple_w_gate, m_ple_w_proj, m_norm_final, v_norm_mix, v_attn_w_qkv, v_attn_w_o, v_rnn_w_in, v_rnn_conv_w, v_rnn_conv_b, v_rnn_w_gate_a, v_rnn_b_gate_a, v_rnn_w_gate_x, v_rnn_b_gate_x, v_rnn_lru_param, v_rnn_w_out, v_norm_ffn, v_ffn_w_up, v_ffn_conv_w, v_ffn_conv_b, v_ffn_w_down, v_norm_ple, v_ple_w_gate, v_ple_w_proj, v_norm_final):
    given = dict(locals())
    local = {n: given[n] for n in WEIGHTS}
    bsz, seq, d = x.shape
    t = bsz * seq
    depth = norm_mix.shape[0]
    width = rnn_w_out.shape[1] * N_DEV
    ffn = ffn_w_down.shape[1] * N_DEV

    buf, spans = _pack([local[n] for n in MATMUL_WEIGHTS], BF16, 2 * SUBLANES)
    got = _unpack(all_gather(buf, name="gather_matmul_weights"), spans,
                  [local[n].shape for n in MATMUL_WEIGHTS], (N_DEV,))
    full = {n: _whole(g, SHARD_AXIS[n]) for n, g in zip(MATMUL_WEIGHTS, got)}
    buf, spans = _pack([local[n] for n in CHANNEL_WEIGHTS], F32, SUBLANES)
    got = _unpack(all_gather(buf, name="gather_channel_weights"), spans,
                  [local[n].shape for n in CHANNEL_WEIGHTS], (N_DEV,))
    full.update({n: _whole(g, SHARD_AXIS[n]) for n, g in zip(CHANNEL_WEIGHTS, got)})
    for n in REPLICATED:
        full[n] = local[n]

    grads = {}

    def stack(name, layer, value, count):
        grads.setdefault(name, [None] * count)[layer] = value

    saved = []
    h0 = x.reshape(t, d)
    for i in range(depth):
        slot = i // 2
        sv = {"x0": h0}
        hn = rms_fwd(h0, full["norm_mix"][i], name=f"l{i}_mix_norm")
        sv["hn"] = hn
        if i % 2 == 0:
            qkv = mm(hn, full["attn_w_qkv"][slot], out_dtype=BF16, name=f"l{i}_qkv")
            o, totals = attn_fwd(qkv, bsz, seq, d, name=f"l{i}_attn")
            h1 = mm(o, full["attn_w_o"][slot], extras=(h0,), epilogue=_add, name=f"l{i}_attn_out")
            sv.update(qkv=qkv, o=o, totals=totals)
        else:
            w_in = full["rnn_w_in"][slot]
            yg = mm(hn, w_in[:, :width], name=f"l{i}_rnn_in_gate")
            yr = mm(hn, w_in[:, width:], name=f"l{i}_rnn_in_rec")
            xr = rnn_conv_fwd(yr, full["rnn_conv_w"][slot], full["rnn_conv_b"][slot], seq, name=f"l{i}_rnn_conv")
            wa = _block_diag(full["rnn_w_gate_a"][slot]).astype(BF16)
            wx = _block_diag(full["rnn_w_gate_x"][slot]).astype(BF16)
            ga = mm(xr, wa, name=f"l{i}_rnn_gate_a")
            gi = mm(xr, wx, name=f"l{i}_rnn_gate_x")
            hs, y = rnn_scan_fwd(ga, gi, xr, yg, full["rnn_b_gate_a"][slot], full["rnn_b_gate_x"][slot],
                                 full["rnn_lru_param"][slot], bsz, seq, name=f"l{i}_rnn_scan")
            h1 = mm(y, full["rnn_w_out"][slot], extras=(h0,), epilogue=_add, name=f"l{i}_rnn_out")
            sv.update(yg=yg, yr=yr, xr=xr, wa=wa, wx=wx, ga=ga, gi=gi, hs=hs, y=y)
        sv["x1"] = h1
        hn2 = rms_fwd(h1, full["norm_ffn"][i], name=f"l{i}_ffn_norm")
        w_up = full["ffn_w_up"][i]
        ug = mm(hn2, w_up[:, :ffn], name=f"l{i}_ffn_up_gate")
        uv = mm(hn2, w_up[:, ffn:], name=f"l{i}_ffn_up_val")
        cw, cb = full["ffn_conv_w"][i], full["ffn_conv_b"][i]
        act = ffn_act_fwd(ug, uv, cw[:, :ffn], cw[:, ffn:], cb[:ffn], cb[ffn:], seq, name=f"l{i}_ffn_act")
        h2 = mm(act, full["ffn_w_down"][i], extras=(h1,), epilogue=_add, name=f"l{i}_ffn_down")
        sv.update(hn2=hn2, ug=ug, uv=uv, act=act, x2=h2)
        hn3 = rms_fwd(h2, full["norm_ple"][i], name=f"l{i}_ple_norm")
        pg = mm(hn3, full["ple_w_gate"][i], name=f"l{i}_ple_gate")
        pin = p[i].reshape(t, p.shape[-1])
        pe = mm(pin, full["ple_w_proj"][i], name=f"l{i}_ple_proj")
        h0 = ple_fwd(h2, pg, pe, name=f"l{i}_ple_mix")
        sv.update(hn3=hn3, pg=pg, pe=pe, pin=pin)
        saved.append(sv)

    dx, g_final, loss_part = final_loss(h0, full["norm_final"], loss_target.reshape(t, d), name="final_loss")
    grads["norm_final"] = g_final
    loss = lax.psum(loss_part, ("x", "y", "c"))

    for i in reversed(range(depth)):
        slot = i // 2
        sv = saved[i]
        dpg, dpe = ple_bwd(dx, sv["pg"], sv["pe"], name=f"l{i}_ple_mix_bwd")
        stack("ple_w_proj", i, mm(sv["pin"], dpe, ta=True, name=f"l{i}_ple_proj_wgrad"), depth)
        stack("ple_w_gate", i, mm(sv["hn3"], dpg, ta=True, name=f"l{i}_ple_gate_wgrad"), depth)
        dhn3 = mm(dpg, full["ple_w_gate"][i], tb=True, name=f"l{i}_ple_gate_dgrad")
        dx, gn = rms_bwd(sv["x2"], full["norm_ple"][i], dhn3, dx, name=f"l{i}_ple_norm_bwd")
        stack("norm_ple", i, gn, depth)
        stack("ffn_w_down", i, mm(sv["act"], dx, ta=True, name=f"l{i}_ffn_down_wgrad"), depth)
        dact = mm(dx, full["ffn_w_down"][i], tb=True, out_dtype=BF16, name=f"l{i}_ffn_down_dgrad")
        cw, cb = full["ffn_conv_w"][i], full["ffn_conv_b"][i]
        taps = cw.shape[0]
        dcg, dcv, sg, svv = ffn_act_bwd(sv["ug"], sv["uv"], cw[:, :ffn], cw[:, ffn:], cb[:ffn], cb[ffn:], dact, seq,
                                        name=f"l{i}_ffn_act_bwd")
        stack("ffn_conv_w", i, jnp.concatenate([sg[:taps], svv[:taps]], axis=1), depth)
        stack("ffn_conv_b", i, jnp.concatenate([sg[taps], svv[taps]], axis=0), depth)
        dug = conv_input_grad(dcg, cw[:, :ffn], seq, out_dtype=BF16, name=f"l{i}_ffn_conv_bwd_gate")
        duv = conv_input_grad(dcv, cw[:, ffn:], seq, out_dtype=BF16, name=f"l{i}_ffn_conv_bwd_val")
        stack("ffn_w_up", i, jnp.concatenate(
            [mm(sv["hn2"], dug, ta=True, name=f"l{i}_ffn_up_wgrad_gate"),
             mm(sv["hn2"], duv, ta=True, name=f"l{i}_ffn_up_wgrad_val")], axis=1), depth)
        w_up = full["ffn_w_up"][i]
        dhn2 = mm(dug, w_up[:, :ffn], tb=True, name=f"l{i}_ffn_up_dgrad_gate")
        dhn2 = mm(duv, w_up[:, ffn:], tb=True, extras=(dhn2,), epilogue=_add, name=f"l{i}_ffn_up_dgrad_val")
        dx, gn = rms_bwd(sv["x1"], full["norm_ffn"][i], dhn2, dx, name=f"l{i}_ffn_norm_bwd")
        stack("norm_ffn", i, gn, depth)
        if i % 2 == 0:
            stack("attn_w_o", slot, mm(sv["o"], dx, ta=True, name=f"l{i}_attn_out_wgrad"), depth // 2)
            do = mm(dx, full["attn_w_o"][slot], tb=True, out_dtype=BF16, name=f"l{i}_attn_out_dgrad")
            dq, dk, dv = attn_bwd(sv["qkv"], sv["totals"], do, bsz, seq, d, name=f"l{i}_attn_bwd")
            dqkv = jnp.concatenate([dq, dk, dv], axis=1).astype(BF16)
            stack("attn_w_qkv", slot, mm(sv["hn"], dqkv, ta=True, name=f"l{i}_qkv_wgrad"), depth // 2)
            dhn = mm(dqkv, full["attn_w_qkv"][slot], tb=True, name=f"l{i}_qkv_dgrad")
        else:
            nrnn = depth // 2
            stack("rnn_w_out", slot, mm(sv["y"], dx, ta=True, name=f"l{i}_rnn_out_wgrad"), nrnn)
            dy = mm(dx, full["rnn_w_out"][slot], tb=True, name=f"l{i}_rnn_out_dgrad")
            dyg, dga, dgi, dxr, stats = rnn_scan_bwd(
                dy, sv["ga"], sv["gi"], sv["xr"], sv["yg"], sv["hs"], full["rnn_b_gate_a"][slot],
                full["rnn_b_gate_x"][slot], full["rnn_lru_param"][slot], bsz, seq, name=f"l{i}_rnn_scan_bwd")
            stack("rnn_b_gate_a", slot, stats[0], nrnn)
            stack("rnn_b_gate_x", slot, stats[1], nrnn)
            stack("rnn_lru_param", slot, stats[2], nrnn)
            stack("rnn_w_gate_a", slot, _diag_blocks(mm(sv["xr"], dga, ta=True, name=f"l{i}_rnn_gate_a_wgrad"),
                                                     RNN_HEADS), nrnn)
            stack("rnn_w_gate_x", slot, _diag_blocks(mm(sv["xr"], dgi, ta=True, name=f"l{i}_rnn_gate_x_wgrad"),
                                                     RNN_HEADS), nrnn)
            dxr = mm(dga, sv["wa"], tb=True, extras=(dxr,), epilogue=_add, name=f"l{i}_rnn_gate_a_dgrad")
            dxr = mm(dgi, sv["wx"], tb=True, extras=(dxr,), epilogue=_add, name=f"l{i}_rnn_gate_x_dgrad")
            rcw = full["rnn_conv_w"][slot]
            rtaps = rcw.shape[0]
            cstats = rnn_conv_wgrad(dxr, sv["yr"], seq, rtaps, name=f"l{i}_rnn_conv_wgrad")
            stack("rnn_conv_w", slot, cstats[:rtaps], nrnn)
            stack("rnn_conv_b", slot, cstats[rtaps], nrnn)
            dyr = conv_input_grad(dxr, rcw, seq, out_dtype=BF16, name=f"l{i}_rnn_conv_bwd")
            stack("rnn_w_in", slot, jnp.concatenate(
                [mm(sv["hn"], dyg, ta=True, name=f"l{i}_rnn_in_wgrad_gate"),
                 mm(sv["hn"], dyr, ta=True, name=f"l{i}_rnn_in_wgrad_rec")], axis=1), nrnn)
            w_in = full["rnn_w_in"][slot]
            dhn = mm(dyg, w_in[:, :width], tb=True, name=f"l{i}_rnn_in_dgrad_gate")
            dhn = mm(dyr, w_in[:, width:], tb=True, extras=(dhn,), epilogue=_add, name=f"l{i}_rnn_in_dgrad_rec")
        dx, gn = rms_bwd(sv["x0"], full["norm_mix"][i], dhn, dx, name=f"l{i}_mix_norm_bwd")
        stack("norm_mix", i, gn, depth)
    grad_x = dx.reshape(bsz, seq, d)

    whole = {n: (jnp.stack(g) if isinstance(g, list) else g) for n, g in grads.items()}

    sharded = [n for n in WEIGHTS if n in SHARD_AXIS]
    cut = [_blocks(whole[n], SHARD_AXIS[n]).reshape(N_DEV, -1) for n in sharded]
    rep = jnp.concatenate([whole[n].reshape(-1) for n in REPLICATED])
    rep_len = rep.size
    rep_rows = -(-rep_len // (N_DEV * LANES * SUBLANES)) * SUBLANES
    rep = jnp.pad(rep, (0, N_DEV * rep_rows * LANES - rep_len)).reshape(N_DEV, rep_rows * LANES)
    pieces, spans, at = [], [], 0
    for a in cut + [rep]:
        rows = -(-a.shape[1] // (LANES * SUBLANES)) * SUBLANES
        pieces.append(jnp.pad(a, ((0, 0), (0, rows * LANES - a.shape[1]))).reshape(N_DEV, rows, LANES))
        spans.append((at, rows))
        at += rows
    parts = jnp.concatenate(pieces, axis=1)
    parts = parts.reshape(4, 2, at, LANES).transpose(1, 0, 2, 3)
    from_sibling = sibling_exchange(parts, name="grads_to_sibling")
    chip_sum = add_sibling(parts, from_sibling, name="grads_add_sibling")
    from_chips = chip_exchange(chip_sum, name="grads_to_chips")
    mine = add_chips(chip_sum, from_chips, name="grads_add_chips")
    local_grads = dict(zip(sharded, _unpack(mine, spans[:-1], [local[n].shape for n in sharded], ())))
    rep_at, _ = spans[-1]
    rep_all = all_gather(mine[rep_at:rep_at + rep_rows], name="gather_replicated_grads").reshape(-1)[:rep_len]
    at = 0
    for n in REPLICATED:
        local_grads[n] = rep_all[at:at + local[n].size].reshape(local[n].shape)
        at += local[n].size

    deltas, new_m, new_v = {}, {}, {}
    for n in WEIGHTS:
        deltas[n], new_m[n], new_v[n] = adamw(local[n], local_grads[n], given["m_" + n], given["v_" + n],
                                              name=f"adamw_{n}")
    return (loss, grad_x, *[local_grads[n] for n in WEIGHTS], *[deltas[n] for n in WEIGHTS],
            *[new_m[n] for n in WEIGHTS], *[new_v[n] for n in WEIGHTS])
```

```python
import functools
import math

import jax
import jax.numpy as jnp
from jax import lax
from jax.experimental import pallas as pl
from jax.experimental.pallas import tpu as pltpu

F32 = jnp.float32
BF16 = jnp.bfloat16

EPS = 1e-6
HEAD_DIM = 64
RNN_HEADS = 16
LRU_C = 8.0
ADAM_LR = 0.001
ADAM_B1 = 0.9
ADAM_B2 = 0.999
ADAM_EPS = 1e-08
ADAM_WD = 0.01
ADAM_STEP = 10

N_DEV = 8
LANES = 128
SUBLANES = 8
VMEM_LIMIT = 56 * 1024 * 1024
MESH = pl.DeviceIdType.MESH
GRAD_ROWS_TILE = 2048
GELU_C = math.sqrt(2.0 / math.pi)
GELU_A = 0.044715

WEIGHTS = ['norm_mix', 'attn_w_qkv', 'attn_w_o', 'rnn_w_in', 'rnn_conv_w', 'rnn_conv_b', 'rnn_w_gate_a',
           'rnn_b_gate_a', 'rnn_w_gate_x', 'rnn_b_gate_x', 'rnn_lru_param', 'rnn_w_out', 'norm_ffn', 'ffn_w_up',
           'ffn_conv_w', 'ffn_conv_b', 'ffn_w_down', 'norm_ple', 'ple_w_gate', 'ple_w_proj', 'norm_final']
SHARD_AXIS = {'attn_w_qkv': 2, 'attn_w_o': 1, 'rnn_w_in': 2, 'rnn_conv_w': 2, 'rnn_conv_b': 1, 'rnn_b_gate_a': 1,
              'rnn_b_gate_x': 1, 'rnn_lru_param': 1, 'rnn_w_out': 1, 'ffn_w_up': 2, 'ffn_conv_w': 2,
              'ffn_w_down': 1, 'ple_w_gate': 1, 'ple_w_proj': 2}
MATMUL_WEIGHTS = ['attn_w_qkv', 'attn_w_o', 'rnn_w_in', 'rnn_w_out', 'ffn_w_up', 'ffn_w_down', 'ple_w_gate',
                  'ple_w_proj']
CHANNEL_WEIGHTS = ['rnn_conv_w', 'rnn_conv_b', 'rnn_b_gate_a', 'rnn_b_gate_x', 'rnn_lru_param', 'ffn_conv_w']
REPLICATED = [n for n in WEIGHTS if n not in SHARD_AXIS]


def _params(*sem):
    return pltpu.CompilerParams(dimension_semantics=sem, vmem_limit_bytes=VMEM_LIMIT)


def _tile(dim, pref, align=LANES):
    if dim <= pref:
        return dim
    t = (pref + pref // 2) // align * align
    while t >= align:
        if dim % t == 0:
            return t
        t -= align
    return dim


def _gelu(x):
    return 0.5 * x * (1.0 + jnp.tanh(GELU_C * (x + GELU_A * x * x * x)))


def _gelu_and_grad(x):
    t = jnp.tanh(GELU_C * (x + GELU_A * x * x * x))
    g = 0.5 * x * (1.0 + t)
    dg = 0.5 * (1.0 + t) + 0.5 * x * (1.0 - t * t) * GELU_C * (1.0 + 3.0 * GELU_A * x * x)
    return g, dg


def _log_sigmoid(x):
    return jnp.minimum(x, 0.0) - jnp.log(1.0 + jnp.exp(-jnp.abs(x)))


MM_VMEM_BUDGET = 36 * 1024 * 1024


def _mm_tiles(m, n, k, ta, a_item, b_item, out_item, n_extra):
    if ta:
        return _tile(m, 1024), _tile(n, 1024), _tile(k, 1024)
    row_bytes = k * a_item + n * (out_item + 4 * n_extra)
    w_bytes = k * n * b_item
    for tm in (1024, 512, 256, 128):
        if m % tm == 0 and 2 * tm * row_bytes + 2 * w_bytes + tm * n * 4 <= MM_VMEM_BUDGET:
            return tm, n, k
    return _tile(m, 512), _tile(n, 512), _tile(k, 1024)


def mm(a, b, *, name, ta=False, tb=False, out_dtype=F32, extras=(), epilogue=None):
    m, k = (a.shape[1], a.shape[0]) if ta else a.shape
    n = b.shape[0] if tb else b.shape[1]
    assert k == (b.shape[1] if tb else b.shape[0]), (a.shape, b.shape, ta, tb)
    tm, tn, tk = _mm_tiles(m, n, k, ta, a.dtype.itemsize, b.dtype.itemsize, jnp.dtype(out_dtype).itemsize,
                           len(extras))
    nk = k // tk
    n_extra = len(extras)
    dims = (((0 if ta else 1,), (1 if tb else 0,)), ((), ()))

    def body(a_ref, b_ref, *rest):
        extra_refs, o_ref = rest[:n_extra], rest[n_extra]

        def finish(acc):
            if epilogue is not None:
                acc = epilogue(acc, *[e[...] for e in extra_refs])
            o_ref[...] = acc.astype(o_ref.dtype)

        part = lax.dot_general(a_ref[...].astype(BF16), b_ref[...].astype(BF16), dims,
                               preferred_element_type=F32)
        if nk == 1:
            finish(part)
        else:
            acc_ref = rest[n_extra + 1]
            kk = pl.program_id(2)

            @pl.when(kk == 0)
            def _():
                acc_ref[...] = part

            @pl.when(kk > 0)
            def _():
                acc_ref[...] += part

            @pl.when(kk == nk - 1)
            def _():
                finish(acc_ref[...])

    a_spec = pl.BlockSpec((tk, tm), lambda i, j, kk: (kk, i)) if ta else pl.BlockSpec((tm, tk), lambda i, j, kk: (i, kk))
    b_spec = pl.BlockSpec((tn, tk), lambda i, j, kk: (j, kk)) if tb else pl.BlockSpec((tk, tn), lambda i, j, kk: (kk, j))
    o_spec = pl.BlockSpec((tm, tn), lambda i, j, kk: (i, j))
    return pl.pallas_call(
        body, name=name, grid=(m // tm, n // tn, nk),
        in_specs=[a_spec, b_spec] + [o_spec] * n_extra, out_specs=o_spec,
        out_shape=jax.ShapeDtypeStruct((m, n), out_dtype),
        scratch_shapes=[pltpu.VMEM((tm, tn), F32)] if nk > 1 else [],
        compiler_params=_params("parallel", "parallel", "arbitrary"),
    )(a, b, *extras)


def _add(acc, res):
    return acc + res


def rms_fwd(x, g, *, name):
    t, d = x.shape
    tr = _tile(t, 512, SUBLANES)

    def body(x_ref, g_ref, o_ref):
        xv = x_ref[...]
        r = lax.rsqrt(jnp.mean(xv * xv, axis=-1, keepdims=True) + EPS)
        o_ref[...] = (xv * r * g_ref[...]).astype(o_ref.dtype)

    row = pl.BlockSpec((tr, d), lambda i: (i, 0))
    return pl.pallas_call(
        body, name=name, grid=(t // tr,), in_specs=[row, pl.BlockSpec((1, d), lambda i: (0, 0))], out_specs=row,
        out_shape=jax.ShapeDtypeStruct((t, d), BF16), compiler_params=_params("parallel"),
    )(x, g.reshape(1, d))


def rms_bwd(x, g, dh, dres, *, name):
    t, d = x.shape
    tr = _tile(t, 512, SUBLANES)

    def body(x_ref, g_ref, dh_ref, dres_ref, dx_ref, dg_ref):
        xv = x_ref[...]
        dhv = dh_ref[...].astype(F32)
        r = lax.rsqrt(jnp.mean(xv * xv, axis=-1, keepdims=True) + EPS)
        xh = xv * r
        u = dhv * g_ref[...]
        dx_ref[...] = dres_ref[...] + r * (u - xh * jnp.mean(u * xh, axis=-1, keepdims=True))
        part = jnp.sum(dhv * xh, axis=0, keepdims=True)

        @pl.when(pl.program_id(0) == 0)
        def _():
            dg_ref[...] = part

        @pl.when(pl.program_id(0) > 0)
        def _():
            dg_ref[...] += part

    row = pl.BlockSpec((tr, d), lambda i: (i, 0))
    vec = pl.BlockSpec((1, d), lambda i: (0, 0))
    dx, dg = pl.pallas_call(
        body, name=name, grid=(t // tr,), in_specs=[row, vec, row, row], out_specs=[row, vec],
        out_shape=[jax.ShapeDtypeStruct((t, d), F32), jax.ShapeDtypeStruct((1, d), F32)],
        compiler_params=_params("arbitrary"),
    )(x, g.reshape(1, d), dh, dres)
    return dx, dg.reshape(d)


def final_loss(x, g, target, *, name):
    t, d = x.shape
    tr = _tile(t, 512, SUBLANES)

    def body(x_ref, g_ref, t_ref, dx_ref, dg_ref, loss_ref):
        xv = x_ref[...]
        gv = g_ref[...]
        r = lax.rsqrt(jnp.mean(xv * xv, axis=-1, keepdims=True) + EPS)
        xh = xv * r
        err = xh * gv - t_ref[...]
        dy = err * (1.0 / d)
        u = dy * gv
        dx_ref[...] = r * (u - xh * jnp.mean(u * xh, axis=-1, keepdims=True))
        dg_part = jnp.sum(dy * xh, axis=0, keepdims=True)
        loss_part = jnp.zeros((1, LANES), F32) + (0.5 / d) * jnp.sum(err * err)

        @pl.when(pl.program_id(0) == 0)
        def _():
            dg_ref[...] = dg_part
            loss_ref[...] = loss_part

        @pl.when(pl.program_id(0) > 0)
        def _():
            dg_ref[...] += dg_part
            loss_ref[...] += loss_part

    row = pl.BlockSpec((tr, d), lambda i: (i, 0))
    vec = pl.BlockSpec((1, d), lambda i: (0, 0))
    dx, dg, loss = pl.pallas_call(
        body, name=name, grid=(t // tr,), in_specs=[row, vec, row],
        out_specs=[row, vec, pl.BlockSpec((1, LANES), lambda i: (0, 0))],
        out_shape=[jax.ShapeDtypeStruct((t, d), F32), jax.ShapeDtypeStruct((1, d), F32),
                   jax.ShapeDtypeStruct((1, LANES), F32)],
        compiler_params=_params("arbitrary"),
    )(x, g.reshape(1, d), target)
    return dx, dg.reshape(d), loss[0, 0]


def _split_dot(x, mat, left):
    hi = x.astype(BF16)
    lo = (x - hi.astype(F32)).astype(BF16)
    if left:
        return (jnp.dot(mat, hi, preferred_element_type=F32) + jnp.dot(mat, lo, preferred_element_type=F32))
    return (jnp.dot(hi, mat, preferred_element_type=F32) + jnp.dot(lo, mat, preferred_element_type=F32))


_NT = (((1,), (1,)), ((), ()))
_TN = (((0,), (0,)), ((), ()))
HEADS_PER_STEP = LANES // HEAD_DIM


def attn_fwd(qkv, b, s, d, *, name):
    t = b * s
    tq = min(256, s)
    nq = s // tq
    pairs = d // LANES
    scale = HEAD_DIM ** -0.5

    def body(q_ref, k_ref, v_ref, o_ref, lt_ref):
        i = pl.program_id(2)
        row = lax.broadcasted_iota(jnp.int32, (tq, tq), 0)
        col = lax.broadcasted_iota(jnp.int32, (tq, tq), 1)
        later = (row > col).astype(BF16)
        causal = col < row
        lanes = [slice(HEAD_DIM * h, HEAD_DIM * (h + 1)) for h in range(HEADS_PER_STEP)]
        qs = [(q_ref[:, sl].astype(F32) * scale).astype(BF16) for sl in lanes]

        def block(j, carry, diag):
            start = pl.multiple_of(j * tq, tq)
            out = []
            for h, sl in enumerate(lanes):
                run, acc = carry[h]
                kb = k_ref[pl.ds(start, tq), sl]
                vb = v_ref[pl.ds(start, tq), sl]
                z = lax.dot_general(qs[h], kb, _NT, preferred_element_type=F32)
                ls = _log_sigmoid(z)
                lk = ls - z
                if diag:
                    lk = jnp.where(causal, lk, 0.0)
                w = jnp.exp(ls + _split_dot(lk, later, left=False) + run)
                if diag:
                    w = jnp.where(causal, w, 0.0)
                acc = acc + jnp.dot(w.astype(BF16), vb, preferred_element_type=F32)
                run = run + jnp.sum(lk, axis=1, keepdims=True)
                out.append((run, acc))
            return tuple(out)

        zero = (jnp.zeros((tq, 1), F32), jnp.zeros((tq, HEAD_DIM), F32))
        carry = block(i, (zero,) * HEADS_PER_STEP, True)
        carry = lax.fori_loop(0, i, lambda jj, c: block(i - 1 - jj, c, False), carry)
        eye = (row == col).astype(F32)
        for h, sl in enumerate(lanes):
            run, acc = carry[h]
            o_ref[:, sl] = acc.astype(o_ref.dtype)
            lt_ref[SUBLANES * h:SUBLANES * (h + 1), :] = lax.dot_general(
                jnp.broadcast_to(run, (tq, SUBLANES)), eye, _TN, precision=lax.Precision.HIGHEST,
                preferred_element_type=F32)

    q_spec = pl.BlockSpec((tq, LANES), lambda bb, p, i: (bb * nq + i, p))
    k_spec = pl.BlockSpec((s, LANES), lambda bb, p, i: (bb, pairs + p))
    v_spec = pl.BlockSpec((s, LANES), lambda bb, p, i: (bb, 2 * pairs + p))
    lt_spec = pl.BlockSpec((None, None, None, HEADS_PER_STEP * SUBLANES, tq), lambda bb, p, i: (bb, p, i, 0, 0))
    return pl.pallas_call(
        body, name=name, grid=(b, pairs, nq), in_specs=[q_spec, k_spec, v_spec], out_specs=[q_spec, lt_spec],
        out_shape=[jax.ShapeDtypeStruct((t, d), BF16),
                   jax.ShapeDtypeStruct((b, pairs, nq, HEADS_PER_STEP * SUBLANES, tq), F32)],
        compiler_params=_params("parallel", "parallel", "arbitrary"),
    )(qkv, qkv, qkv)


def attn_bwd(qkv, totals, do, b, s, d, *, name):
    t = b * s
    tq = min(256, s)
    nq = s // tq
    pairs = d // LANES
    scale = HEAD_DIM ** -0.5

    def body(q_ref, k_ref, v_ref, lt_ref, do_ref, dq_ref, dk_ref, dv_ref):
        i = pl.program_id(2)

        @pl.when(i == 0)
        def _():
            dk_ref[...] = jnp.zeros_like(dk_ref)
            dv_ref[...] = jnp.zeros_like(dv_ref)

        row = lax.broadcasted_iota(jnp.int32, (tq, tq), 0)
        col = lax.broadcasted_iota(jnp.int32, (tq, tq), 1)
        upto = (col <= row).astype(BF16)
        earlier = (col < row).astype(BF16)
        causal = row < col
        lanes = [slice(HEAD_DIM * h, HEAD_DIM * (h + 1)) for h in range(HEADS_PER_STEP)]
        qs = [(q_ref[:, sl].astype(F32) * scale).astype(BF16) for sl in lanes]
        dos = [do_ref[:, sl].astype(BF16) for sl in lanes]
        totals_h = [lt_ref[SUBLANES * h:SUBLANES * h + 1, :] for h in range(HEADS_PER_STEP)]

        def block(j, carry, diag):
            start = pl.multiple_of(j * tq, tq)
            out = []
            for h, sl in enumerate(lanes):
                run, grun, dq = carry[h]
                kb = k_ref[pl.ds(start, tq), sl]
                vb = v_ref[pl.ds(start, tq), sl]
                z = lax.dot_general(kb, qs[h], _NT, preferred_element_type=F32)
                ls = _log_sigmoid(z)
                lk = ls - z
                if diag:
                    lk = jnp.where(causal, lk, 0.0)
                w = jnp.exp(ls + ((totals_h[h] - run) - _split_dot(lk, upto, left=True)))
                if diag:
                    w = jnp.where(causal, w, 0.0)
                g = lax.dot_general(vb, dos[h], _NT, preferred_element_type=F32) * w
                before = grun + _split_dot(g, earlier, left=True)
                dz = g - jnp.exp(ls) * (g + before)
                if diag:
                    dz = jnp.where(causal, dz, 0.0)
                dzb = dz.astype(BF16)
                dv_ref[pl.ds(start, tq), sl] += jnp.dot(w.astype(BF16), dos[h], preferred_element_type=F32)
                dk_ref[pl.ds(start, tq), sl] += jnp.dot(dzb, qs[h], preferred_element_type=F32)
                dq = dq + lax.dot_general(dzb, kb, _TN, preferred_element_type=F32)
                run = run + jnp.sum(lk, axis=0, keepdims=True)
                grun = grun + jnp.sum(g, axis=0, keepdims=True)
                out.append((run, grun, dq))
            return tuple(out)

        zero = (jnp.zeros((1, tq), F32), jnp.zeros((1, tq), F32), jnp.zeros((tq, HEAD_DIM), F32))
        carry = lax.fori_loop(0, i, lambda j, c: block(j, c, False), (zero,) * HEADS_PER_STEP)
        carry = block(i, carry, True)
        for h, sl in enumerate(lanes):
            dq_ref[:, sl] = carry[h][2] * scale

    q_spec = pl.BlockSpec((tq, LANES), lambda bb, p, i: (bb * nq + i, p))
    k_spec = pl.BlockSpec((s, LANES), lambda bb, p, i: (bb, pairs + p))
    v_spec = pl.BlockSpec((s, LANES), lambda bb, p, i: (bb, 2 * pairs + p))
    lt_spec = pl.BlockSpec((None, None, None) + totals.shape[3:], lambda bb, p, i: (bb, p, i, 0, 0))
    kv_out = pl.BlockSpec((s, LANES), lambda bb, p, i: (bb, p))
    out = jax.ShapeDtypeStruct((t, d), F32)
    return pl.pallas_call(
        body, name=name, grid=(b, pairs, nq), in_specs=[q_spec, k_spec, v_spec, lt_spec, q_spec],
        out_specs=[q_spec, kv_out, kv_out], out_shape=[out, out, out],
        compiler_params=_params("parallel", "parallel", "arbitrary"),
    )(qkv, qkv, qkv, totals, do)


def _shift_down(cur, prev8, dist):
    ext = jnp.concatenate([prev8, cur], axis=0)
    return pltpu.roll(ext, dist, 0)[SUBLANES:]


def _shift_up(cur, next8, dist):
    ext = jnp.concatenate([cur, next8], axis=0)
    return pltpu.roll(ext, ext.shape[0] - dist, 0)[:cur.shape[0]]


def _causal_conv(cur, prev8, w_ref, b_ref):
    taps = w_ref.shape[0]
    out = cur * w_ref[taps - 1:taps, :] + b_ref[...]
    for dist in range(1, taps):
        out = out + _shift_down(cur, prev8, dist) * w_ref[taps - 1 - dist:taps - dist, :]
    return out


def _conv_specs(t, rows, tc, time_axis, dtype=F32):
    sub = SUBLANES * (4 // jnp.dtype(dtype).itemsize)
    per = rows // sub
    last = t // sub - 1

    def grid_ids(*ids):
        return ids[time_axis], ids[1 - time_axis]

    def cur(*ids):
        return grid_ids(*ids)

    def prev(*ids):
        i, j = grid_ids(*ids)
        return (jnp.maximum(i * per - 1, 0), j)

    def nxt(*ids):
        i, j = grid_ids(*ids)
        return (jnp.minimum((i + 1) * per, last), j)

    def chan(*ids):
        return (0, grid_ids(*ids)[1])

    return pl.BlockSpec((rows, tc), cur), pl.BlockSpec((sub, tc), prev), pl.BlockSpec((sub, tc), nxt), chan


def _rows_before(ref, keep):
    return ref[...].astype(F32)[-SUBLANES:] * keep


def _rows_after(ref, keep):
    return ref[...].astype(F32)[:SUBLANES] * keep


def _first_in_seq(i, rows, s):
    return (i % (s // rows)) == 0


def _last_in_seq(i, rows, s):
    return (i % (s // rows)) == (s // rows - 1)


def ffn_act_fwd(ug, uv, cwg, cwv, cbg, cbv, s, *, name):
    t, f = ug.shape
    rows, tc = _tile(s, 512, SUBLANES), _tile(f, 256)
    cur, prev, _, chan = _conv_specs(t, rows, tc, 0, ug.dtype)
    taps = cwg.shape[0]

    def body(ug_ref, ugp_ref, uv_ref, uvp_ref, cwg_ref, cwv_ref, cbg_ref, cbv_ref, a_ref):
        keep = jnp.where(_first_in_seq(pl.program_id(0), rows, s), 0.0, 1.0)
        gate = _causal_conv(ug_ref[...].astype(F32), _rows_before(ugp_ref, keep), cwg_ref, cbg_ref)
        val = _causal_conv(uv_ref[...].astype(F32), _rows_before(uvp_ref, keep), cwv_ref, cbv_ref)
        a_ref[...] = (_gelu(gate) * val).astype(a_ref.dtype)

    wspec = pl.BlockSpec((taps, tc), chan)
    bspec = pl.BlockSpec((1, tc), chan)
    return pl.pallas_call(
        body, name=name, grid=(t // rows, f // tc), in_specs=[cur, prev, cur, prev, wspec, wspec, bspec, bspec],
        out_specs=cur, out_shape=jax.ShapeDtypeStruct((t, f), BF16), compiler_params=_params("parallel", "parallel"),
    )(ug, ug, uv, uv, cwg, cwv, cbg.reshape(1, f), cbv.reshape(1, f))


def _accumulate_rows(first, ref, rows):
    for k, r in enumerate(rows):
        @pl.when(first)
        def _(k=k, r=r):
            ref[k:k + 1, :] = r

        @pl.when(jnp.logical_not(first))
        def _(k=k, r=r):
            ref[k:k + 1, :] += r


def _conv_weight_grads(dc, cur, prev8, taps):
    out = []
    for k in range(taps):
        dist = taps - 1 - k
        xs = cur if dist == 0 else _shift_down(cur, prev8, dist)
        out.append(jnp.sum(dc * xs, axis=0, keepdims=True))
    out.append(jnp.sum(dc, axis=0, keepdims=True))
    return out


def ffn_act_bwd(ug, uv, cwg, cwv, cbg, cbv, da, s, *, name):
    t, f = ug.shape
    rows, tc = _tile(s, 512, SUBLANES), _tile(f, 256)
    cur, prev, _, chan = _conv_specs(t, rows, tc, 1, ug.dtype)
    taps = cwg.shape[0]

    def body(ug_ref, ugp_ref, uv_ref, uvp_ref, cwg_ref, cwv_ref, cbg_ref, cbv_ref, da_ref,
             dcg_ref, dcv_ref, wg_ref, wv_ref):
        i = pl.program_id(1)
        keep = jnp.where(_first_in_seq(i, rows, s), 0.0, 1.0)
        ugc, ugp = ug_ref[...].astype(F32), _rows_before(ugp_ref, keep)
        uvc, uvp = uv_ref[...].astype(F32), _rows_before(uvp_ref, keep)
        gate = _causal_conv(ugc, ugp, cwg_ref, cbg_ref)
        val = _causal_conv(uvc, uvp, cwv_ref, cbv_ref)
        act, dact = _gelu_and_grad(gate)
        dav = da_ref[...].astype(F32)
        dgate = dav * val * dact
        dval = dav * act
        dcg_ref[...] = dgate.astype(dcg_ref.dtype)
        dcv_ref[...] = dval.astype(dcv_ref.dtype)
        _accumulate_rows(i == 0, wg_ref, _conv_weight_grads(dgate, ugc, ugp, taps))
        _accumulate_rows(i == 0, wv_ref, _conv_weight_grads(dval, uvc, uvp, taps))

    wspec = pl.BlockSpec((taps, tc), chan)
    bspec = pl.BlockSpec((1, tc), chan)
    gspec = pl.BlockSpec((taps + 1, tc), chan)
    act_shape = jax.ShapeDtypeStruct((t, f), BF16)
    stat_shape = jax.ShapeDtypeStruct((taps + 1, f), F32)
    return pl.pallas_call(
        body, name=name, grid=(f // tc, t // rows),
        in_specs=[cur, prev, cur, prev, wspec, wspec, bspec, bspec, cur],
        out_specs=[cur, cur, gspec, gspec], out_shape=[act_shape, act_shape, stat_shape, stat_shape],
        compiler_params=_params("parallel", "arbitrary"),
    )(ug, ug, uv, uv, cwg, cwv, cbg.reshape(1, f), cbv.reshape(1, f), da)


def conv_input_grad(dc, cw, s, *, name, out_dtype):
    t, f = dc.shape
    rows, tc = _tile(s, 512, SUBLANES), _tile(f, 256)
    cur, _, nxt, chan = _conv_specs(t, rows, tc, 0, dc.dtype)
    taps = cw.shape[0]

    def body(dc_ref, dcn_ref, cw_ref, o_ref):
        keep = jnp.where(_last_in_seq(pl.program_id(0), rows, s), 0.0, 1.0)
        dcc = dc_ref[...].astype(F32)
        dcn = _rows_after(dcn_ref, keep)
        out = dcc * cw_ref[taps - 1:taps, :]
        for dist in range(1, taps):
            out = out + _shift_up(dcc, dcn, dist) * cw_ref[taps - 1 - dist:taps - dist, :]
        o_ref[...] = out.astype(o_ref.dtype)

    return pl.pallas_call(
        body, name=name, grid=(t // rows, f // tc), in_specs=[cur, nxt, pl.BlockSpec((taps, tc), chan)],
        out_specs=cur, out_shape=jax.ShapeDtypeStruct((t, f), out_dtype),
        compiler_params=_params("parallel", "parallel"),
    )(dc, dc, cw)


def rnn_conv_fwd(yr, cw, cb, s, *, name):
    t, w = yr.shape
    rows, tc = _tile(s, 512, SUBLANES), _tile(w, 256)
    cur, prev, _, chan = _conv_specs(t, rows, tc, 0, yr.dtype)
    taps = cw.shape[0]

    def body(y_ref, yp_ref, cw_ref, cb_ref, o_ref):
        keep = jnp.where(_first_in_seq(pl.program_id(0), rows, s), 0.0, 1.0)
        o_ref[...] = _causal_conv(y_ref[...].astype(F32), _rows_before(yp_ref, keep), cw_ref, cb_ref)

    return pl.pallas_call(
        body, name=name, grid=(t // rows, w // tc),
        in_specs=[cur, prev, pl.BlockSpec((taps, tc), chan), pl.BlockSpec((1, tc), chan)], out_specs=cur,
        out_shape=jax.ShapeDtypeStruct((t, w), F32), compiler_params=_params("parallel", "parallel"),
    )(yr, yr, cw, cb.reshape(1, w))


def rnn_conv_wgrad(dxr, yr, s, taps, *, name):
    t, w = yr.shape
    rows, tc = _tile(s, 512, SUBLANES), _tile(w, 256)
    cur, prev, _, chan = _conv_specs(t, rows, tc, 1, yr.dtype)

    def body(d_ref, y_ref, yp_ref, o_ref):
        i = pl.program_id(1)
        keep = jnp.where(_first_in_seq(i, rows, s), 0.0, 1.0)
        grads = _conv_weight_grads(d_ref[...], y_ref[...].astype(F32), _rows_before(yp_ref, keep), taps)
        _accumulate_rows(i == 0, o_ref, grads)

    return pl.pallas_call(
        body, name=name, grid=(w // tc, t // rows), in_specs=[cur, cur, prev],
        out_specs=pl.BlockSpec((taps + 1, tc), chan), out_shape=jax.ShapeDtypeStruct((taps + 1, w), F32),
        compiler_params=_params("parallel", "arbitrary"),
    )(dxr, yr, yr)


def _one_minus_exp(x):
    series = -x * (1.0 + x * (0.5 + x * (1.0 / 6.0)))
    return jnp.where(x > -0.01, series, 1.0 - jnp.exp(x))


def _gates(ga, gi, ba, bx, log_lam):
    ra = jax.nn.sigmoid(ga + ba)
    ri = jax.nn.sigmoid(gi + bx)
    log_a = LRU_C * ra * log_lam
    a = jnp.exp(log_a)
    mult = jnp.sqrt(_one_minus_exp(2.0 * log_a))
    return ra, ri, a, mult


def rnn_scan_fwd(ga, gi, xr, yg, ba, bx, lam, b, s, *, name):
    t, w = xr.shape
    tc = _tile(w, 256)
    blocks = s // SUBLANES

    def body(ga_ref, gi_ref, xr_ref, yg_ref, ba_ref, bx_ref, lam_ref, h_ref, y_ref):
        log_lam = _log_sigmoid(lam_ref[...])
        ridx = lax.broadcasted_iota(jnp.int32, (SUBLANES, tc), 0)

        def step(n, carry):
            r0 = pl.multiple_of(n * SUBLANES, SUBLANES)
            rs = pl.ds(r0, SUBLANES)
            xrv = xr_ref[rs, :]
            _, ri, a, mult = _gates(ga_ref[rs, :], gi_ref[rs, :], ba_ref[...], bx_ref[...], log_lam)
            u = mult * (ri * xrv)
            for dist in (1, 2, 4):
                a_sh = jnp.where(ridx >= dist, pltpu.roll(a, dist, 0), 1.0)
                u_sh = jnp.where(ridx >= dist, pltpu.roll(u, dist, 0), 0.0)
                u = a * u_sh + u
                a = a * a_sh
            hb = u + a * carry
            h_ref[rs, :] = hb
            y_ref[rs, :] = (_gelu(yg_ref[rs, :]) * hb).astype(y_ref.dtype)
            return hb[SUBLANES - 1:SUBLANES, :]

        lax.fori_loop(0, blocks, step, jnp.zeros((1, tc), F32))

    seq = pl.BlockSpec((s, tc), lambda bb, j: (bb, j))
    vec = pl.BlockSpec((1, tc), lambda bb, j: (0, j))
    return pl.pallas_call(
        body, name=name, grid=(b, w // tc), in_specs=[seq, seq, seq, seq, vec, vec, vec], out_specs=[seq, seq],
        out_shape=[jax.ShapeDtypeStruct((t, w), F32), jax.ShapeDtypeStruct((t, w), BF16)],
        compiler_params=_params("parallel", "parallel"),
    )(ga, gi, xr, yg, ba.reshape(1, w), bx.reshape(1, w), lam.reshape(1, w))


def rnn_scan_bwd(dy, ga, gi, xr, yg, h, ba, bx, lam, b, s, *, name):
    t, w = xr.shape
    tc = _tile(w, 256)
    blocks = s // SUBLANES

    def body(dy_ref, ga_ref, gi_ref, xr_ref, yg_ref, h_ref, ba_ref, bx_ref, lam_ref,
             dyg_ref, dga_ref, dgi_ref, dxr_ref, stat_ref):
        lamv = lam_ref[...]
        log_lam = _log_sigmoid(lamv)
        dlog_lam = jax.nn.sigmoid(-lamv)
        ridx = lax.broadcasted_iota(jnp.int32, (SUBLANES, tc), 0)
        last = SUBLANES - 1

        def step(n, carry):
            lam_next, a_next, s_a, s_x, s_l = carry
            blk = blocks - 1 - n
            r0 = pl.multiple_of(blk * SUBLANES, SUBLANES)
            rs = pl.ds(r0, SUBLANES)
            rp = pl.ds(pl.multiple_of(jnp.maximum(blk - 1, 0) * SUBLANES, SUBLANES), SUBLANES)
            xrv = xr_ref[rs, :]
            hv = h_ref[rs, :]
            h_before = jnp.where(blk > 0, h_ref[rp, :][last:, :], 0.0)
            h_prev = jnp.where(ridx >= 1, pltpu.roll(hv, 1, 0), h_before)
            ra, ri, a, mult = _gates(ga_ref[rs, :], gi_ref[rs, :], ba_ref[...], bx_ref[...], log_lam)
            act, dact = _gelu_and_grad(yg_ref[rs, :])
            dyv = dy_ref[rs, :]
            dyg_ref[rs, :] = (dyv * hv * dact).astype(dyg_ref.dtype)
            v = dyv * act
            c = jnp.where(ridx < last, pltpu.roll(a, last, 0), a_next)
            for dist in (1, 2, 4):
                c_sh = jnp.where(ridx < SUBLANES - dist, pltpu.roll(c, SUBLANES - dist, 0), 1.0)
                v_sh = jnp.where(ridx < SUBLANES - dist, pltpu.roll(v, SUBLANES - dist, 0), 0.0)
                v = v + c * v_sh
                c = c * c_sh
            dh = v + c * lam_next
            du_ri_x = dh * xrv
            dmult = du_ri_x * ri
            dri = du_ri_x * mult
            dxr_ref[rs, :] = dh * mult * ri
            dlog_a = dh * h_prev * a - dmult * (a * a) / mult
            dra = dlog_a * (LRU_C * log_lam)
            dpa = dra * ra * (1.0 - ra)
            dpi = dri * ri * (1.0 - ri)
            dga_ref[rs, :] = dpa.astype(dga_ref.dtype)
            dgi_ref[rs, :] = dpi.astype(dgi_ref.dtype)
            s_a = s_a + jnp.sum(dpa, axis=0, keepdims=True)
            s_x = s_x + jnp.sum(dpi, axis=0, keepdims=True)
            s_l = s_l + jnp.sum(dlog_a * ra, axis=0, keepdims=True)
            return dh[0:1, :], a[0:1, :], s_a, s_x, s_l

        zero = jnp.zeros((1, tc), F32)
        _, _, s_a, s_x, s_l = lax.fori_loop(0, blocks, step, (zero, zero, zero, zero, zero))
        _accumulate_rows(pl.program_id(1) == 0, stat_ref, [s_a, s_x, s_l * (LRU_C * dlog_lam)])

    seq = pl.BlockSpec((s, tc), lambda j, bb: (bb, j))
    vec = pl.BlockSpec((1, tc), lambda j, bb: (0, j))
    half = jax.ShapeDtypeStruct((t, w), BF16)
    return pl.pallas_call(
        body, name=name, grid=(w // tc, b), in_specs=[seq, seq, seq, seq, seq, seq, vec, vec, vec],
        out_specs=[seq, seq, seq, seq, pl.BlockSpec((3, tc), lambda j, bb: (0, j))],
        out_shape=[half, half, half, jax.ShapeDtypeStruct((t, w), F32), jax.ShapeDtypeStruct((3, w), F32)],
        compiler_params=_params("parallel", "arbitrary"),
    )(dy, ga, gi, xr, yg, h, ba.reshape(1, w), bx.reshape(1, w), lam.reshape(1, w))


def ple_fwd(x, gate, emb, *, name):
    t, d = x.shape
    tr = _tile(t, 512, SUBLANES)

    def body(x_ref, g_ref, e_ref, o_ref):
        o_ref[...] = x_ref[...] + jax.nn.sigmoid(g_ref[...]) * e_ref[...]

    row = pl.BlockSpec((tr, d), lambda i: (i, 0))
    return pl.pallas_call(body, name=name, grid=(t // tr,), in_specs=[row, row, row], out_specs=row,
                          out_shape=jax.ShapeDtypeStruct((t, d), F32), compiler_params=_params("parallel"))(x, gate, emb)


def ple_bwd(dx, gate, emb, *, name):
    t, d = dx.shape
    tr = _tile(t, 512, SUBLANES)

    def body(dx_ref, g_ref, e_ref, dg_ref, de_ref):
        sg = jax.nn.sigmoid(g_ref[...])
        dxv = dx_ref[...]
        de_ref[...] = (dxv * sg).astype(de_ref.dtype)
        dg_ref[...] = (dxv * e_ref[...] * sg * (1.0 - sg)).astype(dg_ref.dtype)

    row = pl.BlockSpec((tr, d), lambda i: (i, 0))
    half = jax.ShapeDtypeStruct((t, d), BF16)
    return pl.pallas_call(body, name=name, grid=(t // tr,), in_specs=[row, row, row], out_specs=[row, row],
                          out_shape=[half, half], compiler_params=_params("parallel"))(dx, gate, emb)


def adamw(w, g, m, v, *, name):
    shape = w.shape
    cols = shape[-1]
    rows = w.size // cols
    tr = _tile(rows, 1024, SUBLANES)
    bc1 = 1.0 / (1.0 - ADAM_B1 ** ADAM_STEP)
    bc2 = 1.0 / (1.0 - ADAM_B2 ** ADAM_STEP)

    def body(w_ref, g_ref, m_ref, v_ref, d_ref, nm_ref, nv_ref):
        gv = g_ref[...]
        nm = ADAM_B1 * m_ref[...] + (1.0 - ADAM_B1) * gv
        nv = ADAM_B2 * v_ref[...] + (1.0 - ADAM_B2) * (gv * gv)
        d_ref[...] = -ADAM_LR * ((nm * bc1) / (jnp.sqrt(nv * bc2) + ADAM_EPS) + ADAM_WD * w_ref[...])
        nm_ref[...] = nm
        nv_ref[...] = nv

    blk = pl.BlockSpec((tr, cols), lambda i: (i, 0))
    out = jax.ShapeDtypeStruct((rows, cols), F32)
    res = pl.pallas_call(body, name=name, grid=(rows // tr,), in_specs=[blk] * 4, out_specs=[blk] * 3,
                         out_shape=[out] * 3, compiler_params=_params("parallel"),
                         )(*[a.reshape(rows, cols) for a in (w, g, m, v)])
    return [r.reshape(shape) for r in res]


ANY = pl.BlockSpec(memory_space=pl.ANY)


def _place():
    return lax.axis_index("x"), lax.axis_index("y"), lax.axis_index("c")


def all_gather(v, *, name):
    rows, cols = v.shape

    def body(v_ref, out_ref, send_sems, recv_sems, local_sem):
        x, y, c = _place()
        me, sibling = (x, y, c), (x, y, 1 - c)
        chips = [(1 - x, y), (x, 1 - y), (1 - x, 1 - y)]

        def slot(px, py, pc):
            return out_ref.at[4 * px + 2 * py + pc]

        def copy(k, block, to, src=None):
            return pltpu.make_async_remote_copy(
                src_ref=slot(*block) if src is None else src, dst_ref=slot(*block),
                send_sem=send_sems.at[k], recv_sem=recv_sems.at[k], device_id=to, device_id_type=MESH)

        mine = pltpu.make_async_copy(v_ref, slot(*me), local_sem)
        mine.start()
        first = [copy(0, me, sibling, src=v_ref)]
        first += [copy(1 + j, me, (*chip, c), src=v_ref) for j, chip in enumerate(chips)]
        for cp in first:
            cp.start()
        passed = [copy(4 + j, (*chip, c), sibling) for j, chip in enumerate(chips)]
        for j, chip in enumerate(chips):
            copy(1 + j, (*chip, c), me).wait_recv()
            passed[j].start()
        copy(0, sibling, me).wait_recv()
        for j, chip in enumerate(chips):
            copy(4 + j, (*chip, 1 - c), me).wait_recv()
        for cp in first + passed:
            cp.wait_send()
        mine.wait()

    return pl.pallas_call(
        body, name=name, out_shape=jax.ShapeDtypeStruct((N_DEV, rows, cols), v.dtype), in_specs=[ANY],
        out_specs=ANY,
        scratch_shapes=[pltpu.SemaphoreType.DMA((7,)), pltpu.SemaphoreType.DMA((7,)), pltpu.SemaphoreType.DMA(())],
    )(v)


def sibling_exchange(parts, *, name):
    _, quads, rows, cols = parts.shape

    def body(p_ref, got_ref, send_sem, recv_sem):
        x, y, c = _place()
        cp = pltpu.make_async_remote_copy(src_ref=p_ref.at[1 - c], dst_ref=got_ref, send_sem=send_sem,
                                          recv_sem=recv_sem, device_id=(x, y, 1 - c), device_id_type=MESH)
        cp.start()
        cp.wait()

    return pl.pallas_call(
        body, name=name, out_shape=jax.ShapeDtypeStruct((quads, rows, cols), parts.dtype), in_specs=[ANY],
        out_specs=ANY, scratch_shapes=[pltpu.SemaphoreType.DMA(()), pltpu.SemaphoreType.DMA(())],
    )(parts)


def chip_exchange(parts, *, name):
    _, rows, cols = parts.shape

    def body(p_ref, got_ref, send_sems, recv_sems):
        x, y, c = _place()
        chips = [(1 - x, y), (x, 1 - y), (1 - x, 1 - y)]
        copies = [pltpu.make_async_remote_copy(
            src_ref=p_ref.at[2 * cx + cy], dst_ref=got_ref.at[k], send_sem=send_sems.at[k],
            recv_sem=recv_sems.at[k], device_id=(cx, cy, c), device_id_type=MESH)
            for k, (cx, cy) in enumerate(chips)]
        for cp in copies:
            cp.start()
        for cp in copies:
            cp.wait()

    return pl.pallas_call(
        body, name=name, out_shape=jax.ShapeDtypeStruct((3, rows, cols), parts.dtype), in_specs=[ANY],
        out_specs=ANY, scratch_shapes=[pltpu.SemaphoreType.DMA((3,)), pltpu.SemaphoreType.DMA((3,))],
    )(parts)


def add_sibling(parts, got, *, name):
    _, quads, rows, cols = parts.shape
    tr = _tile(rows, GRAD_ROWS_TILE, SUBLANES)

    def body(c_ref, p_ref, g_ref, o_ref):
        o_ref[...] = p_ref[...] + g_ref[...]

    c = lax.axis_index("c").astype(jnp.int32).reshape(1)
    return pl.pallas_call(
        body, name=name,
        grid_spec=pltpu.PrefetchScalarGridSpec(
            num_scalar_prefetch=1, grid=(quads, rows // tr),
            in_specs=[pl.BlockSpec((None, None, tr, cols), lambda q, i, c_ref: (c_ref[0], q, i, 0)),
                      pl.BlockSpec((None, tr, cols), lambda q, i, c_ref: (q, i, 0))],
            out_specs=pl.BlockSpec((None, tr, cols), lambda q, i, c_ref: (q, i, 0))),
        out_shape=jax.ShapeDtypeStruct((quads, rows, cols), parts.dtype),
        compiler_params=_params("parallel", "parallel"),
    )(c, parts, got)


def add_chips(parts, got, *, name):
    _, rows, cols = parts.shape
    tr = _tile(rows, GRAD_ROWS_TILE, SUBLANES)

    def body(q_ref, p_ref, g_ref, o_ref):
        o_ref[...] = ((p_ref[...] + g_ref[0]) + g_ref[1]) + g_ref[2]

    q = (2 * lax.axis_index("x") + lax.axis_index("y")).astype(jnp.int32).reshape(1)
    return pl.pallas_call(
        body, name=name,
        grid_spec=pltpu.PrefetchScalarGridSpec(
            num_scalar_prefetch=1, grid=(rows // tr,),
            in_specs=[pl.BlockSpec((None, tr, cols), lambda i, q_ref: (q_ref[0], i, 0)),
                      pl.BlockSpec((3, tr, cols), lambda i, q_ref: (0, i, 0))],
            out_specs=pl.BlockSpec((tr, cols), lambda i, q_ref: (i, 0))),
        out_shape=jax.ShapeDtypeStruct((rows, cols), parts.dtype), compiler_params=_params("parallel"),
    )(q, parts, got)


def _pack(arrays, dtype, row_align):
    pieces, spans, at = [], [], 0
    for a in arrays:
        flat = a.reshape(-1).astype(dtype)
        rows = -(-flat.size // (LANES * row_align)) * row_align
        pieces.append(jnp.pad(flat, (0, rows * LANES - flat.size)).reshape(rows, LANES))
        spans.append((at, rows))
        at += rows
    return jnp.concatenate(pieces, axis=0), spans


def _unpack(buf, spans, shapes, lead):
    out = []
    for (at, rows), shape in zip(spans, shapes):
        size = math.prod(shape)
        piece = buf[..., at:at + rows, :].reshape(*lead, rows * LANES)[..., :size]
        out.append(piece.reshape(*lead, *shape))
    return out


def _whole(gathered, axis):
    moved = jnp.moveaxis(gathered, 0, axis)
    shape = moved.shape
    return moved.reshape(*shape[:axis], shape[axis] * shape[axis + 1], *shape[axis + 2:])


def _blocks(whole, axis):
    shape = whole.shape
    cut = whole.reshape(*shape[:axis], N_DEV, shape[axis] // N_DEV, *shape[axis + 1:])
    return jnp.moveaxis(cut, axis, 0)


def _block_diag(w):
    heads, n, _ = w.shape
    eye = jnp.eye(heads, dtype=w.dtype)
    return (w[:, :, None, :] * eye[:, None, :, None]).reshape(heads * n, heads * n)


def _diag_blocks(full, heads):
    n = full.shape[0] // heads
    return jnp.stack([full[h * n:(h + 1) * n, h * n:(h + 1) * n] for h in range(heads)])


def kernel(x, p, norm_mix, attn_w_qkv, attn_w_o, rnn_w_in, rnn_conv_w, rnn_conv_b, rnn_w_gate_a, rnn_b_gate_a, rnn_w_gate_x, rnn_b_gate_x, rnn_lru_param, rnn_w_out, norm_ffn, ffn_w_up, ffn_conv_w, ffn_conv_b, ffn_w_down, norm_ple, ple_w_gate, ple_w_proj, norm_final, loss_target, m_norm_mix, m_attn_w_qkv, m_attn_w_o, m_rnn_w_in, m_rnn_conv_w, m_rnn_conv_b, m_rnn_w_gate_a, m_rnn_b_gate_a, m_rnn_w_gate_x, m_rnn_b_gate_x, m_rnn_lru_param, m_rnn_w_out, m_norm_ffn, m_ffn_w_up, m_ffn_conv_w, m_ffn_conv_b, m_ffn_w_down, m_norm_ple, m_ple_w_gate, m_ple_w_proj, m_norm_final, v_norm_mix, v_attn_w_qkv, v_attn_w_o, v_rnn_w_in, v_rnn_conv_w, v_rnn_conv_b, v_rnn_w_gate_a, v_rnn_b_gate_a, v_rnn_w_gate_x, v_rnn_b_gate_x, v_rnn_lru_param, v_rnn_w_out, v_norm_ffn, v_ffn_w_up, v_ffn_conv_w, v_ffn_conv_b, v_ffn_w_down, v_norm_ple, v_ple_w_gate, v_ple_w_proj, v_norm_final):
    given = dict(locals())
    local = {n: given[n] for n in WEIGHTS}
    bsz, seq, d = x.shape
    t = bsz * seq
    depth = norm_mix.shape[0]
    width = rnn_w_out.shape[1] * N_DEV
    ffn = ffn_w_down.shape[1] * N_DEV

    buf, spans = _pack([local[n] for n in MATMUL_WEIGHTS], BF16, 2 * SUBLANES)
    got = _unpack(all_gather(buf, name="gather_matmul_weights"), spans,
                  [local[n].shape for n in MATMUL_WEIGHTS], (N_DEV,))
    full = {n: _whole(g, SHARD_AXIS[n]) for n, g in zip(MATMUL_WEIGHTS, got)}
    buf, spans = _pack([local[n] for n in CHANNEL_WEIGHTS], F32, SUBLANES)
    got = _unpack(all_gather(buf, name="gather_channel_weights"), spans,
                  [local[n].shape for n in CHANNEL_WEIGHTS], (N_DEV,))
    full.update({n: _whole(g, SHARD_AXIS[n]) for n, g in zip(CHANNEL_WEIGHTS, got)})
    for n in REPLICATED:
        full[n] = local[n]

    grads = {}

    def stack(name, layer, value, count):
        grads.setdefault(name, [None] * count)[layer] = value

    saved = []
    h0 = x.reshape(t, d)
    for i in range(depth):
        slot = i // 2
        sv = {"x0": h0}
        hn = rms_fwd(h0, full["norm_mix"][i], name=f"l{i}_mix_norm")
        sv["hn"] = hn
        if i % 2 == 0:
            qkv = mm(hn, full["attn_w_qkv"][slot], out_dtype=BF16, name=f"l{i}_qkv")
            o, totals = attn_fwd(qkv, bsz, seq, d, name=f"l{i}_attn")
            h1 = mm(o, full["attn_w_o"][slot], extras=(h0,), epilogue=_add, name=f"l{i}_attn_out")
            sv.update(qkv=qkv, o=o, totals=totals)
        else:
            w_in = full["rnn_w_in"][slot]
            yg = mm(hn, w_in[:, :width], name=f"l{i}_rnn_in_gate")
            yr = mm(hn, w_in[:, width:], name=f"l{i}_rnn_in_rec")
            xr = rnn_conv_fwd(yr, full["rnn_conv_w"][slot], full["rnn_conv_b"][slot], seq, name=f"l{i}_rnn_conv")
            wa = _block_diag(full["rnn_w_gate_a"][slot]).astype(BF16)
            wx = _block_diag(full["rnn_w_gate_x"][slot]).astype(BF16)
            ga = mm(xr, wa, name=f"l{i}_rnn_gate_a")
            gi = mm(xr, wx, name=f"l{i}_rnn_gate_x")
            hs, y = rnn_scan_fwd(ga, gi, xr, yg, full["rnn_b_gate_a"][slot], full["rnn_b_gate_x"][slot],
                                 full["rnn_lru_param"][slot], bsz, seq, name=f"l{i}_rnn_scan")
            h1 = mm(y, full["rnn_w_out"][slot], extras=(h0,), epilogue=_add, name=f"l{i}_rnn_out")
            sv.update(yg=yg, yr=yr, xr=xr, wa=wa, wx=wx, ga=ga, gi=gi, hs=hs, y=y)
        sv["x1"] = h1
        hn2 = rms_fwd(h1, full["norm_ffn"][i], name=f"l{i}_ffn_norm")
        w_up = full["ffn_w_up"][i]
        ug = mm(hn2, w_up[:, :ffn], out_dtype=BF16, name=f"l{i}_ffn_up_gate")
        uv = mm(hn2, w_up[:, ffn:], out_dtype=BF16, name=f"l{i}_ffn_up_val")
        cw, cb = full["ffn_conv_w"][i], full["ffn_conv_b"][i]
        act = ffn_act_fwd(ug, uv, cw[:, :ffn], cw[:, ffn:], cb[:ffn], cb[ffn:], seq, name=f"l{i}_ffn_act")
        h2 = mm(act, full["ffn_w_down"][i], extras=(h1,), epilogue=_add, name=f"l{i}_ffn_down")
        sv.update(hn2=hn2, ug=ug, uv=uv, act=act, x2=h2)
        hn3 = rms_fwd(h2, full["norm_ple"][i], name=f"l{i}_ple_norm")
        pg = mm(hn3, full["ple_w_gate"][i], name=f"l{i}_ple_gate")
        pin = p[i].reshape(t, p.shape[-1])
        pe = mm(pin, full["ple_w_proj"][i], name=f"l{i}_ple_proj")
        h0 = ple_fwd(h2, pg, pe, name=f"l{i}_ple_mix")
        sv.update(hn3=hn3, pg=pg, pe=pe, pin=pin)
        saved.append(sv)

    dx, g_final, loss_part = final_loss(h0, full["norm_final"], loss_target.reshape(t, d), name="final_loss")
    grads["norm_final"] = g_final
    loss = lax.psum(loss_part, ("x", "y", "c"))

    for i in reversed(range(depth)):
        slot = i // 2
        sv = saved[i]
        dpg, dpe = ple_bwd(dx, sv["pg"], sv["pe"], name=f"l{i}_ple_mix_bwd")
        stack("ple_w_proj", i, mm(sv["pin"], dpe, ta=True, name=f"l{i}_ple_proj_wgrad"), depth)
        stack("ple_w_gate", i, mm(sv["hn3"], dpg, ta=True, name=f"l{i}_ple_gate_wgrad"), depth)
        dhn3 = mm(dpg, full["ple_w_gate"][i], tb=True, name=f"l{i}_ple_gate_dgrad")
        dx, gn = rms_bwd(sv["x2"], full["norm_ple"][i], dhn3, dx, name=f"l{i}_ple_norm_bwd")
        stack("norm_ple", i, gn, depth)
        stack("ffn_w_down", i, mm(sv["act"], dx, ta=True, name=f"l{i}_ffn_down_wgrad"), depth)
        dact = mm(dx, full["ffn_w_down"][i], tb=True, out_dtype=BF16, name=f"l{i}_ffn_down_dgrad")
        cw, cb = full["ffn_conv_w"][i], full["ffn_conv_b"][i]
        taps = cw.shape[0]
        dcg, dcv, sg, svv = ffn_act_bwd(sv["ug"], sv["uv"], cw[:, :ffn], cw[:, ffn:], cb[:ffn], cb[ffn:], dact, seq,
                                        name=f"l{i}_ffn_act_bwd")
        stack("ffn_conv_w", i, jnp.concatenate([sg[:taps], svv[:taps]], axis=1), depth)
        stack("ffn_conv_b", i, jnp.concatenate([sg[taps], svv[taps]], axis=0), depth)
        dug = conv_input_grad(dcg, cw[:, :ffn], seq, out_dtype=BF16, name=f"l{i}_ffn_conv_bwd_gate")
        duv = conv_input_grad(dcv, cw[:, ffn:], seq, out_dtype=BF16, name=f"l{i}_ffn_conv_bwd_val")
        stack("ffn_w_up", i, jnp.concatenate(
            [mm(sv["hn2"], dug, ta=True, name=f"l{i}_ffn_up_wgrad_gate"),
             mm(sv["hn2"], duv, ta=True, name=f"l{i}_ffn_up_wgrad_val")], axis=1), depth)
        w_up = full["ffn_w_up"][i]
        dhn2 = mm(dug, w_up[:, :ffn], tb=True, name=f"l{i}_ffn_up_dgrad_gate")
        dhn2 = mm(duv, w_up[:, ffn:], tb=True, extras=(dhn2,), epilogue=_add, name=f"l{i}_ffn_up_dgrad_val")
        dx, gn = rms_bwd(sv["x1"], full["norm_ffn"][i], dhn2, dx, name=f"l{i}_ffn_norm_bwd")
        stack("norm_ffn", i, gn, depth)
        if i % 2 == 0:
            stack("attn_w_o", slot, mm(sv["o"], dx, ta=True, name=f"l{i}_attn_out_wgrad"), depth // 2)
            do = mm(dx, full["attn_w_o"][slot], tb=True, out_dtype=BF16, name=f"l{i}_attn_out_dgrad")
            dq, dk, dv = attn_bwd(sv["qkv"], sv["totals"], do, bsz, seq, d, name=f"l{i}_attn_bwd")
            dqkv = jnp.concatenate([dq, dk, dv], axis=1).astype(BF16)
            stack("attn_w_qkv", slot, mm(sv["hn"], dqkv, ta=True, name=f"l{i}_qkv_wgrad"), depth // 2)
            dhn = mm(dqkv, full["attn_w_qkv"][slot], tb=True, name=f"l{i}_qkv_dgrad")
        else:
            nrnn = depth // 2
            stack("rnn_w_out", slot, mm(sv["y"], dx, ta=True, name=f"l{i}_rnn_out_wgrad"), nrnn)
            dy = mm(dx, full["rnn_w_out"][slot], tb=True, name=f"l{i}_rnn_out_dgrad")
            dyg, dga, dgi, dxr, stats = rnn_scan_bwd(
                dy, sv["ga"], sv["gi"], sv["xr"], sv["yg"], sv["hs"], full["rnn_b_gate_a"][slot],
                full["rnn_b_gate_x"][slot], full["rnn_lru_param"][slot], bsz, seq, name=f"l{i}_rnn_scan_bwd")
            stack("rnn_b_gate_a", slot, stats[0], nrnn)
            stack("rnn_b_gate_x", slot, stats[1], nrnn)
            stack("rnn_lru_param", slot, stats[2], nrnn)
            stack("rnn_w_gate_a", slot, _diag_blocks(mm(sv["xr"], dga, ta=True, name=f"l{i}_rnn_gate_a_wgrad"),
                                                     RNN_HEADS), nrnn)
            stack("rnn_w_gate_x", slot, _diag_blocks(mm(sv["xr"], dgi, ta=True, name=f"l{i}_rnn_gate_x_wgrad"),
                                                     RNN_HEADS), nrnn)
            dxr = mm(dga, sv["wa"], tb=True, extras=(dxr,), epilogue=_add, name=f"l{i}_rnn_gate_a_dgrad")
            dxr = mm(dgi, sv["wx"], tb=True, extras=(dxr,), epilogue=_add, name=f"l{i}_rnn_gate_x_dgrad")
            rcw = full["rnn_conv_w"][slot]
            rtaps = rcw.shape[0]
            cstats = rnn_conv_wgrad(dxr, sv["yr"], seq, rtaps, name=f"l{i}_rnn_conv_wgrad")
            stack("rnn_conv_w", slot, cstats[:rtaps], nrnn)
            stack("rnn_conv_b", slot, cstats[rtaps], nrnn)
            dyr = conv_input_grad(dxr, rcw, seq, out_dtype=BF16, name=f"l{i}_rnn_conv_bwd")
            stack("rnn_w_in", slot, jnp.concatenate(
                [mm(sv["hn"], dyg, ta=True, name=f"l{i}_rnn_in_wgrad_gate"),
                 mm(sv["hn"], dyr, ta=True, name=f"l{i}_rnn_in_wgrad_rec")], axis=1), nrnn)
            w_in = full["rnn_w_in"][slot]
            dhn = mm(dyg, w_in[:, :width], tb=True, name=f"l{i}_rnn_in_dgrad_gate")
            dhn = mm(dyr, w_in[:, width:], tb=True, extras=(dhn,), epilogue=_add, name=f"l{i}_rnn_in_dgrad_rec")
        dx, gn = rms_bwd(sv["x0"], full["norm_mix"][i], dhn, dx, name=f"l{i}_mix_norm_bwd")
        stack("norm_mix", i, gn, depth)
    grad_x = dx.reshape(bsz, seq, d)

    whole = {n: (jnp.stack(g) if isinstance(g, list) else g) for n, g in grads.items()}

    sharded = [n for n in WEIGHTS if n in SHARD_AXIS]
    cut = [_blocks(whole[n], SHARD_AXIS[n]).reshape(N_DEV, -1) for n in sharded]
    rep = jnp.concatenate([whole[n].reshape(-1) for n in REPLICATED])
    rep_len = rep.size
    rep_rows = -(-rep_len // (N_DEV * LANES * SUBLANES)) * SUBLANES
    rep = jnp.pad(rep, (0, N_DEV * rep_rows * LANES - rep_len)).reshape(N_DEV, rep_rows * LANES)
    pieces, spans, at = [], [], 0
    for a in cut + [rep]:
        rows = -(-a.shape[1] // (LANES * SUBLANES)) * SUBLANES
        pieces.append(jnp.pad(a, ((0, 0), (0, rows * LANES - a.shape[1]))).reshape(N_DEV, rows, LANES))
        spans.append((at, rows))
        at += rows
    tail = -at % GRAD_ROWS_TILE
    pieces.append(jnp.zeros((N_DEV, tail, LANES), F32))
    at += tail
    parts = jnp.concatenate(pieces, axis=1)
    parts = parts.reshape(4, 2, at, LANES).transpose(1, 0, 2, 3)
    from_sibling = sibling_exchange(parts, name="grads_to_sibling")
    chip_sum = add_sibling(parts, from_sibling, name="grads_add_sibling")
    from_chips = chip_exchange(chip_sum, name="grads_to_chips")
    mine = add_chips(chip_sum, from_chips, name="grads_add_chips")
    local_grads = dict(zip(sharded, _unpack(mine, spans[:-1], [local[n].shape for n in sharded], ())))
    rep_at, _ = spans[-1]
    rep_all = all_gather(mine[rep_at:rep_at + rep_rows], name="gather_replicated_grads").reshape(-1)[:rep_len]
    at = 0
    for n in REPLICATED:
        local_grads[n] = rep_all[at:at + local[n].size].reshape(local[n].shape)
        at += local[n].size

    deltas, new_m, new_v = {}, {}, {}
    for n in WEIGHTS:
        deltas[n], new_m[n], new_v[n] = adamw(local[n], local_grads[n], given["m_" + n], given["v_" + n],
                                              name=f"adamw_{n}")
    return (loss, grad_x, *[local_grads[n] for n in WEIGHTS], *[deltas[n] for n in WEIGHTS],
            *[new_m[n] for n in WEIGHTS], *[new_v[n] for n in WEIGHTS])
```

```python
import functools
import math

import jax
import jax.numpy as jnp
from jax import lax
from jax.experimental import pallas as pl
from jax.experimental.pallas import tpu as pltpu

F32 = jnp.float32
BF16 = jnp.bfloat16

EPS = 1e-6
HEAD_DIM = 64
RNN_HEADS = 16
LRU_C = 8.0
ADAM_LR = 0.001
ADAM_B1 = 0.9
ADAM_B2 = 0.999
ADAM_EPS = 1e-08
ADAM_WD = 0.01
ADAM_STEP = 10

N_DEV = 8
LANES = 128
SUBLANES = 8
VMEM_LIMIT = 56 * 1024 * 1024
MESH = pl.DeviceIdType.MESH
GRAD_ROWS_TILE = 2048
GELU_C = math.sqrt(2.0 / math.pi)
GELU_A = 0.044715

WEIGHTS = ['norm_mix', 'attn_w_qkv', 'attn_w_o', 'rnn_w_in', 'rnn_conv_w', 'rnn_conv_b', 'rnn_w_gate_a',
           'rnn_b_gate_a', 'rnn_w_gate_x', 'rnn_b_gate_x', 'rnn_lru_param', 'rnn_w_out', 'norm_ffn', 'ffn_w_up',
           'ffn_conv_w', 'ffn_conv_b', 'ffn_w_down', 'norm_ple', 'ple_w_gate', 'ple_w_proj', 'norm_final']
SHARD_AXIS = {'attn_w_qkv': 2, 'attn_w_o': 1, 'rnn_w_in': 2, 'rnn_conv_w': 2, 'rnn_conv_b': 1, 'rnn_b_gate_a': 1,
              'rnn_b_gate_x': 1, 'rnn_lru_param': 1, 'rnn_w_out': 1, 'ffn_w_up': 2, 'ffn_conv_w': 2,
              'ffn_w_down': 1, 'ple_w_gate': 1, 'ple_w_proj': 2}
MATMUL_WEIGHTS = ['attn_w_qkv', 'attn_w_o', 'rnn_w_in', 'rnn_w_out', 'ffn_w_up', 'ffn_w_down', 'ple_w_gate',
                  'ple_w_proj']
CHANNEL_WEIGHTS = ['rnn_conv_w', 'rnn_conv_b', 'rnn_b_gate_a', 'rnn_b_gate_x', 'rnn_lru_param', 'ffn_conv_w']
REPLICATED = [n for n in WEIGHTS if n not in SHARD_AXIS]


def _params(*sem):
    return pltpu.CompilerParams(dimension_semantics=sem, vmem_limit_bytes=VMEM_LIMIT)


def _tile(dim, pref, align=LANES):
    if dim <= pref:
        return dim
    t = (pref + pref // 2) // align * align
    while t >= align:
        if dim % t == 0:
            return t
        t -= align
    return dim


def _gelu(x):
    return 0.5 * x * (1.0 + jnp.tanh(GELU_C * (x + GELU_A * x * x * x)))


def _gelu_and_grad(x):
    t = jnp.tanh(GELU_C * (x + GELU_A * x * x * x))
    g = 0.5 * x * (1.0 + t)
    dg = 0.5 * (1.0 + t) + 0.5 * x * (1.0 - t * t) * GELU_C * (1.0 + 3.0 * GELU_A * x * x)
    return g, dg


def _log_sigmoid(x):
    return jnp.minimum(x, 0.0) - jnp.log(1.0 + jnp.exp(-jnp.abs(x)))


MM_VMEM_BUDGET = 36 * 1024 * 1024


def _mm_tiles(m, n, k, ta, a_item, b_item, out_item, n_extra):
    if ta:
        return _tile(m, 1024), _tile(n, 1024), _tile(k, 1024)
    row_bytes = k * a_item + n * (out_item + 4 * n_extra)
    w_bytes = k * n * b_item
    for tm in (1024, 512, 256, 128):
        if m % tm == 0 and 2 * tm * row_bytes + 2 * w_bytes + tm * n * 4 <= MM_VMEM_BUDGET:
            return tm, n, k
    return _tile(m, 512), _tile(n, 512), _tile(k, 1024)


def mm(a, b, *, name, ta=False, tb=False, out_dtype=F32, extras=(), epilogue=None):
    m, k = (a.shape[1], a.shape[0]) if ta else a.shape
    n = b.shape[0] if tb else b.shape[1]
    assert k == (b.shape[1] if tb else b.shape[0]), (a.shape, b.shape, ta, tb)
    tm, tn, tk = _mm_tiles(m, n, k, ta, a.dtype.itemsize, b.dtype.itemsize, jnp.dtype(out_dtype).itemsize,
                           len(extras))
    nk = k // tk
    n_extra = len(extras)
    dims = (((0 if ta else 1,), (1 if tb else 0,)), ((), ()))

    def body(a_ref, b_ref, *rest):
        extra_refs, o_ref = rest[:n_extra], rest[n_extra]

        def finish(acc):
            if epilogue is not None:
                acc = epilogue(acc, *[e[...] for e in extra_refs])
            o_ref[...] = acc.astype(o_ref.dtype)

        part = lax.dot_general(a_ref[...].astype(BF16), b_ref[...].astype(BF16), dims,
                               preferred_element_type=F32)
        if nk == 1:
            finish(part)
        else:
            acc_ref = rest[n_extra + 1]
            kk = pl.program_id(2)

            @pl.when(kk == 0)
            def _():
                acc_ref[...] = part

            @pl.when(kk > 0)
            def _():
                acc_ref[...] += part

            @pl.when(kk == nk - 1)
            def _():
                finish(acc_ref[...])

    a_spec = pl.BlockSpec((tk, tm), lambda i, j, kk: (kk, i)) if ta else pl.BlockSpec((tm, tk), lambda i, j, kk: (i, kk))
    b_spec = pl.BlockSpec((tn, tk), lambda i, j, kk: (j, kk)) if tb else pl.BlockSpec((tk, tn), lambda i, j, kk: (kk, j))
    o_spec = pl.BlockSpec((tm, tn), lambda i, j, kk: (i, j))
    return pl.pallas_call(
        body, name=name, grid=(m // tm, n // tn, nk),
        in_specs=[a_spec, b_spec] + [o_spec] * n_extra, out_specs=o_spec,
        out_shape=jax.ShapeDtypeStruct((m, n), out_dtype),
        scratch_shapes=[pltpu.VMEM((tm, tn), F32)] if nk > 1 else [],
        compiler_params=_params("parallel", "parallel", "arbitrary"),
    )(a, b, *extras)


def _add(acc, res):
    return acc + res


def rms_fwd(x, g, *, name):
    t, d = x.shape
    tr = _tile(t, 512, SUBLANES)

    def body(x_ref, g_ref, o_ref):
        xv = x_ref[...]
        r = lax.rsqrt(jnp.mean(xv * xv, axis=-1, keepdims=True) + EPS)
        o_ref[...] = (xv * r * g_ref[...]).astype(o_ref.dtype)

    row = pl.BlockSpec((tr, d), lambda i: (i, 0))
    return pl.pallas_call(
        body, name=name, grid=(t // tr,), in_specs=[row, pl.BlockSpec((1, d), lambda i: (0, 0))], out_specs=row,
        out_shape=jax.ShapeDtypeStruct((t, d), BF16), compiler_params=_params("parallel"),
    )(x, g.reshape(1, d))


def rms_bwd(x, g, dh, dres, *, name):
    t, d = x.shape
    tr = _tile(t, 512, SUBLANES)

    def body(x_ref, g_ref, dh_ref, dres_ref, dx_ref, dg_ref):
        xv = x_ref[...]
        dhv = dh_ref[...].astype(F32)
        r = lax.rsqrt(jnp.mean(xv * xv, axis=-1, keepdims=True) + EPS)
        xh = xv * r
        u = dhv * g_ref[...]
        dx_ref[...] = dres_ref[...] + r * (u - xh * jnp.mean(u * xh, axis=-1, keepdims=True))
        part = jnp.sum(dhv * xh, axis=0, keepdims=True)

        @pl.when(pl.program_id(0) == 0)
        def _():
            dg_ref[...] = part

        @pl.when(pl.program_id(0) > 0)
        def _():
            dg_ref[...] += part

    row = pl.BlockSpec((tr, d), lambda i: (i, 0))
    vec = pl.BlockSpec((1, d), lambda i: (0, 0))
    dx, dg = pl.pallas_call(
        body, name=name, grid=(t // tr,), in_specs=[row, vec, row, row], out_specs=[row, vec],
        out_shape=[jax.ShapeDtypeStruct((t, d), F32), jax.ShapeDtypeStruct((1, d), F32)],
        compiler_params=_params("arbitrary"),
    )(x, g.reshape(1, d), dh, dres)
    return dx, dg.reshape(d)


def final_loss(x, g, target, *, name):
    t, d = x.shape
    tr = _tile(t, 512, SUBLANES)

    def body(x_ref, g_ref, t_ref, dx_ref, dg_ref, loss_ref):
        xv = x_ref[...]
        gv = g_ref[...]
        r = lax.rsqrt(jnp.mean(xv * xv, axis=-1, keepdims=True) + EPS)
        xh = xv * r
        err = xh * gv - t_ref[...]
        dy = err * (1.0 / d)
        u = dy * gv
        dx_ref[...] = r * (u - xh * jnp.mean(u * xh, axis=-1, keepdims=True))
        dg_part = jnp.sum(dy * xh, axis=0, keepdims=True)
        loss_part = jnp.zeros((1, LANES), F32) + (0.5 / d) * jnp.sum(err * err)

        @pl.when(pl.program_id(0) == 0)
        def _():
            dg_ref[...] = dg_part
            loss_ref[...] = loss_part

        @pl.when(pl.program_id(0) > 0)
        def _():
            dg_ref[...] += dg_part
            loss_ref[...] += loss_part

    row = pl.BlockSpec((tr, d), lambda i: (i, 0))
    vec = pl.BlockSpec((1, d), lambda i: (0, 0))
    dx, dg, loss = pl.pallas_call(
        body, name=name, grid=(t // tr,), in_specs=[row, vec, row],
        out_specs=[row, vec, pl.BlockSpec((1, LANES), lambda i: (0, 0))],
        out_shape=[jax.ShapeDtypeStruct((t, d), F32), jax.ShapeDtypeStruct((1, d), F32),
                   jax.ShapeDtypeStruct((1, LANES), F32)],
        compiler_params=_params("arbitrary"),
    )(x, g.reshape(1, d), target)
    return dx, dg.reshape(d), loss[0, 0]


def _split_dot(x, mat, left):
    hi = x.astype(BF16)
    lo = (x - hi.astype(F32)).astype(BF16)
    if left:
        return (jnp.dot(mat, hi, preferred_element_type=F32) + jnp.dot(mat, lo, preferred_element_type=F32))
    return (jnp.dot(hi, mat, preferred_element_type=F32) + jnp.dot(lo, mat, preferred_element_type=F32))


_NT = (((1,), (1,)), ((), ()))
_TN = (((0,), (0,)), ((), ()))
HEADS_PER_STEP = LANES // HEAD_DIM


def attn_fwd(qkv, b, s, d, *, name):
    t = b * s
    tq = min(256, s)
    nq = s // tq
    pairs = d // LANES
    scale = HEAD_DIM ** -0.5

    def body(q_ref, k_ref, v_ref, o_ref, lt_ref):
        i = pl.program_id(2)
        row = lax.broadcasted_iota(jnp.int32, (tq, tq), 0)
        col = lax.broadcasted_iota(jnp.int32, (tq, tq), 1)
        later = (row > col).astype(BF16)
        causal = col < row
        lanes = [slice(HEAD_DIM * h, HEAD_DIM * (h + 1)) for h in range(HEADS_PER_STEP)]
        qs = [(q_ref[:, sl].astype(F32) * scale).astype(BF16) for sl in lanes]

        def block(js, carry, diag):
            starts = [pl.multiple_of(j * tq, tq) for j in js]
            hs = range(HEADS_PER_STEP)
            chains = [(n, h) for n in range(len(js)) for h in hs]
            kbs = {(n, h): k_ref[pl.ds(starts[n], tq), lanes[h]] for n, h in chains}
            vbs = {(n, h): v_ref[pl.ds(starts[n], tq), lanes[h]] for n, h in chains}
            zs = {c: lax.dot_general(qs[c[1]], kbs[c], _NT, preferred_element_type=F32) for c in chains}
            lss = {c: _log_sigmoid(zs[c]) for c in chains}
            lks = {c: lss[c] - zs[c] for c in chains}
            if diag:
                lks = {c: jnp.where(causal, lks[c], 0.0) for c in chains}
            sums = {c: _split_dot(lks[c], later, left=False) for c in chains}
            runs_in, runs = {}, []
            for h in hs:
                run = carry[h][0]
                for n in range(len(js)):
                    runs_in[n, h] = run
                    run = run + jnp.sum(lks[n, h], axis=1, keepdims=True)
                runs.append(run)
            ws = {c: jnp.exp(lss[c] + sums[c] + runs_in[c]) for c in chains}
            if diag:
                ws = {c: jnp.where(causal, ws[c], 0.0) for c in chains}
            pvs = {c: jnp.dot(ws[c].astype(BF16), vbs[c], preferred_element_type=F32) for c in chains}
            accs = [carry[h][1] + sum(pvs[n, h] for n in range(len(js))) for h in hs]
            return tuple(zip(runs, accs))

        zero = (jnp.zeros((tq, 1), F32), jnp.zeros((tq, HEAD_DIM), F32))
        carry = block([i], (zero,) * HEADS_PER_STEP, True)
        odd = i % 2
        carry = lax.cond(odd == 1, lambda c: block([i - 1], c, False), lambda c: c, carry)
        near = i - 1 - odd
        carry = lax.fori_loop(0, i // 2, lambda n, c: block([near - 2 * n, near - 2 * n - 1], c, False), carry)
        eye = (row == col).astype(F32)
        for h, sl in enumerate(lanes):
            run, acc = carry[h]
            o_ref[:, sl] = acc.astype(o_ref.dtype)
            lt_ref[SUBLANES * h:SUBLANES * (h + 1), :] = lax.dot_general(
                jnp.broadcast_to(run, (tq, SUBLANES)), eye, _TN, precision=lax.Precision.HIGHEST,
                preferred_element_type=F32)

    q_spec = pl.BlockSpec((tq, LANES), lambda bb, p, i: (bb * nq + i, p))
    k_spec = pl.BlockSpec((s, LANES), lambda bb, p, i: (bb, pairs + p))
    v_spec = pl.BlockSpec((s, LANES), lambda bb, p, i: (bb, 2 * pairs + p))
    lt_spec = pl.BlockSpec((None, None, None, HEADS_PER_STEP * SUBLANES, tq), lambda bb, p, i: (bb, p, i, 0, 0))
    return pl.pallas_call(
        body, name=name, grid=(b, pairs, nq), in_specs=[q_spec, k_spec, v_spec], out_specs=[q_spec, lt_spec],
        out_shape=[jax.ShapeDtypeStruct((t, d), BF16),
                   jax.ShapeDtypeStruct((b, pairs, nq, HEADS_PER_STEP * SUBLANES, tq), F32)],
        compiler_params=_params("parallel", "parallel", "arbitrary"),
    )(qkv, qkv, qkv)


def attn_bwd(qkv, totals, do, b, s, d, *, name):
    t = b * s
    tq = min(256, s)
    nq = s // tq
    pairs = d // LANES
    scale = HEAD_DIM ** -0.5

    def body(q_ref, k_ref, v_ref, lt_ref, do_ref, dq_ref, dk_ref, dv_ref):
        i = pl.program_id(2)

        @pl.when(i == 0)
        def _():
            dk_ref[...] = jnp.zeros_like(dk_ref)
            dv_ref[...] = jnp.zeros_like(dv_ref)

        row = lax.broadcasted_iota(jnp.int32, (tq, tq), 0)
        col = lax.broadcasted_iota(jnp.int32, (tq, tq), 1)
        upto = (col <= row).astype(BF16)
        earlier = (col < row).astype(BF16)
        causal = row < col
        lanes = [slice(HEAD_DIM * h, HEAD_DIM * (h + 1)) for h in range(HEADS_PER_STEP)]
        qs = [(q_ref[:, sl].astype(F32) * scale).astype(BF16) for sl in lanes]
        dos = [do_ref[:, sl].astype(BF16) for sl in lanes]
        totals_h = [lt_ref[SUBLANES * h:SUBLANES * h + 1, :] for h in range(HEADS_PER_STEP)]

        def block(js, carry, diag):
            starts = [pl.multiple_of(j * tq, tq) for j in js]
            hs = range(HEADS_PER_STEP)
            ns = range(len(js))
            chains = [(n, h) for n in ns for h in hs]
            kbs = {(n, h): k_ref[pl.ds(starts[n], tq), lanes[h]] for n, h in chains}
            vbs = {(n, h): v_ref[pl.ds(starts[n], tq), lanes[h]] for n, h in chains}
            zs = {c: lax.dot_general(kbs[c], qs[c[1]], _NT, preferred_element_type=F32) for c in chains}
            dws = {c: lax.dot_general(vbs[c], dos[c[1]], _NT, preferred_element_type=F32) for c in chains}
            lss = {c: _log_sigmoid(zs[c]) for c in chains}
            lks = {c: lss[c] - zs[c] for c in chains}
            if diag:
                lks = {c: jnp.where(causal, lks[c], 0.0) for c in chains}
            sums = {c: _split_dot(lks[c], upto, left=True) for c in chains}
            runs_in, runs = {}, []
            for h in hs:
                run = carry[h][0]
                for n in ns:
                    runs_in[n, h] = run
                    run = run + jnp.sum(lks[n, h], axis=0, keepdims=True)
                runs.append(run)
            ws = {c: jnp.exp(lss[c] + ((totals_h[c[1]] - runs_in[c]) - sums[c])) for c in chains}
            if diag:
                ws = {c: jnp.where(causal, ws[c], 0.0) for c in chains}
            gs = {c: dws[c] * ws[c] for c in chains}
            gsums = {c: _split_dot(gs[c], earlier, left=True) for c in chains}
            gruns_in, gruns = {}, []
            for h in hs:
                grun = carry[h][1]
                for n in ns:
                    gruns_in[n, h] = grun
                    grun = grun + jnp.sum(gs[n, h], axis=0, keepdims=True)
                gruns.append(grun)
            dzs = {c: gs[c] - jnp.exp(lss[c]) * (gs[c] + (gruns_in[c] + gsums[c])) for c in chains}
            if diag:
                dzs = {c: jnp.where(causal, dzs[c], 0.0) for c in chains}
            dzbs = {c: dzs[c].astype(BF16) for c in chains}
            for n, h in chains:
                dv_ref[pl.ds(starts[n], tq), lanes[h]] += jnp.dot(ws[n, h].astype(BF16), dos[h],
                                                                  preferred_element_type=F32)
                dk_ref[pl.ds(starts[n], tq), lanes[h]] += jnp.dot(dzbs[n, h], qs[h], preferred_element_type=F32)
            dqs = [carry[h][2] + sum(lax.dot_general(dzbs[n, h], kbs[n, h], _TN, preferred_element_type=F32)
                                     for n in ns) for h in hs]
            return tuple(zip(runs, gruns, dqs))

        zero = (jnp.zeros((1, tq), F32), jnp.zeros((1, tq), F32), jnp.zeros((tq, HEAD_DIM), F32))
        carry = lax.fori_loop(0, i // 2, lambda n, c: block([2 * n, 2 * n + 1], c, False),
                              (zero,) * HEADS_PER_STEP)
        carry = lax.cond(i % 2 == 1, lambda c: block([i - 1], c, False), lambda c: c, carry)
        carry = block([i], carry, True)
        for h, sl in enumerate(lanes):
            dq_ref[:, sl] = carry[h][2] * scale

    q_spec = pl.BlockSpec((tq, LANES), lambda bb, p, i: (bb * nq + i, p))
    k_spec = pl.BlockSpec((s, LANES), lambda bb, p, i: (bb, pairs + p))
    v_spec = pl.BlockSpec((s, LANES), lambda bb, p, i: (bb, 2 * pairs + p))
    lt_spec = pl.BlockSpec((None, None, None) + totals.shape[3:], lambda bb, p, i: (bb, p, i, 0, 0))
    kv_out = pl.BlockSpec((s, LANES), lambda bb, p, i: (bb, p))
    out = jax.ShapeDtypeStruct((t, d), F32)
    return pl.pallas_call(
        body, name=name, grid=(b, pairs, nq), in_specs=[q_spec, k_spec, v_spec, lt_spec, q_spec],
        out_specs=[q_spec, kv_out, kv_out], out_shape=[out, out, out],
        compiler_params=_params("parallel", "parallel", "arbitrary"),
    )(qkv, qkv, qkv, totals, do)


def _shift_down(cur, prev8, dist):
    ext = jnp.concatenate([prev8, cur], axis=0)
    return pltpu.roll(ext, dist, 0)[SUBLANES:]


def _shift_up(cur, next8, dist):
    ext = jnp.concatenate([cur, next8], axis=0)
    return pltpu.roll(ext, ext.shape[0] - dist, 0)[:cur.shape[0]]


def _causal_conv(cur, prev8, w_ref, b_ref):
    taps = w_ref.shape[0]
    out = cur * w_ref[taps - 1:taps, :] + b_ref[...]
    for dist in range(1, taps):
        out = out + _shift_down(cur, prev8, dist) * w_ref[taps - 1 - dist:taps - dist, :]
    return out


def _conv_specs(t, rows, tc, time_axis, dtype=F32):
    sub = SUBLANES * (4 // jnp.dtype(dtype).itemsize)
    per = rows // sub
    last = t // sub - 1

    def grid_ids(*ids):
        return ids[time_axis], ids[1 - time_axis]

    def cur(*ids):
        return grid_ids(*ids)

    def prev(*ids):
        i, j = grid_ids(*ids)
        return (jnp.maximum(i * per - 1, 0), j)

    def nxt(*ids):
        i, j = grid_ids(*ids)
        return (jnp.minimum((i + 1) * per, last), j)

    def chan(*ids):
        return (0, grid_ids(*ids)[1])

    return pl.BlockSpec((rows, tc), cur), pl.BlockSpec((sub, tc), prev), pl.BlockSpec((sub, tc), nxt), chan


def _rows_before(ref, keep):
    return ref[...].astype(F32)[-SUBLANES:] * keep


def _rows_after(ref, keep):
    return ref[...].astype(F32)[:SUBLANES] * keep


def _first_in_seq(i, rows, s):
    return (i % (s // rows)) == 0


def _last_in_seq(i, rows, s):
    return (i % (s // rows)) == (s // rows - 1)


def ffn_act_fwd(ug, uv, cwg, cwv, cbg, cbv, s, *, name):
    t, f = ug.shape
    rows, tc = _tile(s, 512, SUBLANES), _tile(f, 256)
    cur, prev, _, chan = _conv_specs(t, rows, tc, 0, ug.dtype)
    taps = cwg.shape[0]

    def body(ug_ref, ugp_ref, uv_ref, uvp_ref, cwg_ref, cwv_ref, cbg_ref, cbv_ref, a_ref):
        keep = jnp.where(_first_in_seq(pl.program_id(0), rows, s), 0.0, 1.0)
        gate = _causal_conv(ug_ref[...].astype(F32), _rows_before(ugp_ref, keep), cwg_ref, cbg_ref)
        val = _causal_conv(uv_ref[...].astype(F32), _rows_before(uvp_ref, keep), cwv_ref, cbv_ref)
        a_ref[...] = (_gelu(gate) * val).astype(a_ref.dtype)

    wspec = pl.BlockSpec((taps, tc), chan)
    bspec = pl.BlockSpec((1, tc), chan)
    return pl.pallas_call(
        body, name=name, grid=(t // rows, f // tc), in_specs=[cur, prev, cur, prev, wspec, wspec, bspec, bspec],
        out_specs=cur, out_shape=jax.ShapeDtypeStruct((t, f), BF16), compiler_params=_params("parallel", "parallel"),
    )(ug, ug, uv, uv, cwg, cwv, cbg.reshape(1, f), cbv.reshape(1, f))


def _accumulate_rows(first, ref, rows):
    for k, r in enumerate(rows):
        @pl.when(first)
        def _(k=k, r=r):
            ref[k:k + 1, :] = r

        @pl.when(jnp.logical_not(first))
        def _(k=k, r=r):
            ref[k:k + 1, :] += r


def _conv_weight_grads(dc, cur, prev8, taps):
    out = []
    for k in range(taps):
        dist = taps - 1 - k
        xs = cur if dist == 0 else _shift_down(cur, prev8, dist)
        out.append(jnp.sum(dc * xs, axis=0, keepdims=True))
    out.append(jnp.sum(dc, axis=0, keepdims=True))
    return out


def ffn_act_bwd(ug, uv, cwg, cwv, cbg, cbv, da, s, *, name):
    t, f = ug.shape
    rows, tc = _tile(s, 512, SUBLANES), _tile(f, 256)
    cur, prev, _, chan = _conv_specs(t, rows, tc, 1, ug.dtype)
    taps = cwg.shape[0]

    def body(ug_ref, ugp_ref, uv_ref, uvp_ref, cwg_ref, cwv_ref, cbg_ref, cbv_ref, da_ref,
             dcg_ref, dcv_ref, wg_ref, wv_ref):
        i = pl.program_id(1)
        keep = jnp.where(_first_in_seq(i, rows, s), 0.0, 1.0)
        ugc, ugp = ug_ref[...].astype(F32), _rows_before(ugp_ref, keep)
        uvc, uvp = uv_ref[...].astype(F32), _rows_before(uvp_ref, keep)
        gate = _causal_conv(ugc, ugp, cwg_ref, cbg_ref)
        val = _causal_conv(uvc, uvp, cwv_ref, cbv_ref)
        act, dact = _gelu_and_grad(gate)
        dav = da_ref[...].astype(F32)
        dgate = dav * val * dact
        dval = dav * act
        dcg_ref[...] = dgate.astype(dcg_ref.dtype)
        dcv_ref[...] = dval.astype(dcv_ref.dtype)
        _accumulate_rows(i == 0, wg_ref, _conv_weight_grads(dgate, ugc, ugp, taps))
        _accumulate_rows(i == 0, wv_ref, _conv_weight_grads(dval, uvc, uvp, taps))

    wspec = pl.BlockSpec((taps, tc), chan)
    bspec = pl.BlockSpec((1, tc), chan)
    gspec = pl.BlockSpec((taps + 1, tc), chan)
    act_shape = jax.ShapeDtypeStruct((t, f), BF16)
    stat_shape = jax.ShapeDtypeStruct((taps + 1, f), F32)
    return pl.pallas_call(
        body, name=name, grid=(f // tc, t // rows),
        in_specs=[cur, prev, cur, prev, wspec, wspec, bspec, bspec, cur],
        out_specs=[cur, cur, gspec, gspec], out_shape=[act_shape, act_shape, stat_shape, stat_shape],
        compiler_params=_params("parallel", "arbitrary"),
    )(ug, ug, uv, uv, cwg, cwv, cbg.reshape(1, f), cbv.reshape(1, f), da)


def conv_input_grad(dc, cw, s, *, name, out_dtype):
    t, f = dc.shape
    rows, tc = _tile(s, 512, SUBLANES), _tile(f, 256)
    cur, _, nxt, chan = _conv_specs(t, rows, tc, 0, dc.dtype)
    taps = cw.shape[0]

    def body(dc_ref, dcn_ref, cw_ref, o_ref):
        keep = jnp.where(_last_in_seq(pl.program_id(0), rows, s), 0.0, 1.0)
        dcc = dc_ref[...].astype(F32)
        dcn = _rows_after(dcn_ref, keep)
        out = dcc * cw_ref[taps - 1:taps, :]
        for dist in range(1, taps):
            out = out + _shift_up(dcc, dcn, dist) * cw_ref[taps - 1 - dist:taps - dist, :]
        o_ref[...] = out.astype(o_ref.dtype)

    return pl.pallas_call(
        body, name=name, grid=(t // rows, f // tc), in_specs=[cur, nxt, pl.BlockSpec((taps, tc), chan)],
        out_specs=cur, out_shape=jax.ShapeDtypeStruct((t, f), out_dtype),
        compiler_params=_params("parallel", "parallel"),
    )(dc, dc, cw)


def rnn_conv_fwd(yr, cw, cb, s, *, name):
    t, w = yr.shape
    rows, tc = _tile(s, 512, SUBLANES), _tile(w, 256)
    cur, prev, _, chan = _conv_specs(t, rows, tc, 0, yr.dtype)
    taps = cw.shape[0]

    def body(y_ref, yp_ref, cw_ref, cb_ref, o_ref):
        keep = jnp.where(_first_in_seq(pl.program_id(0), rows, s), 0.0, 1.0)
        o_ref[...] = _causal_conv(y_ref[...].astype(F32), _rows_before(yp_ref, keep), cw_ref, cb_ref)

    return pl.pallas_call(
        body, name=name, grid=(t // rows, w // tc),
        in_specs=[cur, prev, pl.BlockSpec((taps, tc), chan), pl.BlockSpec((1, tc), chan)], out_specs=cur,
        out_shape=jax.ShapeDtypeStruct((t, w), F32), compiler_params=_params("parallel", "parallel"),
    )(yr, yr, cw, cb.reshape(1, w))


def rnn_conv_wgrad(dxr, yr, s, taps, *, name):
    t, w = yr.shape
    rows, tc = _tile(s, 512, SUBLANES), _tile(w, 256)
    cur, prev, _, chan = _conv_specs(t, rows, tc, 1, yr.dtype)

    def body(d_ref, y_ref, yp_ref, o_ref):
        i = pl.program_id(1)
        keep = jnp.where(_first_in_seq(i, rows, s), 0.0, 1.0)
        grads = _conv_weight_grads(d_ref[...], y_ref[...].astype(F32), _rows_before(yp_ref, keep), taps)
        _accumulate_rows(i == 0, o_ref, grads)

    return pl.pallas_call(
        body, name=name, grid=(w // tc, t // rows), in_specs=[cur, cur, prev],
        out_specs=pl.BlockSpec((taps + 1, tc), chan), out_shape=jax.ShapeDtypeStruct((taps + 1, w), F32),
        compiler_params=_params("parallel", "arbitrary"),
    )(dxr, yr, yr)


SCAN_UNROLL = 4


def _one_minus_exp(x):
    series = -x * (1.0 + x * (0.5 + x * (1.0 / 6.0)))
    return jnp.where(x > -0.01, series, 1.0 - jnp.exp(x))


def _gates(ga, gi, ba, bx, log_lam):
    ra = jax.nn.sigmoid(ga + ba)
    ri = jax.nn.sigmoid(gi + bx)
    log_a = LRU_C * ra * log_lam
    a = jnp.exp(log_a)
    mult = jnp.sqrt(_one_minus_exp(2.0 * log_a))
    return ra, ri, a, mult


def rnn_scan_fwd(ga, gi, xr, yg, ba, bx, lam, b, s, *, name):
    t, w = xr.shape
    tc = _tile(w, 256)
    blocks = s // SUBLANES

    def body(ga_ref, gi_ref, xr_ref, yg_ref, ba_ref, bx_ref, lam_ref, h_ref, y_ref):
        log_lam = _log_sigmoid(lam_ref[...])
        ridx = lax.broadcasted_iota(jnp.int32, (SUBLANES, tc), 0)

        def step(n, carry):
            r0 = pl.multiple_of(n * SUBLANES, SUBLANES)
            rs = pl.ds(r0, SUBLANES)
            xrv = xr_ref[rs, :]
            _, ri, a, mult = _gates(ga_ref[rs, :], gi_ref[rs, :], ba_ref[...], bx_ref[...], log_lam)
            u = mult * (ri * xrv)
            for dist in (1, 2, 4):
                a_sh = jnp.where(ridx >= dist, pltpu.roll(a, dist, 0), 1.0)
                u_sh = jnp.where(ridx >= dist, pltpu.roll(u, dist, 0), 0.0)
                u = a * u_sh + u
                a = a * a_sh
            hb = u + a * carry
            h_ref[rs, :] = hb
            y_ref[rs, :] = (_gelu(yg_ref[rs, :]) * hb).astype(y_ref.dtype)
            return hb[SUBLANES - 1:SUBLANES, :]

        lax.fori_loop(0, blocks, step, jnp.zeros((1, tc), F32), unroll=SCAN_UNROLL)

    seq = pl.BlockSpec((s, tc), lambda bb, j: (bb, j))
    vec = pl.BlockSpec((1, tc), lambda bb, j: (0, j))
    return pl.pallas_call(
        body, name=name, grid=(b, w // tc), in_specs=[seq, seq, seq, seq, vec, vec, vec], out_specs=[seq, seq],
        out_shape=[jax.ShapeDtypeStruct((t, w), F32), jax.ShapeDtypeStruct((t, w), BF16)],
        compiler_params=_params("parallel", "parallel"),
    )(ga, gi, xr, yg, ba.reshape(1, w), bx.reshape(1, w), lam.reshape(1, w))


def rnn_scan_bwd(dy, ga, gi, xr, yg, h, ba, bx, lam, b, s, *, name):
    t, w = xr.shape
    tc = _tile(w, 256)
    blocks = s // SUBLANES

    def body(dy_ref, ga_ref, gi_ref, xr_ref, yg_ref, h_ref, ba_ref, bx_ref, lam_ref,
             dyg_ref, dga_ref, dgi_ref, dxr_ref, stat_ref):
        lamv = lam_ref[...]
        log_lam = _log_sigmoid(lamv)
        dlog_lam = jax.nn.sigmoid(-lamv)
        ridx = lax.broadcasted_iota(jnp.int32, (SUBLANES, tc), 0)
        last = SUBLANES - 1

        def step(n, carry):
            lam_next, a_next, s_a, s_x, s_l = carry
            blk = blocks - 1 - n
            r0 = pl.multiple_of(blk * SUBLANES, SUBLANES)
            rs = pl.ds(r0, SUBLANES)
            rp = pl.ds(pl.multiple_of(jnp.maximum(blk - 1, 0) * SUBLANES, SUBLANES), SUBLANES)
            xrv = xr_ref[rs, :]
            hv = h_ref[rs, :]
            h_before = jnp.where(blk > 0, h_ref[rp, :][last:, :], 0.0)
            h_prev = jnp.where(ridx >= 1, pltpu.roll(hv, 1, 0), h_before)
            ra, ri, a, mult = _gates(ga_ref[rs, :], gi_ref[rs, :], ba_ref[...], bx_ref[...], log_lam)
            act, dact = _gelu_and_grad(yg_ref[rs, :])
            dyv = dy_ref[rs, :]
            dyg_ref[rs, :] = (dyv * hv * dact).astype(dyg_ref.dtype)
            v = dyv * act
            c = jnp.where(ridx < last, pltpu.roll(a, last, 0), a_next)
            for dist in (1, 2, 4):
                c_sh = jnp.where(ridx < SUBLANES - dist, pltpu.roll(c, SUBLANES - dist, 0), 1.0)
                v_sh = jnp.where(ridx < SUBLANES - dist, pltpu.roll(v, SUBLANES - dist, 0), 0.0)
                v = v + c * v_sh
                c = c * c_sh
            dh = v + c * lam_next
            du_ri_x = dh * xrv
            dmult = du_ri_x * ri
            dri = du_ri_x * mult
            dxr_ref[rs, :] = dh * mult * ri
            dlog_a = dh * h_prev * a - dmult * (a * a) / mult
            dra = dlog_a * (LRU_C * log_lam)
            dpa = dra * ra * (1.0 - ra)
            dpi = dri * ri * (1.0 - ri)
            dga_ref[rs, :] = dpa.astype(dga_ref.dtype)
            dgi_ref[rs, :] = dpi.astype(dgi_ref.dtype)
            s_a = s_a + jnp.sum(dpa, axis=0, keepdims=True)
            s_x = s_x + jnp.sum(dpi, axis=0, keepdims=True)
            s_l = s_l + jnp.sum(dlog_a * ra, axis=0, keepdims=True)
            return dh[0:1, :], a[0:1, :], s_a, s_x, s_l

        zero = jnp.zeros((1, tc), F32)
        _, _, s_a, s_x, s_l = lax.fori_loop(0, blocks, step, (zero, zero, zero, zero, zero), unroll=SCAN_UNROLL)
        _accumulate_rows(pl.program_id(1) == 0, stat_ref, [s_a, s_x, s_l * (LRU_C * dlog_lam)])

    seq = pl.BlockSpec((s, tc), lambda j, bb: (bb, j))
    vec = pl.BlockSpec((1, tc), lambda j, bb: (0, j))
    half = jax.ShapeDtypeStruct((t, w), BF16)
    return pl.pallas_call(
        body, name=name, grid=(w // tc, b), in_specs=[seq, seq, seq, seq, seq, seq, vec, vec, vec],
        out_specs=[seq, seq, seq, seq, pl.BlockSpec((3, tc), lambda j, bb: (0, j))],
        out_shape=[half, half, half, jax.ShapeDtypeStruct((t, w), F32), jax.ShapeDtypeStruct((3, w), F32)],
        compiler_params=_params("parallel", "arbitrary"),
    )(dy, ga, gi, xr, yg, h, ba.reshape(1, w), bx.reshape(1, w), lam.reshape(1, w))


def ple_fwd(x, gate, emb, *, name):
    t, d = x.shape
    tr = _tile(t, 512, SUBLANES)

    def body(x_ref, g_ref, e_ref, o_ref):
        o_ref[...] = x_ref[...] + jax.nn.sigmoid(g_ref[...]) * e_ref[...]

    row = pl.BlockSpec((tr, d), lambda i: (i, 0))
    return pl.pallas_call(body, name=name, grid=(t // tr,), in_specs=[row, row, row], out_specs=row,
                          out_shape=jax.ShapeDtypeStruct((t, d), F32), compiler_params=_params("parallel"))(x, gate, emb)


def ple_bwd(dx, gate, emb, *, name):
    t, d = dx.shape
    tr = _tile(t, 512, SUBLANES)

    def body(dx_ref, g_ref, e_ref, dg_ref, de_ref):
        sg = jax.nn.sigmoid(g_ref[...])
        dxv = dx_ref[...]
        de_ref[...] = (dxv * sg).astype(de_ref.dtype)
        dg_ref[...] = (dxv * e_ref[...] * sg * (1.0 - sg)).astype(dg_ref.dtype)

    row = pl.BlockSpec((tr, d), lambda i: (i, 0))
    half = jax.ShapeDtypeStruct((t, d), BF16)
    return pl.pallas_call(body, name=name, grid=(t // tr,), in_specs=[row, row, row], out_specs=[row, row],
                          out_shape=[half, half], compiler_params=_params("parallel"))(dx, gate, emb)


def adamw(w, g, m, v, *, name):
    shape = w.shape
    cols = shape[-1]
    rows = w.size // cols
    tr = _tile(rows, 1024, SUBLANES)
    bc1 = 1.0 / (1.0 - ADAM_B1 ** ADAM_STEP)
    bc2 = 1.0 / (1.0 - ADAM_B2 ** ADAM_STEP)

    def body(w_ref, g_ref, m_ref, v_ref, d_ref, nm_ref, nv_ref):
        gv = g_ref[...]
        nm = ADAM_B1 * m_ref[...] + (1.0 - ADAM_B1) * gv
        nv = ADAM_B2 * v_ref[...] + (1.0 - ADAM_B2) * (gv * gv)
        d_ref[...] = -ADAM_LR * ((nm * bc1) / (jnp.sqrt(nv * bc2) + ADAM_EPS) + ADAM_WD * w_ref[...])
        nm_ref[...] = nm
        nv_ref[...] = nv

    blk = pl.BlockSpec((tr, cols), lambda i: (i, 0))
    out = jax.ShapeDtypeStruct((rows, cols), F32)
    res = pl.pallas_call(body, name=name, grid=(rows // tr,), in_specs=[blk] * 4, out_specs=[blk] * 3,
                         out_shape=[out] * 3, compiler_params=_params("parallel"),
                         )(*[a.reshape(rows, cols) for a in (w, g, m, v)])
    return [r.reshape(shape) for r in res]


ANY = pl.BlockSpec(memory_space=pl.ANY)


def _place():
    return lax.axis_index("x"), lax.axis_index("y"), lax.axis_index("c")


def all_gather(v, *, name):
    rows, cols = v.shape

    def body(v_ref, out_ref, send_sems, recv_sems, local_sem):
        x, y, c = _place()
        me, sibling = (x, y, c), (x, y, 1 - c)
        chips = [(1 - x, y), (x, 1 - y), (1 - x, 1 - y)]

        def slot(px, py, pc):
            return out_ref.at[4 * px + 2 * py + pc]

        def copy(k, block, to, src=None):
            return pltpu.make_async_remote_copy(
                src_ref=slot(*block) if src is None else src, dst_ref=slot(*block),
                send_sem=send_sems.at[k], recv_sem=recv_sems.at[k], device_id=to, device_id_type=MESH)

        mine = pltpu.make_async_copy(v_ref, slot(*me), local_sem)
        mine.start()
        first = [copy(0, me, sibling, src=v_ref)]
        first += [copy(1 + j, me, (*chip, c), src=v_ref) for j, chip in enumerate(chips)]
        for cp in first:
            cp.start()
        passed = [copy(4 + j, (*chip, c), sibling) for j, chip in enumerate(chips)]
        for j, chip in enumerate(chips):
            copy(1 + j, (*chip, c), me).wait_recv()
            passed[j].start()
        copy(0, sibling, me).wait_recv()
        for j, chip in enumerate(chips):
            copy(4 + j, (*chip, 1 - c), me).wait_recv()
        for cp in first + passed:
            cp.wait_send()
        mine.wait()

    return pl.pallas_call(
        body, name=name, out_shape=jax.ShapeDtypeStruct((N_DEV, rows, cols), v.dtype), in_specs=[ANY],
        out_specs=ANY,
        scratch_shapes=[pltpu.SemaphoreType.DMA((7,)), pltpu.SemaphoreType.DMA((7,)), pltpu.SemaphoreType.DMA(())],
    )(v)


def sibling_exchange(parts, *, name):
    _, quads, rows, cols = parts.shape

    def body(p_ref, got_ref, send_sem, recv_sem):
        x, y, c = _place()
        cp = pltpu.make_async_remote_copy(src_ref=p_ref.at[1 - c], dst_ref=got_ref, send_sem=send_sem,
                                          recv_sem=recv_sem, device_id=(x, y, 1 - c), device_id_type=MESH)
        cp.start()
        cp.wait()

    return pl.pallas_call(
        body, name=name, out_shape=jax.ShapeDtypeStruct((quads, rows, cols), parts.dtype), in_specs=[ANY],
        out_specs=ANY, scratch_shapes=[pltpu.SemaphoreType.DMA(()), pltpu.SemaphoreType.DMA(())],
    )(parts)


def chip_exchange(parts, *, name):
    _, rows, cols = parts.shape

    def body(p_ref, got_ref, send_sems, recv_sems):
        x, y, c = _place()
        chips = [(1 - x, y), (x, 1 - y), (1 - x, 1 - y)]
        copies = [pltpu.make_async_remote_copy(
            src_ref=p_ref.at[2 * cx + cy], dst_ref=got_ref.at[k], send_sem=send_sems.at[k],
            recv_sem=recv_sems.at[k], device_id=(cx, cy, c), device_id_type=MESH)
            for k, (cx, cy) in enumerate(chips)]
        for cp in copies:
            cp.start()
        for cp in copies:
            cp.wait()

    return pl.pallas_call(
        body, name=name, out_shape=jax.ShapeDtypeStruct((3, rows, cols), parts.dtype), in_specs=[ANY],
        out_specs=ANY, scratch_shapes=[pltpu.SemaphoreType.DMA((3,)), pltpu.SemaphoreType.DMA((3,))],
    )(parts)


def add_sibling(parts, got, *, name):
    _, quads, rows, cols = parts.shape
    tr = _tile(rows, GRAD_ROWS_TILE, SUBLANES)

    def body(c_ref, p_ref, g_ref, o_ref, ob_ref):
        total = p_ref[...] + g_ref[...]
        o_ref[...] = total
        ob_ref[...] = total.astype(ob_ref.dtype)

    c = lax.axis_index("c").astype(jnp.int32).reshape(1)
    quad = pl.BlockSpec((None, tr, cols), lambda q, i, c_ref: (q, i, 0))
    return pl.pallas_call(
        body, name=name,
        grid_spec=pltpu.PrefetchScalarGridSpec(
            num_scalar_prefetch=1, grid=(quads, rows // tr),
            in_specs=[pl.BlockSpec((None, None, tr, cols), lambda q, i, c_ref: (c_ref[0], q, i, 0)), quad],
            out_specs=[quad, quad]),
        out_shape=[jax.ShapeDtypeStruct((quads, rows, cols), parts.dtype),
                   jax.ShapeDtypeStruct((quads, rows, cols), BF16)],
        compiler_params=_params("parallel", "parallel"),
    )(c, parts, got)


def add_chips(parts, got, *, name):
    _, rows, cols = parts.shape
    tr = _tile(rows, GRAD_ROWS_TILE, SUBLANES)

    def body(q_ref, p_ref, g_ref, o_ref):
        o_ref[...] = ((p_ref[...] + g_ref[0].astype(F32)) + g_ref[1].astype(F32)) + g_ref[2].astype(F32)

    q = (2 * lax.axis_index("x") + lax.axis_index("y")).astype(jnp.int32).reshape(1)
    return pl.pallas_call(
        body, name=name,
        grid_spec=pltpu.PrefetchScalarGridSpec(
            num_scalar_prefetch=1, grid=(rows // tr,),
            in_specs=[pl.BlockSpec((None, tr, cols), lambda i, q_ref: (q_ref[0], i, 0)),
                      pl.BlockSpec((3, tr, cols), lambda i, q_ref: (0, i, 0))],
            out_specs=pl.BlockSpec((tr, cols), lambda i, q_ref: (i, 0))),
        out_shape=jax.ShapeDtypeStruct((rows, cols), parts.dtype), compiler_params=_params("parallel"),
    )(q, parts, got)


def _pack(arrays, dtype, row_align):
    pieces, spans, at = [], [], 0
    for a in arrays:
        flat = a.reshape(-1).astype(dtype)
        rows = -(-flat.size // (LANES * row_align)) * row_align
        pieces.append(jnp.pad(flat, (0, rows * LANES - flat.size)).reshape(rows, LANES))
        spans.append((at, rows))
        at += rows
    return jnp.concatenate(pieces, axis=0), spans


def _unpack(buf, spans, shapes, lead):
    out = []
    for (at, rows), shape in zip(spans, shapes):
        size = math.prod(shape)
        piece = buf[..., at:at + rows, :].reshape(*lead, rows * LANES)[..., :size]
        out.append(piece.reshape(*lead, *shape))
    return out


def _whole(gathered, axis):
    moved = jnp.moveaxis(gathered, 0, axis)
    shape = moved.shape
    return moved.reshape(*shape[:axis], shape[axis] * shape[axis + 1], *shape[axis + 2:])


def _blocks(whole, axis):
    shape = whole.shape
    cut = whole.reshape(*shape[:axis], N_DEV, shape[axis] // N_DEV, *shape[axis + 1:])
    return jnp.moveaxis(cut, axis, 0)


def _block_diag(w):
    heads, n, _ = w.shape
    eye = jnp.eye(heads, dtype=w.dtype)
    return (w[:, :, None, :] * eye[:, None, :, None]).reshape(heads * n, heads * n)


def _diag_blocks(full, heads):
    n = full.shape[0] // heads
    return jnp.stack([full[h * n:(h + 1) * n, h * n:(h + 1) * n] for h in range(heads)])


def kernel(x, p, norm_mix, attn_w_qkv, attn_w_o, rnn_w_in, rnn_conv_w, rnn_conv_b, rnn_w_gate_a, rnn_b_gate_a, rnn_w_gate_x, rnn_b_gate_x, rnn_lru_param, rnn_w_out, norm_ffn, ffn_w_up, ffn_conv_w, ffn_conv_b, ffn_w_down, norm_ple, ple_w_gate, ple_w_proj, norm_final, loss_target, m_norm_mix, m_attn_w_qkv, m_attn_w_o, m_rnn_w_in, m_rnn_conv_w, m_rnn_conv_b, m_rnn_w_gate_a, m_rnn_b_gate_a, m_rnn_w_gate_x, m_rnn_b_gate_x, m_rnn_lru_param, m_rnn_w_out, m_norm_ffn, m_ffn_w_up, m_ffn_conv_w, m_ffn_conv_b, m_ffn_w_down, m_norm_ple, m_ple_w_gate, m_ple_w_proj, m_norm_final, v_norm_mix, v_attn_w_qkv, v_attn_w_o, v_rnn_w_in, v_rnn_conv_w, v_rnn_conv_b, v_rnn_w_gate_a, v_rnn_b_gate_a, v_rnn_w_gate_x, v_rnn_b_gate_x, v_rnn_lru_param, v_rnn_w_out, v_norm_ffn, v_ffn_w_up, v_ffn_conv_w, v_ffn_conv_b, v_ffn_w_down, v_norm_ple, v_ple_w_gate, v_ple_w_proj, v_norm_final):
    given = dict(locals())
    local = {n: given[n] for n in WEIGHTS}
    bsz, seq, d = x.shape
    t = bsz * seq
    depth = norm_mix.shape[0]
    width = rnn_w_out.shape[1] * N_DEV
    ffn = ffn_w_down.shape[1] * N_DEV

    buf, spans = _pack([local[n] for n in MATMUL_WEIGHTS], BF16, 2 * SUBLANES)
    got = _unpack(all_gather(buf, name="gather_matmul_weights"), spans,
                  [local[n].shape for n in MATMUL_WEIGHTS], (N_DEV,))
    full = {n: _whole(g, SHARD_AXIS[n]) for n, g in zip(MATMUL_WEIGHTS, got)}
    buf, spans = _pack([local[n] for n in CHANNEL_WEIGHTS], F32, SUBLANES)
    got = _unpack(all_gather(buf, name="gather_channel_weights"), spans,
                  [local[n].shape for n in CHANNEL_WEIGHTS], (N_DEV,))
    full.update({n: _whole(g, SHARD_AXIS[n]) for n, g in zip(CHANNEL_WEIGHTS, got)})
    for n in REPLICATED:
        full[n] = local[n]

    grads = {}

    def stack(name, layer, value, count):
        grads.setdefault(name, [None] * count)[layer] = value

    saved = []
    h0 = x.reshape(t, d)
    for i in range(depth):
        slot = i // 2
        sv = {"x0": h0}
        hn = rms_fwd(h0, full["norm_mix"][i], name=f"l{i}_mix_norm")
        sv["hn"] = hn
        if i % 2 == 0:
            qkv = mm(hn, full["attn_w_qkv"][slot], out_dtype=BF16, name=f"l{i}_qkv")
            o, totals = attn_fwd(qkv, bsz, seq, d, name=f"l{i}_attn")
            h1 = mm(o, full["attn_w_o"][slot], extras=(h0,), epilogue=_add, name=f"l{i}_attn_out")
            sv.update(qkv=qkv, o=o, totals=totals)
        else:
            w_in = full["rnn_w_in"][slot]
            yg = mm(hn, w_in[:, :width], name=f"l{i}_rnn_in_gate")
            yr = mm(hn, w_in[:, width:], name=f"l{i}_rnn_in_rec")
            xr = rnn_conv_fwd(yr, full["rnn_conv_w"][slot], full["rnn_conv_b"][slot], seq, name=f"l{i}_rnn_conv")
            wa = _block_diag(full["rnn_w_gate_a"][slot]).astype(BF16)
            wx = _block_diag(full["rnn_w_gate_x"][slot]).astype(BF16)
            ga = mm(xr, wa, name=f"l{i}_rnn_gate_a")
            gi = mm(xr, wx, name=f"l{i}_rnn_gate_x")
            hs, y = rnn_scan_fwd(ga, gi, xr, yg, full["rnn_b_gate_a"][slot], full["rnn_b_gate_x"][slot],
                                 full["rnn_lru_param"][slot], bsz, seq, name=f"l{i}_rnn_scan")
            h1 = mm(y, full["rnn_w_out"][slot], extras=(h0,), epilogue=_add, name=f"l{i}_rnn_out")
            sv.update(yg=yg, yr=yr, xr=xr, wa=wa, wx=wx, ga=ga, gi=gi, hs=hs, y=y)
        sv["x1"] = h1
        hn2 = rms_fwd(h1, full["norm_ffn"][i], name=f"l{i}_ffn_norm")
        w_up = full["ffn_w_up"][i]
        ug = mm(hn2, w_up[:, :ffn], out_dtype=BF16, name=f"l{i}_ffn_up_gate")
        uv = mm(hn2, w_up[:, ffn:], out_dtype=BF16, name=f"l{i}_ffn_up_val")
        cw, cb = full["ffn_conv_w"][i], full["ffn_conv_b"][i]
        act = ffn_act_fwd(ug, uv, cw[:, :ffn], cw[:, ffn:], cb[:ffn], cb[ffn:], seq, name=f"l{i}_ffn_act")
        h2 = mm(act, full["ffn_w_down"][i], extras=(h1,), epilogue=_add, name=f"l{i}_ffn_down")
        sv.update(hn2=hn2, ug=ug, uv=uv, act=act, x2=h2)
        hn3 = rms_fwd(h2, full["norm_ple"][i], name=f"l{i}_ple_norm")
        pg = mm(hn3, full["ple_w_gate"][i], name=f"l{i}_ple_gate")
        pin = p[i].reshape(t, p.shape[-1])
        pe = mm(pin, full["ple_w_proj"][i], name=f"l{i}_ple_proj")
        h0 = ple_fwd(h2, pg, pe, name=f"l{i}_ple_mix")
        sv.update(hn3=hn3, pg=pg, pe=pe, pin=pin)
        saved.append(sv)

    dx, g_final, loss_part = final_loss(h0, full["norm_final"], loss_target.reshape(t, d), name="final_loss")
    grads["norm_final"] = g_final
    loss = lax.psum(loss_part, ("x", "y", "c"))

    for i in reversed(range(depth)):
        slot = i // 2
        sv = saved[i]
        dpg, dpe = ple_bwd(dx, sv["pg"], sv["pe"], name=f"l{i}_ple_mix_bwd")
        stack("ple_w_proj", i, mm(sv["pin"], dpe, ta=True, name=f"l{i}_ple_proj_wgrad"), depth)
        stack("ple_w_gate", i, mm(sv["hn3"], dpg, ta=True, name=f"l{i}_ple_gate_wgrad"), depth)
        dhn3 = mm(dpg, full["ple_w_gate"][i], tb=True, name=f"l{i}_ple_gate_dgrad")
        dx, gn = rms_bwd(sv["x2"], full["norm_ple"][i], dhn3, dx, name=f"l{i}_ple_norm_bwd")
        stack("norm_ple", i, gn, depth)
        stack("ffn_w_down", i, mm(sv["act"], dx, ta=True, name=f"l{i}_ffn_down_wgrad"), depth)
        dact = mm(dx, full["ffn_w_down"][i], tb=True, out_dtype=BF16, name=f"l{i}_ffn_down_dgrad")
        cw, cb = full["ffn_conv_w"][i], full["ffn_conv_b"][i]
        taps = cw.shape[0]
        dcg, dcv, sg, svv = ffn_act_bwd(sv["ug"], sv["uv"], cw[:, :ffn], cw[:, ffn:], cb[:ffn], cb[ffn:], dact, seq,
                                        name=f"l{i}_ffn_act_bwd")
        stack("ffn_conv_w", i, jnp.concatenate([sg[:taps], svv[:taps]], axis=1), depth)
        stack("ffn_conv_b", i, jnp.concatenate([sg[taps], svv[taps]], axis=0), depth)
        dug = conv_input_grad(dcg, cw[:, :ffn], seq, out_dtype=BF16, name=f"l{i}_ffn_conv_bwd_gate")
        duv = conv_input_grad(dcv, cw[:, ffn:], seq, out_dtype=BF16, name=f"l{i}_ffn_conv_bwd_val")
        stack("ffn_w_up", i, jnp.concatenate(
            [mm(sv["hn2"], dug, ta=True, name=f"l{i}_ffn_up_wgrad_gate"),
             mm(sv["hn2"], duv, ta=True, name=f"l{i}_ffn_up_wgrad_val")], axis=1), depth)
        w_up = full["ffn_w_up"][i]
        dhn2 = mm(dug, w_up[:, :ffn], tb=True, name=f"l{i}_ffn_up_dgrad_gate")
        dhn2 = mm(duv, w_up[:, ffn:], tb=True, extras=(dhn2,), epilogue=_add, name=f"l{i}_ffn_up_dgrad_val")
        dx, gn = rms_bwd(sv["x1"], full["norm_ffn"][i], dhn2, dx, name=f"l{i}_ffn_norm_bwd")
        stack("norm_ffn", i, gn, depth)
        if i % 2 == 0:
            stack("attn_w_o", slot, mm(sv["o"], dx, ta=True, name=f"l{i}_attn_out_wgrad"), depth // 2)
            do = mm(dx, full["attn_w_o"][slot], tb=True, out_dtype=BF16, name=f"l{i}_attn_out_dgrad")
            dq, dk, dv = attn_bwd(sv["qkv"], sv["totals"], do, bsz, seq, d, name=f"l{i}_attn_bwd")
            dqkv = jnp.concatenate([dq, dk, dv], axis=1).astype(BF16)
            stack("attn_w_qkv", slot, mm(sv["hn"], dqkv, ta=True, name=f"l{i}_qkv_wgrad"), depth // 2)
            dhn = mm(dqkv, full["attn_w_qkv"][slot], tb=True, name=f"l{i}_qkv_dgrad")
        else:
            nrnn = depth // 2
            stack("rnn_w_out", slot, mm(sv["y"], dx, ta=True, name=f"l{i}_rnn_out_wgrad"), nrnn)
            dy = mm(dx, full["rnn_w_out"][slot], tb=True, name=f"l{i}_rnn_out_dgrad")
            dyg, dga, dgi, dxr, stats = rnn_scan_bwd(
                dy, sv["ga"], sv["gi"], sv["xr"], sv["yg"], sv["hs"], full["rnn_b_gate_a"][slot],
                full["rnn_b_gate_x"][slot], full["rnn_lru_param"][slot], bsz, seq, name=f"l{i}_rnn_scan_bwd")
            stack("rnn_b_gate_a", slot, stats[0], nrnn)
            stack("rnn_b_gate_x", slot, stats[1], nrnn)
            stack("rnn_lru_param", slot, stats[2], nrnn)
            stack("rnn_w_gate_a", slot, _diag_blocks(mm(sv["xr"], dga, ta=True, name=f"l{i}_rnn_gate_a_wgrad"),
                                                     RNN_HEADS), nrnn)
            stack("rnn_w_gate_x", slot, _diag_blocks(mm(sv["xr"], dgi, ta=True, name=f"l{i}_rnn_gate_x_wgrad"),
                                                     RNN_HEADS), nrnn)
            dxr = mm(dga, sv["wa"], tb=True, extras=(dxr,), epilogue=_add, name=f"l{i}_rnn_gate_a_dgrad")
            dxr = mm(dgi, sv["wx"], tb=True, extras=(dxr,), epilogue=_add, name=f"l{i}_rnn_gate_x_dgrad")
            rcw = full["rnn_conv_w"][slot]
            rtaps = rcw.shape[0]
            cstats = rnn_conv_wgrad(dxr, sv["yr"], seq, rtaps, name=f"l{i}_rnn_conv_wgrad")
            stack("rnn_conv_w", slot, cstats[:rtaps], nrnn)
            stack("rnn_conv_b", slot, cstats[rtaps], nrnn)
            dyr = conv_input_grad(dxr, rcw, seq, out_dtype=BF16, name=f"l{i}_rnn_conv_bwd")
            stack("rnn_w_in", slot, jnp.concatenate(
                [mm(sv["hn"], dyg, ta=True, name=f"l{i}_rnn_in_wgrad_gate"),
                 mm(sv["hn"], dyr, ta=True, name=f"l{i}_rnn_in_wgrad_rec")], axis=1), nrnn)
            w_in = full["rnn_w_in"][slot]
            dhn = mm(dyg, w_in[:, :width], tb=True, name=f"l{i}_rnn_in_dgrad_gate")
            dhn = mm(dyr, w_in[:, width:], tb=True, extras=(dhn,), epilogue=_add, name=f"l{i}_rnn_in_dgrad_rec")
        dx, gn = rms_bwd(sv["x0"], full["norm_mix"][i], dhn, dx, name=f"l{i}_mix_norm_bwd")
        stack("norm_mix", i, gn, depth)
    grad_x = dx.reshape(bsz, seq, d)

    whole = {n: (jnp.stack(g) if isinstance(g, list) else g) for n, g in grads.items()}

    sharded = [n for n in WEIGHTS if n in SHARD_AXIS]
    cut = [_blocks(whole[n], SHARD_AXIS[n]).reshape(N_DEV, -1) for n in sharded]
    rep = jnp.concatenate([whole[n].reshape(-1) for n in REPLICATED])
    rep_len = rep.size
    rep_rows = -(-rep_len // (N_DEV * LANES * SUBLANES)) * SUBLANES
    rep = jnp.pad(rep, (0, N_DEV * rep_rows * LANES - rep_len)).reshape(N_DEV, rep_rows * LANES)
    pieces, spans, at = [], [], 0
    for a in cut + [rep]:
        rows = -(-a.shape[1] // (LANES * SUBLANES)) * SUBLANES
        pieces.append(jnp.pad(a, ((0, 0), (0, rows * LANES - a.shape[1]))).reshape(N_DEV, rows, LANES))
        spans.append((at, rows))
        at += rows
    tail = -at % GRAD_ROWS_TILE
    pieces.append(jnp.zeros((N_DEV, tail, LANES), F32))
    at += tail
    parts = jnp.concatenate(pieces, axis=1)
    parts = parts.reshape(4, 2, at, LANES).transpose(1, 0, 2, 3)
    from_sibling = sibling_exchange(parts, name="grads_to_sibling")
    chip_sum, chip_sum_bf16 = add_sibling(parts, from_sibling, name="grads_add_sibling")
    from_chips = chip_exchange(chip_sum_bf16, name="grads_to_chips")
    mine = add_chips(chip_sum, from_chips, name="grads_add_chips")
    local_grads = dict(zip(sharded, _unpack(mine, spans[:-1], [local[n].shape for n in sharded], ())))
    rep_at, _ = spans[-1]
    rep_all = all_gather(mine[rep_at:rep_at + rep_rows], name="gather_replicated_grads").reshape(-1)[:rep_len]
    at = 0
    for n in REPLICATED:
        local_grads[n] = rep_all[at:at + local[n].size].reshape(local[n].shape)
        at += local[n].size

    deltas, new_m, new_v = {}, {}, {}
    for n in WEIGHTS:
        deltas[n], new_m[n], new_v[n] = adamw(local[n], local_grads[n], given["m_" + n], given["v_" + n],
                                              name=f"adamw_{n}")
    return (loss, grad_x, *[local_grads[n] for n in WEIGHTS], *[deltas[n] for n in WEIGHTS],
            *[new_m[n] for n in WEIGHTS], *[new_v[n] for n in WEIGHTS])
```

```python
import functools
import math

import jax
import jax.numpy as jnp
from jax import lax
from jax.experimental import pallas as pl
from jax.experimental.pallas import tpu as pltpu

F32 = jnp.float32
BF16 = jnp.bfloat16

EPS = 1e-6
HEAD_DIM = 64
RNN_HEADS = 16
LRU_C = 8.0
ADAM_LR = 0.001
ADAM_B1 = 0.9
ADAM_B2 = 0.999
ADAM_EPS = 1e-08
ADAM_WD = 0.01
ADAM_STEP = 10

N_DEV = 8
LANES = 128
SUBLANES = 8
VMEM_LIMIT = 56 * 1024 * 1024
MESH = pl.DeviceIdType.MESH
GRAD_ROWS_TILE = 2048
GELU_C = math.sqrt(2.0 / math.pi)
GELU_A = 0.044715

WEIGHTS = ['norm_mix', 'attn_w_qkv', 'attn_w_o', 'rnn_w_in', 'rnn_conv_w', 'rnn_conv_b', 'rnn_w_gate_a',
           'rnn_b_gate_a', 'rnn_w_gate_x', 'rnn_b_gate_x', 'rnn_lru_param', 'rnn_w_out', 'norm_ffn', 'ffn_w_up',
           'ffn_conv_w', 'ffn_conv_b', 'ffn_w_down', 'norm_ple', 'ple_w_gate', 'ple_w_proj', 'norm_final']
SHARD_AXIS = {'attn_w_qkv': 2, 'attn_w_o': 1, 'rnn_w_in': 2, 'rnn_conv_w': 2, 'rnn_conv_b': 1, 'rnn_b_gate_a': 1,
              'rnn_b_gate_x': 1, 'rnn_lru_param': 1, 'rnn_w_out': 1, 'ffn_w_up': 2, 'ffn_conv_w': 2,
              'ffn_w_down': 1, 'ple_w_gate': 1, 'ple_w_proj': 2}
MATMUL_WEIGHTS = ['attn_w_qkv', 'attn_w_o', 'rnn_w_in', 'rnn_w_out', 'ffn_w_up', 'ffn_w_down', 'ple_w_gate',
                  'ple_w_proj']
CHANNEL_WEIGHTS = ['rnn_conv_w', 'rnn_conv_b', 'rnn_b_gate_a', 'rnn_b_gate_x', 'rnn_lru_param', 'ffn_conv_w']
REPLICATED = [n for n in WEIGHTS if n not in SHARD_AXIS]


def _params(*sem):
    return pltpu.CompilerParams(dimension_semantics=sem, vmem_limit_bytes=VMEM_LIMIT)


def _tile(dim, pref, align=LANES):
    if dim <= pref:
        return dim
    t = (pref + pref // 2) // align * align
    while t >= align:
        if dim % t == 0:
            return t
        t -= align
    return dim


def _gelu(x):
    return 0.5 * x * (1.0 + jnp.tanh(GELU_C * (x + GELU_A * x * x * x)))


def _gelu_and_grad(x):
    t = jnp.tanh(GELU_C * (x + GELU_A * x * x * x))
    g = 0.5 * x * (1.0 + t)
    dg = 0.5 * (1.0 + t) + 0.5 * x * (1.0 - t * t) * GELU_C * (1.0 + 3.0 * GELU_A * x * x)
    return g, dg


def _log_sigmoid(x):
    return jnp.minimum(x, 0.0) - jnp.log(1.0 + jnp.exp(-jnp.abs(x)))


MM_VMEM_BUDGET = 36 * 1024 * 1024


def _mm_tiles(m, n, k, ta, a_item, b_item, out_item, n_extra):
    if ta:
        return _tile(m, 1024), _tile(n, 1024), _tile(k, 1024)
    row_bytes = k * a_item + n * (out_item + 4 * n_extra)
    w_bytes = k * n * b_item
    for tm in (1024, 512, 256, 128):
        if m % tm == 0 and 2 * tm * row_bytes + 2 * w_bytes + tm * n * 4 <= MM_VMEM_BUDGET:
            return tm, n, k
    return _tile(m, 512), _tile(n, 512), _tile(k, 1024)


def mm(a, b, *, name, ta=False, tb=False, out_dtype=F32, extras=(), epilogue=None):
    m, k = (a.shape[1], a.shape[0]) if ta else a.shape
    n = b.shape[0] if tb else b.shape[1]
    assert k == (b.shape[1] if tb else b.shape[0]), (a.shape, b.shape, ta, tb)
    tm, tn, tk = _mm_tiles(m, n, k, ta, a.dtype.itemsize, b.dtype.itemsize, jnp.dtype(out_dtype).itemsize,
                           len(extras))
    nk = k // tk
    n_extra = len(extras)
    dims = (((0 if ta else 1,), (1 if tb else 0,)), ((), ()))

    def body(a_ref, b_ref, *rest):
        extra_refs, o_ref = rest[:n_extra], rest[n_extra]

        def finish(acc):
            if epilogue is not None:
                acc = epilogue(acc, *[e[...] for e in extra_refs])
            o_ref[...] = acc.astype(o_ref.dtype)

        part = lax.dot_general(a_ref[...].astype(BF16), b_ref[...].astype(BF16), dims,
                               preferred_element_type=F32)
        if nk == 1:
            finish(part)
        else:
            acc_ref = rest[n_extra + 1]
            kk = pl.program_id(2)

            @pl.when(kk == 0)
            def _():
                acc_ref[...] = part

            @pl.when(kk > 0)
            def _():
                acc_ref[...] += part

            @pl.when(kk == nk - 1)
            def _():
                finish(acc_ref[...])

    a_spec = pl.BlockSpec((tk, tm), lambda i, j, kk: (kk, i)) if ta else pl.BlockSpec((tm, tk), lambda i, j, kk: (i, kk))
    b_spec = pl.BlockSpec((tn, tk), lambda i, j, kk: (j, kk)) if tb else pl.BlockSpec((tk, tn), lambda i, j, kk: (kk, j))
    o_spec = pl.BlockSpec((tm, tn), lambda i, j, kk: (i, j))
    return pl.pallas_call(
        body, name=name, grid=(m // tm, n // tn, nk),
        in_specs=[a_spec, b_spec] + [o_spec] * n_extra, out_specs=o_spec,
        out_shape=jax.ShapeDtypeStruct((m, n), out_dtype),
        scratch_shapes=[pltpu.VMEM((tm, tn), F32)] if nk > 1 else [],
        compiler_params=_params("parallel", "parallel", "arbitrary"),
    )(a, b, *extras)


def _add(acc, res):
    return acc + res


def rms_fwd(x, g, *, name):
    t, d = x.shape
    tr = _tile(t, 512, SUBLANES)

    def body(x_ref, g_ref, o_ref):
        xv = x_ref[...]
        r = lax.rsqrt(jnp.mean(xv * xv, axis=-1, keepdims=True) + EPS)
        o_ref[...] = (xv * r * g_ref[...]).astype(o_ref.dtype)

    row = pl.BlockSpec((tr, d), lambda i: (i, 0))
    return pl.pallas_call(
        body, name=name, grid=(t // tr,), in_specs=[row, pl.BlockSpec((1, d), lambda i: (0, 0))], out_specs=row,
        out_shape=jax.ShapeDtypeStruct((t, d), BF16), compiler_params=_params("parallel"),
    )(x, g.reshape(1, d))


def rms_bwd(x, g, dh, dres, *, name):
    t, d = x.shape
    tr = _tile(t, 512, SUBLANES)

    def body(x_ref, g_ref, dh_ref, dres_ref, dx_ref, dg_ref):
        xv = x_ref[...]
        dhv = dh_ref[...].astype(F32)
        r = lax.rsqrt(jnp.mean(xv * xv, axis=-1, keepdims=True) + EPS)
        xh = xv * r
        u = dhv * g_ref[...]
        dx_ref[...] = dres_ref[...] + r * (u - xh * jnp.mean(u * xh, axis=-1, keepdims=True))
        part = jnp.sum(dhv * xh, axis=0, keepdims=True)

        @pl.when(pl.program_id(0) == 0)
        def _():
            dg_ref[...] = part

        @pl.when(pl.program_id(0) > 0)
        def _():
            dg_ref[...] += part

    row = pl.BlockSpec((tr, d), lambda i: (i, 0))
    vec = pl.BlockSpec((1, d), lambda i: (0, 0))
    dx, dg = pl.pallas_call(
        body, name=name, grid=(t // tr,), in_specs=[row, vec, row, row], out_specs=[row, vec],
        out_shape=[jax.ShapeDtypeStruct((t, d), F32), jax.ShapeDtypeStruct((1, d), F32)],
        compiler_params=_params("arbitrary"),
    )(x, g.reshape(1, d), dh, dres)
    return dx, dg.reshape(d)


def final_loss(x, g, target, *, name):
    t, d = x.shape
    tr = _tile(t, 512, SUBLANES)

    def body(x_ref, g_ref, t_ref, dx_ref, dg_ref, loss_ref):
        xv = x_ref[...]
        gv = g_ref[...]
        r = lax.rsqrt(jnp.mean(xv * xv, axis=-1, keepdims=True) + EPS)
        xh = xv * r
        err = xh * gv - t_ref[...]
        dy = err * (1.0 / d)
        u = dy * gv
        dx_ref[...] = r * (u - xh * jnp.mean(u * xh, axis=-1, keepdims=True))
        dg_part = jnp.sum(dy * xh, axis=0, keepdims=True)
        loss_part = jnp.zeros((1, LANES), F32) + (0.5 / d) * jnp.sum(err * err)

        @pl.when(pl.program_id(0) == 0)
        def _():
            dg_ref[...] = dg_part
            loss_ref[...] = loss_part

        @pl.when(pl.program_id(0) > 0)
        def _():
            dg_ref[...] += dg_part
            loss_ref[...] += loss_part

    row = pl.BlockSpec((tr, d), lambda i: (i, 0))
    vec = pl.BlockSpec((1, d), lambda i: (0, 0))
    dx, dg, loss = pl.pallas_call(
        body, name=name, grid=(t // tr,), in_specs=[row, vec, row],
        out_specs=[row, vec, pl.BlockSpec((1, LANES), lambda i: (0, 0))],
        out_shape=[jax.ShapeDtypeStruct((t, d), F32), jax.ShapeDtypeStruct((1, d), F32),
                   jax.ShapeDtypeStruct((1, LANES), F32)],
        compiler_params=_params("arbitrary"),
    )(x, g.reshape(1, d), target)
    return dx, dg.reshape(d), loss[0, 0]


def _split_dot(x, mat, left):
    hi = x.astype(BF16)
    lo = (x - hi.astype(F32)).astype(BF16)
    if left:
        return (jnp.dot(mat, hi, preferred_element_type=F32) + jnp.dot(mat, lo, preferred_element_type=F32))
    return (jnp.dot(hi, mat, preferred_element_type=F32) + jnp.dot(lo, mat, preferred_element_type=F32))


_NT = (((1,), (1,)), ((), ()))
_TN = (((0,), (0,)), ((), ()))
HEADS_PER_STEP = LANES // HEAD_DIM


def attn_fwd(qkv, b, s, d, *, name):
    t = b * s
    tq = min(256, s)
    nq = s // tq
    pairs = d // LANES
    scale = HEAD_DIM ** -0.5

    def body(q_ref, k_ref, v_ref, o_ref, lt_ref):
        i = pl.program_id(2)
        row = lax.broadcasted_iota(jnp.int32, (tq, tq), 0)
        col = lax.broadcasted_iota(jnp.int32, (tq, tq), 1)
        later = (row > col).astype(BF16)
        causal = col < row
        lanes = [slice(HEAD_DIM * h, HEAD_DIM * (h + 1)) for h in range(HEADS_PER_STEP)]
        qs = [(q_ref[:, sl].astype(F32) * scale).astype(BF16) for sl in lanes]

        def block(js, carry, diag):
            starts = [pl.multiple_of(j * tq, tq) for j in js]
            hs = range(HEADS_PER_STEP)
            chains = [(n, h) for n in range(len(js)) for h in hs]
            kbs = {(n, h): k_ref[pl.ds(starts[n], tq), lanes[h]] for n, h in chains}
            vbs = {(n, h): v_ref[pl.ds(starts[n], tq), lanes[h]] for n, h in chains}
            zs = {c: lax.dot_general(qs[c[1]], kbs[c], _NT, preferred_element_type=F32) for c in chains}
            lss = {c: _log_sigmoid(zs[c]) for c in chains}
            lks = {c: lss[c] - zs[c] for c in chains}
            if diag:
                lks = {c: jnp.where(causal, lks[c], 0.0) for c in chains}
            sums = {c: _split_dot(lks[c], later, left=False) for c in chains}
            runs_in, runs = {}, []
            for h in hs:
                run = carry[h][0]
                for n in range(len(js)):
                    runs_in[n, h] = run
                    run = run + jnp.sum(lks[n, h], axis=1, keepdims=True)
                runs.append(run)
            ws = {c: jnp.exp(lss[c] + sums[c] + runs_in[c]) for c in chains}
            if diag:
                ws = {c: jnp.where(causal, ws[c], 0.0) for c in chains}
            pvs = {c: jnp.dot(ws[c].astype(BF16), vbs[c], preferred_element_type=F32) for c in chains}
            accs = [carry[h][1] + sum(pvs[n, h] for n in range(len(js))) for h in hs]
            return tuple(zip(runs, accs))

        zero = (jnp.zeros((tq, 1), F32), jnp.zeros((tq, HEAD_DIM), F32))
        carry = block([i], (zero,) * HEADS_PER_STEP, True)
        odd = i % 2
        carry = lax.cond(odd == 1, lambda c: block([i - 1], c, False), lambda c: c, carry)
        near = i - 1 - odd
        carry = lax.fori_loop(0, i // 2, lambda n, c: block([near - 2 * n, near - 2 * n - 1], c, False), carry)
        eye = (row == col).astype(F32)
        for h, sl in enumerate(lanes):
            run, acc = carry[h]
            o_ref[:, sl] = acc.astype(o_ref.dtype)
            lt_ref[SUBLANES * h:SUBLANES * (h + 1), :] = lax.dot_general(
                jnp.broadcast_to(run, (tq, SUBLANES)), eye, _TN, precision=lax.Precision.HIGHEST,
                preferred_element_type=F32)

    q_spec = pl.BlockSpec((tq, LANES), lambda bb, p, i: (bb * nq + i, p))
    k_spec = pl.BlockSpec((s, LANES), lambda bb, p, i: (bb, pairs + p))
    v_spec = pl.BlockSpec((s, LANES), lambda bb, p, i: (bb, 2 * pairs + p))
    lt_spec = pl.BlockSpec((None, None, None, HEADS_PER_STEP * SUBLANES, tq), lambda bb, p, i: (bb, p, i, 0, 0))
    return pl.pallas_call(
        body, name=name, grid=(b, pairs, nq), in_specs=[q_spec, k_spec, v_spec], out_specs=[q_spec, lt_spec],
        out_shape=[jax.ShapeDtypeStruct((t, d), BF16),
                   jax.ShapeDtypeStruct((b, pairs, nq, HEADS_PER_STEP * SUBLANES, tq), F32)],
        compiler_params=_params("parallel", "parallel", "arbitrary"),
    )(qkv, qkv, qkv)


def attn_bwd(qkv, totals, do, b, s, d, *, name):
    t = b * s
    tq = min(256, s)
    nq = s // tq
    pairs = d // LANES
    scale = HEAD_DIM ** -0.5

    def body(q_ref, k_ref, v_ref, lt_ref, do_ref, dq_ref, dk_ref, dv_ref):
        i = pl.program_id(2)

        @pl.when(i == 0)
        def _():
            dk_ref[...] = jnp.zeros_like(dk_ref)
            dv_ref[...] = jnp.zeros_like(dv_ref)

        row = lax.broadcasted_iota(jnp.int32, (tq, tq), 0)
        col = lax.broadcasted_iota(jnp.int32, (tq, tq), 1)
        upto = (col <= row).astype(BF16)
        earlier = (col < row).astype(BF16)
        causal = row < col
        lanes = [slice(HEAD_DIM * h, HEAD_DIM * (h + 1)) for h in range(HEADS_PER_STEP)]
        qs = [(q_ref[:, sl].astype(F32) * scale).astype(BF16) for sl in lanes]
        dos = [do_ref[:, sl].astype(BF16) for sl in lanes]
        totals_h = [lt_ref[SUBLANES * h:SUBLANES * h + 1, :] for h in range(HEADS_PER_STEP)]

        def block(js, carry, diag):
            starts = [pl.multiple_of(j * tq, tq) for j in js]
            hs = range(HEADS_PER_STEP)
            ns = range(len(js))
            chains = [(n, h) for n in ns for h in hs]
            kbs = {(n, h): k_ref[pl.ds(starts[n], tq), lanes[h]] for n, h in chains}
            vbs = {(n, h): v_ref[pl.ds(starts[n], tq), lanes[h]] for n, h in chains}
            zs = {c: lax.dot_general(kbs[c], qs[c[1]], _NT, preferred_element_type=F32) for c in chains}
            dws = {c: lax.dot_general(vbs[c], dos[c[1]], _NT, preferred_element_type=F32) for c in chains}
            lss = {c: _log_sigmoid(zs[c]) for c in chains}
            lks = {c: lss[c] - zs[c] for c in chains}
            if diag:
                lks = {c: jnp.where(causal, lks[c], 0.0) for c in chains}
            sums = {c: _split_dot(lks[c], upto, left=True) for c in chains}
            runs_in, runs = {}, []
            for h in hs:
                run = carry[h][0]
                for n in ns:
                    runs_in[n, h] = run
                    run = run + jnp.sum(lks[n, h], axis=0, keepdims=True)
                runs.append(run)
            ws = {c: jnp.exp(lss[c] + ((totals_h[c[1]] - runs_in[c]) - sums[c])) for c in chains}
            if diag:
                ws = {c: jnp.where(causal, ws[c], 0.0) for c in chains}
            gs = {c: dws[c] * ws[c] for c in chains}
            gsums = {c: _split_dot(gs[c], earlier, left=True) for c in chains}
            gruns_in, gruns = {}, []
            for h in hs:
                grun = carry[h][1]
                for n in ns:
                    gruns_in[n, h] = grun
                    grun = grun + jnp.sum(gs[n, h], axis=0, keepdims=True)
                gruns.append(grun)
            dzs = {c: gs[c] - jnp.exp(lss[c]) * (gs[c] + (gruns_in[c] + gsums[c])) for c in chains}
            if diag:
                dzs = {c: jnp.where(causal, dzs[c], 0.0) for c in chains}
            dzbs = {c: dzs[c].astype(BF16) for c in chains}
            for n, h in chains:
                dv_ref[pl.ds(starts[n], tq), lanes[h]] += jnp.dot(ws[n, h].astype(BF16), dos[h],
                                                                  preferred_element_type=F32)
                dk_ref[pl.ds(starts[n], tq), lanes[h]] += jnp.dot(dzbs[n, h], qs[h], preferred_element_type=F32)
            dqs = [carry[h][2] + sum(lax.dot_general(dzbs[n, h], kbs[n, h], _TN, preferred_element_type=F32)
                                     for n in ns) for h in hs]
            return tuple(zip(runs, gruns, dqs))

        zero = (jnp.zeros((1, tq), F32), jnp.zeros((1, tq), F32), jnp.zeros((tq, HEAD_DIM), F32))
        carry = lax.fori_loop(0, i // 2, lambda n, c: block([2 * n, 2 * n + 1], c, False),
                              (zero,) * HEADS_PER_STEP)
        carry = lax.cond(i % 2 == 1, lambda c: block([i - 1], c, False), lambda c: c, carry)
        carry = block([i], carry, True)
        for h, sl in enumerate(lanes):
            dq_ref[:, sl] = carry[h][2] * scale

    q_spec = pl.BlockSpec((tq, LANES), lambda bb, p, i: (bb * nq + i, p))
    k_spec = pl.BlockSpec((s, LANES), lambda bb, p, i: (bb, pairs + p))
    v_spec = pl.BlockSpec((s, LANES), lambda bb, p, i: (bb, 2 * pairs + p))
    lt_spec = pl.BlockSpec((None, None, None) + totals.shape[3:], lambda bb, p, i: (bb, p, i, 0, 0))
    kv_out = pl.BlockSpec((s, LANES), lambda bb, p, i: (bb, p))
    out = jax.ShapeDtypeStruct((t, d), F32)
    return pl.pallas_call(
        body, name=name, grid=(b, pairs, nq), in_specs=[q_spec, k_spec, v_spec, lt_spec, q_spec],
        out_specs=[q_spec, kv_out, kv_out], out_shape=[out, out, out],
        compiler_params=_params("parallel", "parallel", "arbitrary"),
    )(qkv, qkv, qkv, totals, do)


def _shift_down(cur, prev8, dist):
    ext = jnp.concatenate([prev8, cur], axis=0)
    return pltpu.roll(ext, dist, 0)[SUBLANES:]


def _shift_up(cur, next8, dist):
    ext = jnp.concatenate([cur, next8], axis=0)
    return pltpu.roll(ext, ext.shape[0] - dist, 0)[:cur.shape[0]]


def _causal_conv(cur, prev8, w_ref, b_ref):
    taps = w_ref.shape[0]
    out = cur * w_ref[taps - 1:taps, :] + b_ref[...]
    for dist in range(1, taps):
        out = out + _shift_down(cur, prev8, dist) * w_ref[taps - 1 - dist:taps - dist, :]
    return out


def _conv_specs(t, rows, tc, time_axis, dtype=F32):
    sub = SUBLANES * (4 // jnp.dtype(dtype).itemsize)
    per = rows // sub
    last = t // sub - 1

    def grid_ids(*ids):
        return ids[time_axis], ids[1 - time_axis]

    def cur(*ids):
        return grid_ids(*ids)

    def prev(*ids):
        i, j = grid_ids(*ids)
        return (jnp.maximum(i * per - 1, 0), j)

    def nxt(*ids):
        i, j = grid_ids(*ids)
        return (jnp.minimum((i + 1) * per, last), j)

    def chan(*ids):
        return (0, grid_ids(*ids)[1])

    return pl.BlockSpec((rows, tc), cur), pl.BlockSpec((sub, tc), prev), pl.BlockSpec((sub, tc), nxt), chan


def _rows_before(ref, keep):
    return ref[...].astype(F32)[-SUBLANES:] * keep


def _rows_after(ref, keep):
    return ref[...].astype(F32)[:SUBLANES] * keep


def _first_in_seq(i, rows, s):
    return (i % (s // rows)) == 0


def _last_in_seq(i, rows, s):
    return (i % (s // rows)) == (s // rows - 1)


def ffn_act_fwd(ug, uv, cwg, cwv, cbg, cbv, s, *, name):
    t, f = ug.shape
    rows, tc = _tile(s, 512, SUBLANES), _tile(f, 256)
    cur, prev, _, chan = _conv_specs(t, rows, tc, 0, ug.dtype)
    taps = cwg.shape[0]

    def body(ug_ref, ugp_ref, uv_ref, uvp_ref, cwg_ref, cwv_ref, cbg_ref, cbv_ref, a_ref):
        keep = jnp.where(_first_in_seq(pl.program_id(0), rows, s), 0.0, 1.0)
        gate = _causal_conv(ug_ref[...].astype(F32), _rows_before(ugp_ref, keep), cwg_ref, cbg_ref)
        val = _causal_conv(uv_ref[...].astype(F32), _rows_before(uvp_ref, keep), cwv_ref, cbv_ref)
        a_ref[...] = (_gelu(gate) * val).astype(a_ref.dtype)

    wspec = pl.BlockSpec((taps, tc), chan)
    bspec = pl.BlockSpec((1, tc), chan)
    return pl.pallas_call(
        body, name=name, grid=(t // rows, f // tc), in_specs=[cur, prev, cur, prev, wspec, wspec, bspec, bspec],
        out_specs=cur, out_shape=jax.ShapeDtypeStruct((t, f), BF16), compiler_params=_params("parallel", "parallel"),
    )(ug, ug, uv, uv, cwg, cwv, cbg.reshape(1, f), cbv.reshape(1, f))


def _accumulate_rows(first, ref, rows):
    for k, r in enumerate(rows):
        @pl.when(first)
        def _(k=k, r=r):
            ref[k:k + 1, :] = r

        @pl.when(jnp.logical_not(first))
        def _(k=k, r=r):
            ref[k:k + 1, :] += r


def _conv_weight_grads(dc, cur, prev8, taps):
    out = []
    for k in range(taps):
        dist = taps - 1 - k
        xs = cur if dist == 0 else _shift_down(cur, prev8, dist)
        out.append(jnp.sum(dc * xs, axis=0, keepdims=True))
    out.append(jnp.sum(dc, axis=0, keepdims=True))
    return out


def _conv_transpose(dc_ext, rows, w_ref):
    taps = w_ref.shape[0]
    out = dc_ext[:rows] * w_ref[taps - 1:taps, :]
    for dist in range(1, taps):
        out = out + pltpu.roll(dc_ext, dc_ext.shape[0] - dist, 0)[:rows] * w_ref[taps - 1 - dist:taps - dist, :]
    return out


def ffn_act_bwd(ug, uv, cwg, cwv, cbg, cbv, da, s, *, name):
    t, f = ug.shape
    rows, tc = _tile(s, 512, SUBLANES), _tile(f, 256)
    cur, prev, nxt, chan = _conv_specs(t, rows, tc, 1, ug.dtype)
    taps = cwg.shape[0]

    def body(ug_ref, ugp_ref, ugn_ref, uv_ref, uvp_ref, uvn_ref, cwg_ref, cwv_ref, cbg_ref, cbv_ref,
             da_ref, dan_ref, dug_ref, duv_ref, wg_ref, wv_ref):
        i = pl.program_id(1)
        keep_before = jnp.where(_first_in_seq(i, rows, s), 0.0, 1.0)
        keep_after = jnp.where(_last_in_seq(i, rows, s), 0.0, 1.0)
        ugp, uvp = _rows_before(ugp_ref, keep_before), _rows_before(uvp_ref, keep_before)
        uge = jnp.concatenate([ug_ref[...].astype(F32), _rows_after(ugn_ref, 1.0)], axis=0)
        uve = jnp.concatenate([uv_ref[...].astype(F32), _rows_after(uvn_ref, 1.0)], axis=0)
        dae = jnp.concatenate([da_ref[...].astype(F32), _rows_after(dan_ref, keep_after)], axis=0)
        gate = _causal_conv(uge, ugp, cwg_ref, cbg_ref)
        val = _causal_conv(uve, uvp, cwv_ref, cbv_ref)
        act, dact = _gelu_and_grad(gate)
        dgate = dae * val * dact
        dval = dae * act
        dug_ref[...] = _conv_transpose(dgate, rows, cwg_ref).astype(dug_ref.dtype)
        duv_ref[...] = _conv_transpose(dval, rows, cwv_ref).astype(duv_ref.dtype)
        _accumulate_rows(i == 0, wg_ref, _conv_weight_grads(dgate[:rows], uge[:rows], ugp, taps))
        _accumulate_rows(i == 0, wv_ref, _conv_weight_grads(dval[:rows], uve[:rows], uvp, taps))

    wspec = pl.BlockSpec((taps, tc), chan)
    bspec = pl.BlockSpec((1, tc), chan)
    gspec = pl.BlockSpec((taps + 1, tc), chan)
    act_shape = jax.ShapeDtypeStruct((t, f), BF16)
    stat_shape = jax.ShapeDtypeStruct((taps + 1, f), F32)
    return pl.pallas_call(
        body, name=name, grid=(f // tc, t // rows),
        in_specs=[cur, prev, nxt, cur, prev, nxt, wspec, wspec, bspec, bspec, cur, nxt],
        out_specs=[cur, cur, gspec, gspec], out_shape=[act_shape, act_shape, stat_shape, stat_shape],
        compiler_params=_params("parallel", "arbitrary"),
    )(ug, ug, ug, uv, uv, uv, cwg, cwv, cbg.reshape(1, f), cbv.reshape(1, f), da, da)


def conv_input_grad(dc, cw, s, *, name, out_dtype):
    t, f = dc.shape
    rows, tc = _tile(s, 512, SUBLANES), _tile(f, 256)
    cur, _, nxt, chan = _conv_specs(t, rows, tc, 0, dc.dtype)
    taps = cw.shape[0]

    def body(dc_ref, dcn_ref, cw_ref, o_ref):
        keep = jnp.where(_last_in_seq(pl.program_id(0), rows, s), 0.0, 1.0)
        dcc = dc_ref[...].astype(F32)
        dcn = _rows_after(dcn_ref, keep)
        out = dcc * cw_ref[taps - 1:taps, :]
        for dist in range(1, taps):
            out = out + _shift_up(dcc, dcn, dist) * cw_ref[taps - 1 - dist:taps - dist, :]
        o_ref[...] = out.astype(o_ref.dtype)

    return pl.pallas_call(
        body, name=name, grid=(t // rows, f // tc), in_specs=[cur, nxt, pl.BlockSpec((taps, tc), chan)],
        out_specs=cur, out_shape=jax.ShapeDtypeStruct((t, f), out_dtype),
        compiler_params=_params("parallel", "parallel"),
    )(dc, dc, cw)


def rnn_conv_fwd(yr, cw, cb, s, *, name):
    t, w = yr.shape
    rows, tc = _tile(s, 512, SUBLANES), _tile(w, 256)
    cur, prev, _, chan = _conv_specs(t, rows, tc, 0, yr.dtype)
    taps = cw.shape[0]

    def body(y_ref, yp_ref, cw_ref, cb_ref, o_ref):
        keep = jnp.where(_first_in_seq(pl.program_id(0), rows, s), 0.0, 1.0)
        o_ref[...] = _causal_conv(y_ref[...].astype(F32), _rows_before(yp_ref, keep), cw_ref, cb_ref)

    return pl.pallas_call(
        body, name=name, grid=(t // rows, w // tc),
        in_specs=[cur, prev, pl.BlockSpec((taps, tc), chan), pl.BlockSpec((1, tc), chan)], out_specs=cur,
        out_shape=jax.ShapeDtypeStruct((t, w), F32), compiler_params=_params("parallel", "parallel"),
    )(yr, yr, cw, cb.reshape(1, w))


def rnn_conv_wgrad(dxr, yr, s, taps, *, name):
    t, w = yr.shape
    rows, tc = _tile(s, 512, SUBLANES), _tile(w, 256)
    cur, prev, _, chan = _conv_specs(t, rows, tc, 1, yr.dtype)

    def body(d_ref, y_ref, yp_ref, o_ref):
        i = pl.program_id(1)
        keep = jnp.where(_first_in_seq(i, rows, s), 0.0, 1.0)
        grads = _conv_weight_grads(d_ref[...], y_ref[...].astype(F32), _rows_before(yp_ref, keep), taps)
        _accumulate_rows(i == 0, o_ref, grads)

    return pl.pallas_call(
        body, name=name, grid=(w // tc, t // rows), in_specs=[cur, cur, prev],
        out_specs=pl.BlockSpec((taps + 1, tc), chan), out_shape=jax.ShapeDtypeStruct((taps + 1, w), F32),
        compiler_params=_params("parallel", "arbitrary"),
    )(dxr, yr, yr)


SCAN_ROWS = 32


def _one_minus_exp(x):
    series = -x * (1.0 + x * (0.5 + x * (1.0 / 6.0)))
    return jnp.where(x > -0.01, series, 1.0 - jnp.exp(x))


def _gates(ga, gi, ba, bx, log_lam):
    ra = jax.nn.sigmoid(ga + ba)
    ri = jax.nn.sigmoid(gi + bx)
    log_a = LRU_C * ra * log_lam
    a = jnp.exp(log_a)
    mult = jnp.sqrt(_one_minus_exp(2.0 * log_a))
    return ra, ri, a, mult


def rnn_scan_fwd(ga, gi, xr, yg, ba, bx, lam, b, s, *, name):
    t, w = xr.shape
    tc = _tile(w, 256)
    rb = min(SCAN_ROWS, s)
    blocks = s // rb
    steps = [1 << e for e in range(rb.bit_length() - 1)]

    def body(ga_ref, gi_ref, xr_ref, yg_ref, ba_ref, bx_ref, lam_ref, h_ref, y_ref):
        log_lam = _log_sigmoid(lam_ref[...])
        ridx = lax.broadcasted_iota(jnp.int32, (rb, tc), 0)

        def step(n, carry):
            rs = pl.ds(pl.multiple_of(n * rb, rb), rb)
            xrv = xr_ref[rs, :]
            _, ri, a, mult = _gates(ga_ref[rs, :], gi_ref[rs, :], ba_ref[...], bx_ref[...], log_lam)
            u = mult * (ri * xrv)
            for dist in steps:
                a_sh = jnp.where(ridx >= dist, pltpu.roll(a, dist, 0), 1.0)
                u_sh = jnp.where(ridx >= dist, pltpu.roll(u, dist, 0), 0.0)
                u = a * u_sh + u
                a = a * a_sh
            hb = u + a * carry
            h_ref[rs, :] = hb
            y_ref[rs, :] = (_gelu(yg_ref[rs, :]) * hb).astype(y_ref.dtype)
            return hb[rb - 1:rb, :]

        lax.fori_loop(0, blocks, step, jnp.zeros((1, tc), F32))

    seq = pl.BlockSpec((s, tc), lambda bb, j: (bb, j))
    vec = pl.BlockSpec((1, tc), lambda bb, j: (0, j))
    return pl.pallas_call(
        body, name=name, grid=(b, w // tc), in_specs=[seq, seq, seq, seq, vec, vec, vec], out_specs=[seq, seq],
        out_shape=[jax.ShapeDtypeStruct((t, w), F32), jax.ShapeDtypeStruct((t, w), BF16)],
        compiler_params=_params("parallel", "parallel"),
    )(ga, gi, xr, yg, ba.reshape(1, w), bx.reshape(1, w), lam.reshape(1, w))


def rnn_scan_bwd(dy, ga, gi, xr, yg, h, ba, bx, lam, b, s, *, name):
    t, w = xr.shape
    tc = _tile(w, 256)
    rb = min(SCAN_ROWS, s)
    blocks = s // rb
    steps = [1 << e for e in range(rb.bit_length() - 1)]

    def body(dy_ref, ga_ref, gi_ref, xr_ref, yg_ref, h_ref, ba_ref, bx_ref, lam_ref,
             dyg_ref, dga_ref, dgi_ref, dxr_ref, stat_ref):
        lamv = lam_ref[...]
        log_lam = _log_sigmoid(lamv)
        dlog_lam = jax.nn.sigmoid(-lamv)
        ridx = lax.broadcasted_iota(jnp.int32, (rb, tc), 0)
        last = rb - 1

        def step(n, carry):
            lam_next, a_next, s_a, s_x, s_l = carry
            blk = blocks - 1 - n
            rs = pl.ds(pl.multiple_of(blk * rb, rb), rb)
            rp = pl.ds(pl.multiple_of(jnp.maximum(blk * rb - SUBLANES, 0), SUBLANES), SUBLANES)
            xrv = xr_ref[rs, :]
            hv = h_ref[rs, :]
            h_before = jnp.where(blk > 0, h_ref[rp, :][SUBLANES - 1:, :], 0.0)
            h_prev = jnp.where(ridx >= 1, pltpu.roll(hv, 1, 0), h_before)
            ra, ri, a, mult = _gates(ga_ref[rs, :], gi_ref[rs, :], ba_ref[...], bx_ref[...], log_lam)
            act, dact = _gelu_and_grad(yg_ref[rs, :])
            dyv = dy_ref[rs, :]
            dyg_ref[rs, :] = (dyv * hv * dact).astype(dyg_ref.dtype)
            v = dyv * act
            c = jnp.where(ridx < last, pltpu.roll(a, last, 0), a_next)
            for dist in steps:
                c_sh = jnp.where(ridx < rb - dist, pltpu.roll(c, rb - dist, 0), 1.0)
                v_sh = jnp.where(ridx < rb - dist, pltpu.roll(v, rb - dist, 0), 0.0)
                v = v + c * v_sh
                c = c * c_sh
            dh = v + c * lam_next
            du_ri_x = dh * xrv
            dmult = du_ri_x * ri
            dri = du_ri_x * mult
            dxr_ref[rs, :] = dh * mult * ri
            dlog_a = dh * h_prev * a - dmult * (a * a) / mult
            dra = dlog_a * (LRU_C * log_lam)
            dpa = dra * ra * (1.0 - ra)
            dpi = dri * ri * (1.0 - ri)
            dga_ref[rs, :] = dpa.astype(dga_ref.dtype)
            dgi_ref[rs, :] = dpi.astype(dgi_ref.dtype)
            s_a = s_a + jnp.sum(dpa, axis=0, keepdims=True)
            s_x = s_x + jnp.sum(dpi, axis=0, keepdims=True)
            s_l = s_l + jnp.sum(dlog_a * ra, axis=0, keepdims=True)
            return dh[0:1, :], a[0:1, :], s_a, s_x, s_l

        zero = jnp.zeros((1, tc), F32)
        _, _, s_a, s_x, s_l = lax.fori_loop(0, blocks, step, (zero, zero, zero, zero, zero))
        _accumulate_rows(pl.program_id(1) == 0, stat_ref, [s_a, s_x, s_l * (LRU_C * dlog_lam)])

    seq = pl.BlockSpec((s, tc), lambda j, bb: (bb, j))
    vec = pl.BlockSpec((1, tc), lambda j, bb: (0, j))
    half = jax.ShapeDtypeStruct((t, w), BF16)
    return pl.pallas_call(
        body, name=name, grid=(w // tc, b), in_specs=[seq, seq, seq, seq, seq, seq, vec, vec, vec],
        out_specs=[seq, seq, seq, seq, pl.BlockSpec((3, tc), lambda j, bb: (0, j))],
        out_shape=[half, half, half, jax.ShapeDtypeStruct((t, w), F32), jax.ShapeDtypeStruct((3, w), F32)],
        compiler_params=_params("parallel", "arbitrary"),
    )(dy, ga, gi, xr, yg, h, ba.reshape(1, w), bx.reshape(1, w), lam.reshape(1, w))


def ple_fwd(x, gate, emb, *, name):
    t, d = x.shape
    tr = _tile(t, 512, SUBLANES)

    def body(x_ref, g_ref, e_ref, o_ref):
        o_ref[...] = x_ref[...] + jax.nn.sigmoid(g_ref[...]) * e_ref[...]

    row = pl.BlockSpec((tr, d), lambda i: (i, 0))
    return pl.pallas_call(body, name=name, grid=(t // tr,), in_specs=[row, row, row], out_specs=row,
                          out_shape=jax.ShapeDtypeStruct((t, d), F32), compiler_params=_params("parallel"))(x, gate, emb)


def ple_bwd(dx, gate, emb, *, name):
    t, d = dx.shape
    tr = _tile(t, 512, SUBLANES)

    def body(dx_ref, g_ref, e_ref, dg_ref, de_ref):
        sg = jax.nn.sigmoid(g_ref[...])
        dxv = dx_ref[...]
        de_ref[...] = (dxv * sg).astype(de_ref.dtype)
        dg_ref[...] = (dxv * e_ref[...] * sg * (1.0 - sg)).astype(dg_ref.dtype)

    row = pl.BlockSpec((tr, d), lambda i: (i, 0))
    half = jax.ShapeDtypeStruct((t, d), BF16)
    return pl.pallas_call(body, name=name, grid=(t // tr,), in_specs=[row, row, row], out_specs=[row, row],
                          out_shape=[half, half], compiler_params=_params("parallel"))(dx, gate, emb)


def adamw(w, g, m, v, *, name):
    shape = w.shape
    cols = shape[-1]
    rows = w.size // cols
    tr = _tile(rows, 1024, SUBLANES)
    bc1 = 1.0 / (1.0 - ADAM_B1 ** ADAM_STEP)
    bc2 = 1.0 / (1.0 - ADAM_B2 ** ADAM_STEP)

    def body(w_ref, g_ref, m_ref, v_ref, d_ref, nm_ref, nv_ref):
        gv = g_ref[...]
        nm = ADAM_B1 * m_ref[...] + (1.0 - ADAM_B1) * gv
        nv = ADAM_B2 * v_ref[...] + (1.0 - ADAM_B2) * (gv * gv)
        d_ref[...] = -ADAM_LR * ((nm * bc1) / (jnp.sqrt(nv * bc2) + ADAM_EPS) + ADAM_WD * w_ref[...])
        nm_ref[...] = nm
        nv_ref[...] = nv

    blk = pl.BlockSpec((tr, cols), lambda i: (i, 0))
    out = jax.ShapeDtypeStruct((rows, cols), F32)
    res = pl.pallas_call(body, name=name, grid=(rows // tr,), in_specs=[blk] * 4, out_specs=[blk] * 3,
                         out_shape=[out] * 3, compiler_params=_params("parallel"),
                         )(*[a.reshape(rows, cols) for a in (w, g, m, v)])
    return [r.reshape(shape) for r in res]


ANY = pl.BlockSpec(memory_space=pl.ANY)


def _place():
    return lax.axis_index("x"), lax.axis_index("y"), lax.axis_index("c")


def all_gather(v, *, name):
    rows, cols = v.shape

    def body(v_ref, out_ref, send_sems, recv_sems, local_sem):
        x, y, c = _place()
        me, sibling = (x, y, c), (x, y, 1 - c)
        chips = [(1 - x, y), (x, 1 - y), (1 - x, 1 - y)]

        def slot(px, py, pc):
            return out_ref.at[4 * px + 2 * py + pc]

        def copy(k, block, to, src=None):
            return pltpu.make_async_remote_copy(
                src_ref=slot(*block) if src is None else src, dst_ref=slot(*block),
                send_sem=send_sems.at[k], recv_sem=recv_sems.at[k], device_id=to, device_id_type=MESH)

        mine = pltpu.make_async_copy(v_ref, slot(*me), local_sem)
        mine.start()
        first = [copy(0, me, sibling, src=v_ref)]
        first += [copy(1 + j, me, (*chip, c), src=v_ref) for j, chip in enumerate(chips)]
        for cp in first:
            cp.start()
        passed = [copy(4 + j, (*chip, c), sibling) for j, chip in enumerate(chips)]
        for j, chip in enumerate(chips):
            copy(1 + j, (*chip, c), me).wait_recv()
            passed[j].start()
        copy(0, sibling, me).wait_recv()
        for j, chip in enumerate(chips):
            copy(4 + j, (*chip, 1 - c), me).wait_recv()
        for cp in first + passed:
            cp.wait_send()
        mine.wait()

    return pl.pallas_call(
        body, name=name, out_shape=jax.ShapeDtypeStruct((N_DEV, rows, cols), v.dtype), in_specs=[ANY],
        out_specs=ANY,
        scratch_shapes=[pltpu.SemaphoreType.DMA((7,)), pltpu.SemaphoreType.DMA((7,)), pltpu.SemaphoreType.DMA(())],
    )(v)


def sibling_exchange(parts, *, name):
    _, quads, rows, cols = parts.shape

    def body(p_ref, got_ref, send_sem, recv_sem):
        x, y, c = _place()
        cp = pltpu.make_async_remote_copy(src_ref=p_ref.at[1 - c], dst_ref=got_ref, send_sem=send_sem,
                                          recv_sem=recv_sem, device_id=(x, y, 1 - c), device_id_type=MESH)
        cp.start()
        cp.wait()

    return pl.pallas_call(
        body, name=name, out_shape=jax.ShapeDtypeStruct((quads, rows, cols), parts.dtype), in_specs=[ANY],
        out_specs=ANY, scratch_shapes=[pltpu.SemaphoreType.DMA(()), pltpu.SemaphoreType.DMA(())],
    )(parts)


def chip_exchange(parts, *, name):
    _, rows, cols = parts.shape

    def body(p_ref, got_ref, send_sems, recv_sems):
        x, y, c = _place()
        chips = [(1 - x, y), (x, 1 - y), (1 - x, 1 - y)]
        copies = [pltpu.make_async_remote_copy(
            src_ref=p_ref.at[2 * cx + cy], dst_ref=got_ref.at[k], send_sem=send_sems.at[k],
            recv_sem=recv_sems.at[k], device_id=(cx, cy, c), device_id_type=MESH)
            for k, (cx, cy) in enumerate(chips)]
        for cp in copies:
            cp.start()
        for cp in copies:
            cp.wait()

    return pl.pallas_call(
        body, name=name, out_shape=jax.ShapeDtypeStruct((3, rows, cols), parts.dtype), in_specs=[ANY],
        out_specs=ANY, scratch_shapes=[pltpu.SemaphoreType.DMA((3,)), pltpu.SemaphoreType.DMA((3,))],
    )(parts)


def add_sibling(parts, got, *, name):
    _, quads, rows, cols = parts.shape
    tr = _tile(rows, GRAD_ROWS_TILE, SUBLANES)

    def body(c_ref, p_ref, g_ref, o_ref, ob_ref):
        total = p_ref[...] + g_ref[...]
        o_ref[...] = total
        ob_ref[...] = total.astype(ob_ref.dtype)

    c = lax.axis_index("c").astype(jnp.int32).reshape(1)
    quad = pl.BlockSpec((None, tr, cols), lambda q, i, c_ref: (q, i, 0))
    return pl.pallas_call(
        body, name=name,
        grid_spec=pltpu.PrefetchScalarGridSpec(
            num_scalar_prefetch=1, grid=(quads, rows // tr),
            in_specs=[pl.BlockSpec((None, None, tr, cols), lambda q, i, c_ref: (c_ref[0], q, i, 0)), quad],
            out_specs=[quad, quad]),
        out_shape=[jax.ShapeDtypeStruct((quads, rows, cols), parts.dtype),
                   jax.ShapeDtypeStruct((quads, rows, cols), BF16)],
        compiler_params=_params("parallel", "parallel"),
    )(c, parts, got)


def add_chips(parts, got, *, name):
    _, rows, cols = parts.shape
    tr = _tile(rows, GRAD_ROWS_TILE, SUBLANES)

    def body(q_ref, p_ref, g_ref, o_ref):
        o_ref[...] = ((p_ref[...] + g_ref[0].astype(F32)) + g_ref[1].astype(F32)) + g_ref[2].astype(F32)

    q = (2 * lax.axis_index("x") + lax.axis_index("y")).astype(jnp.int32).reshape(1)
    return pl.pallas_call(
        body, name=name,
        grid_spec=pltpu.PrefetchScalarGridSpec(
            num_scalar_prefetch=1, grid=(rows // tr,),
            in_specs=[pl.BlockSpec((None, tr, cols), lambda i, q_ref: (q_ref[0], i, 0)),
                      pl.BlockSpec((3, tr, cols), lambda i, q_ref: (0, i, 0))],
            out_specs=pl.BlockSpec((tr, cols), lambda i, q_ref: (i, 0))),
        out_shape=jax.ShapeDtypeStruct((rows, cols), parts.dtype), compiler_params=_params("parallel"),
    )(q, parts, got)


def _pack(arrays, dtype, row_align):
    pieces, spans, at = [], [], 0
    for a in arrays:
        flat = a.reshape(-1).astype(dtype)
        rows = -(-flat.size // (LANES * row_align)) * row_align
        pieces.append(jnp.pad(flat, (0, rows * LANES - flat.size)).reshape(rows, LANES))
        spans.append((at, rows))
        at += rows
    return jnp.concatenate(pieces, axis=0), spans


def _unpack(buf, spans, shapes, lead):
    out = []
    for (at, rows), shape in zip(spans, shapes):
        size = math.prod(shape)
        piece = buf[..., at:at + rows, :].reshape(*lead, rows * LANES)[..., :size]
        out.append(piece.reshape(*lead, *shape))
    return out


def _whole(gathered, axis):
    moved = jnp.moveaxis(gathered, 0, axis)
    shape = moved.shape
    return moved.reshape(*shape[:axis], shape[axis] * shape[axis + 1], *shape[axis + 2:])


def _blocks(whole, axis):
    shape = whole.shape
    cut = whole.reshape(*shape[:axis], N_DEV, shape[axis] // N_DEV, *shape[axis + 1:])
    return jnp.moveaxis(cut, axis, 0)


def _block_diag(w):
    heads, n, _ = w.shape
    eye = jnp.eye(heads, dtype=w.dtype)
    return (w[:, :, None, :] * eye[:, None, :, None]).reshape(heads * n, heads * n)


def _diag_blocks(full, heads):
    n = full.shape[0] // heads
    return jnp.stack([full[h * n:(h + 1) * n, h * n:(h + 1) * n] for h in range(heads)])


def kernel(x, p, norm_mix, attn_w_qkv, attn_w_o, rnn_w_in, rnn_conv_w, rnn_conv_b, rnn_w_gate_a, rnn_b_gate_a, rnn_w_gate_x, rnn_b_gate_x, rnn_lru_param, rnn_w_out, norm_ffn, ffn_w_up, ffn_conv_w, ffn_conv_b, ffn_w_down, norm_ple, ple_w_gate, ple_w_proj, norm_final, loss_target, m_norm_mix, m_attn_w_qkv, m_attn_w_o, m_rnn_w_in, m_rnn_conv_w, m_rnn_conv_b, m_rnn_w_gate_a, m_rnn_b_gate_a, m_rnn_w_gate_x, m_rnn_b_gate_x, m_rnn_lru_param, m_rnn_w_out, m_norm_ffn, m_ffn_w_up, m_ffn_conv_w, m_ffn_conv_b, m_ffn_w_down, m_norm_ple, m_ple_w_gate, m_ple_w_proj, m_norm_final, v_norm_mix, v_attn_w_qkv, v_attn_w_o, v_rnn_w_in, v_rnn_conv_w, v_rnn_conv_b, v_rnn_w_gate_a, v_rnn_b_gate_a, v_rnn_w_gate_x, v_rnn_b_gate_x, v_rnn_lru_param, v_rnn_w_out, v_norm_ffn, v_ffn_w_up, v_ffn_conv_w, v_ffn_conv_b, v_ffn_w_down, v_norm_ple, v_ple_w_gate, v_ple_w_proj, v_norm_final):
    given = dict(locals())
    local = {n: given[n] for n in WEIGHTS}
    bsz, seq, d = x.shape
    t = bsz * seq
    depth = norm_mix.shape[0]
    width = rnn_w_out.shape[1] * N_DEV
    ffn = ffn_w_down.shape[1] * N_DEV

    buf, spans = _pack([local[n] for n in MATMUL_WEIGHTS], BF16, 2 * SUBLANES)
    got = _unpack(all_gather(buf, name="gather_matmul_weights"), spans,
                  [local[n].shape for n in MATMUL_WEIGHTS], (N_DEV,))
    full = {n: _whole(g, SHARD_AXIS[n]) for n, g in zip(MATMUL_WEIGHTS, got)}
    buf, spans = _pack([local[n] for n in CHANNEL_WEIGHTS], F32, SUBLANES)
    got = _unpack(all_gather(buf, name="gather_channel_weights"), spans,
                  [local[n].shape for n in CHANNEL_WEIGHTS], (N_DEV,))
    full.update({n: _whole(g, SHARD_AXIS[n]) for n, g in zip(CHANNEL_WEIGHTS, got)})
    for n in REPLICATED:
        full[n] = local[n]

    grads = {}

    def stack(name, layer, value, count):
        grads.setdefault(name, [None] * count)[layer] = value

    saved = []
    h0 = x.reshape(t, d)
    for i in range(depth):
        slot = i // 2
        sv = {"x0": h0}
        hn = rms_fwd(h0, full["norm_mix"][i], name=f"l{i}_mix_norm")
        sv["hn"] = hn
        if i % 2 == 0:
            qkv = mm(hn, full["attn_w_qkv"][slot], out_dtype=BF16, name=f"l{i}_qkv")
            o, totals = attn_fwd(qkv, bsz, seq, d, name=f"l{i}_attn")
            h1 = mm(o, full["attn_w_o"][slot], extras=(h0,), epilogue=_add, name=f"l{i}_attn_out")
            sv.update(qkv=qkv, o=o, totals=totals)
        else:
            w_in = full["rnn_w_in"][slot]
            yg = mm(hn, w_in[:, :width], name=f"l{i}_rnn_in_gate")
            yr = mm(hn, w_in[:, width:], name=f"l{i}_rnn_in_rec")
            xr = rnn_conv_fwd(yr, full["rnn_conv_w"][slot], full["rnn_conv_b"][slot], seq, name=f"l{i}_rnn_conv")
            wa = _block_diag(full["rnn_w_gate_a"][slot]).astype(BF16)
            wx = _block_diag(full["rnn_w_gate_x"][slot]).astype(BF16)
            ga = mm(xr, wa, name=f"l{i}_rnn_gate_a")
            gi = mm(xr, wx, name=f"l{i}_rnn_gate_x")
            hs, y = rnn_scan_fwd(ga, gi, xr, yg, full["rnn_b_gate_a"][slot], full["rnn_b_gate_x"][slot],
                                 full["rnn_lru_param"][slot], bsz, seq, name=f"l{i}_rnn_scan")
            h1 = mm(y, full["rnn_w_out"][slot], extras=(h0,), epilogue=_add, name=f"l{i}_rnn_out")
            sv.update(yg=yg, yr=yr, xr=xr, wa=wa, wx=wx, ga=ga, gi=gi, hs=hs, y=y)
        sv["x1"] = h1
        hn2 = rms_fwd(h1, full["norm_ffn"][i], name=f"l{i}_ffn_norm")
        w_up = full["ffn_w_up"][i]
        ug = mm(hn2, w_up[:, :ffn], out_dtype=BF16, name=f"l{i}_ffn_up_gate")
        uv = mm(hn2, w_up[:, ffn:], out_dtype=BF16, name=f"l{i}_ffn_up_val")
        cw, cb = full["ffn_conv_w"][i], full["ffn_conv_b"][i]
        act = ffn_act_fwd(ug, uv, cw[:, :ffn], cw[:, ffn:], cb[:ffn], cb[ffn:], seq, name=f"l{i}_ffn_act")
        h2 = mm(act, full["ffn_w_down"][i], extras=(h1,), epilogue=_add, name=f"l{i}_ffn_down")
        sv.update(hn2=hn2, ug=ug, uv=uv, act=act, x2=h2)
        hn3 = rms_fwd(h2, full["norm_ple"][i], name=f"l{i}_ple_norm")
        pg = mm(hn3, full["ple_w_gate"][i], name=f"l{i}_ple_gate")
        pin = p[i].reshape(t, p.shape[-1])
        pe = mm(pin, full["ple_w_proj"][i], name=f"l{i}_ple_proj")
        h0 = ple_fwd(h2, pg, pe, name=f"l{i}_ple_mix")
        sv.update(hn3=hn3, pg=pg, pe=pe, pin=pin)
        saved.append(sv)

    dx, g_final, loss_part = final_loss(h0, full["norm_final"], loss_target.reshape(t, d), name="final_loss")
    grads["norm_final"] = g_final
    loss = lax.psum(loss_part, ("x", "y", "c"))

    for i in reversed(range(depth)):
        slot = i // 2
        sv = saved[i]
        dpg, dpe = ple_bwd(dx, sv["pg"], sv["pe"], name=f"l{i}_ple_mix_bwd")
        stack("ple_w_proj", i, mm(sv["pin"], dpe, ta=True, name=f"l{i}_ple_proj_wgrad"), depth)
        stack("ple_w_gate", i, mm(sv["hn3"], dpg, ta=True, name=f"l{i}_ple_gate_wgrad"), depth)
        dhn3 = mm(dpg, full["ple_w_gate"][i], tb=True, name=f"l{i}_ple_gate_dgrad")
        dx, gn = rms_bwd(sv["x2"], full["norm_ple"][i], dhn3, dx, name=f"l{i}_ple_norm_bwd")
        stack("norm_ple", i, gn, depth)
        stack("ffn_w_down", i, mm(sv["act"], dx, ta=True, name=f"l{i}_ffn_down_wgrad"), depth)
        dact = mm(dx, full["ffn_w_down"][i], tb=True, out_dtype=BF16, name=f"l{i}_ffn_down_dgrad")
        cw, cb = full["ffn_conv_w"][i], full["ffn_conv_b"][i]
        taps = cw.shape[0]
        dug, duv, sg, svv = ffn_act_bwd(sv["ug"], sv["uv"], cw[:, :ffn], cw[:, ffn:], cb[:ffn], cb[ffn:], dact, seq,
                                        name=f"l{i}_ffn_act_bwd")
        stack("ffn_conv_w", i, jnp.concatenate([sg[:taps], svv[:taps]], axis=1), depth)
        stack("ffn_conv_b", i, jnp.concatenate([sg[taps], svv[taps]], axis=0), depth)
        stack("ffn_w_up", i, jnp.concatenate(
            [mm(sv["hn2"], dug, ta=True, name=f"l{i}_ffn_up_wgrad_gate"),
             mm(sv["hn2"], duv, ta=True, name=f"l{i}_ffn_up_wgrad_val")], axis=1), depth)
        w_up = full["ffn_w_up"][i]
        dhn2 = mm(dug, w_up[:, :ffn], tb=True, name=f"l{i}_ffn_up_dgrad_gate")
        dhn2 = mm(duv, w_up[:, ffn:], tb=True, extras=(dhn2,), epilogue=_add, name=f"l{i}_ffn_up_dgrad_val")
        dx, gn = rms_bwd(sv["x1"], full["norm_ffn"][i], dhn2, dx, name=f"l{i}_ffn_norm_bwd")
        stack("norm_ffn", i, gn, depth)
        if i % 2 == 0:
            stack("attn_w_o", slot, mm(sv["o"], dx, ta=True, name=f"l{i}_attn_out_wgrad"), depth // 2)
            do = mm(dx, full["attn_w_o"][slot], tb=True, out_dtype=BF16, name=f"l{i}_attn_out_dgrad")
            dq, dk, dv = attn_bwd(sv["qkv"], sv["totals"], do, bsz, seq, d, name=f"l{i}_attn_bwd")
            dqkv = jnp.concatenate([dq, dk, dv], axis=1).astype(BF16)
            stack("attn_w_qkv", slot, mm(sv["hn"], dqkv, ta=True, name=f"l{i}_qkv_wgrad"), depth // 2)
            dhn = mm(dqkv, full["attn_w_qkv"][slot], tb=True, name=f"l{i}_qkv_dgrad")
        else:
            nrnn = depth // 2
            stack("rnn_w_out", slot, mm(sv["y"], dx, ta=True, name=f"l{i}_rnn_out_wgrad"), nrnn)
            dy = mm(dx, full["rnn_w_out"][slot], tb=True, name=f"l{i}_rnn_out_dgrad")
            dyg, dga, dgi, dxr, stats = rnn_scan_bwd(
                dy, sv["ga"], sv["gi"], sv["xr"], sv["yg"], sv["hs"], full["rnn_b_gate_a"][slot],
                full["rnn_b_gate_x"][slot], full["rnn_lru_param"][slot], bsz, seq, name=f"l{i}_rnn_scan_bwd")
            stack("rnn_b_gate_a", slot, stats[0], nrnn)
            stack("rnn_b_gate_x", slot, stats[1], nrnn)
            stack("rnn_lru_param", slot, stats[2], nrnn)
            stack("rnn_w_gate_a", slot, _diag_blocks(mm(sv["xr"], dga, ta=True, name=f"l{i}_rnn_gate_a_wgrad"),
                                                     RNN_HEADS), nrnn)
            stack("rnn_w_gate_x", slot, _diag_blocks(mm(sv["xr"], dgi, ta=True, name=f"l{i}_rnn_gate_x_wgrad"),
                                                     RNN_HEADS), nrnn)
            dxr = mm(dga, sv["wa"], tb=True, extras=(dxr,), epilogue=_add, name=f"l{i}_rnn_gate_a_dgrad")
            dxr = mm(dgi, sv["wx"], tb=True, extras=(dxr,), epilogue=_add, name=f"l{i}_rnn_gate_x_dgrad")
            rcw = full["rnn_conv_w"][slot]
            rtaps = rcw.shape[0]
            cstats = rnn_conv_wgrad(dxr, sv["yr"], seq, rtaps, name=f"l{i}_rnn_conv_wgrad")
            stack("rnn_conv_w", slot, cstats[:rtaps], nrnn)
            stack("rnn_conv_b", slot, cstats[rtaps], nrnn)
            dyr = conv_input_grad(dxr, rcw, seq, out_dtype=BF16, name=f"l{i}_rnn_conv_bwd")
            stack("rnn_w_in", slot, jnp.concatenate(
                [mm(sv["hn"], dyg, ta=True, name=f"l{i}_rnn_in_wgrad_gate"),
                 mm(sv["hn"], dyr, ta=True, name=f"l{i}_rnn_in_wgrad_rec")], axis=1), nrnn)
            w_in = full["rnn_w_in"][slot]
            dhn = mm(dyg, w_in[:, :width], tb=True, name=f"l{i}_rnn_in_dgrad_gate")
            dhn = mm(dyr, w_in[:, width:], tb=True, extras=(dhn,), epilogue=_add, name=f"l{i}_rnn_in_dgrad_rec")
        dx, gn = rms_bwd(sv["x0"], full["norm_mix"][i], dhn, dx, name=f"l{i}_mix_norm_bwd")
        stack("norm_mix", i, gn, depth)
    grad_x = dx.reshape(bsz, seq, d)

    whole = {n: (jnp.stack(g) if isinstance(g, list) else g) for n, g in grads.items()}

    sharded = [n for n in WEIGHTS if n in SHARD_AXIS]
    cut = [_blocks(whole[n], SHARD_AXIS[n]).reshape(N_DEV, -1) for n in sharded]
    rep = jnp.concatenate([whole[n].reshape(-1) for n in REPLICATED])
    rep_len = rep.size
    rep_rows = -(-rep_len // (N_DEV * LANES * SUBLANES)) * SUBLANES
    rep = jnp.pad(rep, (0, N_DEV * rep_rows * LANES - rep_len)).reshape(N_DEV, rep_rows * LANES)
    pieces, spans, at = [], [], 0
    for a in cut + [rep]:
        rows = -(-a.shape[1] // (LANES * SUBLANES)) * SUBLANES
        pieces.append(jnp.pad(a, ((0, 0), (0, rows * LANES - a.shape[1]))).reshape(N_DEV, rows, LANES))
        spans.append((at, rows))
        at += rows
    tail = -at % GRAD_ROWS_TILE
    pieces.append(jnp.zeros((N_DEV, tail, LANES), F32))
    at += tail
    parts = jnp.concatenate(pieces, axis=1)
    parts = parts.reshape(4, 2, at, LANES).transpose(1, 0, 2, 3)
    from_sibling = sibling_exchange(parts, name="grads_to_sibling")
    chip_sum, chip_sum_bf16 = add_sibling(parts, from_sibling, name="grads_add_sibling")
    from_chips = chip_exchange(chip_sum_bf16, name="grads_to_chips")
    mine = add_chips(chip_sum, from_chips, name="grads_add_chips")
    local_grads = dict(zip(sharded, _unpack(mine, spans[:-1], [local[n].shape for n in sharded], ())))
    rep_at, _ = spans[-1]
    rep_all = all_gather(mine[rep_at:rep_at + rep_rows], name="gather_replicated_grads").reshape(-1)[:rep_len]
    at = 0
    for n in REPLICATED:
        local_grads[n] = rep_all[at:at + local[n].size].reshape(local[n].shape)
        at += local[n].size

    deltas, new_m, new_v = {}, {}, {}
    for n in WEIGHTS:
        deltas[n], new_m[n], new_v[n] = adamw(local[n], local_grads[n], given["m_" + n], given["v_" + n],
                                              name=f"adamw_{n}")
    return (loss, grad_x, *[local_grads[n] for n in WEIGHTS], *[deltas[n] for n in WEIGHTS],
            *[new_m[n] for n in WEIGHTS], *[new_v[n] for n in WEIGHTS])
```

```python
import functools
import math

import jax
import jax.numpy as jnp
from jax import lax
from jax.experimental import pallas as pl
from jax.experimental.pallas import tpu as pltpu

F32 = jnp.float32
BF16 = jnp.bfloat16

EPS = 1e-6
HEAD_DIM = 64
RNN_HEADS = 16
LRU_C = 8.0
ADAM_LR = 0.001
ADAM_B1 = 0.9
ADAM_B2 = 0.999
ADAM_EPS = 1e-08
ADAM_WD = 0.01
ADAM_STEP = 10

N_DEV = 8
LANES = 128
SUBLANES = 8
VMEM_LIMIT = 56 * 1024 * 1024
MESH = pl.DeviceIdType.MESH
GRAD_ROWS_TILE = 2048
GELU_C = math.sqrt(2.0 / math.pi)
GELU_A = 0.044715

WEIGHTS = ['norm_mix', 'attn_w_qkv', 'attn_w_o', 'rnn_w_in', 'rnn_conv_w', 'rnn_conv_b', 'rnn_w_gate_a',
           'rnn_b_gate_a', 'rnn_w_gate_x', 'rnn_b_gate_x', 'rnn_lru_param', 'rnn_w_out', 'norm_ffn', 'ffn_w_up',
           'ffn_conv_w', 'ffn_conv_b', 'ffn_w_down', 'norm_ple', 'ple_w_gate', 'ple_w_proj', 'norm_final']
SHARD_AXIS = {'attn_w_qkv': 2, 'attn_w_o': 1, 'rnn_w_in': 2, 'rnn_conv_w': 2, 'rnn_conv_b': 1, 'rnn_b_gate_a': 1,
              'rnn_b_gate_x': 1, 'rnn_lru_param': 1, 'rnn_w_out': 1, 'ffn_w_up': 2, 'ffn_conv_w': 2,
              'ffn_w_down': 1, 'ple_w_gate': 1, 'ple_w_proj': 2}
MATMUL_WEIGHTS = ['attn_w_qkv', 'attn_w_o', 'rnn_w_in', 'rnn_w_out', 'ffn_w_up', 'ffn_w_down', 'ple_w_gate',
                  'ple_w_proj']
CHANNEL_WEIGHTS = ['rnn_conv_w', 'rnn_conv_b', 'rnn_b_gate_a', 'rnn_b_gate_x', 'rnn_lru_param', 'ffn_conv_w']
REPLICATED = [n for n in WEIGHTS if n not in SHARD_AXIS]


def _params(*sem):
    return pltpu.CompilerParams(dimension_semantics=sem, vmem_limit_bytes=VMEM_LIMIT)


def _tile(dim, pref, align=LANES):
    if dim <= pref:
        return dim
    t = (pref + pref // 2) // align * align
    while t >= align:
        if dim % t == 0:
            return t
        t -= align
    return dim


def _gelu(x):
    return 0.5 * x * (1.0 + jnp.tanh(GELU_C * (x + GELU_A * x * x * x)))


def _gelu_and_grad(x):
    t = jnp.tanh(GELU_C * (x + GELU_A * x * x * x))
    g = 0.5 * x * (1.0 + t)
    dg = 0.5 * (1.0 + t) + 0.5 * x * (1.0 - t * t) * GELU_C * (1.0 + 3.0 * GELU_A * x * x)
    return g, dg


def _log_sigmoid(x):
    return jnp.minimum(x, 0.0) - jnp.log(1.0 + jnp.exp(-jnp.abs(x)))


MM_VMEM_BUDGET = 36 * 1024 * 1024


def _mm_tiles(m, n, k, ta, a_item, b_item, out_item, n_extra):
    if ta:
        return _tile(m, 1024), _tile(n, 1024), _tile(k, 1024)
    row_bytes = k * a_item + n * (out_item + 4 * n_extra)
    w_bytes = k * n * b_item
    for tm in (1024, 512, 256, 128):
        if m % tm == 0 and 2 * tm * row_bytes + 2 * w_bytes + tm * n * 4 <= MM_VMEM_BUDGET:
            return tm, n, k
    return _tile(m, 512), _tile(n, 512), _tile(k, 1024)


def mm(a, b, *, name, ta=False, tb=False, out_dtype=F32, extras=(), epilogue=None):
    m, k = (a.shape[1], a.shape[0]) if ta else a.shape
    n = b.shape[0] if tb else b.shape[1]
    assert k == (b.shape[1] if tb else b.shape[0]), (a.shape, b.shape, ta, tb)
    tm, tn, tk = _mm_tiles(m, n, k, ta, a.dtype.itemsize, b.dtype.itemsize, jnp.dtype(out_dtype).itemsize,
                           len(extras))
    nk = k // tk
    n_extra = len(extras)
    dims = (((0 if ta else 1,), (1 if tb else 0,)), ((), ()))

    def body(a_ref, b_ref, *rest):
        extra_refs, o_ref = rest[:n_extra], rest[n_extra]

        def finish(acc):
            if epilogue is not None:
                acc = epilogue(acc, *[e[...] for e in extra_refs])
            o_ref[...] = acc.astype(o_ref.dtype)

        part = lax.dot_general(a_ref[...].astype(BF16), b_ref[...].astype(BF16), dims,
                               preferred_element_type=F32)
        if nk == 1:
            finish(part)
        else:
            acc_ref = rest[n_extra + 1]
            kk = pl.program_id(2)

            @pl.when(kk == 0)
            def _():
                acc_ref[...] = part

            @pl.when(kk > 0)
            def _():
                acc_ref[...] += part

            @pl.when(kk == nk - 1)
            def _():
                finish(acc_ref[...])

    a_spec = pl.BlockSpec((tk, tm), lambda i, j, kk: (kk, i)) if ta else pl.BlockSpec((tm, tk), lambda i, j, kk: (i, kk))
    b_spec = pl.BlockSpec((tn, tk), lambda i, j, kk: (j, kk)) if tb else pl.BlockSpec((tk, tn), lambda i, j, kk: (kk, j))
    o_spec = pl.BlockSpec((tm, tn), lambda i, j, kk: (i, j))
    return pl.pallas_call(
        body, name=name, grid=(m // tm, n // tn, nk),
        in_specs=[a_spec, b_spec] + [o_spec] * n_extra, out_specs=o_spec,
        out_shape=jax.ShapeDtypeStruct((m, n), out_dtype),
        scratch_shapes=[pltpu.VMEM((tm, tn), F32)] if nk > 1 else [],
        compiler_params=_params("parallel", "parallel", "arbitrary"),
    )(a, b, *extras)


def _add(acc, res):
    return acc + res


def rms_fwd(x, g, *, name):
    t, d = x.shape
    tr = _tile(t, 512, SUBLANES)

    def body(x_ref, g_ref, o_ref):
        xv = x_ref[...]
        r = lax.rsqrt(jnp.mean(xv * xv, axis=-1, keepdims=True) + EPS)
        o_ref[...] = (xv * r * g_ref[...]).astype(o_ref.dtype)

    row = pl.BlockSpec((tr, d), lambda i: (i, 0))
    return pl.pallas_call(
        body, name=name, grid=(t // tr,), in_specs=[row, pl.BlockSpec((1, d), lambda i: (0, 0))], out_specs=row,
        out_shape=jax.ShapeDtypeStruct((t, d), BF16), compiler_params=_params("parallel"),
    )(x, g.reshape(1, d))


def mm_rms_bwd(a, w, x, g, dres, *, name, prev=None):
    t, k = a.shape
    d = w.shape[0]
    assert w.shape[1] == k and x.shape == (t, d)
    n_rows = 4 if prev is not None else 3
    row_bytes = k * a.dtype.itemsize + d * 4 * n_rows
    tr = next(tm for tm in (512, 256, 128, t)
              if t % tm == 0 and 2 * tm * row_bytes + 2 * w.size * w.dtype.itemsize + tm * d * 4 <= MM_VMEM_BUDGET)

    def body(a_ref, w_ref, x_ref, g_ref, dres_ref, *rest):
        dx_ref, dg_ref = rest[-2:]
        dhv = lax.dot_general(a_ref[...].astype(BF16), w_ref[...].astype(BF16), _NT, preferred_element_type=F32)
        if prev is not None:
            dhv = dhv + rest[0][...]
        xv = x_ref[...]
        r = lax.rsqrt(jnp.mean(xv * xv, axis=-1, keepdims=True) + EPS)
        xh = xv * r
        u = dhv * g_ref[...]
        dx_ref[...] = dres_ref[...] + r * (u - xh * jnp.mean(u * xh, axis=-1, keepdims=True))
        part = jnp.sum(dhv * xh, axis=0, keepdims=True)

        @pl.when(pl.program_id(0) == 0)
        def _():
            dg_ref[...] = part

        @pl.when(pl.program_id(0) > 0)
        def _():
            dg_ref[...] += part

    row = pl.BlockSpec((tr, d), lambda i: (i, 0))
    vec = pl.BlockSpec((1, d), lambda i: (0, 0))
    extra = [prev] if prev is not None else []
    dx, dg = pl.pallas_call(
        body, name=name, grid=(t // tr,),
        in_specs=[pl.BlockSpec((tr, k), lambda i: (i, 0)), pl.BlockSpec((d, k), lambda i: (0, 0)), row, vec, row]
        + [row] * len(extra),
        out_specs=[row, vec],
        out_shape=[jax.ShapeDtypeStruct((t, d), F32), jax.ShapeDtypeStruct((1, d), F32)],
        compiler_params=_params("arbitrary"),
    )(a, w, x, g.reshape(1, d), dres, *extra)
    return dx, dg.reshape(d)


def final_loss(x, g, target, *, name):
    t, d = x.shape
    tr = _tile(t, 512, SUBLANES)

    def body(x_ref, g_ref, t_ref, dx_ref, dg_ref, loss_ref):
        xv = x_ref[...]
        gv = g_ref[...]
        r = lax.rsqrt(jnp.mean(xv * xv, axis=-1, keepdims=True) + EPS)
        xh = xv * r
        err = xh * gv - t_ref[...]
        dy = err * (1.0 / d)
        u = dy * gv
        dx_ref[...] = r * (u - xh * jnp.mean(u * xh, axis=-1, keepdims=True))
        dg_part = jnp.sum(dy * xh, axis=0, keepdims=True)
        loss_part = jnp.zeros((1, LANES), F32) + (0.5 / d) * jnp.sum(err * err)

        @pl.when(pl.program_id(0) == 0)
        def _():
            dg_ref[...] = dg_part
            loss_ref[...] = loss_part

        @pl.when(pl.program_id(0) > 0)
        def _():
            dg_ref[...] += dg_part
            loss_ref[...] += loss_part

    row = pl.BlockSpec((tr, d), lambda i: (i, 0))
    vec = pl.BlockSpec((1, d), lambda i: (0, 0))
    dx, dg, loss = pl.pallas_call(
        body, name=name, grid=(t // tr,), in_specs=[row, vec, row],
        out_specs=[row, vec, pl.BlockSpec((1, LANES), lambda i: (0, 0))],
        out_shape=[jax.ShapeDtypeStruct((t, d), F32), jax.ShapeDtypeStruct((1, d), F32),
                   jax.ShapeDtypeStruct((1, LANES), F32)],
        compiler_params=_params("arbitrary"),
    )(x, g.reshape(1, d), target)
    return dx, dg.reshape(d), loss[0, 0]


def _split_dot(x, mat, left):
    hi = x.astype(BF16)
    lo = (x - hi.astype(F32)).astype(BF16)
    if left:
        return (jnp.dot(mat, hi, preferred_element_type=F32) + jnp.dot(mat, lo, preferred_element_type=F32))
    return (jnp.dot(hi, mat, preferred_element_type=F32) + jnp.dot(lo, mat, preferred_element_type=F32))


_NT = (((1,), (1,)), ((), ()))
_TN = (((0,), (0,)), ((), ()))
HEADS_PER_STEP = LANES // HEAD_DIM


def attn_fwd(qkv, b, s, d, *, name):
    t = b * s
    tq = min(256, s)
    nq = s // tq
    pairs = d // LANES
    scale = HEAD_DIM ** -0.5

    def body(q_ref, k_ref, v_ref, o_ref, lt_ref):
        i = pl.program_id(2)
        row = lax.broadcasted_iota(jnp.int32, (tq, tq), 0)
        col = lax.broadcasted_iota(jnp.int32, (tq, tq), 1)
        later = (row > col).astype(BF16)
        causal = col < row
        lanes = [slice(HEAD_DIM * h, HEAD_DIM * (h + 1)) for h in range(HEADS_PER_STEP)]
        qs = [(q_ref[:, sl].astype(F32) * scale).astype(BF16) for sl in lanes]

        def block(js, carry, diag):
            starts = [pl.multiple_of(j * tq, tq) for j in js]
            hs = range(HEADS_PER_STEP)
            chains = [(n, h) for n in range(len(js)) for h in hs]
            kbs = {(n, h): k_ref[pl.ds(starts[n], tq), lanes[h]] for n, h in chains}
            vbs = {(n, h): v_ref[pl.ds(starts[n], tq), lanes[h]] for n, h in chains}
            zs = {c: lax.dot_general(qs[c[1]], kbs[c], _NT, preferred_element_type=F32) for c in chains}
            lss = {c: _log_sigmoid(zs[c]) for c in chains}
            lks = {c: lss[c] - zs[c] for c in chains}
            if diag:
                lks = {c: jnp.where(causal, lks[c], 0.0) for c in chains}
            sums = {c: _split_dot(lks[c], later, left=False) for c in chains}
            runs_in, runs = {}, []
            for h in hs:
                run = carry[h][0]
                for n in range(len(js)):
                    runs_in[n, h] = run
                    run = run + jnp.sum(lks[n, h], axis=1, keepdims=True)
                runs.append(run)
            ws = {c: jnp.exp(lss[c] + sums[c] + runs_in[c]) for c in chains}
            if diag:
                ws = {c: jnp.where(causal, ws[c], 0.0) for c in chains}
            pvs = {c: jnp.dot(ws[c].astype(BF16), vbs[c], preferred_element_type=F32) for c in chains}
            accs = [carry[h][1] + sum(pvs[n, h] for n in range(len(js))) for h in hs]
            return tuple(zip(runs, accs))

        zero = (jnp.zeros((tq, 1), F32), jnp.zeros((tq, HEAD_DIM), F32))
        carry = block([i], (zero,) * HEADS_PER_STEP, True)
        odd = i % 2
        carry = lax.cond(odd == 1, lambda c: block([i - 1], c, False), lambda c: c, carry)
        near = i - 1 - odd
        carry = lax.fori_loop(0, i // 2, lambda n, c: block([near - 2 * n, near - 2 * n - 1], c, False), carry)
        eye = (row == col).astype(F32)
        for h, sl in enumerate(lanes):
            run, acc = carry[h]
            o_ref[:, sl] = acc.astype(o_ref.dtype)
            lt_ref[SUBLANES * h:SUBLANES * (h + 1), :] = lax.dot_general(
                jnp.broadcast_to(run, (tq, SUBLANES)), eye, _TN, precision=lax.Precision.HIGHEST,
                preferred_element_type=F32)

    q_spec = pl.BlockSpec((tq, LANES), lambda bb, p, i: (bb * nq + i, p))
    k_spec = pl.BlockSpec((s, LANES), lambda bb, p, i: (bb, pairs + p))
    v_spec = pl.BlockSpec((s, LANES), lambda bb, p, i: (bb, 2 * pairs + p))
    lt_spec = pl.BlockSpec((None, None, None, HEADS_PER_STEP * SUBLANES, tq), lambda bb, p, i: (bb, p, i, 0, 0))
    return pl.pallas_call(
        body, name=name, grid=(b, pairs, nq), in_specs=[q_spec, k_spec, v_spec], out_specs=[q_spec, lt_spec],
        out_shape=[jax.ShapeDtypeStruct((t, d), BF16),
                   jax.ShapeDtypeStruct((b, pairs, nq, HEADS_PER_STEP * SUBLANES, tq), F32)],
        compiler_params=_params("parallel", "parallel", "arbitrary"),
    )(qkv, qkv, qkv)


def attn_bwd(qkv, totals, do, b, s, d, *, name):
    t = b * s
    tq = min(256, s)
    nq = s // tq
    pairs = d // LANES
    scale = HEAD_DIM ** -0.5

    def body(q_ref, k_ref, v_ref, lt_ref, do_ref, dq_ref, dk_ref, dv_ref):
        i = pl.program_id(2)

        @pl.when(i == 0)
        def _():
            dk_ref[...] = jnp.zeros_like(dk_ref)
            dv_ref[...] = jnp.zeros_like(dv_ref)

        row = lax.broadcasted_iota(jnp.int32, (tq, tq), 0)
        col = lax.broadcasted_iota(jnp.int32, (tq, tq), 1)
        upto = (col <= row).astype(BF16)
        earlier = (col < row).astype(BF16)
        causal = row < col
        lanes = [slice(HEAD_DIM * h, HEAD_DIM * (h + 1)) for h in range(HEADS_PER_STEP)]
        qs = [(q_ref[:, sl].astype(F32) * scale).astype(BF16) for sl in lanes]
        dos = [do_ref[:, sl].astype(BF16) for sl in lanes]
        totals_h = [lt_ref[SUBLANES * h:SUBLANES * h + 1, :] for h in range(HEADS_PER_STEP)]

        def block(js, carry, diag):
            starts = [pl.multiple_of(j * tq, tq) for j in js]
            hs = range(HEADS_PER_STEP)
            ns = range(len(js))
            chains = [(n, h) for n in ns for h in hs]
            kbs = {(n, h): k_ref[pl.ds(starts[n], tq), lanes[h]] for n, h in chains}
            vbs = {(n, h): v_ref[pl.ds(starts[n], tq), lanes[h]] for n, h in chains}
            zs = {c: lax.dot_general(kbs[c], qs[c[1]], _NT, preferred_element_type=F32) for c in chains}
            dws = {c: lax.dot_general(vbs[c], dos[c[1]], _NT, preferred_element_type=F32) for c in chains}
            lss = {c: _log_sigmoid(zs[c]) for c in chains}
            lks = {c: lss[c] - zs[c] for c in chains}
            if diag:
                lks = {c: jnp.where(causal, lks[c], 0.0) for c in chains}
            sums = {c: _split_dot(lks[c], upto, left=True) for c in chains}
            runs_in, runs = {}, []
            for h in hs:
                run = carry[h][0]
                for n in ns:
                    runs_in[n, h] = run
                    run = run + jnp.sum(lks[n, h], axis=0, keepdims=True)
                runs.append(run)
            ws = {c: jnp.exp(lss[c] + ((totals_h[c[1]] - runs_in[c]) - sums[c])) for c in chains}
            if diag:
                ws = {c: jnp.where(causal, ws[c], 0.0) for c in chains}
            gs = {c: dws[c] * ws[c] for c in chains}
            gsums = {c: _split_dot(gs[c], earlier, left=True) for c in chains}
            gruns_in, gruns = {}, []
            for h in hs:
                grun = carry[h][1]
                for n in ns:
                    gruns_in[n, h] = grun
                    grun = grun + jnp.sum(gs[n, h], axis=0, keepdims=True)
                gruns.append(grun)
            dzs = {c: gs[c] - jnp.exp(lss[c]) * (gs[c] + (gruns_in[c] + gsums[c])) for c in chains}
            if diag:
                dzs = {c: jnp.where(causal, dzs[c], 0.0) for c in chains}
            dzbs = {c: dzs[c].astype(BF16) for c in chains}
            for n, h in chains:
                dv_ref[pl.ds(starts[n], tq), lanes[h]] += jnp.dot(ws[n, h].astype(BF16), dos[h],
                                                                  preferred_element_type=F32)
                dk_ref[pl.ds(starts[n], tq), lanes[h]] += jnp.dot(dzbs[n, h], qs[h], preferred_element_type=F32)
            dqs = [carry[h][2] + sum(lax.dot_general(dzbs[n, h], kbs[n, h], _TN, preferred_element_type=F32)
                                     for n in ns) for h in hs]
            return tuple(zip(runs, gruns, dqs))

        zero = (jnp.zeros((1, tq), F32), jnp.zeros((1, tq), F32), jnp.zeros((tq, HEAD_DIM), F32))
        carry = lax.fori_loop(0, i // 2, lambda n, c: block([2 * n, 2 * n + 1], c, False),
                              (zero,) * HEADS_PER_STEP)
        carry = lax.cond(i % 2 == 1, lambda c: block([i - 1], c, False), lambda c: c, carry)
        carry = block([i], carry, True)
        for h, sl in enumerate(lanes):
            dq_ref[:, sl] = carry[h][2] * scale

    q_spec = pl.BlockSpec((tq, LANES), lambda bb, p, i: (bb * nq + i, p))
    k_spec = pl.BlockSpec((s, LANES), lambda bb, p, i: (bb, pairs + p))
    v_spec = pl.BlockSpec((s, LANES), lambda bb, p, i: (bb, 2 * pairs + p))
    lt_spec = pl.BlockSpec((None, None, None) + totals.shape[3:], lambda bb, p, i: (bb, p, i, 0, 0))
    kv_out = pl.BlockSpec((s, LANES), lambda bb, p, i: (bb, p))
    out = jax.ShapeDtypeStruct((t, d), F32)
    return pl.pallas_call(
        body, name=name, grid=(b, pairs, nq), in_specs=[q_spec, k_spec, v_spec, lt_spec, q_spec],
        out_specs=[q_spec, kv_out, kv_out], out_shape=[out, out, out],
        compiler_params=_params("parallel", "parallel", "arbitrary"),
    )(qkv, qkv, qkv, totals, do)


def _shift_down(cur, prev8, dist):
    ext = jnp.concatenate([prev8, cur], axis=0)
    return pltpu.roll(ext, dist, 0)[SUBLANES:]


def _shift_up(cur, next8, dist):
    ext = jnp.concatenate([cur, next8], axis=0)
    return pltpu.roll(ext, ext.shape[0] - dist, 0)[:cur.shape[0]]


def _causal_conv(cur, prev8, w_ref, b_ref):
    taps = w_ref.shape[0]
    out = cur * w_ref[taps - 1:taps, :] + b_ref[...]
    for dist in range(1, taps):
        out = out + _shift_down(cur, prev8, dist) * w_ref[taps - 1 - dist:taps - dist, :]
    return out


def _conv_specs(t, rows, tc, time_axis, dtype=F32):
    sub = SUBLANES * (4 // jnp.dtype(dtype).itemsize)
    per = rows // sub
    last = t // sub - 1

    def grid_ids(*ids):
        return ids[time_axis], ids[1 - time_axis]

    def cur(*ids):
        return grid_ids(*ids)

    def prev(*ids):
        i, j = grid_ids(*ids)
        return (jnp.maximum(i * per - 1, 0), j)

    def nxt(*ids):
        i, j = grid_ids(*ids)
        return (jnp.minimum((i + 1) * per, last), j)

    def chan(*ids):
        return (0, grid_ids(*ids)[1])

    return pl.BlockSpec((rows, tc), cur), pl.BlockSpec((sub, tc), prev), pl.BlockSpec((sub, tc), nxt), chan


def _rows_before(ref, keep):
    return ref[...].astype(F32)[-SUBLANES:] * keep


def _rows_after(ref, keep):
    return ref[...].astype(F32)[:SUBLANES] * keep


def _first_in_seq(i, rows, s):
    return (i % (s // rows)) == 0


def _last_in_seq(i, rows, s):
    return (i % (s // rows)) == (s // rows - 1)


def ffn_act_fwd(ug, uv, cwg, cwv, cbg, cbv, s, *, name):
    t, f = ug.shape
    rows, tc = _tile(s, 512, SUBLANES), _tile(f, 256)
    cur, prev, _, chan = _conv_specs(t, rows, tc, 0, ug.dtype)
    taps = cwg.shape[0]

    def body(ug_ref, ugp_ref, uv_ref, uvp_ref, cwg_ref, cwv_ref, cbg_ref, cbv_ref, a_ref):
        keep = jnp.where(_first_in_seq(pl.program_id(0), rows, s), 0.0, 1.0)
        gate = _causal_conv(ug_ref[...].astype(F32), _rows_before(ugp_ref, keep), cwg_ref, cbg_ref)
        val = _causal_conv(uv_ref[...].astype(F32), _rows_before(uvp_ref, keep), cwv_ref, cbv_ref)
        a_ref[...] = (_gelu(gate) * val).astype(a_ref.dtype)

    wspec = pl.BlockSpec((taps, tc), chan)
    bspec = pl.BlockSpec((1, tc), chan)
    return pl.pallas_call(
        body, name=name, grid=(t // rows, f // tc), in_specs=[cur, prev, cur, prev, wspec, wspec, bspec, bspec],
        out_specs=cur, out_shape=jax.ShapeDtypeStruct((t, f), BF16), compiler_params=_params("parallel", "parallel"),
    )(ug, ug, uv, uv, cwg, cwv, cbg.reshape(1, f), cbv.reshape(1, f))


def _accumulate_rows(first, ref, rows):
    for k, r in enumerate(rows):
        @pl.when(first)
        def _(k=k, r=r):
            ref[k:k + 1, :] = r

        @pl.when(jnp.logical_not(first))
        def _(k=k, r=r):
            ref[k:k + 1, :] += r


def _conv_weight_grads(dc, cur, prev8, taps):
    out = []
    for k in range(taps):
        dist = taps - 1 - k
        xs = cur if dist == 0 else _shift_down(cur, prev8, dist)
        out.append(jnp.sum(dc * xs, axis=0, keepdims=True))
    out.append(jnp.sum(dc, axis=0, keepdims=True))
    return out


def _conv_transpose(dc_ext, rows, w_ref):
    taps = w_ref.shape[0]
    out = dc_ext[:rows] * w_ref[taps - 1:taps, :]
    for dist in range(1, taps):
        out = out + pltpu.roll(dc_ext, dc_ext.shape[0] - dist, 0)[:rows] * w_ref[taps - 1 - dist:taps - dist, :]
    return out


def ffn_act_bwd(ug, uv, cwg, cwv, cbg, cbv, da, s, *, name):
    t, f = ug.shape
    rows, tc = _tile(s, 512, SUBLANES), _tile(f, 256)
    cur, prev, nxt, chan = _conv_specs(t, rows, tc, 1, ug.dtype)
    taps = cwg.shape[0]

    def body(ug_ref, ugp_ref, ugn_ref, uv_ref, uvp_ref, uvn_ref, cwg_ref, cwv_ref, cbg_ref, cbv_ref,
             da_ref, dan_ref, dug_ref, duv_ref, wg_ref, wv_ref):
        i = pl.program_id(1)
        keep_before = jnp.where(_first_in_seq(i, rows, s), 0.0, 1.0)
        keep_after = jnp.where(_last_in_seq(i, rows, s), 0.0, 1.0)
        ugp, uvp = _rows_before(ugp_ref, keep_before), _rows_before(uvp_ref, keep_before)
        uge = jnp.concatenate([ug_ref[...].astype(F32), _rows_after(ugn_ref, 1.0)], axis=0)
        uve = jnp.concatenate([uv_ref[...].astype(F32), _rows_after(uvn_ref, 1.0)], axis=0)
        dae = jnp.concatenate([da_ref[...].astype(F32), _rows_after(dan_ref, keep_after)], axis=0)
        gate = _causal_conv(uge, ugp, cwg_ref, cbg_ref)
        val = _causal_conv(uve, uvp, cwv_ref, cbv_ref)
        act, dact = _gelu_and_grad(gate)
        dgate = dae * val * dact
        dval = dae * act
        dug_ref[...] = _conv_transpose(dgate, rows, cwg_ref).astype(dug_ref.dtype)
        duv_ref[...] = _conv_transpose(dval, rows, cwv_ref).astype(duv_ref.dtype)
        _accumulate_rows(i == 0, wg_ref, _conv_weight_grads(dgate[:rows], uge[:rows], ugp, taps))
        _accumulate_rows(i == 0, wv_ref, _conv_weight_grads(dval[:rows], uve[:rows], uvp, taps))

    wspec = pl.BlockSpec((taps, tc), chan)
    bspec = pl.BlockSpec((1, tc), chan)
    gspec = pl.BlockSpec((taps + 1, tc), chan)
    act_shape = jax.ShapeDtypeStruct((t, f), BF16)
    stat_shape = jax.ShapeDtypeStruct((taps + 1, f), F32)
    return pl.pallas_call(
        body, name=name, grid=(f // tc, t // rows),
        in_specs=[cur, prev, nxt, cur, prev, nxt, wspec, wspec, bspec, bspec, cur, nxt],
        out_specs=[cur, cur, gspec, gspec], out_shape=[act_shape, act_shape, stat_shape, stat_shape],
        compiler_params=_params("parallel", "arbitrary"),
    )(ug, ug, ug, uv, uv, uv, cwg, cwv, cbg.reshape(1, f), cbv.reshape(1, f), da, da)


def conv_input_grad(dc, cw, s, *, name, out_dtype):
    t, f = dc.shape
    rows, tc = _tile(s, 512, SUBLANES), _tile(f, 256)
    cur, _, nxt, chan = _conv_specs(t, rows, tc, 0, dc.dtype)
    taps = cw.shape[0]

    def body(dc_ref, dcn_ref, cw_ref, o_ref):
        keep = jnp.where(_last_in_seq(pl.program_id(0), rows, s), 0.0, 1.0)
        dcc = dc_ref[...].astype(F32)
        dcn = _rows_after(dcn_ref, keep)
        out = dcc * cw_ref[taps - 1:taps, :]
        for dist in range(1, taps):
            out = out + _shift_up(dcc, dcn, dist) * cw_ref[taps - 1 - dist:taps - dist, :]
        o_ref[...] = out.astype(o_ref.dtype)

    return pl.pallas_call(
        body, name=name, grid=(t // rows, f // tc), in_specs=[cur, nxt, pl.BlockSpec((taps, tc), chan)],
        out_specs=cur, out_shape=jax.ShapeDtypeStruct((t, f), out_dtype),
        compiler_params=_params("parallel", "parallel"),
    )(dc, dc, cw)


def rnn_conv_fwd(yr, cw, cb, s, *, name):
    t, w = yr.shape
    rows, tc = _tile(s, 512, SUBLANES), _tile(w, 256)
    cur, prev, _, chan = _conv_specs(t, rows, tc, 0, yr.dtype)
    taps = cw.shape[0]

    def body(y_ref, yp_ref, cw_ref, cb_ref, o_ref):
        keep = jnp.where(_first_in_seq(pl.program_id(0), rows, s), 0.0, 1.0)
        o_ref[...] = _causal_conv(y_ref[...].astype(F32), _rows_before(yp_ref, keep), cw_ref, cb_ref)

    return pl.pallas_call(
        body, name=name, grid=(t // rows, w // tc),
        in_specs=[cur, prev, pl.BlockSpec((taps, tc), chan), pl.BlockSpec((1, tc), chan)], out_specs=cur,
        out_shape=jax.ShapeDtypeStruct((t, w), F32), compiler_params=_params("parallel", "parallel"),
    )(yr, yr, cw, cb.reshape(1, w))


def rnn_conv_wgrad(dxr, yr, s, taps, *, name):
    t, w = yr.shape
    rows, tc = _tile(s, 512, SUBLANES), _tile(w, 256)
    cur, prev, _, chan = _conv_specs(t, rows, tc, 1, yr.dtype)

    def body(d_ref, y_ref, yp_ref, o_ref):
        i = pl.program_id(1)
        keep = jnp.where(_first_in_seq(i, rows, s), 0.0, 1.0)
        grads = _conv_weight_grads(d_ref[...], y_ref[...].astype(F32), _rows_before(yp_ref, keep), taps)
        _accumulate_rows(i == 0, o_ref, grads)

    return pl.pallas_call(
        body, name=name, grid=(w // tc, t // rows), in_specs=[cur, cur, prev],
        out_specs=pl.BlockSpec((taps + 1, tc), chan), out_shape=jax.ShapeDtypeStruct((taps + 1, w), F32),
        compiler_params=_params("parallel", "arbitrary"),
    )(dxr, yr, yr)


SCAN_ROWS = 32


def _one_minus_exp(x):
    series = -x * (1.0 + x * (0.5 + x * (1.0 / 6.0)))
    return jnp.where(x > -0.01, series, 1.0 - jnp.exp(x))


def _gates(ga, gi, ba, bx, log_lam):
    ra = jax.nn.sigmoid(ga + ba)
    ri = jax.nn.sigmoid(gi + bx)
    log_a = LRU_C * ra * log_lam
    a = jnp.exp(log_a)
    mult = jnp.sqrt(_one_minus_exp(2.0 * log_a))
    return ra, ri, a, mult


def rnn_scan_fwd(ga, gi, xr, yg, ba, bx, lam, b, s, *, name):
    t, w = xr.shape
    tc = _tile(w, 256)
    rb = min(SCAN_ROWS, s)
    blocks = s // rb
    steps = [1 << e for e in range(rb.bit_length() - 1)]

    def body(ga_ref, gi_ref, xr_ref, yg_ref, ba_ref, bx_ref, lam_ref, h_ref, y_ref):
        log_lam = _log_sigmoid(lam_ref[...])
        ridx = lax.broadcasted_iota(jnp.int32, (rb, tc), 0)

        def step(n, carry):
            rs = pl.ds(pl.multiple_of(n * rb, rb), rb)
            xrv = xr_ref[rs, :]
            _, ri, a, mult = _gates(ga_ref[rs, :], gi_ref[rs, :], ba_ref[...], bx_ref[...], log_lam)
            u = mult * (ri * xrv)
            for dist in steps:
                a_sh = jnp.where(ridx >= dist, pltpu.roll(a, dist, 0), 1.0)
                u_sh = jnp.where(ridx >= dist, pltpu.roll(u, dist, 0), 0.0)
                u = a * u_sh + u
                a = a * a_sh
            hb = u + a * carry
            h_ref[rs, :] = hb
            y_ref[rs, :] = (_gelu(yg_ref[rs, :]) * hb).astype(y_ref.dtype)
            return hb[rb - 1:rb, :]

        lax.fori_loop(0, blocks, step, jnp.zeros((1, tc), F32))

    seq = pl.BlockSpec((s, tc), lambda bb, j: (bb, j))
    vec = pl.BlockSpec((1, tc), lambda bb, j: (0, j))
    return pl.pallas_call(
        body, name=name, grid=(b, w // tc), in_specs=[seq, seq, seq, seq, vec, vec, vec], out_specs=[seq, seq],
        out_shape=[jax.ShapeDtypeStruct((t, w), F32), jax.ShapeDtypeStruct((t, w), BF16)],
        compiler_params=_params("parallel", "parallel"),
    )(ga, gi, xr, yg, ba.reshape(1, w), bx.reshape(1, w), lam.reshape(1, w))


def rnn_scan_bwd(dy, ga, gi, xr, yg, h, ba, bx, lam, b, s, *, name):
    t, w = xr.shape
    tc = _tile(w, 256)
    rb = min(SCAN_ROWS, s)
    blocks = s // rb
    steps = [1 << e for e in range(rb.bit_length() - 1)]

    def body(dy_ref, ga_ref, gi_ref, xr_ref, yg_ref, h_ref, ba_ref, bx_ref, lam_ref,
             dyg_ref, dga_ref, dgi_ref, dxr_ref, stat_ref):
        lamv = lam_ref[...]
        log_lam = _log_sigmoid(lamv)
        dlog_lam = jax.nn.sigmoid(-lamv)
        ridx = lax.broadcasted_iota(jnp.int32, (rb, tc), 0)
        last = rb - 1

        def step(n, carry):
            lam_next, a_next, s_a, s_x, s_l = carry
            blk = blocks - 1 - n
            rs = pl.ds(pl.multiple_of(blk * rb, rb), rb)
            rp = pl.ds(pl.multiple_of(jnp.maximum(blk * rb - SUBLANES, 0), SUBLANES), SUBLANES)
            xrv = xr_ref[rs, :]
            hv = h_ref[rs, :]
            h_before = jnp.where(blk > 0, h_ref[rp, :][SUBLANES - 1:, :], 0.0)
            h_prev = jnp.where(ridx >= 1, pltpu.roll(hv, 1, 0), h_before)
            ra, ri, a, mult = _gates(ga_ref[rs, :], gi_ref[rs, :], ba_ref[...], bx_ref[...], log_lam)
            act, dact = _gelu_and_grad(yg_ref[rs, :])
            dyv = dy_ref[rs, :]
            dyg_ref[rs, :] = (dyv * hv * dact).astype(dyg_ref.dtype)
            v = dyv * act
            c = jnp.where(ridx < last, pltpu.roll(a, last, 0), a_next)
            for dist in steps:
                c_sh = jnp.where(ridx < rb - dist, pltpu.roll(c, rb - dist, 0), 1.0)
                v_sh = jnp.where(ridx < rb - dist, pltpu.roll(v, rb - dist, 0), 0.0)
                v = v + c * v_sh
                c = c * c_sh
            dh = v + c * lam_next
            du_ri_x = dh * xrv
            dmult = du_ri_x * ri
            dri = du_ri_x * mult
            dxr_ref[rs, :] = dh * mult * ri
            dlog_a = dh * h_prev * a - dmult * (a * a) / mult
            dra = dlog_a * (LRU_C * log_lam)
            dpa = dra * ra * (1.0 - ra)
            dpi = dri * ri * (1.0 - ri)
            dga_ref[rs, :] = dpa.astype(dga_ref.dtype)
            dgi_ref[rs, :] = dpi.astype(dgi_ref.dtype)
            s_a = s_a + jnp.sum(dpa, axis=0, keepdims=True)
            s_x = s_x + jnp.sum(dpi, axis=0, keepdims=True)
            s_l = s_l + jnp.sum(dlog_a * ra, axis=0, keepdims=True)
            return dh[0:1, :], a[0:1, :], s_a, s_x, s_l

        zero = jnp.zeros((1, tc), F32)
        _, _, s_a, s_x, s_l = lax.fori_loop(0, blocks, step, (zero, zero, zero, zero, zero))
        _accumulate_rows(pl.program_id(1) == 0, stat_ref, [s_a, s_x, s_l * (LRU_C * dlog_lam)])

    seq = pl.BlockSpec((s, tc), lambda j, bb: (bb, j))
    vec = pl.BlockSpec((1, tc), lambda j, bb: (0, j))
    half = jax.ShapeDtypeStruct((t, w), BF16)
    return pl.pallas_call(
        body, name=name, grid=(w // tc, b), in_specs=[seq, seq, seq, seq, seq, seq, vec, vec, vec],
        out_specs=[seq, seq, seq, seq, pl.BlockSpec((3, tc), lambda j, bb: (0, j))],
        out_shape=[half, half, half, jax.ShapeDtypeStruct((t, w), F32), jax.ShapeDtypeStruct((3, w), F32)],
        compiler_params=_params("parallel", "arbitrary"),
    )(dy, ga, gi, xr, yg, h, ba.reshape(1, w), bx.reshape(1, w), lam.reshape(1, w))


def _ple_mix(acc, x, gate):
    return x + jax.nn.sigmoid(gate) * acc


def ple_bwd(dx, gate, pin, w_proj, *, name):
    t, d = dx.shape
    k = pin.shape[1]
    tr = _tile(t, 512, SUBLANES)

    def body(dx_ref, g_ref, p_ref, w_ref, dg_ref, de_ref):
        emb = jnp.dot(p_ref[...].astype(BF16), w_ref[...], preferred_element_type=F32)
        sg = jax.nn.sigmoid(g_ref[...])
        dxv = dx_ref[...]
        de_ref[...] = (dxv * sg).astype(de_ref.dtype)
        dg_ref[...] = (dxv * emb * sg * (1.0 - sg)).astype(dg_ref.dtype)

    row = pl.BlockSpec((tr, d), lambda i: (i, 0))
    half = jax.ShapeDtypeStruct((t, d), BF16)
    return pl.pallas_call(
        body, name=name, grid=(t // tr,),
        in_specs=[row, row, pl.BlockSpec((tr, k), lambda i: (i, 0)), pl.BlockSpec((k, d), lambda i: (0, 0))],
        out_specs=[row, row], out_shape=[half, half], compiler_params=_params("parallel"))(dx, gate, pin, w_proj)


def adamw(w, g, m, v, *, name):
    shape = w.shape
    cols = shape[-1]
    rows = w.size // cols
    tr = _tile(rows, 1024, SUBLANES)
    bc1 = 1.0 / (1.0 - ADAM_B1 ** ADAM_STEP)
    bc2 = 1.0 / (1.0 - ADAM_B2 ** ADAM_STEP)

    def body(w_ref, g_ref, m_ref, v_ref, d_ref, nm_ref, nv_ref):
        gv = g_ref[...]
        nm = ADAM_B1 * m_ref[...] + (1.0 - ADAM_B1) * gv
        nv = ADAM_B2 * v_ref[...] + (1.0 - ADAM_B2) * (gv * gv)
        d_ref[...] = -ADAM_LR * ((nm * bc1) / (jnp.sqrt(nv * bc2) + ADAM_EPS) + ADAM_WD * w_ref[...])
        nm_ref[...] = nm
        nv_ref[...] = nv

    blk = pl.BlockSpec((tr, cols), lambda i: (i, 0))
    out = jax.ShapeDtypeStruct((rows, cols), F32)
    res = pl.pallas_call(body, name=name, grid=(rows // tr,), in_specs=[blk] * 4, out_specs=[blk] * 3,
                         out_shape=[out] * 3, compiler_params=_params("parallel"),
                         )(*[a.reshape(rows, cols) for a in (w, g, m, v)])
    return [r.reshape(shape) for r in res]


ANY = pl.BlockSpec(memory_space=pl.ANY)


def _place():
    return lax.axis_index("x"), lax.axis_index("y"), lax.axis_index("c")


def all_gather(v, *, name):
    rows, cols = v.shape

    def body(v_ref, out_ref, send_sems, recv_sems, local_sem):
        x, y, c = _place()
        me, sibling = (x, y, c), (x, y, 1 - c)
        chips = [(1 - x, y), (x, 1 - y), (1 - x, 1 - y)]

        def slot(px, py, pc):
            return out_ref.at[4 * px + 2 * py + pc]

        def copy(k, block, to, src=None):
            return pltpu.make_async_remote_copy(
                src_ref=slot(*block) if src is None else src, dst_ref=slot(*block),
                send_sem=send_sems.at[k], recv_sem=recv_sems.at[k], device_id=to, device_id_type=MESH)

        mine = pltpu.make_async_copy(v_ref, slot(*me), local_sem)
        mine.start()
        first = [copy(0, me, sibling, src=v_ref)]
        first += [copy(1 + j, me, (*chip, c), src=v_ref) for j, chip in enumerate(chips)]
        for cp in first:
            cp.start()
        passed = [copy(4 + j, (*chip, c), sibling) for j, chip in enumerate(chips)]
        for j, chip in enumerate(chips):
            copy(1 + j, (*chip, c), me).wait_recv()
            passed[j].start()
        copy(0, sibling, me).wait_recv()
        for j, chip in enumerate(chips):
            copy(4 + j, (*chip, 1 - c), me).wait_recv()
        for cp in first + passed:
            cp.wait_send()
        mine.wait()

    return pl.pallas_call(
        body, name=name, out_shape=jax.ShapeDtypeStruct((N_DEV, rows, cols), v.dtype), in_specs=[ANY],
        out_specs=ANY,
        scratch_shapes=[pltpu.SemaphoreType.DMA((7,)), pltpu.SemaphoreType.DMA((7,)), pltpu.SemaphoreType.DMA(())],
    )(v)


def sibling_exchange(parts, *, name):
    _, quads, rows, cols = parts.shape

    def body(p_ref, got_ref, send_sem, recv_sem):
        x, y, c = _place()
        cp = pltpu.make_async_remote_copy(src_ref=p_ref.at[1 - c], dst_ref=got_ref, send_sem=send_sem,
                                          recv_sem=recv_sem, device_id=(x, y, 1 - c), device_id_type=MESH)
        cp.start()
        cp.wait()

    return pl.pallas_call(
        body, name=name, out_shape=jax.ShapeDtypeStruct((quads, rows, cols), parts.dtype), in_specs=[ANY],
        out_specs=ANY, scratch_shapes=[pltpu.SemaphoreType.DMA(()), pltpu.SemaphoreType.DMA(())],
    )(parts)


def chip_exchange(parts, *, name):
    _, rows, cols = parts.shape

    def body(p_ref, got_ref, send_sems, recv_sems):
        x, y, c = _place()
        chips = [(1 - x, y), (x, 1 - y), (1 - x, 1 - y)]
        copies = [pltpu.make_async_remote_copy(
            src_ref=p_ref.at[2 * cx + cy], dst_ref=got_ref.at[k], send_sem=send_sems.at[k],
            recv_sem=recv_sems.at[k], device_id=(cx, cy, c), device_id_type=MESH)
            for k, (cx, cy) in enumerate(chips)]
        for cp in copies:
            cp.start()
        for cp in copies:
            cp.wait()

    return pl.pallas_call(
        body, name=name, out_shape=jax.ShapeDtypeStruct((3, rows, cols), parts.dtype), in_specs=[ANY],
        out_specs=ANY, scratch_shapes=[pltpu.SemaphoreType.DMA((3,)), pltpu.SemaphoreType.DMA((3,))],
    )(parts)


def add_sibling(parts, got, *, name):
    _, quads, rows, cols = parts.shape
    tr = _tile(rows, GRAD_ROWS_TILE, SUBLANES)

    def body(c_ref, p_ref, g_ref, o_ref, ob_ref):
        total = p_ref[...] + g_ref[...]
        o_ref[...] = total
        ob_ref[...] = total.astype(ob_ref.dtype)

    c = lax.axis_index("c").astype(jnp.int32).reshape(1)
    quad = pl.BlockSpec((None, tr, cols), lambda q, i, c_ref: (q, i, 0))
    return pl.pallas_call(
        body, name=name,
        grid_spec=pltpu.PrefetchScalarGridSpec(
            num_scalar_prefetch=1, grid=(quads, rows // tr),
            in_specs=[pl.BlockSpec((None, None, tr, cols), lambda q, i, c_ref: (c_ref[0], q, i, 0)), quad],
            out_specs=[quad, quad]),
        out_shape=[jax.ShapeDtypeStruct((quads, rows, cols), parts.dtype),
                   jax.ShapeDtypeStruct((quads, rows, cols), BF16)],
        compiler_params=_params("parallel", "parallel"),
    )(c, parts, got)


def add_chips(parts, got, *, name):
    _, rows, cols = parts.shape
    tr = _tile(rows, GRAD_ROWS_TILE, SUBLANES)

    def body(q_ref, p_ref, g_ref, o_ref):
        o_ref[...] = ((p_ref[...] + g_ref[0].astype(F32)) + g_ref[1].astype(F32)) + g_ref[2].astype(F32)

    q = (2 * lax.axis_index("x") + lax.axis_index("y")).astype(jnp.int32).reshape(1)
    return pl.pallas_call(
        body, name=name,
        grid_spec=pltpu.PrefetchScalarGridSpec(
            num_scalar_prefetch=1, grid=(rows // tr,),
            in_specs=[pl.BlockSpec((None, tr, cols), lambda i, q_ref: (q_ref[0], i, 0)),
                      pl.BlockSpec((3, tr, cols), lambda i, q_ref: (0, i, 0))],
            out_specs=pl.BlockSpec((tr, cols), lambda i, q_ref: (i, 0))),
        out_shape=jax.ShapeDtypeStruct((rows, cols), parts.dtype), compiler_params=_params("parallel"),
    )(q, parts, got)


def _pack(arrays, dtype, row_align):
    pieces, spans, at = [], [], 0
    for a in arrays:
        flat = a.reshape(-1).astype(dtype)
        rows = -(-flat.size // (LANES * row_align)) * row_align
        pieces.append(jnp.pad(flat, (0, rows * LANES - flat.size)).reshape(rows, LANES))
        spans.append((at, rows))
        at += rows
    return jnp.concatenate(pieces, axis=0), spans


def _unpack(buf, spans, shapes, lead):
    out = []
    for (at, rows), shape in zip(spans, shapes):
        size = math.prod(shape)
        piece = buf[..., at:at + rows, :].reshape(*lead, rows * LANES)[..., :size]
        out.append(piece.reshape(*lead, *shape))
    return out


def _whole(gathered, axis):
    moved = jnp.moveaxis(gathered, 0, axis)
    shape = moved.shape
    return moved.reshape(*shape[:axis], shape[axis] * shape[axis + 1], *shape[axis + 2:])


def _blocks(whole, axis):
    shape = whole.shape
    cut = whole.reshape(*shape[:axis], N_DEV, shape[axis] // N_DEV, *shape[axis + 1:])
    return jnp.moveaxis(cut, axis, 0)


def _block_diag(w):
    heads, n, _ = w.shape
    eye = jnp.eye(heads, dtype=w.dtype)
    return (w[:, :, None, :] * eye[:, None, :, None]).reshape(heads * n, heads * n)


def _diag_blocks(full, heads):
    n = full.shape[0] // heads
    return jnp.stack([full[h * n:(h + 1) * n, h * n:(h + 1) * n] for h in range(heads)])


def kernel(x, p, norm_mix, attn_w_qkv, attn_w_o, rnn_w_in, rnn_conv_w, rnn_conv_b, rnn_w_gate_a, rnn_b_gate_a, rnn_w_gate_x, rnn_b_gate_x, rnn_lru_param, rnn_w_out, norm_ffn, ffn_w_up, ffn_conv_w, ffn_conv_b, ffn_w_down, norm_ple, ple_w_gate, ple_w_proj, norm_final, loss_target, m_norm_mix, m_attn_w_qkv, m_attn_w_o, m_rnn_w_in, m_rnn_conv_w, m_rnn_conv_b, m_rnn_w_gate_a, m_rnn_b_gate_a, m_rnn_w_gate_x, m_rnn_b_gate_x, m_rnn_lru_param, m_rnn_w_out, m_norm_ffn, m_ffn_w_up, m_ffn_conv_w, m_ffn_conv_b, m_ffn_w_down, m_norm_ple, m_ple_w_gate, m_ple_w_proj, m_norm_final, v_norm_mix, v_attn_w_qkv, v_attn_w_o, v_rnn_w_in, v_rnn_conv_w, v_rnn_conv_b, v_rnn_w_gate_a, v_rnn_b_gate_a, v_rnn_w_gate_x, v_rnn_b_gate_x, v_rnn_lru_param, v_rnn_w_out, v_norm_ffn, v_ffn_w_up, v_ffn_conv_w, v_ffn_conv_b, v_ffn_w_down, v_norm_ple, v_ple_w_gate, v_ple_w_proj, v_norm_final):
    given = dict(locals())
    local = {n: given[n] for n in WEIGHTS}
    bsz, seq, d = x.shape
    t = bsz * seq
    depth = norm_mix.shape[0]
    width = rnn_w_out.shape[1] * N_DEV
    ffn = ffn_w_down.shape[1] * N_DEV

    buf, spans = _pack([local[n] for n in MATMUL_WEIGHTS], BF16, 2 * SUBLANES)
    got = _unpack(all_gather(buf, name="gather_matmul_weights"), spans,
                  [local[n].shape for n in MATMUL_WEIGHTS], (N_DEV,))
    full = {n: _whole(g, SHARD_AXIS[n]) for n, g in zip(MATMUL_WEIGHTS, got)}
    buf, spans = _pack([local[n] for n in CHANNEL_WEIGHTS], F32, SUBLANES)
    got = _unpack(all_gather(buf, name="gather_channel_weights"), spans,
                  [local[n].shape for n in CHANNEL_WEIGHTS], (N_DEV,))
    full.update({n: _whole(g, SHARD_AXIS[n]) for n, g in zip(CHANNEL_WEIGHTS, got)})
    for n in REPLICATED:
        full[n] = local[n]

    grads = {}

    def stack(name, layer, value, count):
        grads.setdefault(name, [None] * count)[layer] = value

    saved = []
    h0 = x.reshape(t, d)
    for i in range(depth):
        slot = i // 2
        sv = {"x0": h0}
        hn = rms_fwd(h0, full["norm_mix"][i], name=f"l{i}_mix_norm")
        sv["hn"] = hn
        if i % 2 == 0:
            qkv = mm(hn, full["attn_w_qkv"][slot], out_dtype=BF16, name=f"l{i}_qkv")
            o, totals = attn_fwd(qkv, bsz, seq, d, name=f"l{i}_attn")
            h1 = mm(o, full["attn_w_o"][slot], extras=(h0,), epilogue=_add, name=f"l{i}_attn_out")
            sv.update(qkv=qkv, o=o, totals=totals)
        else:
            w_in = full["rnn_w_in"][slot]
            yg = mm(hn, w_in[:, :width], name=f"l{i}_rnn_in_gate")
            yr = mm(hn, w_in[:, width:], name=f"l{i}_rnn_in_rec")
            xr = rnn_conv_fwd(yr, full["rnn_conv_w"][slot], full["rnn_conv_b"][slot], seq, name=f"l{i}_rnn_conv")
            wa = _block_diag(full["rnn_w_gate_a"][slot]).astype(BF16)
            wx = _block_diag(full["rnn_w_gate_x"][slot]).astype(BF16)
            ga = mm(xr, wa, name=f"l{i}_rnn_gate_a")
            gi = mm(xr, wx, name=f"l{i}_rnn_gate_x")
            hs, y = rnn_scan_fwd(ga, gi, xr, yg, full["rnn_b_gate_a"][slot], full["rnn_b_gate_x"][slot],
                                 full["rnn_lru_param"][slot], bsz, seq, name=f"l{i}_rnn_scan")
            h1 = mm(y, full["rnn_w_out"][slot], extras=(h0,), epilogue=_add, name=f"l{i}_rnn_out")
            sv.update(yg=yg, yr=yr, xr=xr, wa=wa, wx=wx, ga=ga, gi=gi, hs=hs, y=y)
        sv["x1"] = h1
        hn2 = rms_fwd(h1, full["norm_ffn"][i], name=f"l{i}_ffn_norm")
        w_up = full["ffn_w_up"][i]
        ug = mm(hn2, w_up[:, :ffn], out_dtype=BF16, name=f"l{i}_ffn_up_gate")
        uv = mm(hn2, w_up[:, ffn:], out_dtype=BF16, name=f"l{i}_ffn_up_val")
        cw, cb = full["ffn_conv_w"][i], full["ffn_conv_b"][i]
        act = ffn_act_fwd(ug, uv, cw[:, :ffn], cw[:, ffn:], cb[:ffn], cb[ffn:], seq, name=f"l{i}_ffn_act")
        h2 = mm(act, full["ffn_w_down"][i], extras=(h1,), epilogue=_add, name=f"l{i}_ffn_down")
        sv.update(hn2=hn2, ug=ug, uv=uv, act=act, x2=h2)
        hn3 = rms_fwd(h2, full["norm_ple"][i], name=f"l{i}_ple_norm")
        pg = mm(hn3, full["ple_w_gate"][i], name=f"l{i}_ple_gate")
        pin = p[i].reshape(t, p.shape[-1])
        h0 = mm(pin, full["ple_w_proj"][i], extras=(h2, pg), epilogue=_ple_mix, name=f"l{i}_ple_proj_mix")
        sv.update(hn3=hn3, pg=pg, pin=pin)
        saved.append(sv)

    dx, g_final, loss_part = final_loss(h0, full["norm_final"], loss_target.reshape(t, d), name="final_loss")
    grads["norm_final"] = g_final
    loss = lax.psum(loss_part, ("x", "y", "c"))

    for i in reversed(range(depth)):
        slot = i // 2
        sv = saved[i]
        dpg, dpe = ple_bwd(dx, sv["pg"], sv["pin"], full["ple_w_proj"][i], name=f"l{i}_ple_mix_bwd")
        stack("ple_w_proj", i, mm(sv["pin"], dpe, ta=True, name=f"l{i}_ple_proj_wgrad"), depth)
        stack("ple_w_gate", i, mm(sv["hn3"], dpg, ta=True, name=f"l{i}_ple_gate_wgrad"), depth)
        dx, gn = mm_rms_bwd(dpg, full["ple_w_gate"][i], sv["x2"], full["norm_ple"][i], dx,
                            name=f"l{i}_ple_gate_dgrad_norm_bwd")
        stack("norm_ple", i, gn, depth)
        stack("ffn_w_down", i, mm(sv["act"], dx, ta=True, name=f"l{i}_ffn_down_wgrad"), depth)
        dact = mm(dx, full["ffn_w_down"][i], tb=True, out_dtype=BF16, name=f"l{i}_ffn_down_dgrad")
        cw, cb = full["ffn_conv_w"][i], full["ffn_conv_b"][i]
        taps = cw.shape[0]
        dug, duv, sg, svv = ffn_act_bwd(sv["ug"], sv["uv"], cw[:, :ffn], cw[:, ffn:], cb[:ffn], cb[ffn:], dact, seq,
                                        name=f"l{i}_ffn_act_bwd")
        stack("ffn_conv_w", i, jnp.concatenate([sg[:taps], svv[:taps]], axis=1), depth)
        stack("ffn_conv_b", i, jnp.concatenate([sg[taps], svv[taps]], axis=0), depth)
        stack("ffn_w_up", i, jnp.concatenate(
            [mm(sv["hn2"], dug, ta=True, name=f"l{i}_ffn_up_wgrad_gate"),
             mm(sv["hn2"], duv, ta=True, name=f"l{i}_ffn_up_wgrad_val")], axis=1), depth)
        w_up = full["ffn_w_up"][i]
        dhn2 = mm(dug, w_up[:, :ffn], tb=True, name=f"l{i}_ffn_up_dgrad_gate")
        dx, gn = mm_rms_bwd(duv, w_up[:, ffn:], sv["x1"], full["norm_ffn"][i], dx, prev=dhn2,
                            name=f"l{i}_ffn_up_dgrad_val_norm_bwd")
        stack("norm_ffn", i, gn, depth)
        if i % 2 == 0:
            stack("attn_w_o", slot, mm(sv["o"], dx, ta=True, name=f"l{i}_attn_out_wgrad"), depth // 2)
            do = mm(dx, full["attn_w_o"][slot], tb=True, out_dtype=BF16, name=f"l{i}_attn_out_dgrad")
            dq, dk, dv = attn_bwd(sv["qkv"], sv["totals"], do, bsz, seq, d, name=f"l{i}_attn_bwd")
            dqkv = jnp.concatenate([dq, dk, dv], axis=1).astype(BF16)
            stack("attn_w_qkv", slot, mm(sv["hn"], dqkv, ta=True, name=f"l{i}_qkv_wgrad"), depth // 2)
            dx, gn = mm_rms_bwd(dqkv, full["attn_w_qkv"][slot], sv["x0"], full["norm_mix"][i], dx,
                                name=f"l{i}_qkv_dgrad_norm_bwd")
        else:
            nrnn = depth // 2
            stack("rnn_w_out", slot, mm(sv["y"], dx, ta=True, name=f"l{i}_rnn_out_wgrad"), nrnn)
            dy = mm(dx, full["rnn_w_out"][slot], tb=True, name=f"l{i}_rnn_out_dgrad")
            dyg, dga, dgi, dxr, stats = rnn_scan_bwd(
                dy, sv["ga"], sv["gi"], sv["xr"], sv["yg"], sv["hs"], full["rnn_b_gate_a"][slot],
                full["rnn_b_gate_x"][slot], full["rnn_lru_param"][slot], bsz, seq, name=f"l{i}_rnn_scan_bwd")
            stack("rnn_b_gate_a", slot, stats[0], nrnn)
            stack("rnn_b_gate_x", slot, stats[1], nrnn)
            stack("rnn_lru_param", slot, stats[2], nrnn)
            stack("rnn_w_gate_a", slot, _diag_blocks(mm(sv["xr"], dga, ta=True, name=f"l{i}_rnn_gate_a_wgrad"),
                                                     RNN_HEADS), nrnn)
            stack("rnn_w_gate_x", slot, _diag_blocks(mm(sv["xr"], dgi, ta=True, name=f"l{i}_rnn_gate_x_wgrad"),
                                                     RNN_HEADS), nrnn)
            dxr = mm(dga, sv["wa"], tb=True, extras=(dxr,), epilogue=_add, name=f"l{i}_rnn_gate_a_dgrad")
            dxr = mm(dgi, sv["wx"], tb=True, extras=(dxr,), epilogue=_add, name=f"l{i}_rnn_gate_x_dgrad")
            rcw = full["rnn_conv_w"][slot]
            rtaps = rcw.shape[0]
            cstats = rnn_conv_wgrad(dxr, sv["yr"], seq, rtaps, name=f"l{i}_rnn_conv_wgrad")
            stack("rnn_conv_w", slot, cstats[:rtaps], nrnn)
            stack("rnn_conv_b", slot, cstats[rtaps], nrnn)
            dyr = conv_input_grad(dxr, rcw, seq, out_dtype=BF16, name=f"l{i}_rnn_conv_bwd")
            stack("rnn_w_in", slot, jnp.concatenate(
                [mm(sv["hn"], dyg, ta=True, name=f"l{i}_rnn_in_wgrad_gate"),
                 mm(sv["hn"], dyr, ta=True, name=f"l{i}_rnn_in_wgrad_rec")], axis=1), nrnn)
            w_in = full["rnn_w_in"][slot]
            dhn = mm(dyg, w_in[:, :width], tb=True, name=f"l{i}_rnn_in_dgrad_gate")
            dx, gn = mm_rms_bwd(dyr, w_in[:, width:], sv["x0"], full["norm_mix"][i], dx, prev=dhn,
                                name=f"l{i}_rnn_in_dgrad_rec_norm_bwd")
        stack("norm_mix", i, gn, depth)
    grad_x = dx.reshape(bsz, seq, d)

    whole = {n: (jnp.stack(g) if isinstance(g, list) else g) for n, g in grads.items()}

    sharded = [n for n in WEIGHTS if n in SHARD_AXIS]
    cut = [_blocks(whole[n], SHARD_AXIS[n]).reshape(N_DEV, -1) for n in sharded]
    rep = jnp.concatenate([whole[n].reshape(-1) for n in REPLICATED])
    rep_len = rep.size
    rep_rows = -(-rep_len // (N_DEV * LANES * SUBLANES)) * SUBLANES
    rep = jnp.pad(rep, (0, N_DEV * rep_rows * LANES - rep_len)).reshape(N_DEV, rep_rows * LANES)
    pieces, spans, at = [], [], 0
    for a in cut + [rep]:
        rows = -(-a.shape[1] // (LANES * SUBLANES)) * SUBLANES
        pieces.append(jnp.pad(a, ((0, 0), (0, rows * LANES - a.shape[1]))).reshape(N_DEV, rows, LANES))
        spans.append((at, rows))
        at += rows
    tail = -at % GRAD_ROWS_TILE
    pieces.append(jnp.zeros((N_DEV, tail, LANES), F32))
    at += tail
    parts = jnp.concatenate(pieces, axis=1)
    parts = parts.reshape(4, 2, at, LANES).transpose(1, 0, 2, 3)
    from_sibling = sibling_exchange(parts, name="grads_to_sibling")
    chip_sum, chip_sum_bf16 = add_sibling(parts, from_sibling, name="grads_add_sibling")
    from_chips = chip_exchange(chip_sum_bf16, name="grads_to_chips")
    mine = add_chips(chip_sum, from_chips, name="grads_add_chips")
    local_grads = dict(zip(sharded, _unpack(mine, spans[:-1], [local[n].shape for n in sharded], ())))
    rep_at, _ = spans[-1]
    rep_all = all_gather(mine[rep_at:rep_at + rep_rows], name="gather_replicated_grads").reshape(-1)[:rep_len]
    at = 0
    for n in REPLICATED:
        local_grads[n] = rep_all[at:at + local[n].size].reshape(local[n].shape)
        at += local[n].size

    deltas, new_m, new_v = {}, {}, {}
    for n in WEIGHTS:
        deltas[n], new_m[n], new_v[n] = adamw(local[n], local_grads[n], given["m_" + n], given["v_" + n],
                                              name=f"adamw_{n}")
    return (loss, grad_x, *[local_grads[n] for n in WEIGHTS], *[deltas[n] for n in WEIGHTS],
            *[new_m[n] for n in WEIGHTS], *[new_v[n] for n in WEIGHTS])
```

```python
import functools
import math
from typing import Callable, NamedTuple

import jax
import jax.numpy as jnp
from jax import lax
from jax.experimental import pallas as pl
from jax.experimental.pallas import tpu as pltpu

F32 = jnp.float32
BF16 = jnp.bfloat16

EPS = 1e-6
HEAD_DIM = 64
RNN_HEADS = 16
LRU_C = 8.0
ADAM_LR = 0.001
ADAM_B1 = 0.9
ADAM_B2 = 0.999
ADAM_EPS = 1e-08
ADAM_WD = 0.01
ADAM_STEP = 10

N_DEV = 8
LANES = 128
SUBLANES = 8
VMEM_LIMIT = 56 * 1024 * 1024
MESH = pl.DeviceIdType.MESH
GRAD_ROWS_TILE = 2048
GELU_C = math.sqrt(2.0 / math.pi)
GELU_A = 0.044715

WEIGHTS = ['norm_mix', 'attn_w_qkv', 'attn_w_o', 'rnn_w_in', 'rnn_conv_w', 'rnn_conv_b', 'rnn_w_gate_a',
           'rnn_b_gate_a', 'rnn_w_gate_x', 'rnn_b_gate_x', 'rnn_lru_param', 'rnn_w_out', 'norm_ffn', 'ffn_w_up',
           'ffn_conv_w', 'ffn_conv_b', 'ffn_w_down', 'norm_ple', 'ple_w_gate', 'ple_w_proj', 'norm_final']
SHARD_AXIS = {'attn_w_qkv': 2, 'attn_w_o': 1, 'rnn_w_in': 2, 'rnn_conv_w': 2, 'rnn_conv_b': 1, 'rnn_b_gate_a': 1,
              'rnn_b_gate_x': 1, 'rnn_lru_param': 1, 'rnn_w_out': 1, 'ffn_w_up': 2, 'ffn_conv_w': 2,
              'ffn_w_down': 1, 'ple_w_gate': 1, 'ple_w_proj': 2}
MATMUL_WEIGHTS = ['attn_w_qkv', 'attn_w_o', 'rnn_w_in', 'rnn_w_out', 'ffn_w_up', 'ffn_w_down', 'ple_w_gate',
                  'ple_w_proj']
CHANNEL_WEIGHTS = ['rnn_conv_w', 'rnn_conv_b', 'rnn_b_gate_a', 'rnn_b_gate_x', 'rnn_lru_param', 'ffn_conv_w']
REPLICATED = [n for n in WEIGHTS if n not in SHARD_AXIS]


def _params(*sem):
    return pltpu.CompilerParams(dimension_semantics=sem, vmem_limit_bytes=VMEM_LIMIT)


def _tile(dim, pref, align=LANES):
    if dim <= pref:
        return dim
    t = (pref + pref // 2) // align * align
    while t >= align:
        if dim % t == 0:
            return t
        t -= align
    return dim


def _gelu(x):
    return 0.5 * x * (1.0 + jnp.tanh(GELU_C * (x + GELU_A * x * x * x)))


def _gelu_and_grad(x):
    t = jnp.tanh(GELU_C * (x + GELU_A * x * x * x))
    g = 0.5 * x * (1.0 + t)
    dg = 0.5 * (1.0 + t) + 0.5 * x * (1.0 - t * t) * GELU_C * (1.0 + 3.0 * GELU_A * x * x)
    return g, dg


def _log_sigmoid(x):
    return jnp.minimum(x, 0.0) - jnp.log(1.0 + jnp.exp(-jnp.abs(x)))


MM_VMEM_BUDGET = 36 * 1024 * 1024


def _mm_tiles(m, n, k, ta, a_item, b_item, out_item, n_extra):
    if ta:
        return _tile(m, 1024), _tile(n, 1024), _tile(k, 1024)
    row_bytes = k * a_item + n * (out_item + 4 * n_extra)
    w_bytes = k * n * b_item
    for tm in (1024, 512, 256, 128):
        if m % tm == 0 and 2 * tm * row_bytes + 2 * w_bytes + tm * n * 4 <= MM_VMEM_BUDGET:
            return tm, n, k
    return _tile(m, 512), _tile(n, 512), _tile(k, 1024)


def mm(a, b, *, name, ta=False, tb=False, out_dtype=F32, extras=(), epilogue=None):
    m, k = (a.shape[1], a.shape[0]) if ta else a.shape
    n = b.shape[0] if tb else b.shape[1]
    assert k == (b.shape[1] if tb else b.shape[0]), (a.shape, b.shape, ta, tb)
    tm, tn, tk = _mm_tiles(m, n, k, ta, a.dtype.itemsize, b.dtype.itemsize, jnp.dtype(out_dtype).itemsize,
                           len(extras))
    nk = k // tk
    n_extra = len(extras)
    dims = (((0 if ta else 1,), (1 if tb else 0,)), ((), ()))

    def body(a_ref, b_ref, *rest):
        extra_refs, o_ref = rest[:n_extra], rest[n_extra]

        def finish(acc):
            if epilogue is not None:
                acc = epilogue(acc, *[e[...] for e in extra_refs])
            o_ref[...] = acc.astype(o_ref.dtype)

        part = lax.dot_general(a_ref[...].astype(BF16), b_ref[...].astype(BF16), dims,
                               preferred_element_type=F32)
        if nk == 1:
            finish(part)
        else:
            acc_ref = rest[n_extra + 1]
            kk = pl.program_id(2)

            @pl.when(kk == 0)
            def _():
                acc_ref[...] = part

            @pl.when(kk > 0)
            def _():
                acc_ref[...] += part

            @pl.when(kk == nk - 1)
            def _():
                finish(acc_ref[...])

    a_spec = pl.BlockSpec((tk, tm), lambda i, j, kk: (kk, i)) if ta else pl.BlockSpec((tm, tk), lambda i, j, kk: (i, kk))
    b_spec = pl.BlockSpec((tn, tk), lambda i, j, kk: (j, kk)) if tb else pl.BlockSpec((tk, tn), lambda i, j, kk: (kk, j))
    o_spec = pl.BlockSpec((tm, tn), lambda i, j, kk: (i, j))
    return pl.pallas_call(
        body, name=name, grid=(m // tm, n // tn, nk),
        in_specs=[a_spec, b_spec] + [o_spec] * n_extra, out_specs=o_spec,
        out_shape=jax.ShapeDtypeStruct((m, n), out_dtype),
        scratch_shapes=[pltpu.VMEM((tm, tn), F32)] if nk > 1 else [],
        compiler_params=_params("parallel", "parallel", "arbitrary"),
    )(a, b, *extras)


def _add(acc, res):
    return acc + res


def rms_fwd(x, g, *, name):
    t, d = x.shape
    tr = _tile(t, 512, SUBLANES)

    def body(x_ref, g_ref, o_ref):
        xv = x_ref[...]
        r = lax.rsqrt(jnp.mean(xv * xv, axis=-1, keepdims=True) + EPS)
        o_ref[...] = (xv * r * g_ref[...]).astype(o_ref.dtype)

    row = pl.BlockSpec((tr, d), lambda i: (i, 0))
    return pl.pallas_call(
        body, name=name, grid=(t // tr,), in_specs=[row, pl.BlockSpec((1, d), lambda i: (0, 0))], out_specs=row,
        out_shape=jax.ShapeDtypeStruct((t, d), BF16), compiler_params=_params("parallel"),
    )(x, g.reshape(1, d))


def mm_rms_bwd(a, w, x, g, dres, *, name, prev=None):
    t, k = a.shape
    d = w.shape[0]
    assert w.shape[1] == k and x.shape == (t, d)
    n_rows = 4 if prev is not None else 3
    row_bytes = k * a.dtype.itemsize + d * 4 * n_rows
    tr = next(tm for tm in (512, 256, 128, t)
              if t % tm == 0 and 2 * tm * row_bytes + 2 * w.size * w.dtype.itemsize + tm * d * 4 <= MM_VMEM_BUDGET)

    def body(a_ref, w_ref, x_ref, g_ref, dres_ref, *rest):
        dx_ref, dg_ref = rest[-2:]
        dhv = lax.dot_general(a_ref[...].astype(BF16), w_ref[...].astype(BF16), _NT, preferred_element_type=F32)
        if prev is not None:
            dhv = dhv + rest[0][...]
        xv = x_ref[...]
        r = lax.rsqrt(jnp.mean(xv * xv, axis=-1, keepdims=True) + EPS)
        xh = xv * r
        u = dhv * g_ref[...]
        dx_ref[...] = dres_ref[...] + r * (u - xh * jnp.mean(u * xh, axis=-1, keepdims=True))
        part = jnp.sum(dhv * xh, axis=0, keepdims=True)

        @pl.when(pl.program_id(0) == 0)
        def _():
            dg_ref[...] = part

        @pl.when(pl.program_id(0) > 0)
        def _():
            dg_ref[...] += part

    row = pl.BlockSpec((tr, d), lambda i: (i, 0))
    vec = pl.BlockSpec((1, d), lambda i: (0, 0))
    extra = [prev] if prev is not None else []
    dx, dg = pl.pallas_call(
        body, name=name, grid=(t // tr,),
        in_specs=[pl.BlockSpec((tr, k), lambda i: (i, 0)), pl.BlockSpec((d, k), lambda i: (0, 0)), row, vec, row]
        + [row] * len(extra),
        out_specs=[row, vec],
        out_shape=[jax.ShapeDtypeStruct((t, d), F32), jax.ShapeDtypeStruct((1, d), F32)],
        compiler_params=_params("arbitrary"),
    )(a, w, x, g.reshape(1, d), dres, *extra)
    return dx, dg.reshape(d)


def final_loss(x, g, target, *, name):
    t, d = x.shape
    tr = _tile(t, 512, SUBLANES)

    def body(x_ref, g_ref, t_ref, dx_ref, dg_ref, loss_ref):
        xv = x_ref[...]
        gv = g_ref[...]
        r = lax.rsqrt(jnp.mean(xv * xv, axis=-1, keepdims=True) + EPS)
        xh = xv * r
        err = xh * gv - t_ref[...]
        dy = err * (1.0 / d)
        u = dy * gv
        dx_ref[...] = r * (u - xh * jnp.mean(u * xh, axis=-1, keepdims=True))
        dg_part = jnp.sum(dy * xh, axis=0, keepdims=True)
        loss_part = jnp.zeros((1, LANES), F32) + (0.5 / d) * jnp.sum(err * err)

        @pl.when(pl.program_id(0) == 0)
        def _():
            dg_ref[...] = dg_part
            loss_ref[...] = loss_part

        @pl.when(pl.program_id(0) > 0)
        def _():
            dg_ref[...] += dg_part
            loss_ref[...] += loss_part

    row = pl.BlockSpec((tr, d), lambda i: (i, 0))
    vec = pl.BlockSpec((1, d), lambda i: (0, 0))
    dx, dg, loss = pl.pallas_call(
        body, name=name, grid=(t // tr,), in_specs=[row, vec, row],
        out_specs=[row, vec, pl.BlockSpec((1, LANES), lambda i: (0, 0))],
        out_shape=[jax.ShapeDtypeStruct((t, d), F32), jax.ShapeDtypeStruct((1, d), F32),
                   jax.ShapeDtypeStruct((1, LANES), F32)],
        compiler_params=_params("arbitrary"),
    )(x, g.reshape(1, d), target)
    return dx, dg.reshape(d), loss[0, 0]


def _split_dot(x, mat, left):
    hi = x.astype(BF16)
    lo = (x - hi.astype(F32)).astype(BF16)
    if left:
        return (jnp.dot(mat, hi, preferred_element_type=F32) + jnp.dot(mat, lo, preferred_element_type=F32))
    return (jnp.dot(hi, mat, preferred_element_type=F32) + jnp.dot(lo, mat, preferred_element_type=F32))


_NT = (((1,), (1,)), ((), ()))
_TN = (((0,), (0,)), ((), ()))
HEADS_PER_STEP = LANES // HEAD_DIM


def attn_fwd(qkv, b, s, d, *, name, rider=None):
    t = b * s
    tq = min(256, s)
    nq = s // tq
    pairs = d // LANES
    scale = HEAD_DIM ** -0.5

    grid = (b, pairs, nq)
    n_ride = len(rider.inputs) if rider else 0

    def body(*refs):
        q_ref, k_ref, v_ref = refs[:3]
        o_ref, lt_ref = refs[3 + n_ride:5 + n_ride]
        _ride(rider, refs, 3, 2, grid)
        i = pl.program_id(2)
        row = lax.broadcasted_iota(jnp.int32, (tq, tq), 0)
        col = lax.broadcasted_iota(jnp.int32, (tq, tq), 1)
        later = (row > col).astype(BF16)
        causal = col < row
        lanes = [slice(HEAD_DIM * h, HEAD_DIM * (h + 1)) for h in range(HEADS_PER_STEP)]
        qs = [(q_ref[:, sl].astype(F32) * scale).astype(BF16) for sl in lanes]

        def block(js, carry, diag):
            starts = [pl.multiple_of(j * tq, tq) for j in js]
            hs = range(HEADS_PER_STEP)
            chains = [(n, h) for n in range(len(js)) for h in hs]
            kbs = {(n, h): k_ref[pl.ds(starts[n], tq), lanes[h]] for n, h in chains}
            vbs = {(n, h): v_ref[pl.ds(starts[n], tq), lanes[h]] for n, h in chains}
            zs = {c: lax.dot_general(qs[c[1]], kbs[c], _NT, preferred_element_type=F32) for c in chains}
            lss = {c: _log_sigmoid(zs[c]) for c in chains}
            lks = {c: lss[c] - zs[c] for c in chains}
            if diag:
                lks = {c: jnp.where(causal, lks[c], 0.0) for c in chains}
            sums = {c: _split_dot(lks[c], later, left=False) for c in chains}
            runs_in, runs = {}, []
            for h in hs:
                run = carry[h][0]
                for n in range(len(js)):
                    runs_in[n, h] = run
                    run = run + jnp.sum(lks[n, h], axis=1, keepdims=True)
                runs.append(run)
            ws = {c: jnp.exp(lss[c] + sums[c] + runs_in[c]) for c in chains}
            if diag:
                ws = {c: jnp.where(causal, ws[c], 0.0) for c in chains}
            pvs = {c: jnp.dot(ws[c].astype(BF16), vbs[c], preferred_element_type=F32) for c in chains}
            accs = [carry[h][1] + sum(pvs[n, h] for n in range(len(js))) for h in hs]
            return tuple(zip(runs, accs))

        zero = (jnp.zeros((tq, 1), F32), jnp.zeros((tq, HEAD_DIM), F32))
        carry = block([i], (zero,) * HEADS_PER_STEP, True)
        odd = i % 2
        carry = lax.cond(odd == 1, lambda c: block([i - 1], c, False), lambda c: c, carry)
        near = i - 1 - odd
        carry = lax.fori_loop(0, i // 2, lambda n, c: block([near - 2 * n, near - 2 * n - 1], c, False), carry)
        eye = (row == col).astype(F32)
        for h, sl in enumerate(lanes):
            run, acc = carry[h]
            o_ref[:, sl] = acc.astype(o_ref.dtype)
            lt_ref[SUBLANES * h:SUBLANES * (h + 1), :] = lax.dot_general(
                jnp.broadcast_to(run, (tq, SUBLANES)), eye, _TN, precision=lax.Precision.HIGHEST,
                preferred_element_type=F32)

    q_spec = pl.BlockSpec((tq, LANES), lambda bb, p, i: (bb * nq + i, p))
    k_spec = pl.BlockSpec((s, LANES), lambda bb, p, i: (bb, pairs + p))
    v_spec = pl.BlockSpec((s, LANES), lambda bb, p, i: (bb, 2 * pairs + p))
    lt_spec = pl.BlockSpec((None, None, None, HEADS_PER_STEP * SUBLANES, tq), lambda bb, p, i: (bb, p, i, 0, 0))
    in_specs, out_specs, out_shape, scratch = _rider_specs(
        rider, [q_spec, k_spec, v_spec], [q_spec, lt_spec],
        [jax.ShapeDtypeStruct((t, d), BF16),
         jax.ShapeDtypeStruct((b, pairs, nq, HEADS_PER_STEP * SUBLANES, tq), F32)])
    order = ("arbitrary",) * 3 if rider else ("parallel", "parallel", "arbitrary")
    return pl.pallas_call(
        body, name=name, grid=grid, in_specs=in_specs, out_specs=out_specs, out_shape=out_shape,
        scratch_shapes=scratch, compiler_params=_params(*order),
    )(qkv, qkv, qkv, *(rider.inputs if rider else ()))


def attn_bwd(qkv, totals, do, b, s, d, *, name, rider=None):
    t = b * s
    tq = min(256, s)
    nq = s // tq
    pairs = d // LANES
    scale = HEAD_DIM ** -0.5

    grid = (b, pairs, nq)
    n_ride = len(rider.inputs) if rider else 0

    def body(*refs):
        q_ref, k_ref, v_ref, lt_ref, do_ref = refs[:5]
        dq_ref, dk_ref, dv_ref = refs[5 + n_ride:8 + n_ride]
        _ride(rider, refs, 5, 3, grid)
        i = pl.program_id(2)

        @pl.when(i == 0)
        def _():
            dk_ref[...] = jnp.zeros_like(dk_ref)
            dv_ref[...] = jnp.zeros_like(dv_ref)

        row = lax.broadcasted_iota(jnp.int32, (tq, tq), 0)
        col = lax.broadcasted_iota(jnp.int32, (tq, tq), 1)
        upto = (col <= row).astype(BF16)
        earlier = (col < row).astype(BF16)
        causal = row < col
        lanes = [slice(HEAD_DIM * h, HEAD_DIM * (h + 1)) for h in range(HEADS_PER_STEP)]
        qs = [(q_ref[:, sl].astype(F32) * scale).astype(BF16) for sl in lanes]
        dos = [do_ref[:, sl].astype(BF16) for sl in lanes]
        totals_h = [lt_ref[SUBLANES * h:SUBLANES * h + 1, :] for h in range(HEADS_PER_STEP)]

        def block(js, carry, diag):
            starts = [pl.multiple_of(j * tq, tq) for j in js]
            hs = range(HEADS_PER_STEP)
            ns = range(len(js))
            chains = [(n, h) for n in ns for h in hs]
            kbs = {(n, h): k_ref[pl.ds(starts[n], tq), lanes[h]] for n, h in chains}
            vbs = {(n, h): v_ref[pl.ds(starts[n], tq), lanes[h]] for n, h in chains}
            zs = {c: lax.dot_general(kbs[c], qs[c[1]], _NT, preferred_element_type=F32) for c in chains}
            dws = {c: lax.dot_general(vbs[c], dos[c[1]], _NT, preferred_element_type=F32) for c in chains}
            lss = {c: _log_sigmoid(zs[c]) for c in chains}
            lks = {c: lss[c] - zs[c] for c in chains}
            if diag:
                lks = {c: jnp.where(causal, lks[c], 0.0) for c in chains}
            sums = {c: _split_dot(lks[c], upto, left=True) for c in chains}
            runs_in, runs = {}, []
            for h in hs:
                run = carry[h][0]
                for n in ns:
                    runs_in[n, h] = run
                    run = run + jnp.sum(lks[n, h], axis=0, keepdims=True)
                runs.append(run)
            ws = {c: jnp.exp(lss[c] + ((totals_h[c[1]] - runs_in[c]) - sums[c])) for c in chains}
            if diag:
                ws = {c: jnp.where(causal, ws[c], 0.0) for c in chains}
            gs = {c: dws[c] * ws[c] for c in chains}
            gsums = {c: _split_dot(gs[c], earlier, left=True) for c in chains}
            gruns_in, gruns = {}, []
            for h in hs:
                grun = carry[h][1]
                for n in ns:
                    gruns_in[n, h] = grun
                    grun = grun + jnp.sum(gs[n, h], axis=0, keepdims=True)
                gruns.append(grun)
            dzs = {c: gs[c] - jnp.exp(lss[c]) * (gs[c] + (gruns_in[c] + gsums[c])) for c in chains}
            if diag:
                dzs = {c: jnp.where(causal, dzs[c], 0.0) for c in chains}
            dzbs = {c: dzs[c].astype(BF16) for c in chains}
            for n, h in chains:
                dv_ref[pl.ds(starts[n], tq), lanes[h]] += jnp.dot(ws[n, h].astype(BF16), dos[h],
                                                                  preferred_element_type=F32)
                dk_ref[pl.ds(starts[n], tq), lanes[h]] += jnp.dot(dzbs[n, h], qs[h], preferred_element_type=F32)
            dqs = [carry[h][2] + sum(lax.dot_general(dzbs[n, h], kbs[n, h], _TN, preferred_element_type=F32)
                                     for n in ns) for h in hs]
            return tuple(zip(runs, gruns, dqs))

        zero = (jnp.zeros((1, tq), F32), jnp.zeros((1, tq), F32), jnp.zeros((tq, HEAD_DIM), F32))
        carry = lax.fori_loop(0, i // 2, lambda n, c: block([2 * n, 2 * n + 1], c, False),
                              (zero,) * HEADS_PER_STEP)
        carry = lax.cond(i % 2 == 1, lambda c: block([i - 1], c, False), lambda c: c, carry)
        carry = block([i], carry, True)
        for h, sl in enumerate(lanes):
            dq_ref[:, sl] = carry[h][2] * scale

    q_spec = pl.BlockSpec((tq, LANES), lambda bb, p, i: (bb * nq + i, p))
    k_spec = pl.BlockSpec((s, LANES), lambda bb, p, i: (bb, pairs + p))
    v_spec = pl.BlockSpec((s, LANES), lambda bb, p, i: (bb, 2 * pairs + p))
    lt_spec = pl.BlockSpec((None, None, None) + totals.shape[3:], lambda bb, p, i: (bb, p, i, 0, 0))
    kv_out = pl.BlockSpec((s, LANES), lambda bb, p, i: (bb, p))
    out = jax.ShapeDtypeStruct((t, d), F32)
    in_specs, out_specs, out_shape, scratch = _rider_specs(
        rider, [q_spec, k_spec, v_spec, lt_spec, q_spec], [q_spec, kv_out, kv_out], [out, out, out])
    order = ("arbitrary",) * 3 if rider else ("parallel", "parallel", "arbitrary")
    return pl.pallas_call(
        body, name=name, grid=grid, in_specs=in_specs, out_specs=out_specs, out_shape=out_shape,
        scratch_shapes=scratch, compiler_params=_params(*order),
    )(qkv, qkv, qkv, totals, do, *(rider.inputs if rider else ()))


def _shift_down(cur, prev8, dist):
    ext = jnp.concatenate([prev8, cur], axis=0)
    return pltpu.roll(ext, dist, 0)[SUBLANES:]


def _shift_up(cur, next8, dist):
    ext = jnp.concatenate([cur, next8], axis=0)
    return pltpu.roll(ext, ext.shape[0] - dist, 0)[:cur.shape[0]]


def _causal_conv(cur, prev8, w_ref, b_ref):
    taps = w_ref.shape[0]
    out = cur * w_ref[taps - 1:taps, :] + b_ref[...]
    for dist in range(1, taps):
        out = out + _shift_down(cur, prev8, dist) * w_ref[taps - 1 - dist:taps - dist, :]
    return out


def _conv_specs(t, rows, tc, time_axis, dtype=F32):
    sub = SUBLANES * (4 // jnp.dtype(dtype).itemsize)
    per = rows // sub
    last = t // sub - 1

    def grid_ids(*ids):
        return ids[time_axis], ids[1 - time_axis]

    def cur(*ids):
        return grid_ids(*ids)

    def prev(*ids):
        i, j = grid_ids(*ids)
        return (jnp.maximum(i * per - 1, 0), j)

    def nxt(*ids):
        i, j = grid_ids(*ids)
        return (jnp.minimum((i + 1) * per, last), j)

    def chan(*ids):
        return (0, grid_ids(*ids)[1])

    return pl.BlockSpec((rows, tc), cur), pl.BlockSpec((sub, tc), prev), pl.BlockSpec((sub, tc), nxt), chan


def _rows_before(ref, keep):
    return ref[...].astype(F32)[-SUBLANES:] * keep


def _rows_after(ref, keep):
    return ref[...].astype(F32)[:SUBLANES] * keep


def _first_in_seq(i, rows, s):
    return (i % (s // rows)) == 0


def _last_in_seq(i, rows, s):
    return (i % (s // rows)) == (s // rows - 1)


def ffn_act_fwd(ug, uv, cwg, cwv, cbg, cbv, s, *, name):
    t, f = ug.shape
    rows, tc = _tile(s, 512, SUBLANES), _tile(f, 256)
    cur, prev, _, chan = _conv_specs(t, rows, tc, 0, ug.dtype)
    taps = cwg.shape[0]

    def body(ug_ref, ugp_ref, uv_ref, uvp_ref, cwg_ref, cwv_ref, cbg_ref, cbv_ref, a_ref):
        keep = jnp.where(_first_in_seq(pl.program_id(0), rows, s), 0.0, 1.0)
        gate = _causal_conv(ug_ref[...].astype(F32), _rows_before(ugp_ref, keep), cwg_ref, cbg_ref)
        val = _causal_conv(uv_ref[...].astype(F32), _rows_before(uvp_ref, keep), cwv_ref, cbv_ref)
        a_ref[...] = (_gelu(gate) * val).astype(a_ref.dtype)

    wspec = pl.BlockSpec((taps, tc), chan)
    bspec = pl.BlockSpec((1, tc), chan)
    return pl.pallas_call(
        body, name=name, grid=(t // rows, f // tc), in_specs=[cur, prev, cur, prev, wspec, wspec, bspec, bspec],
        out_specs=cur, out_shape=jax.ShapeDtypeStruct((t, f), BF16), compiler_params=_params("parallel", "parallel"),
    )(ug, ug, uv, uv, cwg, cwv, cbg.reshape(1, f), cbv.reshape(1, f))


def _accumulate_rows(first, ref, rows):
    for k, r in enumerate(rows):
        @pl.when(first)
        def _(k=k, r=r):
            ref[k:k + 1, :] = r

        @pl.when(jnp.logical_not(first))
        def _(k=k, r=r):
            ref[k:k + 1, :] += r


def _conv_weight_grads(dc, cur, prev8, taps):
    out = []
    for k in range(taps):
        dist = taps - 1 - k
        xs = cur if dist == 0 else _shift_down(cur, prev8, dist)
        out.append(jnp.sum(dc * xs, axis=0, keepdims=True))
    out.append(jnp.sum(dc, axis=0, keepdims=True))
    return out


def _conv_transpose(dc_ext, rows, w_ref):
    taps = w_ref.shape[0]
    out = dc_ext[:rows] * w_ref[taps - 1:taps, :]
    for dist in range(1, taps):
        out = out + pltpu.roll(dc_ext, dc_ext.shape[0] - dist, 0)[:rows] * w_ref[taps - 1 - dist:taps - dist, :]
    return out


def ffn_act_bwd(ug, uv, cwg, cwv, cbg, cbv, da, s, *, name, rider=None):
    t, f = ug.shape
    rows, tc = _tile(s, 512, SUBLANES), _tile(f, 256)
    cur, prev, nxt, chan = _conv_specs(t, rows, tc, 1, ug.dtype)
    taps = cwg.shape[0]

    grid = (f // tc, t // rows)
    n_ride = len(rider.inputs) if rider else 0

    def body(*refs):
        (ug_ref, ugp_ref, ugn_ref, uv_ref, uvp_ref, uvn_ref, cwg_ref, cwv_ref, cbg_ref, cbv_ref,
         da_ref, dan_ref) = refs[:12]
        dug_ref, duv_ref, wg_ref, wv_ref = refs[12 + n_ride:16 + n_ride]
        _ride(rider, refs, 12, 4, grid)
        i = pl.program_id(1)
        keep_before = jnp.where(_first_in_seq(i, rows, s), 0.0, 1.0)
        keep_after = jnp.where(_last_in_seq(i, rows, s), 0.0, 1.0)
        ugp, uvp = _rows_before(ugp_ref, keep_before), _rows_before(uvp_ref, keep_before)
        uge = jnp.concatenate([ug_ref[...].astype(F32), _rows_after(ugn_ref, 1.0)], axis=0)
        uve = jnp.concatenate([uv_ref[...].astype(F32), _rows_after(uvn_ref, 1.0)], axis=0)
        dae = jnp.concatenate([da_ref[...].astype(F32), _rows_after(dan_ref, keep_after)], axis=0)
        gate = _causal_conv(uge, ugp, cwg_ref, cbg_ref)
        val = _causal_conv(uve, uvp, cwv_ref, cbv_ref)
        act, dact = _gelu_and_grad(gate)
        dgate = dae * val * dact
        dval = dae * act
        dug_ref[...] = _conv_transpose(dgate, rows, cwg_ref).astype(dug_ref.dtype)
        duv_ref[...] = _conv_transpose(dval, rows, cwv_ref).astype(duv_ref.dtype)
        _accumulate_rows(i == 0, wg_ref, _conv_weight_grads(dgate[:rows], uge[:rows], ugp, taps))
        _accumulate_rows(i == 0, wv_ref, _conv_weight_grads(dval[:rows], uve[:rows], uvp, taps))

    wspec = pl.BlockSpec((taps, tc), chan)
    bspec = pl.BlockSpec((1, tc), chan)
    gspec = pl.BlockSpec((taps + 1, tc), chan)
    act_shape = jax.ShapeDtypeStruct((t, f), BF16)
    stat_shape = jax.ShapeDtypeStruct((taps + 1, f), F32)
    in_specs, out_specs, out_shape, scratch = _rider_specs(
        rider, [cur, prev, nxt, cur, prev, nxt, wspec, wspec, bspec, bspec, cur, nxt], [cur, cur, gspec, gspec],
        [act_shape, act_shape, stat_shape, stat_shape])
    order = ("arbitrary",) * 2 if rider else ("parallel", "arbitrary")
    return pl.pallas_call(
        body, name=name, grid=grid, in_specs=in_specs, out_specs=out_specs, out_shape=out_shape,
        scratch_shapes=scratch, compiler_params=_params(*order),
    )(ug, ug, ug, uv, uv, uv, cwg, cwv, cbg.reshape(1, f), cbv.reshape(1, f), da, da,
      *(rider.inputs if rider else ()))


def conv_input_grad(dc, cw, s, *, name, out_dtype):
    t, f = dc.shape
    rows, tc = _tile(s, 512, SUBLANES), _tile(f, 256)
    cur, _, nxt, chan = _conv_specs(t, rows, tc, 0, dc.dtype)
    taps = cw.shape[0]

    def body(dc_ref, dcn_ref, cw_ref, o_ref):
        keep = jnp.where(_last_in_seq(pl.program_id(0), rows, s), 0.0, 1.0)
        dcc = dc_ref[...].astype(F32)
        dcn = _rows_after(dcn_ref, keep)
        out = dcc * cw_ref[taps - 1:taps, :]
        for dist in range(1, taps):
            out = out + _shift_up(dcc, dcn, dist) * cw_ref[taps - 1 - dist:taps - dist, :]
        o_ref[...] = out.astype(o_ref.dtype)

    return pl.pallas_call(
        body, name=name, grid=(t // rows, f // tc), in_specs=[cur, nxt, pl.BlockSpec((taps, tc), chan)],
        out_specs=cur, out_shape=jax.ShapeDtypeStruct((t, f), out_dtype),
        compiler_params=_params("parallel", "parallel"),
    )(dc, dc, cw)


def rnn_conv_fwd(yr, cw, cb, s, *, name):
    t, w = yr.shape
    rows, tc = _tile(s, 512, SUBLANES), _tile(w, 256)
    cur, prev, _, chan = _conv_specs(t, rows, tc, 0, yr.dtype)
    taps = cw.shape[0]

    def body(y_ref, yp_ref, cw_ref, cb_ref, o_ref):
        keep = jnp.where(_first_in_seq(pl.program_id(0), rows, s), 0.0, 1.0)
        o_ref[...] = _causal_conv(y_ref[...].astype(F32), _rows_before(yp_ref, keep), cw_ref, cb_ref)

    return pl.pallas_call(
        body, name=name, grid=(t // rows, w // tc),
        in_specs=[cur, prev, pl.BlockSpec((taps, tc), chan), pl.BlockSpec((1, tc), chan)], out_specs=cur,
        out_shape=jax.ShapeDtypeStruct((t, w), F32), compiler_params=_params("parallel", "parallel"),
    )(yr, yr, cw, cb.reshape(1, w))


def rnn_conv_wgrad(dxr, yr, s, taps, *, name):
    t, w = yr.shape
    rows, tc = _tile(s, 512, SUBLANES), _tile(w, 256)
    cur, prev, _, chan = _conv_specs(t, rows, tc, 1, yr.dtype)

    def body(d_ref, y_ref, yp_ref, o_ref):
        i = pl.program_id(1)
        keep = jnp.where(_first_in_seq(i, rows, s), 0.0, 1.0)
        grads = _conv_weight_grads(d_ref[...], y_ref[...].astype(F32), _rows_before(yp_ref, keep), taps)
        _accumulate_rows(i == 0, o_ref, grads)

    return pl.pallas_call(
        body, name=name, grid=(w // tc, t // rows), in_specs=[cur, cur, prev],
        out_specs=pl.BlockSpec((taps + 1, tc), chan), out_shape=jax.ShapeDtypeStruct((taps + 1, w), F32),
        compiler_params=_params("parallel", "arbitrary"),
    )(dxr, yr, yr)


SCAN_ROWS = 32


def _one_minus_exp(x):
    series = -x * (1.0 + x * (0.5 + x * (1.0 / 6.0)))
    return jnp.where(x > -0.01, series, 1.0 - jnp.exp(x))


def _gates(ga, gi, ba, bx, log_lam):
    ra = jax.nn.sigmoid(ga + ba)
    ri = jax.nn.sigmoid(gi + bx)
    log_a = LRU_C * ra * log_lam
    a = jnp.exp(log_a)
    mult = jnp.sqrt(_one_minus_exp(2.0 * log_a))
    return ra, ri, a, mult


def rnn_scan_fwd(ga, gi, xr, yg, ba, bx, lam, b, s, *, name):
    t, w = xr.shape
    tc = _tile(w, 256)
    rb = min(SCAN_ROWS, s)
    blocks = s // rb
    steps = [1 << e for e in range(rb.bit_length() - 1)]

    def body(ga_ref, gi_ref, xr_ref, yg_ref, ba_ref, bx_ref, lam_ref, h_ref, y_ref):
        log_lam = _log_sigmoid(lam_ref[...])
        ridx = lax.broadcasted_iota(jnp.int32, (rb, tc), 0)

        def step(n, carry):
            rs = pl.ds(pl.multiple_of(n * rb, rb), rb)
            xrv = xr_ref[rs, :]
            _, ri, a, mult = _gates(ga_ref[rs, :], gi_ref[rs, :], ba_ref[...], bx_ref[...], log_lam)
            u = mult * (ri * xrv)
            for dist in steps:
                a_sh = jnp.where(ridx >= dist, pltpu.roll(a, dist, 0), 1.0)
                u_sh = jnp.where(ridx >= dist, pltpu.roll(u, dist, 0), 0.0)
                u = a * u_sh + u
                a = a * a_sh
            hb = u + a * carry
            h_ref[rs, :] = hb
            y_ref[rs, :] = (_gelu(yg_ref[rs, :]) * hb).astype(y_ref.dtype)
            return hb[rb - 1:rb, :]

        lax.fori_loop(0, blocks, step, jnp.zeros((1, tc), F32))

    seq = pl.BlockSpec((s, tc), lambda bb, j: (bb, j))
    vec = pl.BlockSpec((1, tc), lambda bb, j: (0, j))
    return pl.pallas_call(
        body, name=name, grid=(b, w // tc), in_specs=[seq, seq, seq, seq, vec, vec, vec], out_specs=[seq, seq],
        out_shape=[jax.ShapeDtypeStruct((t, w), F32), jax.ShapeDtypeStruct((t, w), BF16)],
        compiler_params=_params("parallel", "parallel"),
    )(ga, gi, xr, yg, ba.reshape(1, w), bx.reshape(1, w), lam.reshape(1, w))


def rnn_scan_bwd(dy, ga, gi, xr, yg, h, ba, bx, lam, b, s, *, name):
    t, w = xr.shape
    tc = _tile(w, 256)
    rb = min(SCAN_ROWS, s)
    blocks = s // rb
    steps = [1 << e for e in range(rb.bit_length() - 1)]

    def body(dy_ref, ga_ref, gi_ref, xr_ref, yg_ref, h_ref, ba_ref, bx_ref, lam_ref,
             dyg_ref, dga_ref, dgi_ref, dxr_ref, stat_ref):
        lamv = lam_ref[...]
        log_lam = _log_sigmoid(lamv)
        dlog_lam = jax.nn.sigmoid(-lamv)
        ridx = lax.broadcasted_iota(jnp.int32, (rb, tc), 0)
        last = rb - 1

        def step(n, carry):
            lam_next, a_next, s_a, s_x, s_l = carry
            blk = blocks - 1 - n
            rs = pl.ds(pl.multiple_of(blk * rb, rb), rb)
            rp = pl.ds(pl.multiple_of(jnp.maximum(blk * rb - SUBLANES, 0), SUBLANES), SUBLANES)
            xrv = xr_ref[rs, :]
            hv = h_ref[rs, :]
            h_before = jnp.where(blk > 0, h_ref[rp, :][SUBLANES - 1:, :], 0.0)
            h_prev = jnp.where(ridx >= 1, pltpu.roll(hv, 1, 0), h_before)
            ra, ri, a, mult = _gates(ga_ref[rs, :], gi_ref[rs, :], ba_ref[...], bx_ref[...], log_lam)
            act, dact = _gelu_and_grad(yg_ref[rs, :])
            dyv = dy_ref[rs, :]
            dyg_ref[rs, :] = (dyv * hv * dact).astype(dyg_ref.dtype)
            v = dyv * act
            c = jnp.where(ridx < last, pltpu.roll(a, last, 0), a_next)
            for dist in steps:
                c_sh = jnp.where(ridx < rb - dist, pltpu.roll(c, rb - dist, 0), 1.0)
                v_sh = jnp.where(ridx < rb - dist, pltpu.roll(v, rb - dist, 0), 0.0)
                v = v + c * v_sh
                c = c * c_sh
            dh = v + c * lam_next
            du_ri_x = dh * xrv
            dmult = du_ri_x * ri
            dri = du_ri_x * mult
            dxr_ref[rs, :] = dh * mult * ri
            dlog_a = dh * h_prev * a - dmult * (a * a) / mult
            dra = dlog_a * (LRU_C * log_lam)
            dpa = dra * ra * (1.0 - ra)
            dpi = dri * ri * (1.0 - ri)
            dga_ref[rs, :] = dpa.astype(dga_ref.dtype)
            dgi_ref[rs, :] = dpi.astype(dgi_ref.dtype)
            s_a = s_a + jnp.sum(dpa, axis=0, keepdims=True)
            s_x = s_x + jnp.sum(dpi, axis=0, keepdims=True)
            s_l = s_l + jnp.sum(dlog_a * ra, axis=0, keepdims=True)
            return dh[0:1, :], a[0:1, :], s_a, s_x, s_l

        zero = jnp.zeros((1, tc), F32)
        _, _, s_a, s_x, s_l = lax.fori_loop(0, blocks, step, (zero, zero, zero, zero, zero))
        _accumulate_rows(pl.program_id(1) == 0, stat_ref, [s_a, s_x, s_l * (LRU_C * dlog_lam)])

    seq = pl.BlockSpec((s, tc), lambda j, bb: (bb, j))
    vec = pl.BlockSpec((1, tc), lambda j, bb: (0, j))
    half = jax.ShapeDtypeStruct((t, w), BF16)
    return pl.pallas_call(
        body, name=name, grid=(w // tc, b), in_specs=[seq, seq, seq, seq, seq, seq, vec, vec, vec],
        out_specs=[seq, seq, seq, seq, pl.BlockSpec((3, tc), lambda j, bb: (0, j))],
        out_shape=[half, half, half, jax.ShapeDtypeStruct((t, w), F32), jax.ShapeDtypeStruct((3, w), F32)],
        compiler_params=_params("parallel", "arbitrary"),
    )(dy, ga, gi, xr, yg, h, ba.reshape(1, w), bx.reshape(1, w), lam.reshape(1, w))


def _ple_mix(acc, x, gate):
    return x + jax.nn.sigmoid(gate) * acc


def ple_bwd(dx, gate, pin, w_proj, *, name):
    t, d = dx.shape
    k = pin.shape[1]
    tr = _tile(t, 512, SUBLANES)

    def body(dx_ref, g_ref, p_ref, w_ref, dg_ref, de_ref):
        emb = jnp.dot(p_ref[...].astype(BF16), w_ref[...], preferred_element_type=F32)
        sg = jax.nn.sigmoid(g_ref[...])
        dxv = dx_ref[...]
        de_ref[...] = (dxv * sg).astype(de_ref.dtype)
        dg_ref[...] = (dxv * emb * sg * (1.0 - sg)).astype(dg_ref.dtype)

    row = pl.BlockSpec((tr, d), lambda i: (i, 0))
    half = jax.ShapeDtypeStruct((t, d), BF16)
    return pl.pallas_call(
        body, name=name, grid=(t // tr,),
        in_specs=[row, row, pl.BlockSpec((tr, k), lambda i: (i, 0)), pl.BlockSpec((k, d), lambda i: (0, 0))],
        out_specs=[row, row], out_shape=[half, half], compiler_params=_params("parallel"))(dx, gate, pin, w_proj)


def adamw(w, g, m, v, *, name):
    shape = w.shape
    cols = shape[-1]
    rows = w.size // cols
    tr = _tile(rows, 1024, SUBLANES)
    bc1 = 1.0 / (1.0 - ADAM_B1 ** ADAM_STEP)
    bc2 = 1.0 / (1.0 - ADAM_B2 ** ADAM_STEP)

    def body(w_ref, g_ref, m_ref, v_ref, d_ref, nm_ref, nv_ref):
        gv = g_ref[...]
        nm = ADAM_B1 * m_ref[...] + (1.0 - ADAM_B1) * gv
        nv = ADAM_B2 * v_ref[...] + (1.0 - ADAM_B2) * (gv * gv)
        d_ref[...] = -ADAM_LR * ((nm * bc1) / (jnp.sqrt(nv * bc2) + ADAM_EPS) + ADAM_WD * w_ref[...])
        nm_ref[...] = nm
        nv_ref[...] = nv

    blk = pl.BlockSpec((tr, cols), lambda i: (i, 0))
    out = jax.ShapeDtypeStruct((rows, cols), F32)
    res = pl.pallas_call(body, name=name, grid=(rows // tr,), in_specs=[blk] * 4, out_specs=[blk] * 3,
                         out_shape=[out] * 3, compiler_params=_params("parallel"),
                         )(*[a.reshape(rows, cols) for a in (w, g, m, v)])
    return [r.reshape(shape) for r in res]


ANY = pl.BlockSpec(memory_space=pl.ANY)


def _place():
    return lax.axis_index("x"), lax.axis_index("y"), lax.axis_index("c")


class Rider(NamedTuple):
    inputs: tuple
    out_shapes: tuple
    scratch: tuple
    emit: Callable


def _when(cond):
    return (lambda fn: fn()) if cond is True else pl.when(cond)


def _rider_specs(rider, in_specs, out_specs, out_shape):
    if rider is None:
        return in_specs, out_specs, out_shape, []
    return (in_specs + [ANY] * len(rider.inputs), out_specs + [ANY] * len(rider.out_shapes),
            out_shape + list(rider.out_shapes), list(rider.scratch))


def _ride(rider, refs, n_in, n_out, grid):
    if rider is None:
        return
    ids = [pl.program_id(a) for a in range(len(grid))]
    first = functools.reduce(jnp.logical_and, [i == 0 for i in ids])
    last = functools.reduce(jnp.logical_and, [i == n - 1 for i, n in zip(ids, grid)])
    middle = functools.reduce(jnp.logical_and, [ids[0] == grid[0] // 2] + [i == 0 for i in ids[1:]])
    r_in = refs[n_in:n_in + len(rider.inputs)]
    at = n_in + len(rider.inputs) + n_out
    r_out = refs[at:at + len(rider.out_shapes)]
    rider.emit(first, middle, last, r_in, r_out, refs[at + len(rider.out_shapes):])


def _alone(rider, *, name):
    n_in = len(rider.inputs)

    def body(*refs):
        rider.emit(True, True, True, refs[:n_in], refs[n_in:n_in + len(rider.out_shapes)],
                   refs[n_in + len(rider.out_shapes):])

    return pl.pallas_call(body, name=name, out_shape=list(rider.out_shapes), in_specs=[ANY] * n_in,
                          out_specs=[ANY] * len(rider.out_shapes), scratch_shapes=list(rider.scratch))(*rider.inputs)


def gather_rider(v):
    rows, cols = v.shape

    def emit(first, middle, last, ins, outs, sems):
        (v_ref,), (out_ref,), (send_sems, recv_sems, local_sem) = ins, outs, sems
        x, y, c = _place()
        me, sibling = (x, y, c), (x, y, 1 - c)
        chips = [(1 - x, y), (x, 1 - y), (1 - x, 1 - y)]

        def slot(px, py, pc):
            return out_ref.at[4 * px + 2 * py + pc]

        def copy(k, block, to, src=None):
            return pltpu.make_async_remote_copy(
                src_ref=slot(*block) if src is None else src, dst_ref=slot(*block),
                send_sem=send_sems.at[k], recv_sem=recv_sems.at[k], device_id=to, device_id_type=MESH)

        mine = pltpu.make_async_copy(v_ref, slot(*me), local_sem)
        own = [copy(0, me, sibling, src=v_ref)]
        own += [copy(1 + j, me, (*chip, c), src=v_ref) for j, chip in enumerate(chips)]
        passed = [copy(4 + j, (*chip, c), sibling) for j, chip in enumerate(chips)]

        @_when(first)
        def _():
            mine.start()
            for cp in own:
                cp.start()

        @_when(middle)
        def _():
            for j, chip in enumerate(chips):
                copy(1 + j, (*chip, c), me).wait_recv()
                passed[j].start()

        @_when(last)
        def _():
            copy(0, sibling, me).wait_recv()
            for j, chip in enumerate(chips):
                copy(4 + j, (*chip, 1 - c), me).wait_recv()
            for cp in own + passed:
                cp.wait_send()
            mine.wait()

    return Rider((v,), (jax.ShapeDtypeStruct((N_DEV, rows, cols), v.dtype),),
                 (pltpu.SemaphoreType.DMA((7,)), pltpu.SemaphoreType.DMA((7,)), pltpu.SemaphoreType.DMA(())), emit)


def all_gather(v, *, name):
    return _alone(gather_rider(v), name=name)[0]


def sibling_rider(parts):
    _, quads, rows, cols = parts.shape

    def emit(first, middle, last, ins, outs, sems):
        (p_ref,), (got_ref,), (send_sem, recv_sem) = ins, outs, sems
        x, y, c = _place()
        cp = pltpu.make_async_remote_copy(src_ref=p_ref.at[1 - c], dst_ref=got_ref, send_sem=send_sem,
                                          recv_sem=recv_sem, device_id=(x, y, 1 - c), device_id_type=MESH)
        _when(first)(cp.start)
        _when(last)(cp.wait)

    return Rider((parts,), (jax.ShapeDtypeStruct((quads, rows, cols), parts.dtype),),
                 (pltpu.SemaphoreType.DMA(()), pltpu.SemaphoreType.DMA(())), emit)


def chip_rider(parts):
    _, rows, cols = parts.shape

    def emit(first, middle, last, ins, outs, sems):
        (p_ref,), (got_ref,), (send_sems, recv_sems) = ins, outs, sems
        x, y, c = _place()
        chips = [(1 - x, y), (x, 1 - y), (1 - x, 1 - y)]
        copies = [pltpu.make_async_remote_copy(
            src_ref=p_ref.at[2 * cx + cy], dst_ref=got_ref.at[k], send_sem=send_sems.at[k],
            recv_sem=recv_sems.at[k], device_id=(cx, cy, c), device_id_type=MESH)
            for k, (cx, cy) in enumerate(chips)]

        @_when(first)
        def _():
            for cp in copies:
                cp.start()

        @_when(last)
        def _():
            for cp in copies:
                cp.wait()

    return Rider((parts,), (jax.ShapeDtypeStruct((3, rows, cols), parts.dtype),),
                 (pltpu.SemaphoreType.DMA((3,)), pltpu.SemaphoreType.DMA((3,))), emit)


def add_sibling(parts, got, *, name):
    _, quads, rows, cols = parts.shape
    tr = _tile(rows, GRAD_ROWS_TILE, SUBLANES)

    def body(c_ref, p_ref, g_ref, o_ref, ob_ref):
        total = p_ref[...] + g_ref[...]
        o_ref[...] = total
        ob_ref[...] = total.astype(ob_ref.dtype)

    c = lax.axis_index("c").astype(jnp.int32).reshape(1)
    quad = pl.BlockSpec((None, tr, cols), lambda q, i, c_ref: (q, i, 0))
    return pl.pallas_call(
        body, name=name,
        grid_spec=pltpu.PrefetchScalarGridSpec(
            num_scalar_prefetch=1, grid=(quads, rows // tr),
            in_specs=[pl.BlockSpec((None, None, tr, cols), lambda q, i, c_ref: (c_ref[0], q, i, 0)), quad],
            out_specs=[quad, quad]),
        out_shape=[jax.ShapeDtypeStruct((quads, rows, cols), parts.dtype),
                   jax.ShapeDtypeStruct((quads, rows, cols), BF16)],
        compiler_params=_params("parallel", "parallel"),
    )(c, parts, got)


def add_chips(parts, got, *, name):
    _, rows, cols = parts.shape
    tr = _tile(rows, GRAD_ROWS_TILE, SUBLANES)

    def body(q_ref, p_ref, g_ref, o_ref):
        o_ref[...] = ((p_ref[...] + g_ref[0].astype(F32)) + g_ref[1].astype(F32)) + g_ref[2].astype(F32)

    q = (2 * lax.axis_index("x") + lax.axis_index("y")).astype(jnp.int32).reshape(1)
    return pl.pallas_call(
        body, name=name,
        grid_spec=pltpu.PrefetchScalarGridSpec(
            num_scalar_prefetch=1, grid=(rows // tr,),
            in_specs=[pl.BlockSpec((None, tr, cols), lambda i, q_ref: (q_ref[0], i, 0)),
                      pl.BlockSpec((3, tr, cols), lambda i, q_ref: (0, i, 0))],
            out_specs=pl.BlockSpec((tr, cols), lambda i, q_ref: (i, 0))),
        out_shape=jax.ShapeDtypeStruct((rows, cols), parts.dtype), compiler_params=_params("parallel"),
    )(q, parts, got)


def _pack(arrays, dtype, row_align):
    pieces, spans, at = [], [], 0
    for a in arrays:
        flat = a.reshape(-1).astype(dtype)
        rows = -(-flat.size // (LANES * row_align)) * row_align
        pieces.append(jnp.pad(flat, (0, rows * LANES - flat.size)).reshape(rows, LANES))
        spans.append((at, rows))
        at += rows
    return jnp.concatenate(pieces, axis=0), spans


def _unpack(buf, spans, shapes, lead):
    out = []
    for (at, rows), shape in zip(spans, shapes):
        size = math.prod(shape)
        piece = buf[..., at:at + rows, :].reshape(*lead, rows * LANES)[..., :size]
        out.append(piece.reshape(*lead, *shape))
    return out


def _whole(gathered, axis):
    moved = jnp.moveaxis(gathered, 0, axis)
    shape = moved.shape
    return moved.reshape(*shape[:axis], shape[axis] * shape[axis + 1], *shape[axis + 2:])


def _blocks(whole, axis):
    shape = whole.shape
    cut = whole.reshape(*shape[:axis], N_DEV, shape[axis] // N_DEV, *shape[axis + 1:])
    return jnp.moveaxis(cut, axis, 0)


def _block_diag(w):
    heads, n, _ = w.shape
    eye = jnp.eye(heads, dtype=w.dtype)
    return (w[:, :, None, :] * eye[:, None, :, None]).reshape(heads * n, heads * n)


def _diag_blocks(full, heads):
    n = full.shape[0] // heads
    return jnp.stack([full[h * n:(h + 1) * n, h * n:(h + 1) * n] for h in range(heads)])


def _pack_grads(pieces):
    sharded = [n for n in pieces if n in SHARD_AXIS]
    replicated = [n for n in pieces if n not in SHARD_AXIS]
    cut = [_blocks(pieces[n], SHARD_AXIS[n]).reshape(N_DEV, -1) for n in sharded]
    rep = jnp.concatenate([pieces[n].reshape(-1) for n in replicated])
    rep_len = rep.size
    rep_rows = -(-rep_len // (N_DEV * LANES * SUBLANES)) * SUBLANES
    rep = jnp.pad(rep, (0, N_DEV * rep_rows * LANES - rep_len)).reshape(N_DEV, rep_rows * LANES)
    bufs, spans, at = [], [], 0
    for a in cut + [rep]:
        rows = -(-a.shape[1] // (LANES * SUBLANES)) * SUBLANES
        bufs.append(jnp.pad(a, ((0, 0), (0, rows * LANES - a.shape[1]))).reshape(N_DEV, rows, LANES))
        spans.append((at, rows))
        at += rows
    tail = -at % GRAD_ROWS_TILE
    bufs.append(jnp.zeros((N_DEV, tail, LANES), F32))
    at += tail
    parts = jnp.concatenate(bufs, axis=1)
    parts = parts.reshape(4, 2, at, LANES).transpose(1, 0, 2, 3)
    shapes = [_blocks(pieces[n], SHARD_AXIS[n]).shape[1:] for n in sharded]
    return parts, (sharded, shapes, spans, replicated, [pieces[n].shape for n in replicated], rep_rows)


def _unpack_grads(mine, info):
    sharded, shapes, spans, _, _, rep_rows = info
    rep_at = spans[-1][0]
    return dict(zip(sharded, _unpack(mine, spans[:-1], shapes, ()))), mine[rep_at:rep_at + rep_rows]


LAYER_INDEXED = ['norm_mix', 'attn_w_qkv', 'attn_w_o', 'norm_ffn', 'ffn_w_up', 'ffn_conv_w', 'ffn_conv_b',
                 'ffn_w_down', 'norm_ple', 'ple_w_gate', 'ple_w_proj']


def kernel(x, p, norm_mix, attn_w_qkv, attn_w_o, rnn_w_in, rnn_conv_w, rnn_conv_b, rnn_w_gate_a, rnn_b_gate_a, rnn_w_gate_x, rnn_b_gate_x, rnn_lru_param, rnn_w_out, norm_ffn, ffn_w_up, ffn_conv_w, ffn_conv_b, ffn_w_down, norm_ple, ple_w_gate, ple_w_proj, norm_final, loss_target, m_norm_mix, m_attn_w_qkv, m_attn_w_o, m_rnn_w_in, m_rnn_conv_w, m_rnn_conv_b, m_rnn_w_gate_a, m_rnn_b_gate_a, m_rnn_w_gate_x, m_rnn_b_gate_x, m_rnn_lru_param, m_rnn_w_out, m_norm_ffn, m_ffn_w_up, m_ffn_conv_w, m_ffn_conv_b, m_ffn_w_down, m_norm_ple, m_ple_w_gate, m_ple_w_proj, m_norm_final, v_norm_mix, v_attn_w_qkv, v_attn_w_o, v_rnn_w_in, v_rnn_conv_w, v_rnn_conv_b, v_rnn_w_gate_a, v_rnn_b_gate_a, v_rnn_w_gate_x, v_rnn_b_gate_x, v_rnn_lru_param, v_rnn_w_out, v_norm_ffn, v_ffn_w_up, v_ffn_conv_w, v_ffn_conv_b, v_ffn_w_down, v_norm_ple, v_ple_w_gate, v_ple_w_proj, v_norm_final):
    given = dict(locals())
    local = {n: given[n] for n in WEIGHTS}
    bsz, seq, d = x.shape
    t = bsz * seq
    depth = norm_mix.shape[0]
    width = rnn_w_out.shape[1] * N_DEV
    ffn = ffn_w_down.shape[1] * N_DEV

    assert depth >= 2
    now = [(n, 0) for n in MATMUL_WEIGHTS if n in LAYER_INDEXED]
    later = [(n, j) for n in MATMUL_WEIGHTS for j in range(local[n].shape[0]) if (n, j) not in now]
    full = {n: [None] * local[n].shape[0] for n in MATMUL_WEIGHTS}

    def packed(group):
        return _pack([local[n][j] for n, j in group], BF16, 2 * SUBLANES)

    def place(group, gathered, spans):
        got = _unpack(gathered, spans, [local[n][j].shape for n, j in group], (N_DEV,))
        for (n, j), g in zip(group, got):
            full[n][j] = _whole(g, SHARD_AXIS[n] - 1)

    buf, spans = packed(now)
    place(now, all_gather(buf, name="gather_layer0_weights"), spans)
    later_buf, later_spans = packed(later)
    buf, spans = _pack([local[n] for n in CHANNEL_WEIGHTS], F32, SUBLANES)
    got = _unpack(all_gather(buf, name="gather_channel_weights"), spans,
                  [local[n].shape for n in CHANNEL_WEIGHTS], (N_DEV,))
    full.update({n: _whole(g, SHARD_AXIS[n]) for n, g in zip(CHANNEL_WEIGHTS, got)})
    for n in REPLICATED:
        full[n] = local[n]

    grads = {}

    def stack(name, layer, value, count):
        grads.setdefault(name, [None] * count)[layer] = value

    saved = []
    h0 = x.reshape(t, d)
    for i in range(depth):
        slot = i // 2
        sv = {"x0": h0}
        hn = rms_fwd(h0, full["norm_mix"][i], name=f"l{i}_mix_norm")
        sv["hn"] = hn
        if i % 2 == 0:
            qkv = mm(hn, full["attn_w_qkv"][slot], out_dtype=BF16, name=f"l{i}_qkv")
            if i == 0:
                o, totals, gathered = attn_fwd(qkv, bsz, seq, d, name=f"l{i}_attn", rider=gather_rider(later_buf))
                place(later, gathered, later_spans)
            else:
                o, totals = attn_fwd(qkv, bsz, seq, d, name=f"l{i}_attn")
            h1 = mm(o, full["attn_w_o"][slot], extras=(h0,), epilogue=_add, name=f"l{i}_attn_out")
            sv.update(qkv=qkv, o=o, totals=totals)
        else:
            w_in = full["rnn_w_in"][slot]
            yg = mm(hn, w_in[:, :width], name=f"l{i}_rnn_in_gate")
            yr = mm(hn, w_in[:, width:], name=f"l{i}_rnn_in_rec")
            xr = rnn_conv_fwd(yr, full["rnn_conv_w"][slot], full["rnn_conv_b"][slot], seq, name=f"l{i}_rnn_conv")
            wa = _block_diag(full["rnn_w_gate_a"][slot]).astype(BF16)
            wx = _block_diag(full["rnn_w_gate_x"][slot]).astype(BF16)
            ga = mm(xr, wa, name=f"l{i}_rnn_gate_a")
            gi = mm(xr, wx, name=f"l{i}_rnn_gate_x")
            hs, y = rnn_scan_fwd(ga, gi, xr, yg, full["rnn_b_gate_a"][slot], full["rnn_b_gate_x"][slot],
                                 full["rnn_lru_param"][slot], bsz, seq, name=f"l{i}_rnn_scan")
            h1 = mm(y, full["rnn_w_out"][slot], extras=(h0,), epilogue=_add, name=f"l{i}_rnn_out")
            sv.update(yg=yg, yr=yr, xr=xr, wa=wa, wx=wx, ga=ga, gi=gi, hs=hs, y=y)
        sv["x1"] = h1
        hn2 = rms_fwd(h1, full["norm_ffn"][i], name=f"l{i}_ffn_norm")
        w_up = full["ffn_w_up"][i]
        ug = mm(hn2, w_up[:, :ffn], out_dtype=BF16, name=f"l{i}_ffn_up_gate")
        uv = mm(hn2, w_up[:, ffn:], out_dtype=BF16, name=f"l{i}_ffn_up_val")
        cw, cb = full["ffn_conv_w"][i], full["ffn_conv_b"][i]
        act = ffn_act_fwd(ug, uv, cw[:, :ffn], cw[:, ffn:], cb[:ffn], cb[ffn:], seq, name=f"l{i}_ffn_act")
        h2 = mm(act, full["ffn_w_down"][i], extras=(h1,), epilogue=_add, name=f"l{i}_ffn_down")
        sv.update(hn2=hn2, ug=ug, uv=uv, act=act, x2=h2)
        hn3 = rms_fwd(h2, full["norm_ple"][i], name=f"l{i}_ple_norm")
        pg = mm(hn3, full["ple_w_gate"][i], name=f"l{i}_ple_gate")
        pin = p[i].reshape(t, p.shape[-1])
        h0 = mm(pin, full["ple_w_proj"][i], extras=(h2, pg), epilogue=_ple_mix, name=f"l{i}_ple_proj_mix")
        sv.update(hn3=hn3, pg=pg, pin=pin)
        saved.append(sv)

    dx, g_final, loss_part = final_loss(h0, full["norm_final"], loss_target.reshape(t, d), name="final_loss")
    grads["norm_final"] = g_final
    loss = lax.psum(loss_part, ("x", "y", "c"))

    def later_layers(name):
        return name not in LAYER_INDEXED or len(grads[name]) > 1

    for i in reversed(range(depth)):
        slot = i // 2
        sv = saved[i]
        if i == 0:
            upper = {n: (jnp.stack(grads[n][1:]) if n in LAYER_INDEXED else
                         jnp.stack(grads[n]) if isinstance(grads[n], list) else grads[n])
                     for n in WEIGHTS if later_layers(n)}
            upper_parts, upper_info = _pack_grads(upper)
        dpg, dpe = ple_bwd(dx, sv["pg"], sv["pin"], full["ple_w_proj"][i], name=f"l{i}_ple_mix_bwd")
        stack("ple_w_proj", i, mm(sv["pin"], dpe, ta=True, name=f"l{i}_ple_proj_wgrad"), depth)
        stack("ple_w_gate", i, mm(sv["hn3"], dpg, ta=True, name=f"l{i}_ple_gate_wgrad"), depth)
        dx, gn = mm_rms_bwd(dpg, full["ple_w_gate"][i], sv["x2"], full["norm_ple"][i], dx,
                            name=f"l{i}_ple_gate_dgrad_norm_bwd")
        stack("norm_ple", i, gn, depth)
        stack("ffn_w_down", i, mm(sv["act"], dx, ta=True, name=f"l{i}_ffn_down_wgrad"), depth)
        dact = mm(dx, full["ffn_w_down"][i], tb=True, out_dtype=BF16, name=f"l{i}_ffn_down_dgrad")
        cw, cb = full["ffn_conv_w"][i], full["ffn_conv_b"][i]
        taps = cw.shape[0]
        ride = sibling_rider(upper_parts) if i == 0 else None
        dug, duv, sg, svv, *rode = ffn_act_bwd(sv["ug"], sv["uv"], cw[:, :ffn], cw[:, ffn:], cb[:ffn], cb[ffn:], dact,
                                               seq, name=f"l{i}_ffn_act_bwd", rider=ride)
        if i == 0:
            upper_sum, upper_sum_bf16 = add_sibling(upper_parts, rode[0], name="upper_grads_add_sibling")
        stack("ffn_conv_w", i, jnp.concatenate([sg[:taps], svv[:taps]], axis=1), depth)
        stack("ffn_conv_b", i, jnp.concatenate([sg[taps], svv[taps]], axis=0), depth)
        stack("ffn_w_up", i, jnp.concatenate(
            [mm(sv["hn2"], dug, ta=True, name=f"l{i}_ffn_up_wgrad_gate"),
             mm(sv["hn2"], duv, ta=True, name=f"l{i}_ffn_up_wgrad_val")], axis=1), depth)
        w_up = full["ffn_w_up"][i]
        dhn2 = mm(dug, w_up[:, :ffn], tb=True, name=f"l{i}_ffn_up_dgrad_gate")
        dx, gn = mm_rms_bwd(duv, w_up[:, ffn:], sv["x1"], full["norm_ffn"][i], dx, prev=dhn2,
                            name=f"l{i}_ffn_up_dgrad_val_norm_bwd")
        stack("norm_ffn", i, gn, depth)
        if i % 2 == 0:
            stack("attn_w_o", slot, mm(sv["o"], dx, ta=True, name=f"l{i}_attn_out_wgrad"), depth // 2)
            do = mm(dx, full["attn_w_o"][slot], tb=True, out_dtype=BF16, name=f"l{i}_attn_out_dgrad")
            ride = chip_rider(upper_sum_bf16) if i == 0 else None
            dq, dk, dv, *rode = attn_bwd(sv["qkv"], sv["totals"], do, bsz, seq, d, name=f"l{i}_attn_bwd", rider=ride)
            if i == 0:
                upper_mine = add_chips(upper_sum, rode[0], name="upper_grads_add_chips")
            dqkv = jnp.concatenate([dq, dk, dv], axis=1).astype(BF16)
            stack("attn_w_qkv", slot, mm(sv["hn"], dqkv, ta=True, name=f"l{i}_qkv_wgrad"), depth // 2)
            dx, gn = mm_rms_bwd(dqkv, full["attn_w_qkv"][slot], sv["x0"], full["norm_mix"][i], dx,
                                name=f"l{i}_qkv_dgrad_norm_bwd")
        else:
            nrnn = depth // 2
            stack("rnn_w_out", slot, mm(sv["y"], dx, ta=True, name=f"l{i}_rnn_out_wgrad"), nrnn)
            dy = mm(dx, full["rnn_w_out"][slot], tb=True, name=f"l{i}_rnn_out_dgrad")
            dyg, dga, dgi, dxr, stats = rnn_scan_bwd(
                dy, sv["ga"], sv["gi"], sv["xr"], sv["yg"], sv["hs"], full["rnn_b_gate_a"][slot],
                full["rnn_b_gate_x"][slot], full["rnn_lru_param"][slot], bsz, seq, name=f"l{i}_rnn_scan_bwd")
            stack("rnn_b_gate_a", slot, stats[0], nrnn)
            stack("rnn_b_gate_x", slot, stats[1], nrnn)
            stack("rnn_lru_param", slot, stats[2], nrnn)
            stack("rnn_w_gate_a", slot, _diag_blocks(mm(sv["xr"], dga, ta=True, name=f"l{i}_rnn_gate_a_wgrad"),
                                                     RNN_HEADS), nrnn)
            stack("rnn_w_gate_x", slot, _diag_blocks(mm(sv["xr"], dgi, ta=True, name=f"l{i}_rnn_gate_x_wgrad"),
                                                     RNN_HEADS), nrnn)
            dxr = mm(dga, sv["wa"], tb=True, extras=(dxr,), epilogue=_add, name=f"l{i}_rnn_gate_a_dgrad")
            dxr = mm(dgi, sv["wx"], tb=True, extras=(dxr,), epilogue=_add, name=f"l{i}_rnn_gate_x_dgrad")
            rcw = full["rnn_conv_w"][slot]
            rtaps = rcw.shape[0]
            cstats = rnn_conv_wgrad(dxr, sv["yr"], seq, rtaps, name=f"l{i}_rnn_conv_wgrad")
            stack("rnn_conv_w", slot, cstats[:rtaps], nrnn)
            stack("rnn_conv_b", slot, cstats[rtaps], nrnn)
            dyr = conv_input_grad(dxr, rcw, seq, out_dtype=BF16, name=f"l{i}_rnn_conv_bwd")
            stack("rnn_w_in", slot, jnp.concatenate(
                [mm(sv["hn"], dyg, ta=True, name=f"l{i}_rnn_in_wgrad_gate"),
                 mm(sv["hn"], dyr, ta=True, name=f"l{i}_rnn_in_wgrad_rec")], axis=1), nrnn)
            w_in = full["rnn_w_in"][slot]
            dhn = mm(dyg, w_in[:, :width], tb=True, name=f"l{i}_rnn_in_dgrad_gate")
            dx, gn = mm_rms_bwd(dyr, w_in[:, width:], sv["x0"], full["norm_mix"][i], dx, prev=dhn,
                                name=f"l{i}_rnn_in_dgrad_rec_norm_bwd")
        stack("norm_mix", i, gn, depth)
    grad_x = dx.reshape(bsz, seq, d)

    lower = {n: grads[n][0][None] for n in LAYER_INDEXED}
    lower_parts, lower_info = _pack_grads(lower)
    from_sibling = _alone(sibling_rider(lower_parts), name="grads_to_sibling")[0]
    lower_sum, lower_sum_bf16 = add_sibling(lower_parts, from_sibling, name="grads_add_sibling")
    from_chips = _alone(chip_rider(lower_sum_bf16), name="grads_to_chips")[0]
    lower_mine = add_chips(lower_sum, from_chips, name="grads_add_chips")
    upper_local, upper_rep = _unpack_grads(upper_mine, upper_info)
    lower_local, lower_rep = _unpack_grads(lower_mine, lower_info)
    local_grads = {n: (jnp.concatenate([lower_local[n], upper_local[n]], axis=0) if n in upper_local
                       else lower_local[n]) if n in lower_local else upper_local[n]
                   for n in WEIGHTS if n in SHARD_AXIS}
    rep_all = all_gather(jnp.concatenate([upper_rep, lower_rep], axis=0), name="gather_replicated_grads")
    rep_vecs = {}
    for key, info, rows_at in (("upper", upper_info, 0), ("lower", lower_info, upper_rep.shape[0])):
        vec = rep_all[:, rows_at:rows_at + info[5]].reshape(-1)
        at = 0
        for n, shape in zip(info[3], info[4]):
            rep_vecs[key, n] = vec[at:at + math.prod(shape)].reshape(shape)
            at += math.prod(shape)
    for n in REPLICATED:
        both = [rep_vecs[k, n] for k in ("lower", "upper") if (k, n) in rep_vecs]
        local_grads[n] = both[0] if n not in LAYER_INDEXED else jnp.concatenate(both, axis=0)

    deltas, new_m, new_v = {}, {}, {}
    for n in WEIGHTS:
        deltas[n], new_m[n], new_v[n] = adamw(local[n], local_grads[n], given["m_" + n], given["v_" + n],
                                              name=f"adamw_{n}")
    return (loss, grad_x, *[local_grads[n] for n in WEIGHTS], *[deltas[n] for n in WEIGHTS],
            *[new_m[n] for n in WEIGHTS], *[new_v[n] for n in WEIGHTS])
```

```python
import functools
import math
from typing import Callable, NamedTuple

import jax
import jax.numpy as jnp
from jax import lax
from jax.experimental import pallas as pl
from jax.experimental.pallas import tpu as pltpu

F32 = jnp.float32
BF16 = jnp.bfloat16

EPS = 1e-6
HEAD_DIM = 64
RNN_HEADS = 16
LRU_C = 8.0
ADAM_LR = 0.001
ADAM_B1 = 0.9
ADAM_B2 = 0.999
ADAM_EPS = 1e-08
ADAM_WD = 0.01
ADAM_STEP = 10

N_DEV = 8
LANES = 128
SUBLANES = 8
VMEM_LIMIT = 56 * 1024 * 1024
MESH = pl.DeviceIdType.MESH
GRAD_ROWS_TILE = 2048
GELU_C = math.sqrt(2.0 / math.pi)
GELU_A = 0.044715

WEIGHTS = ['norm_mix', 'attn_w_qkv', 'attn_w_o', 'rnn_w_in', 'rnn_conv_w', 'rnn_conv_b', 'rnn_w_gate_a',
           'rnn_b_gate_a', 'rnn_w_gate_x', 'rnn_b_gate_x', 'rnn_lru_param', 'rnn_w_out', 'norm_ffn', 'ffn_w_up',
           'ffn_conv_w', 'ffn_conv_b', 'ffn_w_down', 'norm_ple', 'ple_w_gate', 'ple_w_proj', 'norm_final']
SHARD_AXIS = {'attn_w_qkv': 2, 'attn_w_o': 1, 'rnn_w_in': 2, 'rnn_conv_w': 2, 'rnn_conv_b': 1, 'rnn_b_gate_a': 1,
              'rnn_b_gate_x': 1, 'rnn_lru_param': 1, 'rnn_w_out': 1, 'ffn_w_up': 2, 'ffn_conv_w': 2,
              'ffn_w_down': 1, 'ple_w_gate': 1, 'ple_w_proj': 2}
MATMUL_WEIGHTS = ['attn_w_qkv', 'attn_w_o', 'rnn_w_in', 'rnn_w_out', 'ffn_w_up', 'ffn_w_down', 'ple_w_gate',
                  'ple_w_proj']
CHANNEL_WEIGHTS = ['rnn_conv_w', 'rnn_conv_b', 'rnn_b_gate_a', 'rnn_b_gate_x', 'rnn_lru_param', 'ffn_conv_w']
REPLICATED = [n for n in WEIGHTS if n not in SHARD_AXIS]


def _params(*sem):
    return pltpu.CompilerParams(dimension_semantics=sem, vmem_limit_bytes=VMEM_LIMIT)


def _tile(dim, pref, align=LANES):
    if dim <= pref:
        return dim
    t = (pref + pref // 2) // align * align
    while t >= align:
        if dim % t == 0:
            return t
        t -= align
    return dim


def _gelu(x):
    return 0.5 * x * (1.0 + jnp.tanh(GELU_C * (x + GELU_A * x * x * x)))


def _gelu_and_grad(x):
    t = jnp.tanh(GELU_C * (x + GELU_A * x * x * x))
    g = 0.5 * x * (1.0 + t)
    dg = 0.5 * (1.0 + t) + 0.5 * x * (1.0 - t * t) * GELU_C * (1.0 + 3.0 * GELU_A * x * x)
    return g, dg


def _log_sigmoid(x):
    return jnp.minimum(x, 0.0) - jnp.log(1.0 + jnp.exp(-jnp.abs(x)))


MM_VMEM_BUDGET = 36 * 1024 * 1024


def _mm_tiles(m, n, k, ta, a_item, b_item, out_item, n_extra):
    if ta:
        return _tile(m, 1024), _tile(n, 1024), _tile(k, 1024)
    row_bytes = k * a_item + n * (out_item + 4 * n_extra)
    w_bytes = k * n * b_item
    for tm in (1024, 512, 256, 128):
        if m % tm == 0 and 2 * tm * row_bytes + 2 * w_bytes + tm * n * 4 <= MM_VMEM_BUDGET:
            return tm, n, k
    return _tile(m, 512), _tile(n, 512), _tile(k, 1024)


def mm(a, b, *, name, ta=False, tb=False, out_dtype=F32, extras=(), epilogue=None):
    m, k = (a.shape[1], a.shape[0]) if ta else a.shape
    n = b.shape[0] if tb else b.shape[1]
    assert k == (b.shape[1] if tb else b.shape[0]), (a.shape, b.shape, ta, tb)
    tm, tn, tk = _mm_tiles(m, n, k, ta, a.dtype.itemsize, b.dtype.itemsize, jnp.dtype(out_dtype).itemsize,
                           len(extras))
    nk = k // tk
    n_extra = len(extras)
    dims = (((0 if ta else 1,), (1 if tb else 0,)), ((), ()))

    def body(a_ref, b_ref, *rest):
        extra_refs, o_ref = rest[:n_extra], rest[n_extra]

        def finish(acc):
            if epilogue is not None:
                acc = epilogue(acc, *[e[...] for e in extra_refs])
            o_ref[...] = acc.astype(o_ref.dtype)

        part = lax.dot_general(a_ref[...].astype(BF16), b_ref[...].astype(BF16), dims,
                               preferred_element_type=F32)
        if nk == 1:
            finish(part)
        else:
            acc_ref = rest[n_extra + 1]
            kk = pl.program_id(2)

            @pl.when(kk == 0)
            def _():
                acc_ref[...] = part

            @pl.when(kk > 0)
            def _():
                acc_ref[...] += part

            @pl.when(kk == nk - 1)
            def _():
                finish(acc_ref[...])

    a_spec = pl.BlockSpec((tk, tm), lambda i, j, kk: (kk, i)) if ta else pl.BlockSpec((tm, tk), lambda i, j, kk: (i, kk))
    b_spec = pl.BlockSpec((tn, tk), lambda i, j, kk: (j, kk)) if tb else pl.BlockSpec((tk, tn), lambda i, j, kk: (kk, j))
    o_spec = pl.BlockSpec((tm, tn), lambda i, j, kk: (i, j))
    return pl.pallas_call(
        body, name=name, grid=(m // tm, n // tn, nk),
        in_specs=[a_spec, b_spec] + [o_spec] * n_extra, out_specs=o_spec,
        out_shape=jax.ShapeDtypeStruct((m, n), out_dtype),
        scratch_shapes=[pltpu.VMEM((tm, tn), F32)] if nk > 1 else [],
        compiler_params=_params("parallel", "parallel", "arbitrary"),
    )(a, b, *extras)


def _add(acc, res):
    return acc + res


def mm_rms(x, g, w, *, name, out_dtype=F32):
    t, d = x.shape
    n = w.shape[1]
    out_item = jnp.dtype(out_dtype).itemsize
    row_bytes = d * 4 + d * 2 + n * out_item
    tr = next(tm for tm in (512, 256, 128, t)
              if t % tm == 0 and 2 * tm * row_bytes + 2 * w.size * w.dtype.itemsize + tm * n * 4 <= MM_VMEM_BUDGET)

    def body(x_ref, g_ref, w_ref, o_ref, hn_ref):
        xv = x_ref[...]
        r = lax.rsqrt(jnp.mean(xv * xv, axis=-1, keepdims=True) + EPS)
        hn = (xv * r * g_ref[...]).astype(BF16)
        hn_ref[...] = hn
        o_ref[...] = jnp.dot(hn, w_ref[...].astype(BF16), preferred_element_type=F32).astype(o_ref.dtype)

    row = pl.BlockSpec((tr, d), lambda i: (i, 0))
    return pl.pallas_call(
        body, name=name, grid=(t // tr,),
        in_specs=[row, pl.BlockSpec((1, d), lambda i: (0, 0)), pl.BlockSpec((d, n), lambda i: (0, 0))],
        out_specs=[pl.BlockSpec((tr, n), lambda i: (i, 0)), row],
        out_shape=[jax.ShapeDtypeStruct((t, n), out_dtype), jax.ShapeDtypeStruct((t, d), BF16)],
        compiler_params=_params("parallel"),
    )(x, g.reshape(1, d), w)


def mm_rms_bwd(a, w, x, g, dres, *, name, prev=None):
    t, k = a.shape
    d = w.shape[0]
    assert w.shape[1] == k and x.shape == (t, d)
    n_rows = 4 if prev is not None else 3
    row_bytes = k * a.dtype.itemsize + d * 4 * n_rows
    tr = next(tm for tm in (512, 256, 128, t)
              if t % tm == 0 and 2 * tm * row_bytes + 2 * w.size * w.dtype.itemsize + tm * d * 4 <= MM_VMEM_BUDGET)

    def body(a_ref, w_ref, x_ref, g_ref, dres_ref, *rest):
        dx_ref, dg_ref = rest[-2:]
        dhv = lax.dot_general(a_ref[...].astype(BF16), w_ref[...].astype(BF16), _NT, preferred_element_type=F32)
        if prev is not None:
            dhv = dhv + rest[0][...]
        xv = x_ref[...]
        r = lax.rsqrt(jnp.mean(xv * xv, axis=-1, keepdims=True) + EPS)
        xh = xv * r
        u = dhv * g_ref[...]
        dx_ref[...] = dres_ref[...] + r * (u - xh * jnp.mean(u * xh, axis=-1, keepdims=True))
        part = jnp.sum(dhv * xh, axis=0, keepdims=True)

        @pl.when(pl.program_id(0) == 0)
        def _():
            dg_ref[...] = part

        @pl.when(pl.program_id(0) > 0)
        def _():
            dg_ref[...] += part

    row = pl.BlockSpec((tr, d), lambda i: (i, 0))
    vec = pl.BlockSpec((1, d), lambda i: (0, 0))
    extra = [prev] if prev is not None else []
    dx, dg = pl.pallas_call(
        body, name=name, grid=(t // tr,),
        in_specs=[pl.BlockSpec((tr, k), lambda i: (i, 0)), pl.BlockSpec((d, k), lambda i: (0, 0)), row, vec, row]
        + [row] * len(extra),
        out_specs=[row, vec],
        out_shape=[jax.ShapeDtypeStruct((t, d), F32), jax.ShapeDtypeStruct((1, d), F32)],
        compiler_params=_params("arbitrary"),
    )(a, w, x, g.reshape(1, d), dres, *extra)
    return dx, dg.reshape(d)


def final_loss(x, g, target, *, name):
    t, d = x.shape
    tr = _tile(t, 512, SUBLANES)

    def body(x_ref, g_ref, t_ref, dx_ref, dg_ref, loss_ref):
        xv = x_ref[...]
        gv = g_ref[...]
        r = lax.rsqrt(jnp.mean(xv * xv, axis=-1, keepdims=True) + EPS)
        xh = xv * r
        err = xh * gv - t_ref[...]
        dy = err * (1.0 / d)
        u = dy * gv
        dx_ref[...] = r * (u - xh * jnp.mean(u * xh, axis=-1, keepdims=True))
        dg_part = jnp.sum(dy * xh, axis=0, keepdims=True)
        loss_part = jnp.zeros((1, LANES), F32) + (0.5 / d) * jnp.sum(err * err)

        @pl.when(pl.program_id(0) == 0)
        def _():
            dg_ref[...] = dg_part
            loss_ref[...] = loss_part

        @pl.when(pl.program_id(0) > 0)
        def _():
            dg_ref[...] += dg_part
            loss_ref[...] += loss_part

    row = pl.BlockSpec((tr, d), lambda i: (i, 0))
    vec = pl.BlockSpec((1, d), lambda i: (0, 0))
    dx, dg, loss = pl.pallas_call(
        body, name=name, grid=(t // tr,), in_specs=[row, vec, row],
        out_specs=[row, vec, pl.BlockSpec((1, LANES), lambda i: (0, 0))],
        out_shape=[jax.ShapeDtypeStruct((t, d), F32), jax.ShapeDtypeStruct((1, d), F32),
                   jax.ShapeDtypeStruct((1, LANES), F32)],
        compiler_params=_params("arbitrary"),
    )(x, g.reshape(1, d), target)
    return dx, dg.reshape(d), loss[0, 0]


def _split_dot(x, mat, left):
    hi = x.astype(BF16)
    lo = (x - hi.astype(F32)).astype(BF16)
    if left:
        return (jnp.dot(mat, hi, preferred_element_type=F32) + jnp.dot(mat, lo, preferred_element_type=F32))
    return (jnp.dot(hi, mat, preferred_element_type=F32) + jnp.dot(lo, mat, preferred_element_type=F32))


_NT = (((1,), (1,)), ((), ()))
_TN = (((0,), (0,)), ((), ()))
HEADS_PER_STEP = LANES // HEAD_DIM


def attn_fwd(qkv, b, s, d, *, name, rider=None):
    t = b * s
    tq = min(256, s)
    nq = s // tq
    pairs = d // LANES
    scale = HEAD_DIM ** -0.5

    grid = (b, pairs, nq)
    n_ride = len(rider.inputs) if rider else 0

    def body(*refs):
        q_ref, k_ref, v_ref = refs[:3]
        o_ref, lt_ref = refs[3 + n_ride:5 + n_ride]
        _ride(rider, refs, 3, 2, grid)
        i = pl.program_id(2)
        row = lax.broadcasted_iota(jnp.int32, (tq, tq), 0)
        col = lax.broadcasted_iota(jnp.int32, (tq, tq), 1)
        later = (row > col).astype(BF16)
        causal = col < row
        lanes = [slice(HEAD_DIM * h, HEAD_DIM * (h + 1)) for h in range(HEADS_PER_STEP)]
        qs = [(q_ref[:, sl].astype(F32) * scale).astype(BF16) for sl in lanes]

        def block(js, carry, diag):
            starts = [pl.multiple_of(j * tq, tq) for j in js]
            hs = range(HEADS_PER_STEP)
            chains = [(n, h) for n in range(len(js)) for h in hs]
            kbs = {(n, h): k_ref[pl.ds(starts[n], tq), lanes[h]] for n, h in chains}
            vbs = {(n, h): v_ref[pl.ds(starts[n], tq), lanes[h]] for n, h in chains}
            zs = {c: lax.dot_general(qs[c[1]], kbs[c], _NT, preferred_element_type=F32) for c in chains}
            lss = {c: _log_sigmoid(zs[c]) for c in chains}
            lks = {c: lss[c] - zs[c] for c in chains}
            if diag:
                lks = {c: jnp.where(causal, lks[c], 0.0) for c in chains}
            sums = {c: _split_dot(lks[c], later, left=False) for c in chains}
            runs_in, runs = {}, []
            for h in hs:
                run = carry[h][0]
                for n in range(len(js)):
                    runs_in[n, h] = run
                    run = run + jnp.sum(lks[n, h], axis=1, keepdims=True)
                runs.append(run)
            ws = {c: jnp.exp(lss[c] + sums[c] + runs_in[c]) for c in chains}
            if diag:
                ws = {c: jnp.where(causal, ws[c], 0.0) for c in chains}
            pvs = {c: jnp.dot(ws[c].astype(BF16), vbs[c], preferred_element_type=F32) for c in chains}
            accs = [carry[h][1] + sum(pvs[n, h] for n in range(len(js))) for h in hs]
            return tuple(zip(runs, accs))

        zero = (jnp.zeros((tq, 1), F32), jnp.zeros((tq, HEAD_DIM), F32))
        carry = block([i], (zero,) * HEADS_PER_STEP, True)
        odd = i % 2
        carry = lax.cond(odd == 1, lambda c: block([i - 1], c, False), lambda c: c, carry)
        near = i - 1 - odd
        carry = lax.fori_loop(0, i // 2, lambda n, c: block([near - 2 * n, near - 2 * n - 1], c, False), carry)
        eye = (row == col).astype(F32)
        for h, sl in enumerate(lanes):
            run, acc = carry[h]
            o_ref[:, sl] = acc.astype(o_ref.dtype)
            lt_ref[SUBLANES * h:SUBLANES * (h + 1), :] = lax.dot_general(
                jnp.broadcast_to(run, (tq, SUBLANES)), eye, _TN, precision=lax.Precision.HIGHEST,
                preferred_element_type=F32)

    q_spec = pl.BlockSpec((tq, LANES), lambda bb, p, i: (bb * nq + i, p))
    k_spec = pl.BlockSpec((s, LANES), lambda bb, p, i: (bb, pairs + p))
    v_spec = pl.BlockSpec((s, LANES), lambda bb, p, i: (bb, 2 * pairs + p))
    lt_spec = pl.BlockSpec((None, None, None, HEADS_PER_STEP * SUBLANES, tq), lambda bb, p, i: (bb, p, i, 0, 0))
    in_specs, out_specs, out_shape, scratch = _rider_specs(
        rider, [q_spec, k_spec, v_spec], [q_spec, lt_spec],
        [jax.ShapeDtypeStruct((t, d), BF16),
         jax.ShapeDtypeStruct((b, pairs, nq, HEADS_PER_STEP * SUBLANES, tq), F32)])
    order = ("arbitrary",) * 3 if rider else ("parallel", "parallel", "arbitrary")
    return pl.pallas_call(
        body, name=name, grid=grid, in_specs=in_specs, out_specs=out_specs, out_shape=out_shape,
        scratch_shapes=scratch, compiler_params=_params(*order),
    )(qkv, qkv, qkv, *(rider.inputs if rider else ()))


def attn_bwd(qkv, totals, do, b, s, d, *, name, rider=None):
    t = b * s
    tq = min(256, s)
    nq = s // tq
    pairs = d // LANES
    scale = HEAD_DIM ** -0.5

    grid = (b, pairs, nq)
    n_ride = len(rider.inputs) if rider else 0

    def body(*refs):
        q_ref, k_ref, v_ref, lt_ref, do_ref = refs[:5]
        dq_ref, dk_ref, dv_ref = refs[5 + n_ride:8 + n_ride]
        _ride(rider, refs, 5, 3, grid)
        i = pl.program_id(2)

        @pl.when(i == 0)
        def _():
            dk_ref[...] = jnp.zeros_like(dk_ref)
            dv_ref[...] = jnp.zeros_like(dv_ref)

        row = lax.broadcasted_iota(jnp.int32, (tq, tq), 0)
        col = lax.broadcasted_iota(jnp.int32, (tq, tq), 1)
        upto = (col <= row).astype(BF16)
        earlier = (col < row).astype(BF16)
        causal = row < col
        lanes = [slice(HEAD_DIM * h, HEAD_DIM * (h + 1)) for h in range(HEADS_PER_STEP)]
        qs = [(q_ref[:, sl].astype(F32) * scale).astype(BF16) for sl in lanes]
        dos = [do_ref[:, sl].astype(BF16) for sl in lanes]
        totals_h = [lt_ref[SUBLANES * h:SUBLANES * h + 1, :] for h in range(HEADS_PER_STEP)]

        def block(js, carry, diag):
            starts = [pl.multiple_of(j * tq, tq) for j in js]
            hs = range(HEADS_PER_STEP)
            ns = range(len(js))
            chains = [(n, h) for n in ns for h in hs]
            kbs = {(n, h): k_ref[pl.ds(starts[n], tq), lanes[h]] for n, h in chains}
            vbs = {(n, h): v_ref[pl.ds(starts[n], tq), lanes[h]] for n, h in chains}
            zs = {c: lax.dot_general(kbs[c], qs[c[1]], _NT, preferred_element_type=F32) for c in chains}
            dws = {c: lax.dot_general(vbs[c], dos[c[1]], _NT, preferred_element_type=F32) for c in chains}
            lss = {c: _log_sigmoid(zs[c]) for c in chains}
            lks = {c: lss[c] - zs[c] for c in chains}
            if diag:
                lks = {c: jnp.where(causal, lks[c], 0.0) for c in chains}
            sums = {c: _split_dot(lks[c], upto, left=True) for c in chains}
            runs_in, runs = {}, []
            for h in hs:
                run = carry[h][0]
                for n in ns:
                    runs_in[n, h] = run
                    run = run + jnp.sum(lks[n, h], axis=0, keepdims=True)
                runs.append(run)
            ws = {c: jnp.exp(lss[c] + ((totals_h[c[1]] - runs_in[c]) - sums[c])) for c in chains}
            if diag:
                ws = {c: jnp.where(causal, ws[c], 0.0) for c in chains}
            gs = {c: dws[c] * ws[c] for c in chains}
            gsums = {c: _split_dot(gs[c], earlier, left=True) for c in chains}
            gruns_in, gruns = {}, []
            for h in hs:
                grun = carry[h][1]
                for n in ns:
                    gruns_in[n, h] = grun
                    grun = grun + jnp.sum(gs[n, h], axis=0, keepdims=True)
                gruns.append(grun)
            dzs = {c: gs[c] - jnp.exp(lss[c]) * (gs[c] + (gruns_in[c] + gsums[c])) for c in chains}
            if diag:
                dzs = {c: jnp.where(causal, dzs[c], 0.0) for c in chains}
            dzbs = {c: dzs[c].astype(BF16) for c in chains}
            for n, h in chains:
                dv_ref[pl.ds(starts[n], tq), lanes[h]] += jnp.dot(ws[n, h].astype(BF16), dos[h],
                                                                  preferred_element_type=F32)
                dk_ref[pl.ds(starts[n], tq), lanes[h]] += jnp.dot(dzbs[n, h], qs[h], preferred_element_type=F32)
            dqs = [carry[h][2] + sum(lax.dot_general(dzbs[n, h], kbs[n, h], _TN, preferred_element_type=F32)
                                     for n in ns) for h in hs]
            return tuple(zip(runs, gruns, dqs))

        zero = (jnp.zeros((1, tq), F32), jnp.zeros((1, tq), F32), jnp.zeros((tq, HEAD_DIM), F32))
        carry = lax.fori_loop(0, i // 2, lambda n, c: block([2 * n, 2 * n + 1], c, False),
                              (zero,) * HEADS_PER_STEP)
        carry = lax.cond(i % 2 == 1, lambda c: block([i - 1], c, False), lambda c: c, carry)
        carry = block([i], carry, True)
        for h, sl in enumerate(lanes):
            dq_ref[:, sl] = carry[h][2] * scale

    q_spec = pl.BlockSpec((tq, LANES), lambda bb, p, i: (bb * nq + i, p))
    k_spec = pl.BlockSpec((s, LANES), lambda bb, p, i: (bb, pairs + p))
    v_spec = pl.BlockSpec((s, LANES), lambda bb, p, i: (bb, 2 * pairs + p))
    lt_spec = pl.BlockSpec((None, None, None) + totals.shape[3:], lambda bb, p, i: (bb, p, i, 0, 0))
    kv_out = pl.BlockSpec((s, LANES), lambda bb, p, i: (bb, p))
    out = jax.ShapeDtypeStruct((t, d), F32)
    in_specs, out_specs, out_shape, scratch = _rider_specs(
        rider, [q_spec, k_spec, v_spec, lt_spec, q_spec], [q_spec, kv_out, kv_out], [out, out, out])
    order = ("arbitrary",) * 3 if rider else ("parallel", "parallel", "arbitrary")
    return pl.pallas_call(
        body, name=name, grid=grid, in_specs=in_specs, out_specs=out_specs, out_shape=out_shape,
        scratch_shapes=scratch, compiler_params=_params(*order),
    )(qkv, qkv, qkv, totals, do, *(rider.inputs if rider else ()))


def _shift_down(cur, prev8, dist):
    ext = jnp.concatenate([prev8, cur], axis=0)
    return pltpu.roll(ext, dist, 0)[SUBLANES:]


def _shift_up(cur, next8, dist):
    ext = jnp.concatenate([cur, next8], axis=0)
    return pltpu.roll(ext, ext.shape[0] - dist, 0)[:cur.shape[0]]


def _causal_conv(cur, prev8, w_ref, b_ref):
    taps = w_ref.shape[0]
    out = cur * w_ref[taps - 1:taps, :] + b_ref[...]
    for dist in range(1, taps):
        out = out + _shift_down(cur, prev8, dist) * w_ref[taps - 1 - dist:taps - dist, :]
    return out


def _conv_specs(t, rows, tc, time_axis, dtype=F32):
    sub = SUBLANES * (4 // jnp.dtype(dtype).itemsize)
    per = rows // sub
    last = t // sub - 1

    def grid_ids(*ids):
        return ids[time_axis], ids[1 - time_axis]

    def cur(*ids):
        return grid_ids(*ids)

    def prev(*ids):
        i, j = grid_ids(*ids)
        return (jnp.maximum(i * per - 1, 0), j)

    def nxt(*ids):
        i, j = grid_ids(*ids)
        return (jnp.minimum((i + 1) * per, last), j)

    def chan(*ids):
        return (0, grid_ids(*ids)[1])

    return pl.BlockSpec((rows, tc), cur), pl.BlockSpec((sub, tc), prev), pl.BlockSpec((sub, tc), nxt), chan


def _rows_before(ref, keep):
    return ref[...].astype(F32)[-SUBLANES:] * keep


def _rows_after(ref, keep):
    return ref[...].astype(F32)[:SUBLANES] * keep


def _first_in_seq(i, rows, s):
    return (i % (s // rows)) == 0


def _last_in_seq(i, rows, s):
    return (i % (s // rows)) == (s // rows - 1)


FFN_ACT_ROWS, FFN_ACT_COLS = 256, 1024


def ffn_act_fwd(ug, uv, cwg, cwv, cbg, cbv, s, *, name):
    t, f = ug.shape
    rows, tc = _tile(s, FFN_ACT_ROWS, SUBLANES), _tile(f, FFN_ACT_COLS)
    cur, prev, _, chan = _conv_specs(t, rows, tc, 0, ug.dtype)
    taps = cwg.shape[0]

    def body(ug_ref, ugp_ref, uv_ref, uvp_ref, cwg_ref, cwv_ref, cbg_ref, cbv_ref, a_ref):
        keep = jnp.where(_first_in_seq(pl.program_id(0), rows, s), 0.0, 1.0)
        gate = _causal_conv(ug_ref[...].astype(F32), _rows_before(ugp_ref, keep), cwg_ref, cbg_ref)
        val = _causal_conv(uv_ref[...].astype(F32), _rows_before(uvp_ref, keep), cwv_ref, cbv_ref)
        a_ref[...] = (_gelu(gate) * val).astype(a_ref.dtype)

    wspec = pl.BlockSpec((taps, tc), chan)
    bspec = pl.BlockSpec((1, tc), chan)
    return pl.pallas_call(
        body, name=name, grid=(t // rows, f // tc), in_specs=[cur, prev, cur, prev, wspec, wspec, bspec, bspec],
        out_specs=cur, out_shape=jax.ShapeDtypeStruct((t, f), BF16), compiler_params=_params("parallel", "parallel"),
    )(ug, ug, uv, uv, cwg, cwv, cbg.reshape(1, f), cbv.reshape(1, f))


def _accumulate_rows(first, ref, rows):
    for k, r in enumerate(rows):
        @pl.when(first)
        def _(k=k, r=r):
            ref[k:k + 1, :] = r

        @pl.when(jnp.logical_not(first))
        def _(k=k, r=r):
            ref[k:k + 1, :] += r


def _conv_weight_grads(dc, cur, prev8, taps):
    out = []
    for k in range(taps):
        dist = taps - 1 - k
        xs = cur if dist == 0 else _shift_down(cur, prev8, dist)
        out.append(jnp.sum(dc * xs, axis=0, keepdims=True))
    out.append(jnp.sum(dc, axis=0, keepdims=True))
    return out


def _conv_transpose(dc_ext, rows, w_ref):
    taps = w_ref.shape[0]
    out = dc_ext[:rows] * w_ref[taps - 1:taps, :]
    for dist in range(1, taps):
        out = out + pltpu.roll(dc_ext, dc_ext.shape[0] - dist, 0)[:rows] * w_ref[taps - 1 - dist:taps - dist, :]
    return out


def ffn_act_bwd(ug, uv, cwg, cwv, cbg, cbv, da, s, *, name, rider=None):
    t, f = ug.shape
    rows, tc = _tile(s, FFN_ACT_ROWS, SUBLANES), _tile(f, FFN_ACT_COLS)
    cur, prev, nxt, chan = _conv_specs(t, rows, tc, 1, ug.dtype)
    taps = cwg.shape[0]

    grid = (f // tc, t // rows)
    n_ride = len(rider.inputs) if rider else 0

    def body(*refs):
        (ug_ref, ugp_ref, ugn_ref, uv_ref, uvp_ref, uvn_ref, cwg_ref, cwv_ref, cbg_ref, cbv_ref,
         da_ref, dan_ref) = refs[:12]
        dug_ref, duv_ref, wg_ref, wv_ref = refs[12 + n_ride:16 + n_ride]
        _ride(rider, refs, 12, 4, grid)
        i = pl.program_id(1)
        keep_before = jnp.where(_first_in_seq(i, rows, s), 0.0, 1.0)
        keep_after = jnp.where(_last_in_seq(i, rows, s), 0.0, 1.0)
        ugp, uvp = _rows_before(ugp_ref, keep_before), _rows_before(uvp_ref, keep_before)
        uge = jnp.concatenate([ug_ref[...].astype(F32), _rows_after(ugn_ref, 1.0)], axis=0)
        uve = jnp.concatenate([uv_ref[...].astype(F32), _rows_after(uvn_ref, 1.0)], axis=0)
        dae = jnp.concatenate([da_ref[...].astype(F32), _rows_after(dan_ref, keep_after)], axis=0)
        gate = _causal_conv(uge, ugp, cwg_ref, cbg_ref)
        val = _causal_conv(uve, uvp, cwv_ref, cbv_ref)
        act, dact = _gelu_and_grad(gate)
        dgate = dae * val * dact
        dval = dae * act
        dug_ref[...] = _conv_transpose(dgate, rows, cwg_ref).astype(dug_ref.dtype)
        duv_ref[...] = _conv_transpose(dval, rows, cwv_ref).astype(duv_ref.dtype)
        _accumulate_rows(i == 0, wg_ref, _conv_weight_grads(dgate[:rows], uge[:rows], ugp, taps))
        _accumulate_rows(i == 0, wv_ref, _conv_weight_grads(dval[:rows], uve[:rows], uvp, taps))

    wspec = pl.BlockSpec((taps, tc), chan)
    bspec = pl.BlockSpec((1, tc), chan)
    gspec = pl.BlockSpec((taps + 1, tc), chan)
    act_shape = jax.ShapeDtypeStruct((t, f), BF16)
    stat_shape = jax.ShapeDtypeStruct((taps + 1, f), F32)
    in_specs, out_specs, out_shape, scratch = _rider_specs(
        rider, [cur, prev, nxt, cur, prev, nxt, wspec, wspec, bspec, bspec, cur, nxt], [cur, cur, gspec, gspec],
        [act_shape, act_shape, stat_shape, stat_shape])
    order = ("arbitrary",) * 2 if rider else ("parallel", "arbitrary")
    return pl.pallas_call(
        body, name=name, grid=grid, in_specs=in_specs, out_specs=out_specs, out_shape=out_shape,
        scratch_shapes=scratch, compiler_params=_params(*order),
    )(ug, ug, ug, uv, uv, uv, cwg, cwv, cbg.reshape(1, f), cbv.reshape(1, f), da, da,
      *(rider.inputs if rider else ()))


def conv_input_grad(dc, cw, s, *, name, out_dtype):
    t, f = dc.shape
    rows, tc = _tile(s, 512, SUBLANES), _tile(f, 256)
    cur, _, nxt, chan = _conv_specs(t, rows, tc, 0, dc.dtype)
    taps = cw.shape[0]

    def body(dc_ref, dcn_ref, cw_ref, o_ref):
        keep = jnp.where(_last_in_seq(pl.program_id(0), rows, s), 0.0, 1.0)
        dcc = dc_ref[...].astype(F32)
        dcn = _rows_after(dcn_ref, keep)
        out = dcc * cw_ref[taps - 1:taps, :]
        for dist in range(1, taps):
            out = out + _shift_up(dcc, dcn, dist) * cw_ref[taps - 1 - dist:taps - dist, :]
        o_ref[...] = out.astype(o_ref.dtype)

    return pl.pallas_call(
        body, name=name, grid=(t // rows, f // tc), in_specs=[cur, nxt, pl.BlockSpec((taps, tc), chan)],
        out_specs=cur, out_shape=jax.ShapeDtypeStruct((t, f), out_dtype),
        compiler_params=_params("parallel", "parallel"),
    )(dc, dc, cw)


def rnn_conv_fwd(yr, cw, cb, s, *, name):
    t, w = yr.shape
    rows, tc = _tile(s, 512, SUBLANES), _tile(w, 256)
    cur, prev, _, chan = _conv_specs(t, rows, tc, 0, yr.dtype)
    taps = cw.shape[0]

    def body(y_ref, yp_ref, cw_ref, cb_ref, o_ref):
        keep = jnp.where(_first_in_seq(pl.program_id(0), rows, s), 0.0, 1.0)
        o_ref[...] = _causal_conv(y_ref[...].astype(F32), _rows_before(yp_ref, keep), cw_ref, cb_ref)

    return pl.pallas_call(
        body, name=name, grid=(t // rows, w // tc),
        in_specs=[cur, prev, pl.BlockSpec((taps, tc), chan), pl.BlockSpec((1, tc), chan)], out_specs=cur,
        out_shape=jax.ShapeDtypeStruct((t, w), F32), compiler_params=_params("parallel", "parallel"),
    )(yr, yr, cw, cb.reshape(1, w))


def rnn_conv_wgrad(dxr, yr, s, taps, *, name):
    t, w = yr.shape
    rows, tc = _tile(s, 512, SUBLANES), _tile(w, 256)
    cur, prev, _, chan = _conv_specs(t, rows, tc, 1, yr.dtype)

    def body(d_ref, y_ref, yp_ref, o_ref):
        i = pl.program_id(1)
        keep = jnp.where(_first_in_seq(i, rows, s), 0.0, 1.0)
        grads = _conv_weight_grads(d_ref[...], y_ref[...].astype(F32), _rows_before(yp_ref, keep), taps)
        _accumulate_rows(i == 0, o_ref, grads)

    return pl.pallas_call(
        body, name=name, grid=(w // tc, t // rows), in_specs=[cur, cur, prev],
        out_specs=pl.BlockSpec((taps + 1, tc), chan), out_shape=jax.ShapeDtypeStruct((taps + 1, w), F32),
        compiler_params=_params("parallel", "arbitrary"),
    )(dxr, yr, yr)


SCAN_ROWS = 32


def _one_minus_exp(x):
    series = -x * (1.0 + x * (0.5 + x * (1.0 / 6.0)))
    return jnp.where(x > -0.01, series, 1.0 - jnp.exp(x))


def _gates(ga, gi, ba, bx, log_lam):
    ra = jax.nn.sigmoid(ga + ba)
    ri = jax.nn.sigmoid(gi + bx)
    log_a = LRU_C * ra * log_lam
    a = jnp.exp(log_a)
    mult = jnp.sqrt(_one_minus_exp(2.0 * log_a))
    return ra, ri, a, mult


def rnn_scan_fwd(ga, gi, xr, yg, ba, bx, lam, b, s, *, name):
    t, w = xr.shape
    tc = _tile(w, 256)
    rb = min(SCAN_ROWS, s)
    blocks = s // rb
    steps = [1 << e for e in range(rb.bit_length() - 1)]

    def body(ga_ref, gi_ref, xr_ref, yg_ref, ba_ref, bx_ref, lam_ref, h_ref, y_ref):
        log_lam = _log_sigmoid(lam_ref[...])
        ridx = lax.broadcasted_iota(jnp.int32, (rb, tc), 0)

        def step(n, carry):
            rs = pl.ds(pl.multiple_of(n * rb, rb), rb)
            xrv = xr_ref[rs, :]
            _, ri, a, mult = _gates(ga_ref[rs, :], gi_ref[rs, :], ba_ref[...], bx_ref[...], log_lam)
            u = mult * (ri * xrv)
            for dist in steps:
                a_sh = jnp.where(ridx >= dist, pltpu.roll(a, dist, 0), 1.0)
                u_sh = jnp.where(ridx >= dist, pltpu.roll(u, dist, 0), 0.0)
                u = a * u_sh + u
                a = a * a_sh
            hb = u + a * carry
            h_ref[rs, :] = hb
            y_ref[rs, :] = (_gelu(yg_ref[rs, :]) * hb).astype(y_ref.dtype)
            return hb[rb - 1:rb, :]

        lax.fori_loop(0, blocks, step, jnp.zeros((1, tc), F32))

    seq = pl.BlockSpec((s, tc), lambda bb, j: (bb, j))
    vec = pl.BlockSpec((1, tc), lambda bb, j: (0, j))
    return pl.pallas_call(
        body, name=name, grid=(b, w // tc), in_specs=[seq, seq, seq, seq, vec, vec, vec], out_specs=[seq, seq],
        out_shape=[jax.ShapeDtypeStruct((t, w), F32), jax.ShapeDtypeStruct((t, w), BF16)],
        compiler_params=_params("parallel", "parallel"),
    )(ga, gi, xr, yg, ba.reshape(1, w), bx.reshape(1, w), lam.reshape(1, w))


def rnn_scan_bwd(dy, ga, gi, xr, yg, h, ba, bx, lam, b, s, *, name):
    t, w = xr.shape
    tc = _tile(w, 256)
    rb = min(SCAN_ROWS, s)
    blocks = s // rb
    steps = [1 << e for e in range(rb.bit_length() - 1)]

    def body(dy_ref, ga_ref, gi_ref, xr_ref, yg_ref, h_ref, ba_ref, bx_ref, lam_ref,
             dyg_ref, dga_ref, dgi_ref, dxr_ref, stat_ref):
        lamv = lam_ref[...]
        log_lam = _log_sigmoid(lamv)
        dlog_lam = jax.nn.sigmoid(-lamv)
        ridx = lax.broadcasted_iota(jnp.int32, (rb, tc), 0)
        last = rb - 1

        def step(n, carry):
            lam_next, a_next, s_a, s_x, s_l = carry
            blk = blocks - 1 - n
            rs = pl.ds(pl.multiple_of(blk * rb, rb), rb)
            rp = pl.ds(pl.multiple_of(jnp.maximum(blk * rb - SUBLANES, 0), SUBLANES), SUBLANES)
            xrv = xr_ref[rs, :]
            hv = h_ref[rs, :]
            h_before = jnp.where(blk > 0, h_ref[rp, :][SUBLANES - 1:, :], 0.0)
            h_prev = jnp.where(ridx >= 1, pltpu.roll(hv, 1, 0), h_before)
            ra, ri, a, mult = _gates(ga_ref[rs, :], gi_ref[rs, :], ba_ref[...], bx_ref[...], log_lam)
            act, dact = _gelu_and_grad(yg_ref[rs, :])
            dyv = dy_ref[rs, :]
            dyg_ref[rs, :] = (dyv * hv * dact).astype(dyg_ref.dtype)
            v = dyv * act
            c = jnp.where(ridx < last, pltpu.roll(a, last, 0), a_next)
            for dist in steps:
                c_sh = jnp.where(ridx < rb - dist, pltpu.roll(c, rb - dist, 0), 1.0)
                v_sh = jnp.where(ridx < rb - dist, pltpu.roll(v, rb - dist, 0), 0.0)
                v = v + c * v_sh
                c = c * c_sh
            dh = v + c * lam_next
            du_ri_x = dh * xrv
            dmult = du_ri_x * ri
            dri = du_ri_x * mult
            dxr_ref[rs, :] = dh * mult * ri
            dlog_a = dh * h_prev * a - dmult * (a * a) / mult
            dra = dlog_a * (LRU_C * log_lam)
            dpa = dra * ra * (1.0 - ra)
            dpi = dri * ri * (1.0 - ri)
            dga_ref[rs, :] = dpa.astype(dga_ref.dtype)
            dgi_ref[rs, :] = dpi.astype(dgi_ref.dtype)
            s_a = s_a + jnp.sum(dpa, axis=0, keepdims=True)
            s_x = s_x + jnp.sum(dpi, axis=0, keepdims=True)
            s_l = s_l + jnp.sum(dlog_a * ra, axis=0, keepdims=True)
            return dh[0:1, :], a[0:1, :], s_a, s_x, s_l

        zero = jnp.zeros((1, tc), F32)
        _, _, s_a, s_x, s_l = lax.fori_loop(0, blocks, step, (zero, zero, zero, zero, zero))
        _accumulate_rows(pl.program_id(1) == 0, stat_ref, [s_a, s_x, s_l * (LRU_C * dlog_lam)])

    seq = pl.BlockSpec((s, tc), lambda j, bb: (bb, j))
    vec = pl.BlockSpec((1, tc), lambda j, bb: (0, j))
    half = jax.ShapeDtypeStruct((t, w), BF16)
    return pl.pallas_call(
        body, name=name, grid=(w // tc, b), in_specs=[seq, seq, seq, seq, seq, seq, vec, vec, vec],
        out_specs=[seq, seq, seq, seq, pl.BlockSpec((3, tc), lambda j, bb: (0, j))],
        out_shape=[half, half, half, jax.ShapeDtypeStruct((t, w), F32), jax.ShapeDtypeStruct((3, w), F32)],
        compiler_params=_params("parallel", "arbitrary"),
    )(dy, ga, gi, xr, yg, h, ba.reshape(1, w), bx.reshape(1, w), lam.reshape(1, w))


def _ple_mix(acc, x, gate):
    return x + jax.nn.sigmoid(gate) * acc


def ple_bwd(dx, gate, pin, w_proj, *, name):
    t, d = dx.shape
    k = pin.shape[1]
    tr = _tile(t, 512, SUBLANES)

    def body(dx_ref, g_ref, p_ref, w_ref, dg_ref, de_ref):
        emb = jnp.dot(p_ref[...].astype(BF16), w_ref[...], preferred_element_type=F32)
        sg = jax.nn.sigmoid(g_ref[...])
        dxv = dx_ref[...]
        de_ref[...] = (dxv * sg).astype(de_ref.dtype)
        dg_ref[...] = (dxv * emb * sg * (1.0 - sg)).astype(dg_ref.dtype)

    row = pl.BlockSpec((tr, d), lambda i: (i, 0))
    half = jax.ShapeDtypeStruct((t, d), BF16)
    return pl.pallas_call(
        body, name=name, grid=(t // tr,),
        in_specs=[row, row, pl.BlockSpec((tr, k), lambda i: (i, 0)), pl.BlockSpec((k, d), lambda i: (0, 0))],
        out_specs=[row, row], out_shape=[half, half], compiler_params=_params("parallel"))(dx, gate, pin, w_proj)


def adamw(w, g, m, v, *, name):
    shape = w.shape
    cols = shape[-1]
    rows = w.size // cols
    tr = _tile(rows, 1024, SUBLANES)
    bc1 = 1.0 / (1.0 - ADAM_B1 ** ADAM_STEP)
    bc2 = 1.0 / (1.0 - ADAM_B2 ** ADAM_STEP)

    def body(w_ref, g_ref, m_ref, v_ref, d_ref, nm_ref, nv_ref):
        gv = g_ref[...]
        nm = ADAM_B1 * m_ref[...] + (1.0 - ADAM_B1) * gv
        nv = ADAM_B2 * v_ref[...] + (1.0 - ADAM_B2) * (gv * gv)
        d_ref[...] = -ADAM_LR * ((nm * bc1) / (jnp.sqrt(nv * bc2) + ADAM_EPS) + ADAM_WD * w_ref[...])
        nm_ref[...] = nm
        nv_ref[...] = nv

    blk = pl.BlockSpec((tr, cols), lambda i: (i, 0))
    out = jax.ShapeDtypeStruct((rows, cols), F32)
    res = pl.pallas_call(body, name=name, grid=(rows // tr,), in_specs=[blk] * 4, out_specs=[blk] * 3,
                         out_shape=[out] * 3, compiler_params=_params("parallel"),
                         )(*[a.reshape(rows, cols) for a in (w, g, m, v)])
    return [r.reshape(shape) for r in res]


ANY = pl.BlockSpec(memory_space=pl.ANY)


def _place():
    return lax.axis_index("x"), lax.axis_index("y"), lax.axis_index("c")


class Rider(NamedTuple):
    inputs: tuple
    out_shapes: tuple
    scratch: tuple
    emit: Callable


def _when(cond):
    return (lambda fn: fn()) if cond is True else pl.when(cond)


def _rider_specs(rider, in_specs, out_specs, out_shape):
    if rider is None:
        return in_specs, out_specs, out_shape, []
    return (in_specs + [ANY] * len(rider.inputs), out_specs + [ANY] * len(rider.out_shapes),
            out_shape + list(rider.out_shapes), list(rider.scratch))


def _ride(rider, refs, n_in, n_out, grid):
    if rider is None:
        return
    ids = [pl.program_id(a) for a in range(len(grid))]
    first = functools.reduce(jnp.logical_and, [i == 0 for i in ids])
    last = functools.reduce(jnp.logical_and, [i == n - 1 for i, n in zip(ids, grid)])
    middle = functools.reduce(jnp.logical_and, [ids[0] == grid[0] // 2] + [i == 0 for i in ids[1:]])
    r_in = refs[n_in:n_in + len(rider.inputs)]
    at = n_in + len(rider.inputs) + n_out
    r_out = refs[at:at + len(rider.out_shapes)]
    rider.emit(first, middle, last, r_in, r_out, refs[at + len(rider.out_shapes):])


def _alone(rider, *, name):
    n_in = len(rider.inputs)

    def body(*refs):
        rider.emit(True, True, True, refs[:n_in], refs[n_in:n_in + len(rider.out_shapes)],
                   refs[n_in + len(rider.out_shapes):])

    return pl.pallas_call(body, name=name, out_shape=list(rider.out_shapes), in_specs=[ANY] * n_in,
                          out_specs=[ANY] * len(rider.out_shapes), scratch_shapes=list(rider.scratch))(*rider.inputs)


def gather_rider(v):
    rows, cols = v.shape

    def emit(first, middle, last, ins, outs, sems):
        (v_ref,), (out_ref,), (send_sems, recv_sems, local_sem) = ins, outs, sems
        x, y, c = _place()
        me, sibling = (x, y, c), (x, y, 1 - c)
        chips = [(1 - x, y), (x, 1 - y), (1 - x, 1 - y)]

        def slot(px, py, pc):
            return out_ref.at[4 * px + 2 * py + pc]

        def copy(k, block, to, src=None):
            return pltpu.make_async_remote_copy(
                src_ref=slot(*block) if src is None else src, dst_ref=slot(*block),
                send_sem=send_sems.at[k], recv_sem=recv_sems.at[k], device_id=to, device_id_type=MESH)

        mine = pltpu.make_async_copy(v_ref, slot(*me), local_sem)
        own = [copy(0, me, sibling, src=v_ref)]
        own += [copy(1 + j, me, (*chip, c), src=v_ref) for j, chip in enumerate(chips)]
        passed = [copy(4 + j, (*chip, c), sibling) for j, chip in enumerate(chips)]

        @_when(first)
        def _():
            mine.start()
            for cp in own:
                cp.start()

        @_when(middle)
        def _():
            for j, chip in enumerate(chips):
                copy(1 + j, (*chip, c), me).wait_recv()
                passed[j].start()

        @_when(last)
        def _():
            copy(0, sibling, me).wait_recv()
            for j, chip in enumerate(chips):
                copy(4 + j, (*chip, 1 - c), me).wait_recv()
            for cp in own + passed:
                cp.wait_send()
            mine.wait()

    return Rider((v,), (jax.ShapeDtypeStruct((N_DEV, rows, cols), v.dtype),),
                 (pltpu.SemaphoreType.DMA((7,)), pltpu.SemaphoreType.DMA((7,)), pltpu.SemaphoreType.DMA(())), emit)


def all_gather(v, *, name):
    return _alone(gather_rider(v), name=name)[0]


def sibling_rider(parts):
    _, quads, rows, cols = parts.shape

    def emit(first, middle, last, ins, outs, sems):
        (p_ref,), (got_ref,), (send_sem, recv_sem) = ins, outs, sems
        x, y, c = _place()
        cp = pltpu.make_async_remote_copy(src_ref=p_ref.at[1 - c], dst_ref=got_ref, send_sem=send_sem,
                                          recv_sem=recv_sem, device_id=(x, y, 1 - c), device_id_type=MESH)
        _when(first)(cp.start)
        _when(last)(cp.wait)

    return Rider((parts,), (jax.ShapeDtypeStruct((quads, rows, cols), parts.dtype),),
                 (pltpu.SemaphoreType.DMA(()), pltpu.SemaphoreType.DMA(())), emit)


def chip_rider(parts):
    _, rows, cols = parts.shape

    def emit(first, middle, last, ins, outs, sems):
        (p_ref,), (got_ref,), (send_sems, recv_sems) = ins, outs, sems
        x, y, c = _place()
        chips = [(1 - x, y), (x, 1 - y), (1 - x, 1 - y)]
        copies = [pltpu.make_async_remote_copy(
            src_ref=p_ref.at[2 * cx + cy], dst_ref=got_ref.at[k], send_sem=send_sems.at[k],
            recv_sem=recv_sems.at[k], device_id=(cx, cy, c), device_id_type=MESH)
            for k, (cx, cy) in enumerate(chips)]

        @_when(first)
        def _():
            for cp in copies:
                cp.start()

        @_when(last)
        def _():
            for cp in copies:
                cp.wait()

    return Rider((parts,), (jax.ShapeDtypeStruct((3, rows, cols), parts.dtype),),
                 (pltpu.SemaphoreType.DMA((3,)), pltpu.SemaphoreType.DMA((3,))), emit)


def add_sibling(parts, got, *, name):
    _, quads, rows, cols = parts.shape
    tr = _tile(rows, GRAD_ROWS_TILE, SUBLANES)

    def body(c_ref, p_ref, g_ref, o_ref, ob_ref):
        total = p_ref[...] + g_ref[...]
        o_ref[...] = total
        ob_ref[...] = total.astype(ob_ref.dtype)

    c = lax.axis_index("c").astype(jnp.int32).reshape(1)
    quad = pl.BlockSpec((None, tr, cols), lambda q, i, c_ref: (q, i, 0))
    return pl.pallas_call(
        body, name=name,
        grid_spec=pltpu.PrefetchScalarGridSpec(
            num_scalar_prefetch=1, grid=(quads, rows // tr),
            in_specs=[pl.BlockSpec((None, None, tr, cols), lambda q, i, c_ref: (c_ref[0], q, i, 0)), quad],
            out_specs=[quad, quad]),
        out_shape=[jax.ShapeDtypeStruct((quads, rows, cols), parts.dtype),
                   jax.ShapeDtypeStruct((quads, rows, cols), BF16)],
        compiler_params=_params("parallel", "parallel"),
    )(c, parts, got)


def add_chips(parts, got, *, name):
    _, rows, cols = parts.shape
    tr = _tile(rows, GRAD_ROWS_TILE, SUBLANES)

    def body(q_ref, p_ref, g_ref, o_ref):
        o_ref[...] = ((p_ref[...] + g_ref[0].astype(F32)) + g_ref[1].astype(F32)) + g_ref[2].astype(F32)

    q = (2 * lax.axis_index("x") + lax.axis_index("y")).astype(jnp.int32).reshape(1)
    return pl.pallas_call(
        body, name=name,
        grid_spec=pltpu.PrefetchScalarGridSpec(
            num_scalar_prefetch=1, grid=(rows // tr,),
            in_specs=[pl.BlockSpec((None, tr, cols), lambda i, q_ref: (q_ref[0], i, 0)),
                      pl.BlockSpec((3, tr, cols), lambda i, q_ref: (0, i, 0))],
            out_specs=pl.BlockSpec((tr, cols), lambda i, q_ref: (i, 0))),
        out_shape=jax.ShapeDtypeStruct((rows, cols), parts.dtype), compiler_params=_params("parallel"),
    )(q, parts, got)


def _pack(arrays, dtype, row_align):
    pieces, spans, at = [], [], 0
    for a in arrays:
        flat = a.reshape(-1).astype(dtype)
        rows = -(-flat.size // (LANES * row_align)) * row_align
        pieces.append(jnp.pad(flat, (0, rows * LANES - flat.size)).reshape(rows, LANES))
        spans.append((at, rows))
        at += rows
    return jnp.concatenate(pieces, axis=0), spans


def _unpack(buf, spans, shapes, lead):
    out = []
    for (at, rows), shape in zip(spans, shapes):
        size = math.prod(shape)
        piece = buf[..., at:at + rows, :].reshape(*lead, rows * LANES)[..., :size]
        out.append(piece.reshape(*lead, *shape))
    return out


def _whole(gathered, axis):
    moved = jnp.moveaxis(gathered, 0, axis)
    shape = moved.shape
    return moved.reshape(*shape[:axis], shape[axis] * shape[axis + 1], *shape[axis + 2:])


def _blocks(whole, axis):
    shape = whole.shape
    cut = whole.reshape(*shape[:axis], N_DEV, shape[axis] // N_DEV, *shape[axis + 1:])
    return jnp.moveaxis(cut, axis, 0)


def _block_diag(w):
    heads, n, _ = w.shape
    eye = jnp.eye(heads, dtype=w.dtype)
    return (w[:, :, None, :] * eye[:, None, :, None]).reshape(heads * n, heads * n)


def _diag_blocks(full, heads):
    n = full.shape[0] // heads
    return jnp.stack([full[h * n:(h + 1) * n, h * n:(h + 1) * n] for h in range(heads)])


def _pack_grads(pieces):
    sharded = [n for n in pieces if n in SHARD_AXIS]
    replicated = [n for n in pieces if n not in SHARD_AXIS]
    cut = [_blocks(pieces[n], SHARD_AXIS[n]).reshape(N_DEV, -1) for n in sharded]
    rep = jnp.concatenate([pieces[n].reshape(-1) for n in replicated])
    rep_len = rep.size
    rep_rows = -(-rep_len // (N_DEV * LANES * SUBLANES)) * SUBLANES
    rep = jnp.pad(rep, (0, N_DEV * rep_rows * LANES - rep_len)).reshape(N_DEV, rep_rows * LANES)
    bufs, spans, at = [], [], 0
    for a in cut + [rep]:
        rows = -(-a.shape[1] // (LANES * SUBLANES)) * SUBLANES
        bufs.append(jnp.pad(a, ((0, 0), (0, rows * LANES - a.shape[1]))).reshape(N_DEV, rows, LANES))
        spans.append((at, rows))
        at += rows
    tail = -at % GRAD_ROWS_TILE
    bufs.append(jnp.zeros((N_DEV, tail, LANES), F32))
    at += tail
    parts = jnp.concatenate(bufs, axis=1)
    parts = parts.reshape(4, 2, at, LANES).transpose(1, 0, 2, 3)
    shapes = [_blocks(pieces[n], SHARD_AXIS[n]).shape[1:] for n in sharded]
    return parts, (sharded, shapes, spans, replicated, [pieces[n].shape for n in replicated], rep_rows)


def _unpack_grads(mine, info):
    sharded, shapes, spans, _, _, rep_rows = info
    rep_at = spans[-1][0]
    return dict(zip(sharded, _unpack(mine, spans[:-1], shapes, ()))), mine[rep_at:rep_at + rep_rows]


LAYER_INDEXED = ['norm_mix', 'attn_w_qkv', 'attn_w_o', 'norm_ffn', 'ffn_w_up', 'ffn_conv_w', 'ffn_conv_b',
                 'ffn_w_down', 'norm_ple', 'ple_w_gate', 'ple_w_proj']


def kernel(x, p, norm_mix, attn_w_qkv, attn_w_o, rnn_w_in, rnn_conv_w, rnn_conv_b, rnn_w_gate_a, rnn_b_gate_a, rnn_w_gate_x, rnn_b_gate_x, rnn_lru_param, rnn_w_out, norm_ffn, ffn_w_up, ffn_conv_w, ffn_conv_b, ffn_w_down, norm_ple, ple_w_gate, ple_w_proj, norm_final, loss_target, m_norm_mix, m_attn_w_qkv, m_attn_w_o, m_rnn_w_in, m_rnn_conv_w, m_rnn_conv_b, m_rnn_w_gate_a, m_rnn_b_gate_a, m_rnn_w_gate_x, m_rnn_b_gate_x, m_rnn_lru_param, m_rnn_w_out, m_norm_ffn, m_ffn_w_up, m_ffn_conv_w, m_ffn_conv_b, m_ffn_w_down, m_norm_ple, m_ple_w_gate, m_ple_w_proj, m_norm_final, v_norm_mix, v_attn_w_qkv, v_attn_w_o, v_rnn_w_in, v_rnn_conv_w, v_rnn_conv_b, v_rnn_w_gate_a, v_rnn_b_gate_a, v_rnn_w_gate_x, v_rnn_b_gate_x, v_rnn_lru_param, v_rnn_w_out, v_norm_ffn, v_ffn_w_up, v_ffn_conv_w, v_ffn_conv_b, v_ffn_w_down, v_norm_ple, v_ple_w_gate, v_ple_w_proj, v_norm_final):
    given = dict(locals())
    local = {n: given[n] for n in WEIGHTS}
    bsz, seq, d = x.shape
    t = bsz * seq
    depth = norm_mix.shape[0]
    width = rnn_w_out.shape[1] * N_DEV
    ffn = ffn_w_down.shape[1] * N_DEV

    assert depth >= 2
    now = [(n, 0) for n in MATMUL_WEIGHTS if n in LAYER_INDEXED]
    later = [(n, j) for n in MATMUL_WEIGHTS for j in range(local[n].shape[0]) if (n, j) not in now]
    full = {n: [None] * local[n].shape[0] for n in MATMUL_WEIGHTS}

    def packed(group):
        return _pack([local[n][j] for n, j in group], BF16, 2 * SUBLANES)

    def place(group, gathered, spans):
        got = _unpack(gathered, spans, [local[n][j].shape for n, j in group], (N_DEV,))
        for (n, j), g in zip(group, got):
            full[n][j] = _whole(g, SHARD_AXIS[n] - 1)

    buf, spans = packed(now)
    place(now, all_gather(buf, name="gather_layer0_weights"), spans)
    later_buf, later_spans = packed(later)
    buf, spans = _pack([local[n] for n in CHANNEL_WEIGHTS], F32, SUBLANES)
    got = _unpack(all_gather(buf, name="gather_channel_weights"), spans,
                  [local[n].shape for n in CHANNEL_WEIGHTS], (N_DEV,))
    full.update({n: _whole(g, SHARD_AXIS[n]) for n, g in zip(CHANNEL_WEIGHTS, got)})
    for n in REPLICATED:
        full[n] = local[n]

    grads = {}

    def stack(name, layer, value, count):
        grads.setdefault(name, [None] * count)[layer] = value

    saved = []
    h0 = x.reshape(t, d)
    for i in range(depth):
        slot = i // 2
        sv = {"x0": h0}
        if i % 2 == 0:
            qkv, hn = mm_rms(h0, full["norm_mix"][i], full["attn_w_qkv"][slot], out_dtype=BF16, name=f"l{i}_norm_qkv")
            if i == 0:
                o, totals, gathered = attn_fwd(qkv, bsz, seq, d, name=f"l{i}_attn", rider=gather_rider(later_buf))
                place(later, gathered, later_spans)
            else:
                o, totals = attn_fwd(qkv, bsz, seq, d, name=f"l{i}_attn")
            h1 = mm(o, full["attn_w_o"][slot], extras=(h0,), epilogue=_add, name=f"l{i}_attn_out")
            sv.update(qkv=qkv, o=o, totals=totals)
        else:
            w_in = full["rnn_w_in"][slot]
            yg, hn = mm_rms(h0, full["norm_mix"][i], w_in[:, :width], name=f"l{i}_norm_rnn_in_gate")
            yr = mm(hn, w_in[:, width:], name=f"l{i}_rnn_in_rec")
            xr = rnn_conv_fwd(yr, full["rnn_conv_w"][slot], full["rnn_conv_b"][slot], seq, name=f"l{i}_rnn_conv")
            wa = _block_diag(full["rnn_w_gate_a"][slot]).astype(BF16)
            wx = _block_diag(full["rnn_w_gate_x"][slot]).astype(BF16)
            ga = mm(xr, wa, name=f"l{i}_rnn_gate_a")
            gi = mm(xr, wx, name=f"l{i}_rnn_gate_x")
            hs, y = rnn_scan_fwd(ga, gi, xr, yg, full["rnn_b_gate_a"][slot], full["rnn_b_gate_x"][slot],
                                 full["rnn_lru_param"][slot], bsz, seq, name=f"l{i}_rnn_scan")
            h1 = mm(y, full["rnn_w_out"][slot], extras=(h0,), epilogue=_add, name=f"l{i}_rnn_out")
            sv.update(yg=yg, yr=yr, xr=xr, wa=wa, wx=wx, ga=ga, gi=gi, hs=hs, y=y)
        sv.update(hn=hn, x1=h1)
        w_up = full["ffn_w_up"][i]
        ug, hn2 = mm_rms(h1, full["norm_ffn"][i], w_up[:, :ffn], out_dtype=BF16, name=f"l{i}_norm_ffn_up_gate")
        uv = mm(hn2, w_up[:, ffn:], out_dtype=BF16, name=f"l{i}_ffn_up_val")
        cw, cb = full["ffn_conv_w"][i], full["ffn_conv_b"][i]
        act = ffn_act_fwd(ug, uv, cw[:, :ffn], cw[:, ffn:], cb[:ffn], cb[ffn:], seq, name=f"l{i}_ffn_act")
        h2 = mm(act, full["ffn_w_down"][i], extras=(h1,), epilogue=_add, name=f"l{i}_ffn_down")
        sv.update(hn2=hn2, ug=ug, uv=uv, act=act, x2=h2)
        pg, hn3 = mm_rms(h2, full["norm_ple"][i], full["ple_w_gate"][i], name=f"l{i}_norm_ple_gate")
        pin = p[i].reshape(t, p.shape[-1])
        h0 = mm(pin, full["ple_w_proj"][i], extras=(h2, pg), epilogue=_ple_mix, name=f"l{i}_ple_proj_mix")
        sv.update(hn3=hn3, pg=pg, pin=pin)
        saved.append(sv)

    dx, g_final, loss_part = final_loss(h0, full["norm_final"], loss_target.reshape(t, d), name="final_loss")
    grads["norm_final"] = g_final
    loss = lax.psum(loss_part, ("x", "y", "c"))

    def later_layers(name):
        return name not in LAYER_INDEXED or len(grads.get(name, ())) > 1

    for i in reversed(range(depth)):
        slot = i // 2
        sv = saved[i]
        if i == 0:
            upper = {n: (jnp.stack(grads[n][1:]) if n in LAYER_INDEXED else
                         jnp.stack(grads[n]) if isinstance(grads[n], list) else grads[n])
                     for n in WEIGHTS if later_layers(n)}
            upper_parts, upper_info = _pack_grads(upper)
        dpg, dpe = ple_bwd(dx, sv["pg"], sv["pin"], full["ple_w_proj"][i], name=f"l{i}_ple_mix_bwd")
        stack("ple_w_proj", i, mm(sv["pin"], dpe, ta=True, name=f"l{i}_ple_proj_wgrad"), depth)
        stack("ple_w_gate", i, mm(sv["hn3"], dpg, ta=True, name=f"l{i}_ple_gate_wgrad"), depth)
        dx, gn = mm_rms_bwd(dpg, full["ple_w_gate"][i], sv["x2"], full["norm_ple"][i], dx,
                            name=f"l{i}_ple_gate_dgrad_norm_bwd")
        stack("norm_ple", i, gn, depth)
        stack("ffn_w_down", i, mm(sv["act"], dx, ta=True, name=f"l{i}_ffn_down_wgrad"), depth)
        dact = mm(dx, full["ffn_w_down"][i], tb=True, out_dtype=BF16, name=f"l{i}_ffn_down_dgrad")
        cw, cb = full["ffn_conv_w"][i], full["ffn_conv_b"][i]
        taps = cw.shape[0]
        ride = sibling_rider(upper_parts) if i == 0 else None
        dug, duv, sg, svv, *rode = ffn_act_bwd(sv["ug"], sv["uv"], cw[:, :ffn], cw[:, ffn:], cb[:ffn], cb[ffn:], dact,
                                               seq, name=f"l{i}_ffn_act_bwd", rider=ride)
        if i == 0:
            upper_sum, upper_sum_bf16 = add_sibling(upper_parts, rode[0], name="upper_grads_add_sibling")
        stack("ffn_conv_w", i, jnp.concatenate([sg[:taps], svv[:taps]], axis=1), depth)
        stack("ffn_conv_b", i, jnp.concatenate([sg[taps], svv[taps]], axis=0), depth)
        stack("ffn_w_up", i, jnp.concatenate(
            [mm(sv["hn2"], dug, ta=True, name=f"l{i}_ffn_up_wgrad_gate"),
             mm(sv["hn2"], duv, ta=True, name=f"l{i}_ffn_up_wgrad_val")], axis=1), depth)
        w_up = full["ffn_w_up"][i]
        dhn2 = mm(dug, w_up[:, :ffn], tb=True, name=f"l{i}_ffn_up_dgrad_gate")
        dx, gn = mm_rms_bwd(duv, w_up[:, ffn:], sv["x1"], full["norm_ffn"][i], dx, prev=dhn2,
                            name=f"l{i}_ffn_up_dgrad_val_norm_bwd")
        stack("norm_ffn", i, gn, depth)
        if i % 2 == 0:
            stack("attn_w_o", slot, mm(sv["o"], dx, ta=True, name=f"l{i}_attn_out_wgrad"), depth // 2)
            do = mm(dx, full["attn_w_o"][slot], tb=True, out_dtype=BF16, name=f"l{i}_attn_out_dgrad")
            ride = chip_rider(upper_sum_bf16) if i == 0 else None
            dq, dk, dv, *rode = attn_bwd(sv["qkv"], sv["totals"], do, bsz, seq, d, name=f"l{i}_attn_bwd", rider=ride)
            if i == 0:
                upper_mine = add_chips(upper_sum, rode[0], name="upper_grads_add_chips")
            dqkv = jnp.concatenate([dq, dk, dv], axis=1).astype(BF16)
            stack("attn_w_qkv", slot, mm(sv["hn"], dqkv, ta=True, name=f"l{i}_qkv_wgrad"), depth // 2)
            dx, gn = mm_rms_bwd(dqkv, full["attn_w_qkv"][slot], sv["x0"], full["norm_mix"][i], dx,
                                name=f"l{i}_qkv_dgrad_norm_bwd")
        else:
            nrnn = depth // 2
            stack("rnn_w_out", slot, mm(sv["y"], dx, ta=True, name=f"l{i}_rnn_out_wgrad"), nrnn)
            dy = mm(dx, full["rnn_w_out"][slot], tb=True, name=f"l{i}_rnn_out_dgrad")
            dyg, dga, dgi, dxr, stats = rnn_scan_bwd(
                dy, sv["ga"], sv["gi"], sv["xr"], sv["yg"], sv["hs"], full["rnn_b_gate_a"][slot],
                full["rnn_b_gate_x"][slot], full["rnn_lru_param"][slot], bsz, seq, name=f"l{i}_rnn_scan_bwd")
            stack("rnn_b_gate_a", slot, stats[0], nrnn)
            stack("rnn_b_gate_x", slot, stats[1], nrnn)
            stack("rnn_lru_param", slot, stats[2], nrnn)
            stack("rnn_w_gate_a", slot, _diag_blocks(mm(sv["xr"], dga, ta=True, name=f"l{i}_rnn_gate_a_wgrad"),
                                                     RNN_HEADS), nrnn)
            stack("rnn_w_gate_x", slot, _diag_blocks(mm(sv["xr"], dgi, ta=True, name=f"l{i}_rnn_gate_x_wgrad"),
                                                     RNN_HEADS), nrnn)
            dxr = mm(dga, sv["wa"], tb=True, extras=(dxr,), epilogue=_add, name=f"l{i}_rnn_gate_a_dgrad")
            dxr = mm(dgi, sv["wx"], tb=True, extras=(dxr,), epilogue=_add, name=f"l{i}_rnn_gate_x_dgrad")
            rcw = full["rnn_conv_w"][slot]
            rtaps = rcw.shape[0]
            cstats = rnn_conv_wgrad(dxr, sv["yr"], seq, rtaps, name=f"l{i}_rnn_conv_wgrad")
            stack("rnn_conv_w", slot, cstats[:rtaps], nrnn)
            stack("rnn_conv_b", slot, cstats[rtaps], nrnn)
            dyr = conv_input_grad(dxr, rcw, seq, out_dtype=BF16, name=f"l{i}_rnn_conv_bwd")
            stack("rnn_w_in", slot, jnp.concatenate(
                [mm(sv["hn"], dyg, ta=True, name=f"l{i}_rnn_in_wgrad_gate"),
                 mm(sv["hn"], dyr, ta=True, name=f"l{i}_rnn_in_wgrad_rec")], axis=1), nrnn)
            w_in = full["rnn_w_in"][slot]
            dhn = mm(dyg, w_in[:, :width], tb=True, name=f"l{i}_rnn_in_dgrad_gate")
            dx, gn = mm_rms_bwd(dyr, w_in[:, width:], sv["x0"], full["norm_mix"][i], dx, prev=dhn,
                                name=f"l{i}_rnn_in_dgrad_rec_norm_bwd")
        stack("norm_mix", i, gn, depth)
    grad_x = dx.reshape(bsz, seq, d)

    lower = {n: grads[n][0][None] for n in LAYER_INDEXED}
    lower_parts, lower_info = _pack_grads(lower)
    from_sibling = _alone(sibling_rider(lower_parts), name="grads_to_sibling")[0]
    lower_sum, lower_sum_bf16 = add_sibling(lower_parts, from_sibling, name="grads_add_sibling")
    from_chips = _alone(chip_rider(lower_sum_bf16), name="grads_to_chips")[0]
    lower_mine = add_chips(lower_sum, from_chips, name="grads_add_chips")
    upper_local, upper_rep = _unpack_grads(upper_mine, upper_info)
    lower_local, lower_rep = _unpack_grads(lower_mine, lower_info)
    local_grads = {n: (jnp.concatenate([lower_local[n], upper_local[n]], axis=0) if n in upper_local
                       else lower_local[n]) if n in lower_local else upper_local[n]
                   for n in WEIGHTS if n in SHARD_AXIS}
    rep_all = all_gather(jnp.concatenate([upper_rep, lower_rep], axis=0), name="gather_replicated_grads")
    rep_vecs = {}
    for key, info, rows_at in (("upper", upper_info, 0), ("lower", lower_info, upper_rep.shape[0])):
        vec = rep_all[:, rows_at:rows_at + info[5]].reshape(-1)
        at = 0
        for n, shape in zip(info[3], info[4]):
            rep_vecs[key, n] = vec[at:at + math.prod(shape)].reshape(shape)
            at += math.prod(shape)
    for n in REPLICATED:
        both = [rep_vecs[k, n] for k in ("lower", "upper") if (k, n) in rep_vecs]
        local_grads[n] = both[0] if n not in LAYER_INDEXED else jnp.concatenate(both, axis=0)

    deltas, new_m, new_v = {}, {}, {}
    for n in WEIGHTS:
        deltas[n], new_m[n], new_v[n] = adamw(local[n], local_grads[n], given["m_" + n], given["v_" + n],
                                              name=f"adamw_{n}")
    return (loss, grad_x, *[local_grads[n] for n in WEIGHTS], *[deltas[n] for n in WEIGHTS],
            *[new_m[n] for n in WEIGHTS], *[new_v[n] for n in WEIGHTS])
```

```python
import functools
import math
from typing import Callable, NamedTuple

import jax
import jax.numpy as jnp
from jax import lax
from jax.experimental import pallas as pl
from jax.experimental.pallas import tpu as pltpu

F32 = jnp.float32
BF16 = jnp.bfloat16

EPS = 1e-6
HEAD_DIM = 64
RNN_HEADS = 16
LRU_C = 8.0
ADAM_LR = 0.001
ADAM_B1 = 0.9
ADAM_B2 = 0.999
ADAM_EPS = 1e-08
ADAM_WD = 0.01
ADAM_STEP = 10

N_DEV = 8
LANES = 128
SUBLANES = 8
VMEM_LIMIT = 56 * 1024 * 1024
MESH = pl.DeviceIdType.MESH
GRAD_ROWS_TILE = 2048
GELU_C = math.sqrt(2.0 / math.pi)
GELU_A = 0.044715

WEIGHTS = ['norm_mix', 'attn_w_qkv', 'attn_w_o', 'rnn_w_in', 'rnn_conv_w', 'rnn_conv_b', 'rnn_w_gate_a',
           'rnn_b_gate_a', 'rnn_w_gate_x', 'rnn_b_gate_x', 'rnn_lru_param', 'rnn_w_out', 'norm_ffn', 'ffn_w_up',
           'ffn_conv_w', 'ffn_conv_b', 'ffn_w_down', 'norm_ple', 'ple_w_gate', 'ple_w_proj', 'norm_final']
SHARD_AXIS = {'attn_w_qkv': 2, 'attn_w_o': 1, 'rnn_w_in': 2, 'rnn_conv_w': 2, 'rnn_conv_b': 1, 'rnn_b_gate_a': 1,
              'rnn_b_gate_x': 1, 'rnn_lru_param': 1, 'rnn_w_out': 1, 'ffn_w_up': 2, 'ffn_conv_w': 2,
              'ffn_w_down': 1, 'ple_w_gate': 1, 'ple_w_proj': 2}
MATMUL_WEIGHTS = ['attn_w_qkv', 'attn_w_o', 'rnn_w_in', 'rnn_w_out', 'ffn_w_up', 'ffn_w_down', 'ple_w_gate',
                  'ple_w_proj']
CHANNEL_WEIGHTS = ['rnn_conv_w', 'rnn_conv_b', 'rnn_b_gate_a', 'rnn_b_gate_x', 'rnn_lru_param', 'ffn_conv_w']
REPLICATED = [n for n in WEIGHTS if n not in SHARD_AXIS]


def _params(*sem):
    return pltpu.CompilerParams(dimension_semantics=sem, vmem_limit_bytes=VMEM_LIMIT)


def _tile(dim, pref, align=LANES):
    if dim <= pref:
        return dim
    t = (pref + pref // 2) // align * align
    while t >= align:
        if dim % t == 0:
            return t
        t -= align
    return dim


def _gelu(x):
    return 0.5 * x * (1.0 + jnp.tanh(GELU_C * (x + GELU_A * x * x * x)))


def _gelu_and_grad(x):
    t = jnp.tanh(GELU_C * (x + GELU_A * x * x * x))
    g = 0.5 * x * (1.0 + t)
    dg = 0.5 * (1.0 + t) + 0.5 * x * (1.0 - t * t) * GELU_C * (1.0 + 3.0 * GELU_A * x * x)
    return g, dg


def _log_sigmoid(x):
    return jnp.minimum(x, 0.0) - jnp.log(1.0 + jnp.exp(-jnp.abs(x)))


MM_VMEM_BUDGET = 36 * 1024 * 1024


def _mm_tiles(m, n, k, ta, a_item, b_item, out_item, n_extra):
    if ta:
        return _tile(m, 1024), _tile(n, 1024), _tile(k, 1024)
    row_bytes = k * a_item + n * (out_item + 4 * n_extra)
    w_bytes = k * n * b_item
    for tm in (1024, 512, 256, 128):
        if m % tm == 0 and 2 * tm * row_bytes + 2 * w_bytes + tm * n * 4 <= MM_VMEM_BUDGET:
            return tm, n, k
    return _tile(m, 512), _tile(n, 512), _tile(k, 1024)


def mm(a, b, *, name, ta=False, tb=False, out_dtype=F32, extras=(), epilogue=None):
    m, k = (a.shape[1], a.shape[0]) if ta else a.shape
    n = b.shape[0] if tb else b.shape[1]
    assert k == (b.shape[1] if tb else b.shape[0]), (a.shape, b.shape, ta, tb)
    tm, tn, tk = _mm_tiles(m, n, k, ta, a.dtype.itemsize, b.dtype.itemsize, jnp.dtype(out_dtype).itemsize,
                           len(extras))
    nk = k // tk
    n_extra = len(extras)
    dims = (((0 if ta else 1,), (1 if tb else 0,)), ((), ()))

    def body(a_ref, b_ref, *rest):
        extra_refs, o_ref = rest[:n_extra], rest[n_extra]

        def finish(acc):
            if epilogue is not None:
                acc = epilogue(acc, *[e[...] for e in extra_refs])
            o_ref[...] = acc.astype(o_ref.dtype)

        part = lax.dot_general(a_ref[...].astype(BF16), b_ref[...].astype(BF16), dims,
                               preferred_element_type=F32)
        if nk == 1:
            finish(part)
        else:
            acc_ref = rest[n_extra + 1]
            kk = pl.program_id(2)

            @pl.when(kk == 0)
            def _():
                acc_ref[...] = part

            @pl.when(kk > 0)
            def _():
                acc_ref[...] += part

            @pl.when(kk == nk - 1)
            def _():
                finish(acc_ref[...])

    a_spec = pl.BlockSpec((tk, tm), lambda i, j, kk: (kk, i)) if ta else pl.BlockSpec((tm, tk), lambda i, j, kk: (i, kk))
    b_spec = pl.BlockSpec((tn, tk), lambda i, j, kk: (j, kk)) if tb else pl.BlockSpec((tk, tn), lambda i, j, kk: (kk, j))
    o_spec = pl.BlockSpec((tm, tn), lambda i, j, kk: (i, j))
    return pl.pallas_call(
        body, name=name, grid=(m // tm, n // tn, nk),
        in_specs=[a_spec, b_spec] + [o_spec] * n_extra, out_specs=o_spec,
        out_shape=jax.ShapeDtypeStruct((m, n), out_dtype),
        scratch_shapes=[pltpu.VMEM((tm, tn), F32)] if nk > 1 else [],
        compiler_params=_params("parallel", "parallel", "arbitrary"),
    )(a, b, *extras)


def _add(acc, res):
    return acc + res


def mm_rms(x, g, w, *, name, out_dtype=F32):
    t, d = x.shape
    n = w.shape[1]
    out_item = jnp.dtype(out_dtype).itemsize
    row_bytes = d * 4 + d * 2 + n * out_item
    tr = next(tm for tm in (512, 256, 128, t)
              if t % tm == 0 and 2 * tm * row_bytes + 2 * w.size * w.dtype.itemsize + tm * n * 4 <= MM_VMEM_BUDGET)

    def body(x_ref, g_ref, w_ref, o_ref, hn_ref):
        xv = x_ref[...]
        r = lax.rsqrt(jnp.mean(xv * xv, axis=-1, keepdims=True) + EPS)
        hn = (xv * r * g_ref[...]).astype(BF16)
        hn_ref[...] = hn
        o_ref[...] = jnp.dot(hn, w_ref[...].astype(BF16), preferred_element_type=F32).astype(o_ref.dtype)

    row = pl.BlockSpec((tr, d), lambda i: (i, 0))
    return pl.pallas_call(
        body, name=name, grid=(t // tr,),
        in_specs=[row, pl.BlockSpec((1, d), lambda i: (0, 0)), pl.BlockSpec((d, n), lambda i: (0, 0))],
        out_specs=[pl.BlockSpec((tr, n), lambda i: (i, 0)), row],
        out_shape=[jax.ShapeDtypeStruct((t, n), out_dtype), jax.ShapeDtypeStruct((t, d), BF16)],
        compiler_params=_params("parallel"),
    )(x, g.reshape(1, d), w)


def mm_rms_bwd(a, w, x, g, dres, *, name, prev=None):
    t, k = a.shape
    d = w.shape[0]
    assert w.shape[1] == k and x.shape == (t, d)
    n_rows = 4 if prev is not None else 3
    row_bytes = k * a.dtype.itemsize + d * 4 * n_rows
    tr = next(tm for tm in (512, 256, 128, t)
              if t % tm == 0 and 2 * tm * row_bytes + 2 * w.size * w.dtype.itemsize + tm * d * 4 <= MM_VMEM_BUDGET)

    def body(a_ref, w_ref, x_ref, g_ref, dres_ref, *rest):
        dx_ref, dg_ref = rest[-2:]
        dhv = lax.dot_general(a_ref[...].astype(BF16), w_ref[...].astype(BF16), _NT, preferred_element_type=F32)
        if prev is not None:
            dhv = dhv + rest[0][...]
        xv = x_ref[...]
        r = lax.rsqrt(jnp.mean(xv * xv, axis=-1, keepdims=True) + EPS)
        xh = xv * r
        u = dhv * g_ref[...]
        dx_ref[...] = dres_ref[...] + r * (u - xh * jnp.mean(u * xh, axis=-1, keepdims=True))
        part = jnp.sum(dhv * xh, axis=0, keepdims=True)

        @pl.when(pl.program_id(0) == 0)
        def _():
            dg_ref[...] = part

        @pl.when(pl.program_id(0) > 0)
        def _():
            dg_ref[...] += part

    row = pl.BlockSpec((tr, d), lambda i: (i, 0))
    vec = pl.BlockSpec((1, d), lambda i: (0, 0))
    extra = [prev] if prev is not None else []
    dx, dg = pl.pallas_call(
        body, name=name, grid=(t // tr,),
        in_specs=[pl.BlockSpec((tr, k), lambda i: (i, 0)), pl.BlockSpec((d, k), lambda i: (0, 0)), row, vec, row]
        + [row] * len(extra),
        out_specs=[row, vec],
        out_shape=[jax.ShapeDtypeStruct((t, d), F32), jax.ShapeDtypeStruct((1, d), F32)],
        compiler_params=_params("arbitrary"),
    )(a, w, x, g.reshape(1, d), dres, *extra)
    return dx, dg.reshape(d)


def final_loss(x, g, target, *, name):
    t, d = x.shape
    tr = _tile(t, 512, SUBLANES)

    def body(x_ref, g_ref, t_ref, dx_ref, dg_ref, loss_ref):
        xv = x_ref[...]
        gv = g_ref[...]
        r = lax.rsqrt(jnp.mean(xv * xv, axis=-1, keepdims=True) + EPS)
        xh = xv * r
        err = xh * gv - t_ref[...]
        dy = err * (1.0 / d)
        u = dy * gv
        dx_ref[...] = r * (u - xh * jnp.mean(u * xh, axis=-1, keepdims=True))
        dg_part = jnp.sum(dy * xh, axis=0, keepdims=True)
        loss_part = jnp.zeros((1, LANES), F32) + (0.5 / d) * jnp.sum(err * err)

        @pl.when(pl.program_id(0) == 0)
        def _():
            dg_ref[...] = dg_part
            loss_ref[...] = loss_part

        @pl.when(pl.program_id(0) > 0)
        def _():
            dg_ref[...] += dg_part
            loss_ref[...] += loss_part

    row = pl.BlockSpec((tr, d), lambda i: (i, 0))
    vec = pl.BlockSpec((1, d), lambda i: (0, 0))
    dx, dg, loss = pl.pallas_call(
        body, name=name, grid=(t // tr,), in_specs=[row, vec, row],
        out_specs=[row, vec, pl.BlockSpec((1, LANES), lambda i: (0, 0))],
        out_shape=[jax.ShapeDtypeStruct((t, d), F32), jax.ShapeDtypeStruct((1, d), F32),
                   jax.ShapeDtypeStruct((1, LANES), F32)],
        compiler_params=_params("arbitrary"),
    )(x, g.reshape(1, d), target)
    return dx, dg.reshape(d), loss[0, 0]


def _split_dot(x, mat, left):
    hi = x.astype(BF16)
    lo = (x - hi.astype(F32)).astype(BF16)
    if left:
        return (jnp.dot(mat, hi, preferred_element_type=F32) + jnp.dot(mat, lo, preferred_element_type=F32))
    return (jnp.dot(hi, mat, preferred_element_type=F32) + jnp.dot(lo, mat, preferred_element_type=F32))


_NT = (((1,), (1,)), ((), ()))
_TN = (((0,), (0,)), ((), ()))
HEADS_PER_STEP = LANES // HEAD_DIM


def attn_fwd(qkv, b, s, d, *, name, rider=None):
    t = b * s
    tq = min(256, s)
    nq = s // tq
    pairs = d // LANES
    scale = HEAD_DIM ** -0.5

    grid = (b, pairs, nq)
    n_ride = len(rider.inputs) if rider else 0

    def body(*refs):
        q_ref, k_ref, v_ref = refs[:3]
        o_ref, lt_ref = refs[3 + n_ride:5 + n_ride]
        _ride(rider, refs, 3, 2, grid)
        i = pl.program_id(2)
        row = lax.broadcasted_iota(jnp.int32, (tq, tq), 0)
        col = lax.broadcasted_iota(jnp.int32, (tq, tq), 1)
        later = (row > col).astype(BF16)
        causal = col < row
        lanes = [slice(HEAD_DIM * h, HEAD_DIM * (h + 1)) for h in range(HEADS_PER_STEP)]
        qs = [(q_ref[:, sl].astype(F32) * scale).astype(BF16) for sl in lanes]

        def block(js, carry, diag):
            starts = [pl.multiple_of(j * tq, tq) for j in js]
            hs = range(HEADS_PER_STEP)
            chains = [(n, h) for n in range(len(js)) for h in hs]
            masked = [c for c in chains if diag and c[0] == 0]
            kbs = {(n, h): k_ref[pl.ds(starts[n], tq), lanes[h]] for n, h in chains}
            vbs = {(n, h): v_ref[pl.ds(starts[n], tq), lanes[h]] for n, h in chains}
            zs = {c: lax.dot_general(qs[c[1]], kbs[c], _NT, preferred_element_type=F32) for c in chains}
            lss = {c: _log_sigmoid(zs[c]) for c in chains}
            lks = {c: lss[c] - zs[c] for c in chains}
            for c in masked:
                lks[c] = jnp.where(causal, lks[c], 0.0)
            sums = {c: _split_dot(lks[c], later, left=False) for c in chains}
            runs_in, runs = {}, []
            for h in hs:
                run = carry[h][0]
                for n in range(len(js)):
                    runs_in[n, h] = run
                    run = run + jnp.sum(lks[n, h], axis=1, keepdims=True)
                runs.append(run)
            ws = {c: jnp.exp(lss[c] + sums[c] + runs_in[c]) for c in chains}
            for c in masked:
                ws[c] = jnp.where(causal, ws[c], 0.0)
            pvs = {c: jnp.dot(ws[c].astype(BF16), vbs[c], preferred_element_type=F32) for c in chains}
            accs = [carry[h][1] + sum(pvs[n, h] for n in range(len(js))) for h in hs]
            return tuple(zip(runs, accs))

        zero = (jnp.zeros((tq, 1), F32), jnp.zeros((tq, HEAD_DIM), F32))
        odd = i % 2
        carry = lax.cond(odd == 1, lambda c: block([i, i - 1], c, True), lambda c: block([i], c, True),
                         (zero,) * HEADS_PER_STEP)
        near = i - 1 - odd
        carry = lax.fori_loop(0, i // 2, lambda n, c: block([near - 2 * n, near - 2 * n - 1], c, False), carry)
        eye = (row == col).astype(F32)
        for h, sl in enumerate(lanes):
            run, acc = carry[h]
            o_ref[:, sl] = acc.astype(o_ref.dtype)
            lt_ref[SUBLANES * h:SUBLANES * (h + 1), :] = lax.dot_general(
                jnp.broadcast_to(run, (tq, SUBLANES)), eye, _TN, precision=lax.Precision.HIGHEST,
                preferred_element_type=F32)

    q_spec = pl.BlockSpec((tq, LANES), lambda bb, p, i: (bb * nq + i, p))
    k_spec = pl.BlockSpec((s, LANES), lambda bb, p, i: (bb, pairs + p))
    v_spec = pl.BlockSpec((s, LANES), lambda bb, p, i: (bb, 2 * pairs + p))
    lt_spec = pl.BlockSpec((None, None, None, HEADS_PER_STEP * SUBLANES, tq), lambda bb, p, i: (bb, p, i, 0, 0))
    in_specs, out_specs, out_shape, scratch = _rider_specs(
        rider, [q_spec, k_spec, v_spec], [q_spec, lt_spec],
        [jax.ShapeDtypeStruct((t, d), BF16),
         jax.ShapeDtypeStruct((b, pairs, nq, HEADS_PER_STEP * SUBLANES, tq), F32)])
    order = ("arbitrary",) * 3 if rider else ("parallel", "parallel", "arbitrary")
    return pl.pallas_call(
        body, name=name, grid=grid, in_specs=in_specs, out_specs=out_specs, out_shape=out_shape,
        scratch_shapes=scratch, compiler_params=_params(*order),
    )(qkv, qkv, qkv, *(rider.inputs if rider else ()))


def attn_bwd(qkv, totals, do, b, s, d, *, name, rider=None):
    t = b * s
    tq = min(256, s)
    nq = s // tq
    pairs = d // LANES
    scale = HEAD_DIM ** -0.5

    grid = (b, pairs, nq)
    n_ride = len(rider.inputs) if rider else 0

    def body(*refs):
        q_ref, k_ref, v_ref, lt_ref, do_ref = refs[:5]
        dq_ref, dk_ref, dv_ref = refs[5 + n_ride:8 + n_ride]
        _ride(rider, refs, 5, 3, grid)
        i = pl.program_id(2)

        @pl.when(i == 0)
        def _():
            dk_ref[...] = jnp.zeros_like(dk_ref)
            dv_ref[...] = jnp.zeros_like(dv_ref)

        row = lax.broadcasted_iota(jnp.int32, (tq, tq), 0)
        col = lax.broadcasted_iota(jnp.int32, (tq, tq), 1)
        upto = (col <= row).astype(BF16)
        earlier = (col < row).astype(BF16)
        causal = row < col
        lanes = [slice(HEAD_DIM * h, HEAD_DIM * (h + 1)) for h in range(HEADS_PER_STEP)]
        qs = [(q_ref[:, sl].astype(F32) * scale).astype(BF16) for sl in lanes]
        dos = [do_ref[:, sl].astype(BF16) for sl in lanes]
        totals_h = [lt_ref[SUBLANES * h:SUBLANES * h + 1, :] for h in range(HEADS_PER_STEP)]

        def block(js, carry, diag):
            starts = [pl.multiple_of(j * tq, tq) for j in js]
            hs = range(HEADS_PER_STEP)
            ns = range(len(js))
            chains = [(n, h) for n in ns for h in hs]
            masked = [c for c in chains if diag and c[0] == len(js) - 1]
            kbs = {(n, h): k_ref[pl.ds(starts[n], tq), lanes[h]] for n, h in chains}
            vbs = {(n, h): v_ref[pl.ds(starts[n], tq), lanes[h]] for n, h in chains}
            zs = {c: lax.dot_general(kbs[c], qs[c[1]], _NT, preferred_element_type=F32) for c in chains}
            dws = {c: lax.dot_general(vbs[c], dos[c[1]], _NT, preferred_element_type=F32) for c in chains}
            lss = {c: _log_sigmoid(zs[c]) for c in chains}
            lks = {c: lss[c] - zs[c] for c in chains}
            for c in masked:
                lks[c] = jnp.where(causal, lks[c], 0.0)
            sums = {c: _split_dot(lks[c], upto, left=True) for c in chains}
            runs_in, runs = {}, []
            for h in hs:
                run = carry[h][0]
                for n in ns:
                    runs_in[n, h] = run
                    run = run + jnp.sum(lks[n, h], axis=0, keepdims=True)
                runs.append(run)
            ws = {c: jnp.exp(lss[c] + ((totals_h[c[1]] - runs_in[c]) - sums[c])) for c in chains}
            for c in masked:
                ws[c] = jnp.where(causal, ws[c], 0.0)
            gs = {c: dws[c] * ws[c] for c in chains}
            gsums = {c: _split_dot(gs[c], earlier, left=True) for c in chains}
            gruns_in, gruns = {}, []
            for h in hs:
                grun = carry[h][1]
                for n in ns:
                    gruns_in[n, h] = grun
                    grun = grun + jnp.sum(gs[n, h], axis=0, keepdims=True)
                gruns.append(grun)
            dzs = {c: gs[c] - jnp.exp(lss[c]) * (gs[c] + (gruns_in[c] + gsums[c])) for c in chains}
            for c in masked:
                dzs[c] = jnp.where(causal, dzs[c], 0.0)
            dzbs = {c: dzs[c].astype(BF16) for c in chains}
            for n, h in chains:
                dv_ref[pl.ds(starts[n], tq), lanes[h]] += jnp.dot(ws[n, h].astype(BF16), dos[h],
                                                                  preferred_element_type=F32)
                dk_ref[pl.ds(starts[n], tq), lanes[h]] += jnp.dot(dzbs[n, h], qs[h], preferred_element_type=F32)
            dqs = [carry[h][2] + sum(lax.dot_general(dzbs[n, h], kbs[n, h], _TN, preferred_element_type=F32)
                                     for n in ns) for h in hs]
            return tuple(zip(runs, gruns, dqs))

        zero = (jnp.zeros((1, tq), F32), jnp.zeros((1, tq), F32), jnp.zeros((tq, HEAD_DIM), F32))
        carry = lax.fori_loop(0, i // 2, lambda n, c: block([2 * n, 2 * n + 1], c, False),
                              (zero,) * HEADS_PER_STEP)
        carry = lax.cond(i % 2 == 1, lambda c: block([i - 1, i], c, True), lambda c: block([i], c, True), carry)
        for h, sl in enumerate(lanes):
            dq_ref[:, sl] = carry[h][2] * scale

    q_spec = pl.BlockSpec((tq, LANES), lambda bb, p, i: (bb * nq + i, p))
    k_spec = pl.BlockSpec((s, LANES), lambda bb, p, i: (bb, pairs + p))
    v_spec = pl.BlockSpec((s, LANES), lambda bb, p, i: (bb, 2 * pairs + p))
    lt_spec = pl.BlockSpec((None, None, None) + totals.shape[3:], lambda bb, p, i: (bb, p, i, 0, 0))
    kv_out = pl.BlockSpec((s, LANES), lambda bb, p, i: (bb, p))
    out = jax.ShapeDtypeStruct((t, d), F32)
    in_specs, out_specs, out_shape, scratch = _rider_specs(
        rider, [q_spec, k_spec, v_spec, lt_spec, q_spec], [q_spec, kv_out, kv_out], [out, out, out])
    order = ("arbitrary",) * 3 if rider else ("parallel", "parallel", "arbitrary")
    return pl.pallas_call(
        body, name=name, grid=grid, in_specs=in_specs, out_specs=out_specs, out_shape=out_shape,
        scratch_shapes=scratch, compiler_params=_params(*order),
    )(qkv, qkv, qkv, totals, do, *(rider.inputs if rider else ()))


def _shift_down(cur, prev8, dist):
    ext = jnp.concatenate([prev8, cur], axis=0)
    return pltpu.roll(ext, dist, 0)[SUBLANES:]


def _shift_up(cur, next8, dist):
    ext = jnp.concatenate([cur, next8], axis=0)
    return pltpu.roll(ext, ext.shape[0] - dist, 0)[:cur.shape[0]]


def _causal_conv(cur, prev8, w_ref, b_ref):
    taps = w_ref.shape[0]
    out = cur * w_ref[taps - 1:taps, :] + b_ref[...]
    for dist in range(1, taps):
        out = out + _shift_down(cur, prev8, dist) * w_ref[taps - 1 - dist:taps - dist, :]
    return out


def _conv_specs(t, rows, tc, time_axis, dtype=F32):
    sub = SUBLANES * (4 // jnp.dtype(dtype).itemsize)
    per = rows // sub
    last = t // sub - 1

    def grid_ids(*ids):
        return ids[time_axis], ids[1 - time_axis]

    def cur(*ids):
        return grid_ids(*ids)

    def prev(*ids):
        i, j = grid_ids(*ids)
        return (jnp.maximum(i * per - 1, 0), j)

    def nxt(*ids):
        i, j = grid_ids(*ids)
        return (jnp.minimum((i + 1) * per, last), j)

    def chan(*ids):
        return (0, grid_ids(*ids)[1])

    return pl.BlockSpec((rows, tc), cur), pl.BlockSpec((sub, tc), prev), pl.BlockSpec((sub, tc), nxt), chan


def _rows_before(ref, keep):
    return ref[...].astype(F32)[-SUBLANES:] * keep


def _rows_after(ref, keep):
    return ref[...].astype(F32)[:SUBLANES] * keep


def _first_in_seq(i, rows, s):
    return (i % (s // rows)) == 0


def _last_in_seq(i, rows, s):
    return (i % (s // rows)) == (s // rows - 1)


FFN_ACT_ROWS, FFN_ACT_COLS = 256, 1024


def ffn_act_fwd(ug, uv, cwg, cwv, cbg, cbv, s, *, name):
    t, f = ug.shape
    rows, tc = _tile(s, FFN_ACT_ROWS, SUBLANES), _tile(f, FFN_ACT_COLS)
    cur, prev, _, chan = _conv_specs(t, rows, tc, 0, ug.dtype)
    taps = cwg.shape[0]

    def body(ug_ref, ugp_ref, uv_ref, uvp_ref, cwg_ref, cwv_ref, cbg_ref, cbv_ref, a_ref):
        keep = jnp.where(_first_in_seq(pl.program_id(0), rows, s), 0.0, 1.0)
        gate = _causal_conv(ug_ref[...].astype(F32), _rows_before(ugp_ref, keep), cwg_ref, cbg_ref)
        val = _causal_conv(uv_ref[...].astype(F32), _rows_before(uvp_ref, keep), cwv_ref, cbv_ref)
        a_ref[...] = (_gelu(gate) * val).astype(a_ref.dtype)

    wspec = pl.BlockSpec((taps, tc), chan)
    bspec = pl.BlockSpec((1, tc), chan)
    return pl.pallas_call(
        body, name=name, grid=(t // rows, f // tc), in_specs=[cur, prev, cur, prev, wspec, wspec, bspec, bspec],
        out_specs=cur, out_shape=jax.ShapeDtypeStruct((t, f), BF16), compiler_params=_params("parallel", "parallel"),
    )(ug, ug, uv, uv, cwg, cwv, cbg.reshape(1, f), cbv.reshape(1, f))


def _accumulate_rows(first, ref, rows):
    for k, r in enumerate(rows):
        @pl.when(first)
        def _(k=k, r=r):
            ref[k:k + 1, :] = r

        @pl.when(jnp.logical_not(first))
        def _(k=k, r=r):
            ref[k:k + 1, :] += r


def _conv_weight_grads(dc, cur, prev8, taps):
    out = []
    for k in range(taps):
        dist = taps - 1 - k
        xs = cur if dist == 0 else _shift_down(cur, prev8, dist)
        out.append(jnp.sum(dc * xs, axis=0, keepdims=True))
    out.append(jnp.sum(dc, axis=0, keepdims=True))
    return out


def _conv_transpose(dc_ext, rows, w_ref):
    taps = w_ref.shape[0]
    out = dc_ext[:rows] * w_ref[taps - 1:taps, :]
    for dist in range(1, taps):
        out = out + pltpu.roll(dc_ext, dc_ext.shape[0] - dist, 0)[:rows] * w_ref[taps - 1 - dist:taps - dist, :]
    return out


def ffn_act_bwd(ug, uv, cwg, cwv, cbg, cbv, da, s, *, name, rider=None):
    t, f = ug.shape
    rows, tc = _tile(s, FFN_ACT_ROWS, SUBLANES), _tile(f, FFN_ACT_COLS)
    cur, prev, nxt, chan = _conv_specs(t, rows, tc, 1, ug.dtype)
    taps = cwg.shape[0]

    grid = (f // tc, t // rows)
    n_ride = len(rider.inputs) if rider else 0

    def body(*refs):
        (ug_ref, ugp_ref, ugn_ref, uv_ref, uvp_ref, uvn_ref, cwg_ref, cwv_ref, cbg_ref, cbv_ref,
         da_ref, dan_ref) = refs[:12]
        dug_ref, duv_ref, wg_ref, wv_ref = refs[12 + n_ride:16 + n_ride]
        _ride(rider, refs, 12, 4, grid)
        i = pl.program_id(1)
        keep_before = jnp.where(_first_in_seq(i, rows, s), 0.0, 1.0)
        keep_after = jnp.where(_last_in_seq(i, rows, s), 0.0, 1.0)
        ugp, uvp = _rows_before(ugp_ref, keep_before), _rows_before(uvp_ref, keep_before)
        uge = jnp.concatenate([ug_ref[...].astype(F32), _rows_after(ugn_ref, 1.0)], axis=0)
        uve = jnp.concatenate([uv_ref[...].astype(F32), _rows_after(uvn_ref, 1.0)], axis=0)
        dae = jnp.concatenate([da_ref[...].astype(F32), _rows_after(dan_ref, keep_after)], axis=0)
        gate = _causal_conv(uge, ugp, cwg_ref, cbg_ref)
        val = _causal_conv(uve, uvp, cwv_ref, cbv_ref)
        act, dact = _gelu_and_grad(gate)
        dgate = dae * val * dact
        dval = dae * act
        dug_ref[...] = _conv_transpose(dgate, rows, cwg_ref).astype(dug_ref.dtype)
        duv_ref[...] = _conv_transpose(dval, rows, cwv_ref).astype(duv_ref.dtype)
        _accumulate_rows(i == 0, wg_ref, _conv_weight_grads(dgate[:rows], uge[:rows], ugp, taps))
        _accumulate_rows(i == 0, wv_ref, _conv_weight_grads(dval[:rows], uve[:rows], uvp, taps))

    wspec = pl.BlockSpec((taps, tc), chan)
    bspec = pl.BlockSpec((1, tc), chan)
    gspec = pl.BlockSpec((taps + 1, tc), chan)
    act_shape = jax.ShapeDtypeStruct((t, f), BF16)
    stat_shape = jax.ShapeDtypeStruct((taps + 1, f), F32)
    in_specs, out_specs, out_shape, scratch = _rider_specs(
        rider, [cur, prev, nxt, cur, prev, nxt, wspec, wspec, bspec, bspec, cur, nxt], [cur, cur, gspec, gspec],
        [act_shape, act_shape, stat_shape, stat_shape])
    order = ("arbitrary",) * 2 if rider else ("parallel", "arbitrary")
    return pl.pallas_call(
        body, name=name, grid=grid, in_specs=in_specs, out_specs=out_specs, out_shape=out_shape,
        scratch_shapes=scratch, compiler_params=_params(*order),
    )(ug, ug, ug, uv, uv, uv, cwg, cwv, cbg.reshape(1, f), cbv.reshape(1, f), da, da,
      *(rider.inputs if rider else ()))


def conv_input_grad(dc, cw, s, *, name, out_dtype):
    t, f = dc.shape
    rows, tc = _tile(s, 512, SUBLANES), _tile(f, 256)
    cur, _, nxt, chan = _conv_specs(t, rows, tc, 0, dc.dtype)
    taps = cw.shape[0]

    def body(dc_ref, dcn_ref, cw_ref, o_ref):
        keep = jnp.where(_last_in_seq(pl.program_id(0), rows, s), 0.0, 1.0)
        dcc = dc_ref[...].astype(F32)
        dcn = _rows_after(dcn_ref, keep)
        out = dcc * cw_ref[taps - 1:taps, :]
        for dist in range(1, taps):
            out = out + _shift_up(dcc, dcn, dist) * cw_ref[taps - 1 - dist:taps - dist, :]
        o_ref[...] = out.astype(o_ref.dtype)

    return pl.pallas_call(
        body, name=name, grid=(t // rows, f // tc), in_specs=[cur, nxt, pl.BlockSpec((taps, tc), chan)],
        out_specs=cur, out_shape=jax.ShapeDtypeStruct((t, f), out_dtype),
        compiler_params=_params("parallel", "parallel"),
    )(dc, dc, cw)


def rnn_conv_fwd(yr, cw, cb, s, *, name):
    t, w = yr.shape
    rows, tc = _tile(s, 512, SUBLANES), _tile(w, 256)
    cur, prev, _, chan = _conv_specs(t, rows, tc, 0, yr.dtype)
    taps = cw.shape[0]

    def body(y_ref, yp_ref, cw_ref, cb_ref, o_ref):
        keep = jnp.where(_first_in_seq(pl.program_id(0), rows, s), 0.0, 1.0)
        o_ref[...] = _causal_conv(y_ref[...].astype(F32), _rows_before(yp_ref, keep), cw_ref, cb_ref)

    return pl.pallas_call(
        body, name=name, grid=(t // rows, w // tc),
        in_specs=[cur, prev, pl.BlockSpec((taps, tc), chan), pl.BlockSpec((1, tc), chan)], out_specs=cur,
        out_shape=jax.ShapeDtypeStruct((t, w), F32), compiler_params=_params("parallel", "parallel"),
    )(yr, yr, cw, cb.reshape(1, w))


def rnn_conv_wgrad(dxr, yr, s, taps, *, name):
    t, w = yr.shape
    rows, tc = _tile(s, 512, SUBLANES), _tile(w, 256)
    cur, prev, _, chan = _conv_specs(t, rows, tc, 1, yr.dtype)

    def body(d_ref, y_ref, yp_ref, o_ref):
        i = pl.program_id(1)
        keep = jnp.where(_first_in_seq(i, rows, s), 0.0, 1.0)
        grads = _conv_weight_grads(d_ref[...], y_ref[...].astype(F32), _rows_before(yp_ref, keep), taps)
        _accumulate_rows(i == 0, o_ref, grads)

    return pl.pallas_call(
        body, name=name, grid=(w // tc, t // rows), in_specs=[cur, cur, prev],
        out_specs=pl.BlockSpec((taps + 1, tc), chan), out_shape=jax.ShapeDtypeStruct((taps + 1, w), F32),
        compiler_params=_params("parallel", "arbitrary"),
    )(dxr, yr, yr)


SCAN_ROWS = 32


def _one_minus_exp(x):
    series = -x * (1.0 + x * (0.5 + x * (1.0 / 6.0)))
    return jnp.where(x > -0.01, series, 1.0 - jnp.exp(x))


def _gates(ga, gi, ba, bx, log_lam):
    ra = jax.nn.sigmoid(ga + ba)
    ri = jax.nn.sigmoid(gi + bx)
    log_a = LRU_C * ra * log_lam
    a = jnp.exp(log_a)
    mult = jnp.sqrt(_one_minus_exp(2.0 * log_a))
    return ra, ri, a, mult


def rnn_scan_fwd(ga, gi, xr, yg, ba, bx, lam, b, s, *, name):
    t, w = xr.shape
    tc = _tile(w, 256)
    rb = min(SCAN_ROWS, s)
    blocks = s // rb
    steps = [1 << e for e in range(rb.bit_length() - 1)]

    def body(ga_ref, gi_ref, xr_ref, yg_ref, ba_ref, bx_ref, lam_ref, h_ref, y_ref):
        log_lam = _log_sigmoid(lam_ref[...])
        ridx = lax.broadcasted_iota(jnp.int32, (rb, tc), 0)

        def step(n, carry):
            rs = pl.ds(pl.multiple_of(n * rb, rb), rb)
            xrv = xr_ref[rs, :]
            _, ri, a, mult = _gates(ga_ref[rs, :], gi_ref[rs, :], ba_ref[...], bx_ref[...], log_lam)
            u = mult * (ri * xrv)
            for dist in steps:
                a_sh = jnp.where(ridx >= dist, pltpu.roll(a, dist, 0), 1.0)
                u_sh = jnp.where(ridx >= dist, pltpu.roll(u, dist, 0), 0.0)
                u = a * u_sh + u
                a = a * a_sh
            hb = u + a * carry
            h_ref[rs, :] = hb
            y_ref[rs, :] = (_gelu(yg_ref[rs, :]) * hb).astype(y_ref.dtype)
            return hb[rb - 1:rb, :]

        lax.fori_loop(0, blocks, step, jnp.zeros((1, tc), F32))

    seq = pl.BlockSpec((s, tc), lambda bb, j: (bb, j))
    vec = pl.BlockSpec((1, tc), lambda bb, j: (0, j))
    return pl.pallas_call(
        body, name=name, grid=(b, w // tc), in_specs=[seq, seq, seq, seq, vec, vec, vec], out_specs=[seq, seq],
        out_shape=[jax.ShapeDtypeStruct((t, w), F32), jax.ShapeDtypeStruct((t, w), BF16)],
        compiler_params=_params("parallel", "parallel"),
    )(ga, gi, xr, yg, ba.reshape(1, w), bx.reshape(1, w), lam.reshape(1, w))


def rnn_scan_bwd(dy, ga, gi, xr, yg, h, ba, bx, lam, b, s, *, name):
    t, w = xr.shape
    tc = _tile(w, 256)
    rb = min(SCAN_ROWS, s)
    blocks = s // rb
    steps = [1 << e for e in range(rb.bit_length() - 1)]

    def body(dy_ref, ga_ref, gi_ref, xr_ref, yg_ref, h_ref, ba_ref, bx_ref, lam_ref,
             dyg_ref, dga_ref, dgi_ref, dxr_ref, stat_ref):
        lamv = lam_ref[...]
        log_lam = _log_sigmoid(lamv)
        dlog_lam = jax.nn.sigmoid(-lamv)
        ridx = lax.broadcasted_iota(jnp.int32, (rb, tc), 0)
        last = rb - 1

        def step(n, carry):
            lam_next, a_next, s_a, s_x, s_l = carry
            blk = blocks - 1 - n
            rs = pl.ds(pl.multiple_of(blk * rb, rb), rb)
            rp = pl.ds(pl.multiple_of(jnp.maximum(blk * rb - SUBLANES, 0), SUBLANES), SUBLANES)
            xrv = xr_ref[rs, :]
            hv = h_ref[rs, :]
            h_before = jnp.where(blk > 0, h_ref[rp, :][SUBLANES - 1:, :], 0.0)
            h_prev = jnp.where(ridx >= 1, pltpu.roll(hv, 1, 0), h_before)
            ra, ri, a, mult = _gates(ga_ref[rs, :], gi_ref[rs, :], ba_ref[...], bx_ref[...], log_lam)
            act, dact = _gelu_and_grad(yg_ref[rs, :])
            dyv = dy_ref[rs, :]
            dyg_ref[rs, :] = (dyv * hv * dact).astype(dyg_ref.dtype)
            v = dyv * act
            c = jnp.where(ridx < last, pltpu.roll(a, last, 0), a_next)
            for dist in steps:
                c_sh = jnp.where(ridx < rb - dist, pltpu.roll(c, rb - dist, 0), 1.0)
                v_sh = jnp.where(ridx < rb - dist, pltpu.roll(v, rb - dist, 0), 0.0)
                v = v + c * v_sh
                c = c * c_sh
            dh = v + c * lam_next
            du_ri_x = dh * xrv
            dmult = du_ri_x * ri
            dri = du_ri_x * mult
            dxr_ref[rs, :] = dh * mult * ri
            dlog_a = dh * h_prev * a - dmult * (a * a) / mult
            dra = dlog_a * (LRU_C * log_lam)
            dpa = dra * ra * (1.0 - ra)
            dpi = dri * ri * (1.0 - ri)
            dga_ref[rs, :] = dpa.astype(dga_ref.dtype)
            dgi_ref[rs, :] = dpi.astype(dgi_ref.dtype)
            s_a = s_a + jnp.sum(dpa, axis=0, keepdims=True)
            s_x = s_x + jnp.sum(dpi, axis=0, keepdims=True)
            s_l = s_l + jnp.sum(dlog_a * ra, axis=0, keepdims=True)
            return dh[0:1, :], a[0:1, :], s_a, s_x, s_l

        zero = jnp.zeros((1, tc), F32)
        _, _, s_a, s_x, s_l = lax.fori_loop(0, blocks, step, (zero, zero, zero, zero, zero))
        _accumulate_rows(pl.program_id(1) == 0, stat_ref, [s_a, s_x, s_l * (LRU_C * dlog_lam)])

    seq = pl.BlockSpec((s, tc), lambda j, bb: (bb, j))
    vec = pl.BlockSpec((1, tc), lambda j, bb: (0, j))
    half = jax.ShapeDtypeStruct((t, w), BF16)
    return pl.pallas_call(
        body, name=name, grid=(w // tc, b), in_specs=[seq, seq, seq, seq, seq, seq, vec, vec, vec],
        out_specs=[seq, seq, seq, seq, pl.BlockSpec((3, tc), lambda j, bb: (0, j))],
        out_shape=[half, half, half, jax.ShapeDtypeStruct((t, w), F32), jax.ShapeDtypeStruct((3, w), F32)],
        compiler_params=_params("parallel", "arbitrary"),
    )(dy, ga, gi, xr, yg, h, ba.reshape(1, w), bx.reshape(1, w), lam.reshape(1, w))


def _ple_mix(acc, x, gate):
    return x + jax.nn.sigmoid(gate) * acc


def ple_bwd(dx, gate, pin, w_proj, *, name):
    t, d = dx.shape
    k = pin.shape[1]
    tr = _tile(t, 512, SUBLANES)

    def body(dx_ref, g_ref, p_ref, w_ref, dg_ref, de_ref):
        emb = jnp.dot(p_ref[...].astype(BF16), w_ref[...], preferred_element_type=F32)
        sg = jax.nn.sigmoid(g_ref[...])
        dxv = dx_ref[...]
        de_ref[...] = (dxv * sg).astype(de_ref.dtype)
        dg_ref[...] = (dxv * emb * sg * (1.0 - sg)).astype(dg_ref.dtype)

    row = pl.BlockSpec((tr, d), lambda i: (i, 0))
    half = jax.ShapeDtypeStruct((t, d), BF16)
    return pl.pallas_call(
        body, name=name, grid=(t // tr,),
        in_specs=[row, row, pl.BlockSpec((tr, k), lambda i: (i, 0)), pl.BlockSpec((k, d), lambda i: (0, 0))],
        out_specs=[row, row], out_shape=[half, half], compiler_params=_params("parallel"))(dx, gate, pin, w_proj)


def adamw(w, g, m, v, *, name):
    shape = w.shape
    cols = shape[-1]
    rows = w.size // cols
    tr = _tile(rows, 1024, SUBLANES)
    bc1 = 1.0 / (1.0 - ADAM_B1 ** ADAM_STEP)
    bc2 = 1.0 / (1.0 - ADAM_B2 ** ADAM_STEP)

    def body(w_ref, g_ref, m_ref, v_ref, d_ref, nm_ref, nv_ref):
        gv = g_ref[...]
        nm = ADAM_B1 * m_ref[...] + (1.0 - ADAM_B1) * gv
        nv = ADAM_B2 * v_ref[...] + (1.0 - ADAM_B2) * (gv * gv)
        d_ref[...] = -ADAM_LR * ((nm * bc1) / (jnp.sqrt(nv * bc2) + ADAM_EPS) + ADAM_WD * w_ref[...])
        nm_ref[...] = nm
        nv_ref[...] = nv

    blk = pl.BlockSpec((tr, cols), lambda i: (i, 0))
    out = jax.ShapeDtypeStruct((rows, cols), F32)
    res = pl.pallas_call(body, name=name, grid=(rows // tr,), in_specs=[blk] * 4, out_specs=[blk] * 3,
                         out_shape=[out] * 3, compiler_params=_params("parallel"),
                         )(*[a.reshape(rows, cols) for a in (w, g, m, v)])
    return [r.reshape(shape) for r in res]


ANY = pl.BlockSpec(memory_space=pl.ANY)


def _place():
    return lax.axis_index("x"), lax.axis_index("y"), lax.axis_index("c")


class Rider(NamedTuple):
    inputs: tuple
    out_shapes: tuple
    scratch: tuple
    emit: Callable


def _when(cond):
    return (lambda fn: fn()) if cond is True else pl.when(cond)


def _rider_specs(rider, in_specs, out_specs, out_shape):
    if rider is None:
        return in_specs, out_specs, out_shape, []
    return (in_specs + [ANY] * len(rider.inputs), out_specs + [ANY] * len(rider.out_shapes),
            out_shape + list(rider.out_shapes), list(rider.scratch))


def _ride(rider, refs, n_in, n_out, grid):
    if rider is None:
        return
    ids = [pl.program_id(a) for a in range(len(grid))]
    first = functools.reduce(jnp.logical_and, [i == 0 for i in ids])
    last = functools.reduce(jnp.logical_and, [i == n - 1 for i, n in zip(ids, grid)])
    middle = functools.reduce(jnp.logical_and, [ids[0] == (3 * grid[0]) // 4] + [i == 0 for i in ids[1:]])
    r_in = refs[n_in:n_in + len(rider.inputs)]
    at = n_in + len(rider.inputs) + n_out
    r_out = refs[at:at + len(rider.out_shapes)]
    rider.emit(first, middle, last, r_in, r_out, refs[at + len(rider.out_shapes):])


def _alone(rider, *, name):
    n_in = len(rider.inputs)

    def body(*refs):
        rider.emit(True, True, True, refs[:n_in], refs[n_in:n_in + len(rider.out_shapes)],
                   refs[n_in + len(rider.out_shapes):])

    return pl.pallas_call(body, name=name, out_shape=list(rider.out_shapes), in_specs=[ANY] * n_in,
                          out_specs=[ANY] * len(rider.out_shapes), scratch_shapes=list(rider.scratch))(*rider.inputs)


def gather_rider(v):
    rows, cols = v.shape

    def emit(first, middle, last, ins, outs, sems):
        (v_ref,), (out_ref,), (send_sems, recv_sems, local_sem) = ins, outs, sems
        x, y, c = _place()
        me, sibling = (x, y, c), (x, y, 1 - c)
        chips = [(1 - x, y), (x, 1 - y), (1 - x, 1 - y)]

        def slot(px, py, pc):
            return out_ref.at[4 * px + 2 * py + pc]

        def copy(k, block, to, src=None):
            return pltpu.make_async_remote_copy(
                src_ref=slot(*block) if src is None else src, dst_ref=slot(*block),
                send_sem=send_sems.at[k], recv_sem=recv_sems.at[k], device_id=to, device_id_type=MESH)

        mine = pltpu.make_async_copy(v_ref, slot(*me), local_sem)
        own = [copy(0, me, sibling, src=v_ref)]
        own += [copy(1 + j, me, (*chip, c), src=v_ref) for j, chip in enumerate(chips)]
        passed = [copy(4 + j, (*chip, c), sibling) for j, chip in enumerate(chips)]

        @_when(first)
        def _():
            mine.start()
            for cp in own:
                cp.start()

        @_when(middle)
        def _():
            for j, chip in enumerate(chips):
                copy(1 + j, (*chip, c), me).wait_recv()
                passed[j].start()

        @_when(last)
        def _():
            copy(0, sibling, me).wait_recv()
            for j, chip in enumerate(chips):
                copy(4 + j, (*chip, 1 - c), me).wait_recv()
            for cp in own + passed:
                cp.wait_send()
            mine.wait()

    return Rider((v,), (jax.ShapeDtypeStruct((N_DEV, rows, cols), v.dtype),),
                 (pltpu.SemaphoreType.DMA((7,)), pltpu.SemaphoreType.DMA((7,)), pltpu.SemaphoreType.DMA(())), emit)


def all_gather(v, *, name):
    return _alone(gather_rider(v), name=name)[0]


def sibling_rider(parts):
    _, quads, rows, cols = parts.shape

    def emit(first, middle, last, ins, outs, sems):
        (p_ref,), (got_ref,), (send_sem, recv_sem) = ins, outs, sems
        x, y, c = _place()
        cp = pltpu.make_async_remote_copy(src_ref=p_ref.at[1 - c], dst_ref=got_ref, send_sem=send_sem,
                                          recv_sem=recv_sem, device_id=(x, y, 1 - c), device_id_type=MESH)
        _when(first)(cp.start)
        _when(last)(cp.wait)

    return Rider((parts,), (jax.ShapeDtypeStruct((quads, rows, cols), parts.dtype),),
                 (pltpu.SemaphoreType.DMA(()), pltpu.SemaphoreType.DMA(())), emit)


def chip_rider(parts):
    _, rows, cols = parts.shape

    def emit(first, middle, last, ins, outs, sems):
        (p_ref,), (got_ref,), (send_sems, recv_sems) = ins, outs, sems
        x, y, c = _place()
        chips = [(1 - x, y), (x, 1 - y), (1 - x, 1 - y)]
        copies = [pltpu.make_async_remote_copy(
            src_ref=p_ref.at[2 * cx + cy], dst_ref=got_ref.at[k], send_sem=send_sems.at[k],
            recv_sem=recv_sems.at[k], device_id=(cx, cy, c), device_id_type=MESH)
            for k, (cx, cy) in enumerate(chips)]

        @_when(first)
        def _():
            for cp in copies:
                cp.start()

        @_when(last)
        def _():
            for cp in copies:
                cp.wait()

    return Rider((parts,), (jax.ShapeDtypeStruct((3, rows, cols), parts.dtype),),
                 (pltpu.SemaphoreType.DMA((3,)), pltpu.SemaphoreType.DMA((3,))), emit)


def add_sibling(parts, got, *, name):
    _, quads, rows, cols = parts.shape
    tr = _tile(rows, GRAD_ROWS_TILE, SUBLANES)

    def body(c_ref, p_ref, g_ref, o_ref, ob_ref):
        total = p_ref[...] + g_ref[...]
        o_ref[...] = total
        ob_ref[...] = total.astype(ob_ref.dtype)

    c = lax.axis_index("c").astype(jnp.int32).reshape(1)
    quad = pl.BlockSpec((None, tr, cols), lambda q, i, c_ref: (q, i, 0))
    return pl.pallas_call(
        body, name=name,
        grid_spec=pltpu.PrefetchScalarGridSpec(
            num_scalar_prefetch=1, grid=(quads, rows // tr),
            in_specs=[pl.BlockSpec((None, None, tr, cols), lambda q, i, c_ref: (c_ref[0], q, i, 0)), quad],
            out_specs=[quad, quad]),
        out_shape=[jax.ShapeDtypeStruct((quads, rows, cols), parts.dtype),
                   jax.ShapeDtypeStruct((quads, rows, cols), BF16)],
        compiler_params=_params("parallel", "parallel"),
    )(c, parts, got)


def add_chips(parts, got, *, name):
    _, rows, cols = parts.shape
    tr = _tile(rows, GRAD_ROWS_TILE, SUBLANES)

    def body(q_ref, p_ref, g_ref, o_ref):
        o_ref[...] = ((p_ref[...] + g_ref[0].astype(F32)) + g_ref[1].astype(F32)) + g_ref[2].astype(F32)

    q = (2 * lax.axis_index("x") + lax.axis_index("y")).astype(jnp.int32).reshape(1)
    return pl.pallas_call(
        body, name=name,
        grid_spec=pltpu.PrefetchScalarGridSpec(
            num_scalar_prefetch=1, grid=(rows // tr,),
            in_specs=[pl.BlockSpec((None, tr, cols), lambda i, q_ref: (q_ref[0], i, 0)),
                      pl.BlockSpec((3, tr, cols), lambda i, q_ref: (0, i, 0))],
            out_specs=pl.BlockSpec((tr, cols), lambda i, q_ref: (i, 0))),
        out_shape=jax.ShapeDtypeStruct((rows, cols), parts.dtype), compiler_params=_params("parallel"),
    )(q, parts, got)


def _pack(arrays, dtype, row_align):
    pieces, spans, at = [], [], 0
    for a in arrays:
        flat = a.reshape(-1).astype(dtype)
        rows = -(-flat.size // (LANES * row_align)) * row_align
        pieces.append(jnp.pad(flat, (0, rows * LANES - flat.size)).reshape(rows, LANES))
        spans.append((at, rows))
        at += rows
    return jnp.concatenate(pieces, axis=0), spans


def _unpack(buf, spans, shapes, lead):
    out = []
    for (at, rows), shape in zip(spans, shapes):
        size = math.prod(shape)
        piece = buf[..., at:at + rows, :].reshape(*lead, rows * LANES)[..., :size]
        out.append(piece.reshape(*lead, *shape))
    return out


def _whole(gathered, axis):
    moved = jnp.moveaxis(gathered, 0, axis)
    shape = moved.shape
    return moved.reshape(*shape[:axis], shape[axis] * shape[axis + 1], *shape[axis + 2:])


def _blocks(whole, axis):
    shape = whole.shape
    cut = whole.reshape(*shape[:axis], N_DEV, shape[axis] // N_DEV, *shape[axis + 1:])
    return jnp.moveaxis(cut, axis, 0)


def _block_diag(w):
    heads, n, _ = w.shape
    eye = jnp.eye(heads, dtype=w.dtype)
    return (w[:, :, None, :] * eye[:, None, :, None]).reshape(heads * n, heads * n)


def _diag_blocks(full, heads):
    n = full.shape[0] // heads
    return jnp.stack([full[h * n:(h + 1) * n, h * n:(h + 1) * n] for h in range(heads)])


def _pack_grads(pieces):
    sharded = [n for n in pieces if n in SHARD_AXIS]
    replicated = [n for n in pieces if n not in SHARD_AXIS]
    cut = [_blocks(pieces[n], SHARD_AXIS[n]).reshape(N_DEV, -1) for n in sharded]
    rep = jnp.concatenate([pieces[n].reshape(-1) for n in replicated])
    rep_len = rep.size
    rep_rows = -(-rep_len // (N_DEV * LANES * SUBLANES)) * SUBLANES
    rep = jnp.pad(rep, (0, N_DEV * rep_rows * LANES - rep_len)).reshape(N_DEV, rep_rows * LANES)
    bufs, spans, at = [], [], 0
    for a in cut + [rep]:
        rows = -(-a.shape[1] // (LANES * SUBLANES)) * SUBLANES
        bufs.append(jnp.pad(a, ((0, 0), (0, rows * LANES - a.shape[1]))).reshape(N_DEV, rows, LANES))
        spans.append((at, rows))
        at += rows
    tail = -at % GRAD_ROWS_TILE
    bufs.append(jnp.zeros((N_DEV, tail, LANES), F32))
    at += tail
    parts = jnp.concatenate(bufs, axis=1)
    parts = parts.reshape(4, 2, at, LANES).transpose(1, 0, 2, 3)
    shapes = [_blocks(pieces[n], SHARD_AXIS[n]).shape[1:] for n in sharded]
    return parts, (sharded, shapes, spans, replicated, [pieces[n].shape for n in replicated], rep_rows)


def _unpack_grads(mine, info):
    sharded, shapes, spans, _, _, rep_rows = info
    rep_at = spans[-1][0]
    return dict(zip(sharded, _unpack(mine, spans[:-1], shapes, ()))), mine[rep_at:rep_at + rep_rows]


LAYER_INDEXED = ['norm_mix', 'attn_w_qkv', 'attn_w_o', 'norm_ffn', 'ffn_w_up', 'ffn_conv_w', 'ffn_conv_b',
                 'ffn_w_down', 'norm_ple', 'ple_w_gate', 'ple_w_proj']


def kernel(x, p, norm_mix, attn_w_qkv, attn_w_o, rnn_w_in, rnn_conv_w, rnn_conv_b, rnn_w_gate_a, rnn_b_gate_a, rnn_w_gate_x, rnn_b_gate_x, rnn_lru_param, rnn_w_out, norm_ffn, ffn_w_up, ffn_conv_w, ffn_conv_b, ffn_w_down, norm_ple, ple_w_gate, ple_w_proj, norm_final, loss_target, m_norm_mix, m_attn_w_qkv, m_attn_w_o, m_rnn_w_in, m_rnn_conv_w, m_rnn_conv_b, m_rnn_w_gate_a, m_rnn_b_gate_a, m_rnn_w_gate_x, m_rnn_b_gate_x, m_rnn_lru_param, m_rnn_w_out, m_norm_ffn, m_ffn_w_up, m_ffn_conv_w, m_ffn_conv_b, m_ffn_w_down, m_norm_ple, m_ple_w_gate, m_ple_w_proj, m_norm_final, v_norm_mix, v_attn_w_qkv, v_attn_w_o, v_rnn_w_in, v_rnn_conv_w, v_rnn_conv_b, v_rnn_w_gate_a, v_rnn_b_gate_a, v_rnn_w_gate_x, v_rnn_b_gate_x, v_rnn_lru_param, v_rnn_w_out, v_norm_ffn, v_ffn_w_up, v_ffn_conv_w, v_ffn_conv_b, v_ffn_w_down, v_norm_ple, v_ple_w_gate, v_ple_w_proj, v_norm_final):
    given = dict(locals())
    local = {n: given[n] for n in WEIGHTS}
    bsz, seq, d = x.shape
    t = bsz * seq
    depth = norm_mix.shape[0]
    width = rnn_w_out.shape[1] * N_DEV
    ffn = ffn_w_down.shape[1] * N_DEV

    assert depth >= 2
    now = [("attn_w_qkv", 0)]
    later = [(n, j) for n in MATMUL_WEIGHTS for j in range(local[n].shape[0]) if (n, j) not in now]
    full = {n: [None] * local[n].shape[0] for n in MATMUL_WEIGHTS}

    def packed(group):
        return _pack([local[n][j] for n, j in group], BF16, 2 * SUBLANES)

    def place(group, gathered, spans):
        got = _unpack(gathered, spans, [local[n][j].shape for n, j in group], (N_DEV,))
        for (n, j), g in zip(group, got):
            full[n][j] = _whole(g, SHARD_AXIS[n] - 1)

    buf, spans = packed(now)
    place(now, all_gather(buf, name="gather_layer0_weights"), spans)
    later_buf, later_spans = packed(later)
    buf, spans = _pack([local[n] for n in CHANNEL_WEIGHTS], F32, SUBLANES)
    got = _unpack(all_gather(buf, name="gather_channel_weights"), spans,
                  [local[n].shape for n in CHANNEL_WEIGHTS], (N_DEV,))
    full.update({n: _whole(g, SHARD_AXIS[n]) for n, g in zip(CHANNEL_WEIGHTS, got)})
    for n in REPLICATED:
        full[n] = local[n]

    grads = {}

    def stack(name, layer, value, count):
        grads.setdefault(name, [None] * count)[layer] = value

    saved = []
    h0 = x.reshape(t, d)
    for i in range(depth):
        slot = i // 2
        sv = {"x0": h0}
        if i % 2 == 0:
            qkv, hn = mm_rms(h0, full["norm_mix"][i], full["attn_w_qkv"][slot], out_dtype=BF16, name=f"l{i}_norm_qkv")
            if i == 0:
                o, totals, gathered = attn_fwd(qkv, bsz, seq, d, name=f"l{i}_attn", rider=gather_rider(later_buf))
                place(later, gathered, later_spans)
            else:
                o, totals = attn_fwd(qkv, bsz, seq, d, name=f"l{i}_attn")
            h1 = mm(o, full["attn_w_o"][slot], extras=(h0,), epilogue=_add, name=f"l{i}_attn_out")
            sv.update(qkv=qkv, o=o, totals=totals)
        else:
            w_in = full["rnn_w_in"][slot]
            yg, hn = mm_rms(h0, full["norm_mix"][i], w_in[:, :width], name=f"l{i}_norm_rnn_in_gate")
            yr = mm(hn, w_in[:, width:], name=f"l{i}_rnn_in_rec")
            xr = rnn_conv_fwd(yr, full["rnn_conv_w"][slot], full["rnn_conv_b"][slot], seq, name=f"l{i}_rnn_conv")
            wa = _block_diag(full["rnn_w_gate_a"][slot]).astype(BF16)
            wx = _block_diag(full["rnn_w_gate_x"][slot]).astype(BF16)
            ga = mm(xr, wa, name=f"l{i}_rnn_gate_a")
            gi = mm(xr, wx, name=f"l{i}_rnn_gate_x")
            hs, y = rnn_scan_fwd(ga, gi, xr, yg, full["rnn_b_gate_a"][slot], full["rnn_b_gate_x"][slot],
                                 full["rnn_lru_param"][slot], bsz, seq, name=f"l{i}_rnn_scan")
            h1 = mm(y, full["rnn_w_out"][slot], extras=(h0,), epilogue=_add, name=f"l{i}_rnn_out")
            sv.update(yg=yg, yr=yr, xr=xr, wa=wa, wx=wx, ga=ga, gi=gi, hs=hs, y=y)
        sv.update(hn=hn, x1=h1)
        w_up = full["ffn_w_up"][i]
        ug, hn2 = mm_rms(h1, full["norm_ffn"][i], w_up[:, :ffn], out_dtype=BF16, name=f"l{i}_norm_ffn_up_gate")
        uv = mm(hn2, w_up[:, ffn:], out_dtype=BF16, name=f"l{i}_ffn_up_val")
        cw, cb = full["ffn_conv_w"][i], full["ffn_conv_b"][i]
        act = ffn_act_fwd(ug, uv, cw[:, :ffn], cw[:, ffn:], cb[:ffn], cb[ffn:], seq, name=f"l{i}_ffn_act")
        h2 = mm(act, full["ffn_w_down"][i], extras=(h1,), epilogue=_add, name=f"l{i}_ffn_down")
        sv.update(hn2=hn2, ug=ug, uv=uv, act=act, x2=h2)
        pg, hn3 = mm_rms(h2, full["norm_ple"][i], full["ple_w_gate"][i], name=f"l{i}_norm_ple_gate")
        pin = p[i].reshape(t, p.shape[-1])
        h0 = mm(pin, full["ple_w_proj"][i], extras=(h2, pg), epilogue=_ple_mix, name=f"l{i}_ple_proj_mix")
        sv.update(hn3=hn3, pg=pg, pin=pin)
        saved.append(sv)

    dx, g_final, loss_part = final_loss(h0, full["norm_final"], loss_target.reshape(t, d), name="final_loss")
    grads["norm_final"] = g_final
    loss = lax.psum(loss_part, ("x", "y", "c"))

    def later_layers(name):
        return name not in LAYER_INDEXED or len(grads.get(name, ())) > 1

    for i in reversed(range(depth)):
        slot = i // 2
        sv = saved[i]
        if i == 0:
            upper = {n: (jnp.stack(grads[n][1:]) if n in LAYER_INDEXED else
                         jnp.stack(grads[n]) if isinstance(grads[n], list) else grads[n])
                     for n in WEIGHTS if later_layers(n)}
            upper_parts, upper_info = _pack_grads(upper)
        dpg, dpe = ple_bwd(dx, sv["pg"], sv["pin"], full["ple_w_proj"][i], name=f"l{i}_ple_mix_bwd")
        stack("ple_w_proj", i, mm(sv["pin"], dpe, ta=True, name=f"l{i}_ple_proj_wgrad"), depth)
        stack("ple_w_gate", i, mm(sv["hn3"], dpg, ta=True, name=f"l{i}_ple_gate_wgrad"), depth)
        dx, gn = mm_rms_bwd(dpg, full["ple_w_gate"][i], sv["x2"], full["norm_ple"][i], dx,
                            name=f"l{i}_ple_gate_dgrad_norm_bwd")
        stack("norm_ple", i, gn, depth)
        stack("ffn_w_down", i, mm(sv["act"], dx, ta=True, name=f"l{i}_ffn_down_wgrad"), depth)
        dact = mm(dx, full["ffn_w_down"][i], tb=True, out_dtype=BF16, name=f"l{i}_ffn_down_dgrad")
        cw, cb = full["ffn_conv_w"][i], full["ffn_conv_b"][i]
        taps = cw.shape[0]
        ride = sibling_rider(upper_parts) if i == 0 else None
        dug, duv, sg, svv, *rode = ffn_act_bwd(sv["ug"], sv["uv"], cw[:, :ffn], cw[:, ffn:], cb[:ffn], cb[ffn:], dact,
                                               seq, name=f"l{i}_ffn_act_bwd", rider=ride)
        if i == 0:
            upper_sum, upper_sum_bf16 = add_sibling(upper_parts, rode[0], name="upper_grads_add_sibling")
        stack("ffn_conv_w", i, jnp.concatenate([sg[:taps], svv[:taps]], axis=1), depth)
        stack("ffn_conv_b", i, jnp.concatenate([sg[taps], svv[taps]], axis=0), depth)
        stack("ffn_w_up", i, jnp.concatenate(
            [mm(sv["hn2"], dug, ta=True, name=f"l{i}_ffn_up_wgrad_gate"),
             mm(sv["hn2"], duv, ta=True, name=f"l{i}_ffn_up_wgrad_val")], axis=1), depth)
        w_up = full["ffn_w_up"][i]
        dhn2 = mm(dug, w_up[:, :ffn], tb=True, name=f"l{i}_ffn_up_dgrad_gate")
        dx, gn = mm_rms_bwd(duv, w_up[:, ffn:], sv["x1"], full["norm_ffn"][i], dx, prev=dhn2,
                            name=f"l{i}_ffn_up_dgrad_val_norm_bwd")
        stack("norm_ffn", i, gn, depth)
        if i % 2 == 0:
            stack("attn_w_o", slot, mm(sv["o"], dx, ta=True, name=f"l{i}_attn_out_wgrad"), depth // 2)
            do = mm(dx, full["attn_w_o"][slot], tb=True, out_dtype=BF16, name=f"l{i}_attn_out_dgrad")
            ride = chip_rider(upper_sum_bf16) if i == 0 else None
            dq, dk, dv, *rode = attn_bwd(sv["qkv"], sv["totals"], do, bsz, seq, d, name=f"l{i}_attn_bwd", rider=ride)
            if i == 0:
                upper_mine = add_chips(upper_sum, rode[0], name="upper_grads_add_chips")
            dqkv = jnp.concatenate([dq, dk, dv], axis=1).astype(BF16)
            stack("attn_w_qkv", slot, mm(sv["hn"], dqkv, ta=True, name=f"l{i}_qkv_wgrad"), depth // 2)
            dx, gn = mm_rms_bwd(dqkv, full["attn_w_qkv"][slot], sv["x0"], full["norm_mix"][i], dx,
                                name=f"l{i}_qkv_dgrad_norm_bwd")
        else:
            nrnn = depth // 2
            stack("rnn_w_out", slot, mm(sv["y"], dx, ta=True, name=f"l{i}_rnn_out_wgrad"), nrnn)
            dy = mm(dx, full["rnn_w_out"][slot], tb=True, name=f"l{i}_rnn_out_dgrad")
            dyg, dga, dgi, dxr, stats = rnn_scan_bwd(
                dy, sv["ga"], sv["gi"], sv["xr"], sv["yg"], sv["hs"], full["rnn_b_gate_a"][slot],
                full["rnn_b_gate_x"][slot], full["rnn_lru_param"][slot], bsz, seq, name=f"l{i}_rnn_scan_bwd")
            stack("rnn_b_gate_a", slot, stats[0], nrnn)
            stack("rnn_b_gate_x", slot, stats[1], nrnn)
            stack("rnn_lru_param", slot, stats[2], nrnn)
            stack("rnn_w_gate_a", slot, _diag_blocks(mm(sv["xr"], dga, ta=True, name=f"l{i}_rnn_gate_a_wgrad"),
                                                     RNN_HEADS), nrnn)
            stack("rnn_w_gate_x", slot, _diag_blocks(mm(sv["xr"], dgi, ta=True, name=f"l{i}_rnn_gate_x_wgrad"),
                                                     RNN_HEADS), nrnn)
            dxr = mm(dga, sv["wa"], tb=True, extras=(dxr,), epilogue=_add, name=f"l{i}_rnn_gate_a_dgrad")
            dxr = mm(dgi, sv["wx"], tb=True, extras=(dxr,), epilogue=_add, name=f"l{i}_rnn_gate_x_dgrad")
            rcw = full["rnn_conv_w"][slot]
            rtaps = rcw.shape[0]
            cstats = rnn_conv_wgrad(dxr, sv["yr"], seq, rtaps, name=f"l{i}_rnn_conv_wgrad")
            stack("rnn_conv_w", slot, cstats[:rtaps], nrnn)
            stack("rnn_conv_b", slot, cstats[rtaps], nrnn)
            dyr = conv_input_grad(dxr, rcw, seq, out_dtype=BF16, name=f"l{i}_rnn_conv_bwd")
            stack("rnn_w_in", slot, jnp.concatenate(
                [mm(sv["hn"], dyg, ta=True, name=f"l{i}_rnn_in_wgrad_gate"),
                 mm(sv["hn"], dyr, ta=True, name=f"l{i}_rnn_in_wgrad_rec")], axis=1), nrnn)
            w_in = full["rnn_w_in"][slot]
            dhn = mm(dyg, w_in[:, :width], tb=True, name=f"l{i}_rnn_in_dgrad_gate")
            dx, gn = mm_rms_bwd(dyr, w_in[:, width:], sv["x0"], full["norm_mix"][i], dx, prev=dhn,
                                name=f"l{i}_rnn_in_dgrad_rec_norm_bwd")
        stack("norm_mix", i, gn, depth)
    grad_x = dx.reshape(bsz, seq, d)

    lower = {n: grads[n][0][None] for n in LAYER_INDEXED}
    lower_parts, lower_info = _pack_grads(lower)
    from_sibling = _alone(sibling_rider(lower_parts), name="grads_to_sibling")[0]
    lower_sum, lower_sum_bf16 = add_sibling(lower_parts, from_sibling, name="grads_add_sibling")
    from_chips = _alone(chip_rider(lower_sum_bf16), name="grads_to_chips")[0]
    lower_mine = add_chips(lower_sum, from_chips, name="grads_add_chips")
    upper_local, upper_rep = _unpack_grads(upper_mine, upper_info)
    lower_local, lower_rep = _unpack_grads(lower_mine, lower_info)
    local_grads = {n: (jnp.concatenate([lower_local[n], upper_local[n]], axis=0) if n in upper_local
                       else lower_local[n]) if n in lower_local else upper_local[n]
                   for n in WEIGHTS if n in SHARD_AXIS}
    rep_all = all_gather(jnp.concatenate([upper_rep, lower_rep], axis=0), name="gather_replicated_grads")
    rep_vecs = {}
    for key, info, rows_at in (("upper", upper_info, 0), ("lower", lower_info, upper_rep.shape[0])):
        vec = rep_all[:, rows_at:rows_at + info[5]].reshape(-1)
        at = 0
        for n, shape in zip(info[3], info[4]):
            rep_vecs[key, n] = vec[at:at + math.prod(shape)].reshape(shape)
            at += math.prod(shape)
    for n in REPLICATED:
        both = [rep_vecs[k, n] for k in ("lower", "upper") if (k, n) in rep_vecs]
        local_grads[n] = both[0] if n not in LAYER_INDEXED else jnp.concatenate(both, axis=0)

    deltas, new_m, new_v = {}, {}, {}
    for n in WEIGHTS:
        deltas[n], new_m[n], new_v[n] = adamw(local[n], local_grads[n], given["m_" + n], given["v_" + n],
                                              name=f"adamw_{n}")
    return (loss, grad_x, *[local_grads[n] for n in WEIGHTS], *[deltas[n] for n in WEIGHTS],
            *[new_m[n] for n in WEIGHTS], *[new_v[n] for n in WEIGHTS])
```

```python
import functools
import math
from typing import Callable, NamedTuple

import jax
import jax.numpy as jnp
from jax import lax
from jax.experimental import pallas as pl
from jax.experimental.pallas import tpu as pltpu

F32 = jnp.float32
BF16 = jnp.bfloat16

EPS = 1e-6
HEAD_DIM = 64
RNN_HEADS = 16
LRU_C = 8.0
ADAM_LR = 0.001
ADAM_B1 = 0.9
ADAM_B2 = 0.999
ADAM_EPS = 1e-08
ADAM_WD = 0.01
ADAM_STEP = 10

N_DEV = 8
LANES = 128
SUBLANES = 8
VMEM_LIMIT = 56 * 1024 * 1024
MESH = pl.DeviceIdType.MESH
GRAD_ROWS_TILE = 2048
GELU_C = math.sqrt(2.0 / math.pi)
GELU_A = 0.044715

WEIGHTS = ['norm_mix', 'attn_w_qkv', 'attn_w_o', 'rnn_w_in', 'rnn_conv_w', 'rnn_conv_b', 'rnn_w_gate_a',
           'rnn_b_gate_a', 'rnn_w_gate_x', 'rnn_b_gate_x', 'rnn_lru_param', 'rnn_w_out', 'norm_ffn', 'ffn_w_up',
           'ffn_conv_w', 'ffn_conv_b', 'ffn_w_down', 'norm_ple', 'ple_w_gate', 'ple_w_proj', 'norm_final']
SHARD_AXIS = {'attn_w_qkv': 2, 'attn_w_o': 1, 'rnn_w_in': 2, 'rnn_conv_w': 2, 'rnn_conv_b': 1, 'rnn_b_gate_a': 1,
              'rnn_b_gate_x': 1, 'rnn_lru_param': 1, 'rnn_w_out': 1, 'ffn_w_up': 2, 'ffn_conv_w': 2,
              'ffn_w_down': 1, 'ple_w_gate': 1, 'ple_w_proj': 2}
MATMUL_WEIGHTS = ['attn_w_qkv', 'attn_w_o', 'rnn_w_in', 'rnn_w_out', 'ffn_w_up', 'ffn_w_down', 'ple_w_gate',
                  'ple_w_proj']
CHANNEL_WEIGHTS = ['rnn_conv_w', 'rnn_conv_b', 'rnn_b_gate_a', 'rnn_b_gate_x', 'rnn_lru_param', 'ffn_conv_w']
REPLICATED = [n for n in WEIGHTS if n not in SHARD_AXIS]


def _params(*sem):
    return pltpu.CompilerParams(dimension_semantics=sem, vmem_limit_bytes=VMEM_LIMIT)


def _tile(dim, pref, align=LANES):
    if dim <= pref:
        return dim
    t = (pref + pref // 2) // align * align
    while t >= align:
        if dim % t == 0:
            return t
        t -= align
    return dim


def _gelu(x):
    return 0.5 * x * (1.0 + jnp.tanh(GELU_C * (x + GELU_A * x * x * x)))


def _gelu_and_grad(x):
    t = jnp.tanh(GELU_C * (x + GELU_A * x * x * x))
    g = 0.5 * x * (1.0 + t)
    dg = 0.5 * (1.0 + t) + 0.5 * x * (1.0 - t * t) * GELU_C * (1.0 + 3.0 * GELU_A * x * x)
    return g, dg


def _log_sigmoid(x):
    return jnp.minimum(x, 0.0) - jnp.log(1.0 + jnp.exp(-jnp.abs(x)))


MM_VMEM_BUDGET = 36 * 1024 * 1024


def _mm_tiles(m, n, k, ta, a_item, b_item, out_item, n_extra):
    if ta:
        return _tile(m, 1024), _tile(n, 1024), _tile(k, 1024)
    row_bytes = k * a_item + n * (out_item + 4 * n_extra)
    w_bytes = k * n * b_item
    for tm in (1024, 512, 256, 128):
        if m % tm == 0 and 2 * tm * row_bytes + 2 * w_bytes + tm * n * 4 <= MM_VMEM_BUDGET:
            return tm, n, k
    return _tile(m, 512), _tile(n, 512), _tile(k, 1024)


def mm(a, b, *, name, ta=False, tb=False, out_dtype=F32, extras=(), epilogue=None):
    m, k = (a.shape[1], a.shape[0]) if ta else a.shape
    n = b.shape[0] if tb else b.shape[1]
    assert k == (b.shape[1] if tb else b.shape[0]), (a.shape, b.shape, ta, tb)
    tm, tn, tk = _mm_tiles(m, n, k, ta, a.dtype.itemsize, b.dtype.itemsize, jnp.dtype(out_dtype).itemsize,
                           len(extras))
    nk = k // tk
    n_extra = len(extras)
    dims = (((0 if ta else 1,), (1 if tb else 0,)), ((), ()))

    def body(a_ref, b_ref, *rest):
        extra_refs, o_ref = rest[:n_extra], rest[n_extra]

        def finish(acc):
            if epilogue is not None:
                acc = epilogue(acc, *[e[...] for e in extra_refs])
            o_ref[...] = acc.astype(o_ref.dtype)

        part = lax.dot_general(a_ref[...].astype(BF16), b_ref[...].astype(BF16), dims,
                               preferred_element_type=F32)
        if nk == 1:
            finish(part)
        else:
            acc_ref = rest[n_extra + 1]
            kk = pl.program_id(2)

            @pl.when(kk == 0)
            def _():
                acc_ref[...] = part

            @pl.when(kk > 0)
            def _():
                acc_ref[...] += part

            @pl.when(kk == nk - 1)
            def _():
                finish(acc_ref[...])

    a_spec = pl.BlockSpec((tk, tm), lambda i, j, kk: (kk, i)) if ta else pl.BlockSpec((tm, tk), lambda i, j, kk: (i, kk))
    b_spec = pl.BlockSpec((tn, tk), lambda i, j, kk: (j, kk)) if tb else pl.BlockSpec((tk, tn), lambda i, j, kk: (kk, j))
    o_spec = pl.BlockSpec((tm, tn), lambda i, j, kk: (i, j))
    return pl.pallas_call(
        body, name=name, grid=(m // tm, n // tn, nk),
        in_specs=[a_spec, b_spec] + [o_spec] * n_extra, out_specs=o_spec,
        out_shape=jax.ShapeDtypeStruct((m, n), out_dtype),
        scratch_shapes=[pltpu.VMEM((tm, tn), F32)] if nk > 1 else [],
        compiler_params=_params("parallel", "parallel", "arbitrary"),
    )(a, b, *extras)


def _add(acc, res):
    return acc + res


def mm_rms(x, g, w, *, name, out_dtype=F32):
    t, d = x.shape
    n = w.shape[1]
    out_item = jnp.dtype(out_dtype).itemsize
    row_bytes = d * 4 + d * 2 + n * out_item
    tr = next(tm for tm in (512, 256, 128, t)
              if t % tm == 0 and 2 * tm * row_bytes + 2 * w.size * w.dtype.itemsize + tm * n * 4 <= MM_VMEM_BUDGET)

    def body(x_ref, g_ref, w_ref, o_ref, hn_ref):
        xv = x_ref[...]
        r = lax.rsqrt(jnp.mean(xv * xv, axis=-1, keepdims=True) + EPS)
        hn = (xv * r * g_ref[...]).astype(BF16)
        hn_ref[...] = hn
        o_ref[...] = jnp.dot(hn, w_ref[...].astype(BF16), preferred_element_type=F32).astype(o_ref.dtype)

    row = pl.BlockSpec((tr, d), lambda i: (i, 0))
    return pl.pallas_call(
        body, name=name, grid=(t // tr,),
        in_specs=[row, pl.BlockSpec((1, d), lambda i: (0, 0)), pl.BlockSpec((d, n), lambda i: (0, 0))],
        out_specs=[pl.BlockSpec((tr, n), lambda i: (i, 0)), row],
        out_shape=[jax.ShapeDtypeStruct((t, n), out_dtype), jax.ShapeDtypeStruct((t, d), BF16)],
        compiler_params=_params("parallel"),
    )(x, g.reshape(1, d), w)


def mm_rms_bwd(a, w, x, g, dres, *, name, prev=None):
    t, k = a.shape
    d = w.shape[0]
    assert w.shape[1] == k and x.shape == (t, d)
    n_rows = 4 if prev is not None else 3
    row_bytes = k * a.dtype.itemsize + d * 4 * n_rows
    tr = next(tm for tm in (512, 256, 128, t)
              if t % tm == 0 and 2 * tm * row_bytes + 2 * w.size * w.dtype.itemsize + tm * d * 4 <= MM_VMEM_BUDGET)

    def body(a_ref, w_ref, x_ref, g_ref, dres_ref, *rest):
        dx_ref, dg_ref = rest[-2:]
        dhv = lax.dot_general(a_ref[...].astype(BF16), w_ref[...].astype(BF16), _NT, preferred_element_type=F32)
        if prev is not None:
            dhv = dhv + rest[0][...]
        xv = x_ref[...]
        r = lax.rsqrt(jnp.mean(xv * xv, axis=-1, keepdims=True) + EPS)
        xh = xv * r
        u = dhv * g_ref[...]
        dx_ref[...] = dres_ref[...] + r * (u - xh * jnp.mean(u * xh, axis=-1, keepdims=True))
        part = jnp.sum(dhv * xh, axis=0, keepdims=True)

        @pl.when(pl.program_id(0) == 0)
        def _():
            dg_ref[...] = part

        @pl.when(pl.program_id(0) > 0)
        def _():
            dg_ref[...] += part

    row = pl.BlockSpec((tr, d), lambda i: (i, 0))
    vec = pl.BlockSpec((1, d), lambda i: (0, 0))
    extra = [prev] if prev is not None else []
    dx, dg = pl.pallas_call(
        body, name=name, grid=(t // tr,),
        in_specs=[pl.BlockSpec((tr, k), lambda i: (i, 0)), pl.BlockSpec((d, k), lambda i: (0, 0)), row, vec, row]
        + [row] * len(extra),
        out_specs=[row, vec],
        out_shape=[jax.ShapeDtypeStruct((t, d), F32), jax.ShapeDtypeStruct((1, d), F32)],
        compiler_params=_params("arbitrary"),
    )(a, w, x, g.reshape(1, d), dres, *extra)
    return dx, dg.reshape(d)


def final_loss(x, g, target, *, name):
    t, d = x.shape
    tr = _tile(t, 512, SUBLANES)

    def body(x_ref, g_ref, t_ref, dx_ref, dg_ref, loss_ref):
        xv = x_ref[...]
        gv = g_ref[...]
        r = lax.rsqrt(jnp.mean(xv * xv, axis=-1, keepdims=True) + EPS)
        xh = xv * r
        err = xh * gv - t_ref[...]
        dy = err * (1.0 / d)
        u = dy * gv
        dx_ref[...] = r * (u - xh * jnp.mean(u * xh, axis=-1, keepdims=True))
        dg_part = jnp.sum(dy * xh, axis=0, keepdims=True)
        loss_part = jnp.zeros((1, LANES), F32) + (0.5 / d) * jnp.sum(err * err)

        @pl.when(pl.program_id(0) == 0)
        def _():
            dg_ref[...] = dg_part
            loss_ref[...] = loss_part

        @pl.when(pl.program_id(0) > 0)
        def _():
            dg_ref[...] += dg_part
            loss_ref[...] += loss_part

    row = pl.BlockSpec((tr, d), lambda i: (i, 0))
    vec = pl.BlockSpec((1, d), lambda i: (0, 0))
    dx, dg, loss = pl.pallas_call(
        body, name=name, grid=(t // tr,), in_specs=[row, vec, row],
        out_specs=[row, vec, pl.BlockSpec((1, LANES), lambda i: (0, 0))],
        out_shape=[jax.ShapeDtypeStruct((t, d), F32), jax.ShapeDtypeStruct((1, d), F32),
                   jax.ShapeDtypeStruct((1, LANES), F32)],
        compiler_params=_params("arbitrary"),
    )(x, g.reshape(1, d), target)
    return dx, dg.reshape(d), loss[0, 0]


def _split_dot(x, mat, left):
    hi = x.astype(BF16)
    lo = (x - hi.astype(F32)).astype(BF16)
    if left:
        return (jnp.dot(mat, hi, preferred_element_type=F32) + jnp.dot(mat, lo, preferred_element_type=F32))
    return (jnp.dot(hi, mat, preferred_element_type=F32) + jnp.dot(lo, mat, preferred_element_type=F32))


_NT = (((1,), (1,)), ((), ()))
_TN = (((0,), (0,)), ((), ()))
HEADS_PER_STEP = LANES // HEAD_DIM


def attn_fwd(qkv, b, s, d, *, name, rider=None):
    t = b * s
    tq = min(256, s)
    nq = s // tq
    pairs = d // LANES
    scale = HEAD_DIM ** -0.5

    grid = (b, pairs, nq)
    n_ride = len(rider.inputs) if rider else 0

    def body(*refs):
        q_ref, k_ref, v_ref = refs[:3]
        o_ref, lt_ref = refs[3 + n_ride:5 + n_ride]
        _ride(rider, refs, 3, 2, grid)
        i = pl.program_id(2)
        row = lax.broadcasted_iota(jnp.int32, (tq, tq), 0)
        col = lax.broadcasted_iota(jnp.int32, (tq, tq), 1)
        later = (row > col).astype(BF16)
        causal = col < row
        lanes = [slice(HEAD_DIM * h, HEAD_DIM * (h + 1)) for h in range(HEADS_PER_STEP)]
        qs = [(q_ref[:, sl].astype(F32) * scale).astype(BF16) for sl in lanes]

        def block(js, carry, diag):
            starts = [pl.multiple_of(j * tq, tq) for j in js]
            hs = range(HEADS_PER_STEP)
            chains = [(n, h) for n in range(len(js)) for h in hs]
            masked = [c for c in chains if diag and c[0] == 0]
            kbs = {(n, h): k_ref[pl.ds(starts[n], tq), lanes[h]] for n, h in chains}
            vbs = {(n, h): v_ref[pl.ds(starts[n], tq), lanes[h]] for n, h in chains}
            zs = {c: lax.dot_general(qs[c[1]], kbs[c], _NT, preferred_element_type=F32) for c in chains}
            lss = {c: _log_sigmoid(zs[c]) for c in chains}
            lks = {c: lss[c] - zs[c] for c in chains}
            for c in masked:
                lks[c] = jnp.where(causal, lks[c], 0.0)
            sums = {c: _split_dot(lks[c], later, left=False) for c in chains}
            runs_in, runs = {}, []
            for h in hs:
                run = carry[h][0]
                for n in range(len(js)):
                    runs_in[n, h] = run
                    run = run + (sums[n, h][:, 0:1] + lks[n, h][:, 0:1])
                runs.append(run)
            ws = {c: jnp.exp(lss[c] + sums[c] + runs_in[c]) for c in chains}
            for c in masked:
                ws[c] = jnp.where(causal, ws[c], 0.0)
            pvs = {c: jnp.dot(ws[c].astype(BF16), vbs[c], preferred_element_type=F32) for c in chains}
            accs = [carry[h][1] + sum(pvs[n, h] for n in range(len(js))) for h in hs]
            return tuple(zip(runs, accs))

        zero = (jnp.zeros((tq, 1), F32), jnp.zeros((tq, HEAD_DIM), F32))
        odd = i % 2
        carry = lax.cond(odd == 1, lambda c: block([i, i - 1], c, True), lambda c: block([i], c, True),
                         (zero,) * HEADS_PER_STEP)
        near = i - 1 - odd
        carry = lax.fori_loop(0, i // 2, lambda n, c: block([near - 2 * n, near - 2 * n - 1], c, False), carry)
        eye = (row == col).astype(F32)
        for h, sl in enumerate(lanes):
            run, acc = carry[h]
            o_ref[:, sl] = acc.astype(o_ref.dtype)
            lt_ref[SUBLANES * h:SUBLANES * (h + 1), :] = lax.dot_general(
                jnp.broadcast_to(run, (tq, SUBLANES)), eye, _TN, precision=lax.Precision.HIGHEST,
                preferred_element_type=F32)

    q_spec = pl.BlockSpec((tq, LANES), lambda bb, p, i: (bb * nq + i, p))
    k_spec = pl.BlockSpec((s, LANES), lambda bb, p, i: (bb, pairs + p))
    v_spec = pl.BlockSpec((s, LANES), lambda bb, p, i: (bb, 2 * pairs + p))
    lt_spec = pl.BlockSpec((None, None, None, HEADS_PER_STEP * SUBLANES, tq), lambda bb, p, i: (bb, p, i, 0, 0))
    in_specs, out_specs, out_shape, scratch = _rider_specs(
        rider, [q_spec, k_spec, v_spec], [q_spec, lt_spec],
        [jax.ShapeDtypeStruct((t, d), BF16),
         jax.ShapeDtypeStruct((b, pairs, nq, HEADS_PER_STEP * SUBLANES, tq), F32)])
    order = ("arbitrary",) * 3 if rider else ("parallel", "parallel", "arbitrary")
    return pl.pallas_call(
        body, name=name, grid=grid, in_specs=in_specs, out_specs=out_specs, out_shape=out_shape,
        scratch_shapes=scratch, compiler_params=_params(*order),
    )(qkv, qkv, qkv, *(rider.inputs if rider else ()))


def attn_bwd(qkv, totals, do, b, s, d, *, name, rider=None):
    t = b * s
    tq = min(256, s)
    nq = s // tq
    pairs = d // LANES
    scale = HEAD_DIM ** -0.5

    grid = (b, pairs, nq)
    n_ride = len(rider.inputs) if rider else 0

    def body(*refs):
        q_ref, k_ref, v_ref, lt_ref, do_ref = refs[:5]
        dq_ref, dk_ref, dv_ref = refs[5 + n_ride:8 + n_ride]
        _ride(rider, refs, 5, 3, grid)
        i = pl.program_id(2)

        @pl.when(i == 0)
        def _():
            dk_ref[...] = jnp.zeros_like(dk_ref)
            dv_ref[...] = jnp.zeros_like(dv_ref)

        row = lax.broadcasted_iota(jnp.int32, (tq, tq), 0)
        col = lax.broadcasted_iota(jnp.int32, (tq, tq), 1)
        upto = (col <= row).astype(BF16)
        earlier = (col < row).astype(BF16)
        causal = row < col
        lanes = [slice(HEAD_DIM * h, HEAD_DIM * (h + 1)) for h in range(HEADS_PER_STEP)]
        qs = [(q_ref[:, sl].astype(F32) * scale).astype(BF16) for sl in lanes]
        dos = [do_ref[:, sl].astype(BF16) for sl in lanes]
        totals_h = [lt_ref[SUBLANES * h:SUBLANES * h + 1, :] for h in range(HEADS_PER_STEP)]

        def block(js, carry, diag):
            starts = [pl.multiple_of(j * tq, tq) for j in js]
            hs = range(HEADS_PER_STEP)
            ns = range(len(js))
            chains = [(n, h) for n in ns for h in hs]
            masked = [c for c in chains if diag and c[0] == len(js) - 1]
            kbs = {(n, h): k_ref[pl.ds(starts[n], tq), lanes[h]] for n, h in chains}
            vbs = {(n, h): v_ref[pl.ds(starts[n], tq), lanes[h]] for n, h in chains}
            zs = {c: lax.dot_general(kbs[c], qs[c[1]], _NT, preferred_element_type=F32) for c in chains}
            dws = {c: lax.dot_general(vbs[c], dos[c[1]], _NT, preferred_element_type=F32) for c in chains}
            lss = {c: _log_sigmoid(zs[c]) for c in chains}
            lks = {c: lss[c] - zs[c] for c in chains}
            for c in masked:
                lks[c] = jnp.where(causal, lks[c], 0.0)
            sums = {c: _split_dot(lks[c], upto, left=True) for c in chains}
            runs_in, runs = {}, []
            for h in hs:
                run = carry[h][0]
                for n in ns:
                    runs_in[n, h] = run
                    run = run + sums[n, h][tq - 1:tq, :]
                runs.append(run)
            ws = {c: jnp.exp(lss[c] + ((totals_h[c[1]] - runs_in[c]) - sums[c])) for c in chains}
            for c in masked:
                ws[c] = jnp.where(causal, ws[c], 0.0)
            gs = {c: dws[c] * ws[c] for c in chains}
            gsums = {c: _split_dot(gs[c], earlier, left=True) for c in chains}
            gruns_in, gruns = {}, []
            for h in hs:
                grun = carry[h][1]
                for n in ns:
                    gruns_in[n, h] = grun
                    grun = grun + (gsums[n, h][tq - 1:tq, :] + gs[n, h][tq - 1:tq, :])
                gruns.append(grun)
            dzs = {c: gs[c] - jnp.exp(lss[c]) * (gs[c] + (gruns_in[c] + gsums[c])) for c in chains}
            for c in masked:
                dzs[c] = jnp.where(causal, dzs[c], 0.0)
            dzbs = {c: dzs[c].astype(BF16) for c in chains}
            for n, h in chains:
                dv_ref[pl.ds(starts[n], tq), lanes[h]] += jnp.dot(ws[n, h].astype(BF16), dos[h],
                                                                  preferred_element_type=F32)
                dk_ref[pl.ds(starts[n], tq), lanes[h]] += jnp.dot(dzbs[n, h], qs[h], preferred_element_type=F32)
            dqs = [carry[h][2] + sum(lax.dot_general(dzbs[n, h], kbs[n, h], _TN, preferred_element_type=F32)
                                     for n in ns) for h in hs]
            return tuple(zip(runs, gruns, dqs))

        zero = (jnp.zeros((1, tq), F32), jnp.zeros((1, tq), F32), jnp.zeros((tq, HEAD_DIM), F32))
        carry = lax.fori_loop(0, i // 2, lambda n, c: block([2 * n, 2 * n + 1], c, False),
                              (zero,) * HEADS_PER_STEP)
        carry = lax.cond(i % 2 == 1, lambda c: block([i - 1, i], c, True), lambda c: block([i], c, True), carry)
        for h, sl in enumerate(lanes):
            dq_ref[:, sl] = carry[h][2] * scale

    q_spec = pl.BlockSpec((tq, LANES), lambda bb, p, i: (bb * nq + i, p))
    k_spec = pl.BlockSpec((s, LANES), lambda bb, p, i: (bb, pairs + p))
    v_spec = pl.BlockSpec((s, LANES), lambda bb, p, i: (bb, 2 * pairs + p))
    lt_spec = pl.BlockSpec((None, None, None) + totals.shape[3:], lambda bb, p, i: (bb, p, i, 0, 0))
    kv_out = pl.BlockSpec((s, LANES), lambda bb, p, i: (bb, p))
    out = jax.ShapeDtypeStruct((t, d), F32)
    in_specs, out_specs, out_shape, scratch = _rider_specs(
        rider, [q_spec, k_spec, v_spec, lt_spec, q_spec], [q_spec, kv_out, kv_out], [out, out, out])
    order = ("arbitrary",) * 3 if rider else ("parallel", "parallel", "arbitrary")
    return pl.pallas_call(
        body, name=name, grid=grid, in_specs=in_specs, out_specs=out_specs, out_shape=out_shape,
        scratch_shapes=scratch, compiler_params=_params(*order),
    )(qkv, qkv, qkv, totals, do, *(rider.inputs if rider else ()))


def _shift_down(cur, prev8, dist):
    ext = jnp.concatenate([prev8, cur], axis=0)
    return pltpu.roll(ext, dist, 0)[SUBLANES:]


def _shift_up(cur, next8, dist):
    ext = jnp.concatenate([cur, next8], axis=0)
    return pltpu.roll(ext, ext.shape[0] - dist, 0)[:cur.shape[0]]


def _delayed(cur, prev8, taps):
    return [cur if dist == 0 else _shift_down(cur, prev8, dist) for dist in range(taps - 1, -1, -1)]


def _causal_conv(cur, prev8, w_ref, b_ref, delayed=None):
    taps = w_ref.shape[0]
    delayed = _delayed(cur, prev8, taps) if delayed is None else delayed
    out = delayed[taps - 1] * w_ref[taps - 1:taps, :] + b_ref[...]
    for k in range(taps - 1):
        out = out + delayed[k] * w_ref[k:k + 1, :]
    return out


def _conv_specs(t, rows, tc, time_axis, dtype=F32):
    sub = SUBLANES * (4 // jnp.dtype(dtype).itemsize)
    per = rows // sub
    last = t // sub - 1

    def grid_ids(*ids):
        return ids[time_axis], ids[1 - time_axis]

    def cur(*ids):
        return grid_ids(*ids)

    def prev(*ids):
        i, j = grid_ids(*ids)
        return (jnp.maximum(i * per - 1, 0), j)

    def nxt(*ids):
        i, j = grid_ids(*ids)
        return (jnp.minimum((i + 1) * per, last), j)

    def chan(*ids):
        return (0, grid_ids(*ids)[1])

    return pl.BlockSpec((rows, tc), cur), pl.BlockSpec((sub, tc), prev), pl.BlockSpec((sub, tc), nxt), chan


def _rows_before(ref, keep):
    return ref[...].astype(F32)[-SUBLANES:] * keep


def _rows_after(ref, keep):
    return ref[...].astype(F32)[:SUBLANES] * keep


def _first_in_seq(i, rows, s):
    return (i % (s // rows)) == 0


def _last_in_seq(i, rows, s):
    return (i % (s // rows)) == (s // rows - 1)


FFN_ACT_ROWS, FFN_ACT_COLS = 256, 1024


def ffn_act_fwd(ug, uv, cwg, cwv, cbg, cbv, s, *, name):
    t, f = ug.shape
    rows, tc = _tile(s, FFN_ACT_ROWS, SUBLANES), _tile(f, FFN_ACT_COLS)
    cur, prev, _, chan = _conv_specs(t, rows, tc, 0, ug.dtype)
    taps = cwg.shape[0]

    def body(ug_ref, ugp_ref, uv_ref, uvp_ref, cwg_ref, cwv_ref, cbg_ref, cbv_ref, a_ref):
        keep = jnp.where(_first_in_seq(pl.program_id(0), rows, s), 0.0, 1.0)
        gate = _causal_conv(ug_ref[...].astype(F32), _rows_before(ugp_ref, keep), cwg_ref, cbg_ref)
        val = _causal_conv(uv_ref[...].astype(F32), _rows_before(uvp_ref, keep), cwv_ref, cbv_ref)
        a_ref[...] = (_gelu(gate) * val).astype(a_ref.dtype)

    wspec = pl.BlockSpec((taps, tc), chan)
    bspec = pl.BlockSpec((1, tc), chan)
    return pl.pallas_call(
        body, name=name, grid=(t // rows, f // tc), in_specs=[cur, prev, cur, prev, wspec, wspec, bspec, bspec],
        out_specs=cur, out_shape=jax.ShapeDtypeStruct((t, f), BF16), compiler_params=_params("parallel", "parallel"),
    )(ug, ug, uv, uv, cwg, cwv, cbg.reshape(1, f), cbv.reshape(1, f))


def _accumulate_rows(first, ref, rows):
    for k, r in enumerate(rows):
        @pl.when(first)
        def _(k=k, r=r):
            ref[k:k + 1, :] = r

        @pl.when(jnp.logical_not(first))
        def _(k=k, r=r):
            ref[k:k + 1, :] += r


def _conv_weight_grads(dc, delayed):
    out = [jnp.sum(dc * xs, axis=0, keepdims=True) for xs in delayed]
    out.append(jnp.sum(dc, axis=0, keepdims=True))
    return out


def _conv_transpose(dc_ext, rows, w_ref):
    taps = w_ref.shape[0]
    out = dc_ext[:rows] * w_ref[taps - 1:taps, :]
    for dist in range(1, taps):
        out = out + pltpu.roll(dc_ext, dc_ext.shape[0] - dist, 0)[:rows] * w_ref[taps - 1 - dist:taps - dist, :]
    return out


def ffn_act_bwd(ug, uv, cwg, cwv, cbg, cbv, da, s, *, name, rider=None):
    t, f = ug.shape
    rows, tc = _tile(s, FFN_ACT_ROWS, SUBLANES), _tile(f, FFN_ACT_COLS)
    cur, prev, nxt, chan = _conv_specs(t, rows, tc, 1, ug.dtype)
    taps = cwg.shape[0]

    grid = (f // tc, t // rows)
    n_ride = len(rider.inputs) if rider else 0

    def body(*refs):
        (ug_ref, ugp_ref, ugn_ref, uv_ref, uvp_ref, uvn_ref, cwg_ref, cwv_ref, cbg_ref, cbv_ref,
         da_ref, dan_ref) = refs[:12]
        dug_ref, duv_ref, wg_ref, wv_ref = refs[12 + n_ride:16 + n_ride]
        _ride(rider, refs, 12, 4, grid)
        i = pl.program_id(1)
        keep_before = jnp.where(_first_in_seq(i, rows, s), 0.0, 1.0)
        keep_after = jnp.where(_last_in_seq(i, rows, s), 0.0, 1.0)
        ugp, uvp = _rows_before(ugp_ref, keep_before), _rows_before(uvp_ref, keep_before)
        uge = jnp.concatenate([ug_ref[...].astype(F32), _rows_after(ugn_ref, 1.0)], axis=0)
        uve = jnp.concatenate([uv_ref[...].astype(F32), _rows_after(uvn_ref, 1.0)], axis=0)
        dae = jnp.concatenate([da_ref[...].astype(F32), _rows_after(dan_ref, keep_after)], axis=0)
        ug_delayed, uv_delayed = _delayed(uge, ugp, taps), _delayed(uve, uvp, taps)
        gate = _causal_conv(uge, ugp, cwg_ref, cbg_ref, ug_delayed)
        val = _causal_conv(uve, uvp, cwv_ref, cbv_ref, uv_delayed)
        act, dact = _gelu_and_grad(gate)
        dgate = dae * val * dact
        dval = dae * act
        dug_ref[...] = _conv_transpose(dgate, rows, cwg_ref).astype(dug_ref.dtype)
        duv_ref[...] = _conv_transpose(dval, rows, cwv_ref).astype(duv_ref.dtype)
        _accumulate_rows(i == 0, wg_ref, _conv_weight_grads(dgate[:rows], [x[:rows] for x in ug_delayed]))
        _accumulate_rows(i == 0, wv_ref, _conv_weight_grads(dval[:rows], [x[:rows] for x in uv_delayed]))

    wspec = pl.BlockSpec((taps, tc), chan)
    bspec = pl.BlockSpec((1, tc), chan)
    gspec = pl.BlockSpec((taps + 1, tc), chan)
    act_shape = jax.ShapeDtypeStruct((t, f), BF16)
    stat_shape = jax.ShapeDtypeStruct((taps + 1, f), F32)
    in_specs, out_specs, out_shape, scratch = _rider_specs(
        rider, [cur, prev, nxt, cur, prev, nxt, wspec, wspec, bspec, bspec, cur, nxt], [cur, cur, gspec, gspec],
        [act_shape, act_shape, stat_shape, stat_shape])
    order = ("arbitrary",) * 2 if rider else ("parallel", "arbitrary")
    return pl.pallas_call(
        body, name=name, grid=grid, in_specs=in_specs, out_specs=out_specs, out_shape=out_shape,
        scratch_shapes=scratch, compiler_params=_params(*order),
    )(ug, ug, ug, uv, uv, uv, cwg, cwv, cbg.reshape(1, f), cbv.reshape(1, f), da, da,
      *(rider.inputs if rider else ()))


def conv_input_grad(dc, cw, s, *, name, out_dtype):
    t, f = dc.shape
    rows, tc = _tile(s, 512, SUBLANES), _tile(f, 256)
    cur, _, nxt, chan = _conv_specs(t, rows, tc, 0, dc.dtype)
    taps = cw.shape[0]

    def body(dc_ref, dcn_ref, cw_ref, o_ref):
        keep = jnp.where(_last_in_seq(pl.program_id(0), rows, s), 0.0, 1.0)
        dcc = dc_ref[...].astype(F32)
        dcn = _rows_after(dcn_ref, keep)
        out = dcc * cw_ref[taps - 1:taps, :]
        for dist in range(1, taps):
            out = out + _shift_up(dcc, dcn, dist) * cw_ref[taps - 1 - dist:taps - dist, :]
        o_ref[...] = out.astype(o_ref.dtype)

    return pl.pallas_call(
        body, name=name, grid=(t // rows, f // tc), in_specs=[cur, nxt, pl.BlockSpec((taps, tc), chan)],
        out_specs=cur, out_shape=jax.ShapeDtypeStruct((t, f), out_dtype),
        compiler_params=_params("parallel", "parallel"),
    )(dc, dc, cw)


def rnn_conv_fwd(yr, cw, cb, s, *, name):
    t, w = yr.shape
    rows, tc = _tile(s, 512, SUBLANES), _tile(w, 256)
    cur, prev, _, chan = _conv_specs(t, rows, tc, 0, yr.dtype)
    taps = cw.shape[0]

    def body(y_ref, yp_ref, cw_ref, cb_ref, o_ref):
        keep = jnp.where(_first_in_seq(pl.program_id(0), rows, s), 0.0, 1.0)
        o_ref[...] = _causal_conv(y_ref[...].astype(F32), _rows_before(yp_ref, keep), cw_ref, cb_ref)

    return pl.pallas_call(
        body, name=name, grid=(t // rows, w // tc),
        in_specs=[cur, prev, pl.BlockSpec((taps, tc), chan), pl.BlockSpec((1, tc), chan)], out_specs=cur,
        out_shape=jax.ShapeDtypeStruct((t, w), F32), compiler_params=_params("parallel", "parallel"),
    )(yr, yr, cw, cb.reshape(1, w))


def rnn_conv_wgrad(dxr, yr, s, taps, *, name):
    t, w = yr.shape
    rows, tc = _tile(s, 512, SUBLANES), _tile(w, 256)
    cur, prev, _, chan = _conv_specs(t, rows, tc, 1, yr.dtype)

    def body(d_ref, y_ref, yp_ref, o_ref):
        i = pl.program_id(1)
        keep = jnp.where(_first_in_seq(i, rows, s), 0.0, 1.0)
        grads = _conv_weight_grads(d_ref[...], _delayed(y_ref[...].astype(F32), _rows_before(yp_ref, keep), taps))
        _accumulate_rows(i == 0, o_ref, grads)

    return pl.pallas_call(
        body, name=name, grid=(w // tc, t // rows), in_specs=[cur, cur, prev],
        out_specs=pl.BlockSpec((taps + 1, tc), chan), out_shape=jax.ShapeDtypeStruct((taps + 1, w), F32),
        compiler_params=_params("parallel", "arbitrary"),
    )(dxr, yr, yr)


SCAN_ROWS = 32


def _one_minus_exp(x):
    series = -x * (1.0 + x * (0.5 + x * (1.0 / 6.0)))
    return jnp.where(x > -0.01, series, 1.0 - jnp.exp(x))


def _gates(ga, gi, ba, bx, log_lam):
    ra = jax.nn.sigmoid(ga + ba)
    ri = jax.nn.sigmoid(gi + bx)
    log_a = LRU_C * ra * log_lam
    a = jnp.exp(log_a)
    mult = jnp.sqrt(_one_minus_exp(2.0 * log_a))
    return ra, ri, a, mult


def rnn_scan_fwd(ga, gi, xr, yg, ba, bx, lam, b, s, *, name):
    t, w = xr.shape
    tc = _tile(w, 256)
    rb = min(SCAN_ROWS, s)
    blocks = s // rb
    steps = [1 << e for e in range(rb.bit_length() - 1)]

    def body(ga_ref, gi_ref, xr_ref, yg_ref, ba_ref, bx_ref, lam_ref, h_ref, y_ref):
        log_lam = _log_sigmoid(lam_ref[...])
        ridx = lax.broadcasted_iota(jnp.int32, (rb, tc), 0)

        def step(n, carry):
            rs = pl.ds(pl.multiple_of(n * rb, rb), rb)
            xrv = xr_ref[rs, :]
            _, ri, a, mult = _gates(ga_ref[rs, :], gi_ref[rs, :], ba_ref[...], bx_ref[...], log_lam)
            u = mult * (ri * xrv)
            for dist in steps:
                a_sh = jnp.where(ridx >= dist, pltpu.roll(a, dist, 0), 1.0)
                u_sh = jnp.where(ridx >= dist, pltpu.roll(u, dist, 0), 0.0)
                u = a * u_sh + u
                a = a * a_sh
            hb = u + a * carry
            h_ref[rs, :] = hb
            y_ref[rs, :] = (_gelu(yg_ref[rs, :]) * hb).astype(y_ref.dtype)
            return hb[rb - 1:rb, :]

        lax.fori_loop(0, blocks, step, jnp.zeros((1, tc), F32))

    seq = pl.BlockSpec((s, tc), lambda bb, j: (bb, j))
    vec = pl.BlockSpec((1, tc), lambda bb, j: (0, j))
    return pl.pallas_call(
        body, name=name, grid=(b, w // tc), in_specs=[seq, seq, seq, seq, vec, vec, vec], out_specs=[seq, seq],
        out_shape=[jax.ShapeDtypeStruct((t, w), F32), jax.ShapeDtypeStruct((t, w), BF16)],
        compiler_params=_params("parallel", "parallel"),
    )(ga, gi, xr, yg, ba.reshape(1, w), bx.reshape(1, w), lam.reshape(1, w))


def rnn_scan_bwd(dy, ga, gi, xr, yg, h, ba, bx, lam, b, s, *, name):
    t, w = xr.shape
    tc = _tile(w, 256)
    rb = min(SCAN_ROWS, s)
    blocks = s // rb
    steps = [1 << e for e in range(rb.bit_length() - 1)]

    def body(dy_ref, ga_ref, gi_ref, xr_ref, yg_ref, h_ref, ba_ref, bx_ref, lam_ref,
             dyg_ref, dga_ref, dgi_ref, dxr_ref, stat_ref):
        lamv = lam_ref[...]
        log_lam = _log_sigmoid(lamv)
        dlog_lam = jax.nn.sigmoid(-lamv)
        ridx = lax.broadcasted_iota(jnp.int32, (rb, tc), 0)
        last = rb - 1

        def step(n, carry):
            lam_next, a_next, s_a, s_x, s_l = carry
            blk = blocks - 1 - n
            rs = pl.ds(pl.multiple_of(blk * rb, rb), rb)
            rp = pl.ds(pl.multiple_of(jnp.maximum(blk * rb - SUBLANES, 0), SUBLANES), SUBLANES)
            xrv = xr_ref[rs, :]
            hv = h_ref[rs, :]
            h_before = jnp.where(blk > 0, h_ref[rp, :][SUBLANES - 1:, :], 0.0)
            h_prev = jnp.where(ridx >= 1, pltpu.roll(hv, 1, 0), h_before)
            ra, ri, a, mult = _gates(ga_ref[rs, :], gi_ref[rs, :], ba_ref[...], bx_ref[...], log_lam)
            act, dact = _gelu_and_grad(yg_ref[rs, :])
            dyv = dy_ref[rs, :]
            dyg_ref[rs, :] = (dyv * hv * dact).astype(dyg_ref.dtype)
            v = dyv * act
            c = jnp.where(ridx < last, pltpu.roll(a, last, 0), a_next)
            for dist in steps:
                c_sh = jnp.where(ridx < rb - dist, pltpu.roll(c, rb - dist, 0), 1.0)
                v_sh = jnp.where(ridx < rb - dist, pltpu.roll(v, rb - dist, 0), 0.0)
                v = v + c * v_sh
                c = c * c_sh
            dh = v + c * lam_next
            du_ri_x = dh * xrv
            dmult = du_ri_x * ri
            dri = du_ri_x * mult
            dxr_ref[rs, :] = dh * mult * ri
            dlog_a = dh * h_prev * a - dmult * (a * a) / mult
            dra = dlog_a * (LRU_C * log_lam)
            dpa = dra * ra * (1.0 - ra)
            dpi = dri * ri * (1.0 - ri)
            dga_ref[rs, :] = dpa.astype(dga_ref.dtype)
            dgi_ref[rs, :] = dpi.astype(dgi_ref.dtype)
            s_a = s_a + jnp.sum(dpa, axis=0, keepdims=True)
            s_x = s_x + jnp.sum(dpi, axis=0, keepdims=True)
            s_l = s_l + jnp.sum(dlog_a * ra, axis=0, keepdims=True)
            return dh[0:1, :], a[0:1, :], s_a, s_x, s_l

        zero = jnp.zeros((1, tc), F32)
        _, _, s_a, s_x, s_l = lax.fori_loop(0, blocks, step, (zero, zero, zero, zero, zero))
        _accumulate_rows(pl.program_id(1) == 0, stat_ref, [s_a, s_x, s_l * (LRU_C * dlog_lam)])

    seq = pl.BlockSpec((s, tc), lambda j, bb: (bb, j))
    vec = pl.BlockSpec((1, tc), lambda j, bb: (0, j))
    half = jax.ShapeDtypeStruct((t, w), BF16)
    return pl.pallas_call(
        body, name=name, grid=(w // tc, b), in_specs=[seq, seq, seq, seq, seq, seq, vec, vec, vec],
        out_specs=[seq, seq, seq, seq, pl.BlockSpec((3, tc), lambda j, bb: (0, j))],
        out_shape=[half, half, half, jax.ShapeDtypeStruct((t, w), F32), jax.ShapeDtypeStruct((3, w), F32)],
        compiler_params=_params("parallel", "arbitrary"),
    )(dy, ga, gi, xr, yg, h, ba.reshape(1, w), bx.reshape(1, w), lam.reshape(1, w))


def _ple_mix(acc, x, gate):
    return x + jax.nn.sigmoid(gate) * acc


def ple_bwd(dx, gate, pin, w_proj, *, name):
    t, d = dx.shape
    k = pin.shape[1]
    tr = _tile(t, 512, SUBLANES)

    def body(dx_ref, g_ref, p_ref, w_ref, dg_ref, de_ref):
        emb = jnp.dot(p_ref[...].astype(BF16), w_ref[...], preferred_element_type=F32)
        sg = jax.nn.sigmoid(g_ref[...])
        dxv = dx_ref[...]
        de_ref[...] = (dxv * sg).astype(de_ref.dtype)
        dg_ref[...] = (dxv * emb * sg * (1.0 - sg)).astype(dg_ref.dtype)

    row = pl.BlockSpec((tr, d), lambda i: (i, 0))
    half = jax.ShapeDtypeStruct((t, d), BF16)
    return pl.pallas_call(
        body, name=name, grid=(t // tr,),
        in_specs=[row, row, pl.BlockSpec((tr, k), lambda i: (i, 0)), pl.BlockSpec((k, d), lambda i: (0, 0))],
        out_specs=[row, row], out_shape=[half, half], compiler_params=_params("parallel"))(dx, gate, pin, w_proj)


def adamw(w, g, m, v, *, name):
    shape = w.shape
    cols = shape[-1]
    rows = w.size // cols
    tr = _tile(rows, 1024, SUBLANES)
    bc1 = 1.0 / (1.0 - ADAM_B1 ** ADAM_STEP)
    bc2 = 1.0 / (1.0 - ADAM_B2 ** ADAM_STEP)

    def body(w_ref, g_ref, m_ref, v_ref, d_ref, nm_ref, nv_ref):
        gv = g_ref[...]
        nm = ADAM_B1 * m_ref[...] + (1.0 - ADAM_B1) * gv
        nv = ADAM_B2 * v_ref[...] + (1.0 - ADAM_B2) * (gv * gv)
        d_ref[...] = -ADAM_LR * ((nm * bc1) / (jnp.sqrt(nv * bc2) + ADAM_EPS) + ADAM_WD * w_ref[...])
        nm_ref[...] = nm
        nv_ref[...] = nv

    blk = pl.BlockSpec((tr, cols), lambda i: (i, 0))
    out = jax.ShapeDtypeStruct((rows, cols), F32)
    res = pl.pallas_call(body, name=name, grid=(rows // tr,), in_specs=[blk] * 4, out_specs=[blk] * 3,
                         out_shape=[out] * 3, compiler_params=_params("parallel"),
                         )(*[a.reshape(rows, cols) for a in (w, g, m, v)])
    return [r.reshape(shape) for r in res]


ANY = pl.BlockSpec(memory_space=pl.ANY)


def _place():
    return lax.axis_index("x"), lax.axis_index("y"), lax.axis_index("c")


class Rider(NamedTuple):
    inputs: tuple
    out_shapes: tuple
    scratch: tuple
    emit: Callable


def _when(cond):
    return (lambda fn: fn()) if cond is True else pl.when(cond)


def _rider_specs(rider, in_specs, out_specs, out_shape):
    if rider is None:
        return in_specs, out_specs, out_shape, []
    return (in_specs + [ANY] * len(rider.inputs), out_specs + [ANY] * len(rider.out_shapes),
            out_shape + list(rider.out_shapes), list(rider.scratch))


def _ride(rider, refs, n_in, n_out, grid):
    if rider is None:
        return
    ids = [pl.program_id(a) for a in range(len(grid))]
    first = functools.reduce(jnp.logical_and, [i == 0 for i in ids])
    last = functools.reduce(jnp.logical_and, [i == n - 1 for i, n in zip(ids, grid)])
    middle = functools.reduce(jnp.logical_and, [ids[0] == (3 * grid[0]) // 4] + [i == 0 for i in ids[1:]])
    r_in = refs[n_in:n_in + len(rider.inputs)]
    at = n_in + len(rider.inputs) + n_out
    r_out = refs[at:at + len(rider.out_shapes)]
    rider.emit(first, middle, last, r_in, r_out, refs[at + len(rider.out_shapes):])


def _alone(rider, *, name):
    n_in = len(rider.inputs)

    def body(*refs):
        rider.emit(True, True, True, refs[:n_in], refs[n_in:n_in + len(rider.out_shapes)],
                   refs[n_in + len(rider.out_shapes):])

    return pl.pallas_call(body, name=name, out_shape=list(rider.out_shapes), in_specs=[ANY] * n_in,
                          out_specs=[ANY] * len(rider.out_shapes), scratch_shapes=list(rider.scratch))(*rider.inputs)


def gather_rider(v):
    rows, cols = v.shape

    def emit(first, middle, last, ins, outs, sems):
        (v_ref,), (out_ref,), (send_sems, recv_sems, local_sem) = ins, outs, sems
        x, y, c = _place()
        me, sibling = (x, y, c), (x, y, 1 - c)
        chips = [(1 - x, y), (x, 1 - y), (1 - x, 1 - y)]

        def slot(px, py, pc):
            return out_ref.at[4 * px + 2 * py + pc]

        def copy(k, block, to, src=None):
            return pltpu.make_async_remote_copy(
                src_ref=slot(*block) if src is None else src, dst_ref=slot(*block),
                send_sem=send_sems.at[k], recv_sem=recv_sems.at[k], device_id=to, device_id_type=MESH)

        mine = pltpu.make_async_copy(v_ref, slot(*me), local_sem)
        own = [copy(0, me, sibling, src=v_ref)]
        own += [copy(1 + j, me, (*chip, c), src=v_ref) for j, chip in enumerate(chips)]
        passed = [copy(4 + j, (*chip, c), sibling) for j, chip in enumerate(chips)]

        @_when(first)
        def _():
            mine.start()
            for cp in own:
                cp.start()

        @_when(middle)
        def _():
            for j, chip in enumerate(chips):
                copy(1 + j, (*chip, c), me).wait_recv()
                passed[j].start()

        @_when(last)
        def _():
            copy(0, sibling, me).wait_recv()
            for j, chip in enumerate(chips):
                copy(4 + j, (*chip, 1 - c), me).wait_recv()
            for cp in own + passed:
                cp.wait_send()
            mine.wait()

    return Rider((v,), (jax.ShapeDtypeStruct((N_DEV, rows, cols), v.dtype),),
                 (pltpu.SemaphoreType.DMA((7,)), pltpu.SemaphoreType.DMA((7,)), pltpu.SemaphoreType.DMA(())), emit)


def all_gather(v, *, name):
    return _alone(gather_rider(v), name=name)[0]


def sibling_rider(parts):
    _, quads, rows, cols = parts.shape

    def emit(first, middle, last, ins, outs, sems):
        (p_ref,), (got_ref,), (send_sem, recv_sem) = ins, outs, sems
        x, y, c = _place()
        cp = pltpu.make_async_remote_copy(src_ref=p_ref.at[1 - c], dst_ref=got_ref, send_sem=send_sem,
                                          recv_sem=recv_sem, device_id=(x, y, 1 - c), device_id_type=MESH)
        _when(first)(cp.start)
        _when(last)(cp.wait)

    return Rider((parts,), (jax.ShapeDtypeStruct((quads, rows, cols), parts.dtype),),
                 (pltpu.SemaphoreType.DMA(()), pltpu.SemaphoreType.DMA(())), emit)


def chip_rider(parts):
    _, rows, cols = parts.shape

    def emit(first, middle, last, ins, outs, sems):
        (p_ref,), (got_ref,), (send_sems, recv_sems) = ins, outs, sems
        x, y, c = _place()
        chips = [(1 - x, y), (x, 1 - y), (1 - x, 1 - y)]
        copies = [pltpu.make_async_remote_copy(
            src_ref=p_ref.at[2 * cx + cy], dst_ref=got_ref.at[k], send_sem=send_sems.at[k],
            recv_sem=recv_sems.at[k], device_id=(cx, cy, c), device_id_type=MESH)
            for k, (cx, cy) in enumerate(chips)]

        @_when(first)
        def _():
            for cp in copies:
                cp.start()

        @_when(last)
        def _():
            for cp in copies:
                cp.wait()

    return Rider((parts,), (jax.ShapeDtypeStruct((3, rows, cols), parts.dtype),),
                 (pltpu.SemaphoreType.DMA((3,)), pltpu.SemaphoreType.DMA((3,))), emit)


def add_sibling(parts, got, *, name):
    _, quads, rows, cols = parts.shape
    tr = _tile(rows, GRAD_ROWS_TILE, SUBLANES)

    def body(c_ref, p_ref, g_ref, o_ref, ob_ref):
        total = p_ref[...] + g_ref[...]
        o_ref[...] = total
        ob_ref[...] = total.astype(ob_ref.dtype)

    c = lax.axis_index("c").astype(jnp.int32).reshape(1)
    quad = pl.BlockSpec((None, tr, cols), lambda q, i, c_ref: (q, i, 0))
    return pl.pallas_call(
        body, name=name,
        grid_spec=pltpu.PrefetchScalarGridSpec(
            num_scalar_prefetch=1, grid=(quads, rows // tr),
            in_specs=[pl.BlockSpec((None, None, tr, cols), lambda q, i, c_ref: (c_ref[0], q, i, 0)), quad],
            out_specs=[quad, quad]),
        out_shape=[jax.ShapeDtypeStruct((quads, rows, cols), parts.dtype),
                   jax.ShapeDtypeStruct((quads, rows, cols), BF16)],
        compiler_params=_params("parallel", "parallel"),
    )(c, parts, got)


def add_chips(parts, got, *, name):
    _, rows, cols = parts.shape
    tr = _tile(rows, GRAD_ROWS_TILE, SUBLANES)

    def body(q_ref, p_ref, g_ref, o_ref):
        o_ref[...] = ((p_ref[...] + g_ref[0].astype(F32)) + g_ref[1].astype(F32)) + g_ref[2].astype(F32)

    q = (2 * lax.axis_index("x") + lax.axis_index("y")).astype(jnp.int32).reshape(1)
    return pl.pallas_call(
        body, name=name,
        grid_spec=pltpu.PrefetchScalarGridSpec(
            num_scalar_prefetch=1, grid=(rows // tr,),
            in_specs=[pl.BlockSpec((None, tr, cols), lambda i, q_ref: (q_ref[0], i, 0)),
                      pl.BlockSpec((3, tr, cols), lambda i, q_ref: (0, i, 0))],
            out_specs=pl.BlockSpec((tr, cols), lambda i, q_ref: (i, 0))),
        out_shape=jax.ShapeDtypeStruct((rows, cols), parts.dtype), compiler_params=_params("parallel"),
    )(q, parts, got)


def _pack(arrays, dtype, row_align):
    pieces, spans, at = [], [], 0
    for a in arrays:
        flat = a.reshape(-1).astype(dtype)
        rows = -(-flat.size // (LANES * row_align)) * row_align
        pieces.append(jnp.pad(flat, (0, rows * LANES - flat.size)).reshape(rows, LANES))
        spans.append((at, rows))
        at += rows
    return jnp.concatenate(pieces, axis=0), spans


def _unpack(buf, spans, shapes, lead):
    out = []
    for (at, rows), shape in zip(spans, shapes):
        size = math.prod(shape)
        piece = buf[..., at:at + rows, :].reshape(*lead, rows * LANES)[..., :size]
        out.append(piece.reshape(*lead, *shape))
    return out


def _whole(gathered, axis):
    moved = jnp.moveaxis(gathered, 0, axis)
    shape = moved.shape
    return moved.reshape(*shape[:axis], shape[axis] * shape[axis + 1], *shape[axis + 2:])


def _blocks(whole, axis):
    shape = whole.shape
    cut = whole.reshape(*shape[:axis], N_DEV, shape[axis] // N_DEV, *shape[axis + 1:])
    return jnp.moveaxis(cut, axis, 0)


def _block_diag(w):
    heads, n, _ = w.shape
    eye = jnp.eye(heads, dtype=w.dtype)
    return (w[:, :, None, :] * eye[:, None, :, None]).reshape(heads * n, heads * n)


def _diag_blocks(full, heads):
    n = full.shape[0] // heads
    return jnp.stack([full[h * n:(h + 1) * n, h * n:(h + 1) * n] for h in range(heads)])


def _pack_grads(pieces):
    sharded = [n for n in pieces if n in SHARD_AXIS]
    replicated = [n for n in pieces if n not in SHARD_AXIS]
    cut = [_blocks(pieces[n], SHARD_AXIS[n]).reshape(N_DEV, -1) for n in sharded]
    rep = jnp.concatenate([pieces[n].reshape(-1) for n in replicated])
    rep_len = rep.size
    rep_rows = -(-rep_len // (N_DEV * LANES * SUBLANES)) * SUBLANES
    rep = jnp.pad(rep, (0, N_DEV * rep_rows * LANES - rep_len)).reshape(N_DEV, rep_rows * LANES)
    bufs, spans, at = [], [], 0
    for a in cut + [rep]:
        rows = -(-a.shape[1] // (LANES * SUBLANES)) * SUBLANES
        bufs.append(jnp.pad(a, ((0, 0), (0, rows * LANES - a.shape[1]))).reshape(N_DEV, rows, LANES))
        spans.append((at, rows))
        at += rows
    tail = -at % GRAD_ROWS_TILE
    bufs.append(jnp.zeros((N_DEV, tail, LANES), F32))
    at += tail
    parts = jnp.concatenate(bufs, axis=1)
    parts = parts.reshape(4, 2, at, LANES).transpose(1, 0, 2, 3)
    shapes = [_blocks(pieces[n], SHARD_AXIS[n]).shape[1:] for n in sharded]
    return parts, (sharded, shapes, spans, replicated, [pieces[n].shape for n in replicated], rep_rows)


def _unpack_grads(mine, info):
    sharded, shapes, spans, _, _, rep_rows = info
    rep_at = spans[-1][0]
    return dict(zip(sharded, _unpack(mine, spans[:-1], shapes, ()))), mine[rep_at:rep_at + rep_rows]


LAYER_INDEXED = ['norm_mix', 'attn_w_qkv', 'attn_w_o', 'norm_ffn', 'ffn_w_up', 'ffn_conv_w', 'ffn_conv_b',
                 'ffn_w_down', 'norm_ple', 'ple_w_gate', 'ple_w_proj']


def kernel(x, p, norm_mix, attn_w_qkv, attn_w_o, rnn_w_in, rnn_conv_w, rnn_conv_b, rnn_w_gate_a, rnn_b_gate_a, rnn_w_gate_x, rnn_b_gate_x, rnn_lru_param, rnn_w_out, norm_ffn, ffn_w_up, ffn_conv_w, ffn_conv_b, ffn_w_down, norm_ple, ple_w_gate, ple_w_proj, norm_final, loss_target, m_norm_mix, m_attn_w_qkv, m_attn_w_o, m_rnn_w_in, m_rnn_conv_w, m_rnn_conv_b, m_rnn_w_gate_a, m_rnn_b_gate_a, m_rnn_w_gate_x, m_rnn_b_gate_x, m_rnn_lru_param, m_rnn_w_out, m_norm_ffn, m_ffn_w_up, m_ffn_conv_w, m_ffn_conv_b, m_ffn_w_down, m_norm_ple, m_ple_w_gate, m_ple_w_proj, m_norm_final, v_norm_mix, v_attn_w_qkv, v_attn_w_o, v_rnn_w_in, v_rnn_conv_w, v_rnn_conv_b, v_rnn_w_gate_a, v_rnn_b_gate_a, v_rnn_w_gate_x, v_rnn_b_gate_x, v_rnn_lru_param, v_rnn_w_out, v_norm_ffn, v_ffn_w_up, v_ffn_conv_w, v_ffn_conv_b, v_ffn_w_down, v_norm_ple, v_ple_w_gate, v_ple_w_proj, v_norm_final):
    given = dict(locals())
    local = {n: given[n] for n in WEIGHTS}
    bsz, seq, d = x.shape
    t = bsz * seq
    depth = norm_mix.shape[0]
    width = rnn_w_out.shape[1] * N_DEV
    ffn = ffn_w_down.shape[1] * N_DEV

    assert depth >= 2
    now = [("attn_w_qkv", 0)]
    later = [(n, j) for n in MATMUL_WEIGHTS for j in range(local[n].shape[0]) if (n, j) not in now]
    full = {n: [None] * local[n].shape[0] for n in MATMUL_WEIGHTS}

    def packed(group):
        return _pack([local[n][j] for n, j in group], BF16, 2 * SUBLANES)

    def place(group, gathered, spans):
        got = _unpack(gathered, spans, [local[n][j].shape for n, j in group], (N_DEV,))
        for (n, j), g in zip(group, got):
            full[n][j] = _whole(g, SHARD_AXIS[n] - 1)

    buf, spans = packed(now)
    place(now, all_gather(buf, name="gather_layer0_weights"), spans)
    later_buf, later_spans = packed(later)
    buf, spans = _pack([local[n] for n in CHANNEL_WEIGHTS], F32, SUBLANES)
    got = _unpack(all_gather(buf, name="gather_channel_weights"), spans,
                  [local[n].shape for n in CHANNEL_WEIGHTS], (N_DEV,))
    full.update({n: _whole(g, SHARD_AXIS[n]) for n, g in zip(CHANNEL_WEIGHTS, got)})
    for n in REPLICATED:
        full[n] = local[n]

    grads = {}

    def stack(name, layer, value, count):
        grads.setdefault(name, [None] * count)[layer] = value

    saved = []
    h0 = x.reshape(t, d)
    for i in range(depth):
        slot = i // 2
        sv = {"x0": h0}
        if i % 2 == 0:
            qkv, hn = mm_rms(h0, full["norm_mix"][i], full["attn_w_qkv"][slot], out_dtype=BF16, name=f"l{i}_norm_qkv")
            if i == 0:
                o, totals, gathered = attn_fwd(qkv, bsz, seq, d, name=f"l{i}_attn", rider=gather_rider(later_buf))
                place(later, gathered, later_spans)
            else:
                o, totals = attn_fwd(qkv, bsz, seq, d, name=f"l{i}_attn")
            h1 = mm(o, full["attn_w_o"][slot], extras=(h0,), epilogue=_add, name=f"l{i}_attn_out")
            sv.update(qkv=qkv, o=o, totals=totals)
        else:
            w_in = full["rnn_w_in"][slot]
            yg, hn = mm_rms(h0, full["norm_mix"][i], w_in[:, :width], name=f"l{i}_norm_rnn_in_gate")
            yr = mm(hn, w_in[:, width:], name=f"l{i}_rnn_in_rec")
            xr = rnn_conv_fwd(yr, full["rnn_conv_w"][slot], full["rnn_conv_b"][slot], seq, name=f"l{i}_rnn_conv")
            wa = _block_diag(full["rnn_w_gate_a"][slot]).astype(BF16)
            wx = _block_diag(full["rnn_w_gate_x"][slot]).astype(BF16)
            ga = mm(xr, wa, name=f"l{i}_rnn_gate_a")
            gi = mm(xr, wx, name=f"l{i}_rnn_gate_x")
            hs, y = rnn_scan_fwd(ga, gi, xr, yg, full["rnn_b_gate_a"][slot], full["rnn_b_gate_x"][slot],
                                 full["rnn_lru_param"][slot], bsz, seq, name=f"l{i}_rnn_scan")
            h1 = mm(y, full["rnn_w_out"][slot], extras=(h0,), epilogue=_add, name=f"l{i}_rnn_out")
            sv.update(yg=yg, yr=yr, xr=xr, wa=wa, wx=wx, ga=ga, gi=gi, hs=hs, y=y)
        sv.update(hn=hn, x1=h1)
        w_up = full["ffn_w_up"][i]
        ug, hn2 = mm_rms(h1, full["norm_ffn"][i], w_up[:, :ffn], out_dtype=BF16, name=f"l{i}_norm_ffn_up_gate")
        uv = mm(hn2, w_up[:, ffn:], out_dtype=BF16, name=f"l{i}_ffn_up_val")
        cw, cb = full["ffn_conv_w"][i], full["ffn_conv_b"][i]
        act = ffn_act_fwd(ug, uv, cw[:, :ffn], cw[:, ffn:], cb[:ffn], cb[ffn:], seq, name=f"l{i}_ffn_act")
        h2 = mm(act, full["ffn_w_down"][i], extras=(h1,), epilogue=_add, name=f"l{i}_ffn_down")
        sv.update(hn2=hn2, ug=ug, uv=uv, act=act, x2=h2)
        pg, hn3 = mm_rms(h2, full["norm_ple"][i], full["ple_w_gate"][i], name=f"l{i}_norm_ple_gate")
        pin = p[i].reshape(t, p.shape[-1])
        h0 = mm(pin, full["ple_w_proj"][i], extras=(h2, pg), epilogue=_ple_mix, name=f"l{i}_ple_proj_mix")
        sv.update(hn3=hn3, pg=pg, pin=pin)
        saved.append(sv)

    dx, g_final, loss_part = final_loss(h0, full["norm_final"], loss_target.reshape(t, d), name="final_loss")
    grads["norm_final"] = g_final
    loss = lax.psum(loss_part, ("x", "y", "c"))

    def later_layers(name):
        return name not in LAYER_INDEXED or len(grads.get(name, ())) > 1

    for i in reversed(range(depth)):
        slot = i // 2
        sv = saved[i]
        if i == 0:
            upper = {n: (jnp.stack(grads[n][1:]) if n in LAYER_INDEXED else
                         jnp.stack(grads[n]) if isinstance(grads[n], list) else grads[n])
                     for n in WEIGHTS if later_layers(n)}
            upper_parts, upper_info = _pack_grads(upper)
        dpg, dpe = ple_bwd(dx, sv["pg"], sv["pin"], full["ple_w_proj"][i], name=f"l{i}_ple_mix_bwd")
        stack("ple_w_proj", i, mm(sv["pin"], dpe, ta=True, name=f"l{i}_ple_proj_wgrad"), depth)
        stack("ple_w_gate", i, mm(sv["hn3"], dpg, ta=True, name=f"l{i}_ple_gate_wgrad"), depth)
        dx, gn = mm_rms_bwd(dpg, full["ple_w_gate"][i], sv["x2"], full["norm_ple"][i], dx,
                            name=f"l{i}_ple_gate_dgrad_norm_bwd")
        stack("norm_ple", i, gn, depth)
        stack("ffn_w_down", i, mm(sv["act"], dx, ta=True, name=f"l{i}_ffn_down_wgrad"), depth)
        dact = mm(dx, full["ffn_w_down"][i], tb=True, out_dtype=BF16, name=f"l{i}_ffn_down_dgrad")
        cw, cb = full["ffn_conv_w"][i], full["ffn_conv_b"][i]
        taps = cw.shape[0]
        ride = sibling_rider(upper_parts) if i == 0 else None
        dug, duv, sg, svv, *rode = ffn_act_bwd(sv["ug"], sv["uv"], cw[:, :ffn], cw[:, ffn:], cb[:ffn], cb[ffn:], dact,
                                               seq, name=f"l{i}_ffn_act_bwd", rider=ride)
        if i == 0:
            upper_sum, upper_sum_bf16 = add_sibling(upper_parts, rode[0], name="upper_grads_add_sibling")
        stack("ffn_conv_w", i, jnp.concatenate([sg[:taps], svv[:taps]], axis=1), depth)
        stack("ffn_conv_b", i, jnp.concatenate([sg[taps], svv[taps]], axis=0), depth)
        stack("ffn_w_up", i, jnp.concatenate(
            [mm(sv["hn2"], dug, ta=True, name=f"l{i}_ffn_up_wgrad_gate"),
             mm(sv["hn2"], duv, ta=True, name=f"l{i}_ffn_up_wgrad_val")], axis=1), depth)
        w_up = full["ffn_w_up"][i]
        dhn2 = mm(dug, w_up[:, :ffn], tb=True, name=f"l{i}_ffn_up_dgrad_gate")
        dx, gn = mm_rms_bwd(duv, w_up[:, ffn:], sv["x1"], full["norm_ffn"][i], dx, prev=dhn2,
                            name=f"l{i}_ffn_up_dgrad_val_norm_bwd")
        stack("norm_ffn", i, gn, depth)
        if i % 2 == 0:
            stack("attn_w_o", slot, mm(sv["o"], dx, ta=True, name=f"l{i}_attn_out_wgrad"), depth // 2)
            do = mm(dx, full["attn_w_o"][slot], tb=True, out_dtype=BF16, name=f"l{i}_attn_out_dgrad")
            ride = chip_rider(upper_sum_bf16) if i == 0 else None
            dq, dk, dv, *rode = attn_bwd(sv["qkv"], sv["totals"], do, bsz, seq, d, name=f"l{i}_attn_bwd", rider=ride)
            if i == 0:
                upper_mine = add_chips(upper_sum, rode[0], name="upper_grads_add_chips")
            dqkv = jnp.concatenate([dq, dk, dv], axis=1).astype(BF16)
            stack("attn_w_qkv", slot, mm(sv["hn"], dqkv, ta=True, name=f"l{i}_qkv_wgrad"), depth // 2)
            dx, gn = mm_rms_bwd(dqkv, full["attn_w_qkv"][slot], sv["x0"], full["norm_mix"][i], dx,
                                name=f"l{i}_qkv_dgrad_norm_bwd")
        else:
            nrnn = depth // 2
            stack("rnn_w_out", slot, mm(sv["y"], dx, ta=True, name=f"l{i}_rnn_out_wgrad"), nrnn)
            dy = mm(dx, full["rnn_w_out"][slot], tb=True, name=f"l{i}_rnn_out_dgrad")
            dyg, dga, dgi, dxr, stats = rnn_scan_bwd(
                dy, sv["ga"], sv["gi"], sv["xr"], sv["yg"], sv["hs"], full["rnn_b_gate_a"][slot],
                full["rnn_b_gate_x"][slot], full["rnn_lru_param"][slot], bsz, seq, name=f"l{i}_rnn_scan_bwd")
            stack("rnn_b_gate_a", slot, stats[0], nrnn)
            stack("rnn_b_gate_x", slot, stats[1], nrnn)
            stack("rnn_lru_param", slot, stats[2], nrnn)
            stack("rnn_w_gate_a", slot, _diag_blocks(mm(sv["xr"], dga, ta=True, name=f"l{i}_rnn_gate_a_wgrad"),
                                                     RNN_HEADS), nrnn)
            stack("rnn_w_gate_x", slot, _diag_blocks(mm(sv["xr"], dgi, ta=True, name=f"l{i}_rnn_gate_x_wgrad"),
                                                     RNN_HEADS), nrnn)
            dxr = mm(dga, sv["wa"], tb=True, extras=(dxr,), epilogue=_add, name=f"l{i}_rnn_gate_a_dgrad")
            dxr = mm(dgi, sv["wx"], tb=True, extras=(dxr,), epilogue=_add, name=f"l{i}_rnn_gate_x_dgrad")
            rcw = full["rnn_conv_w"][slot]
            rtaps = rcw.shape[0]
            cstats = rnn_conv_wgrad(dxr, sv["yr"], seq, rtaps, name=f"l{i}_rnn_conv_wgrad")
            stack("rnn_conv_w", slot, cstats[:rtaps], nrnn)
            stack("rnn_conv_b", slot, cstats[rtaps], nrnn)
            dyr = conv_input_grad(dxr, rcw, seq, out_dtype=BF16, name=f"l{i}_rnn_conv_bwd")
            stack("rnn_w_in", slot, jnp.concatenate(
                [mm(sv["hn"], dyg, ta=True, name=f"l{i}_rnn_in_wgrad_gate"),
                 mm(sv["hn"], dyr, ta=True, name=f"l{i}_rnn_in_wgrad_rec")], axis=1), nrnn)
            w_in = full["rnn_w_in"][slot]
            dhn = mm(dyg, w_in[:, :width], tb=True, name=f"l{i}_rnn_in_dgrad_gate")
            dx, gn = mm_rms_bwd(dyr, w_in[:, width:], sv["x0"], full["norm_mix"][i], dx, prev=dhn,
                                name=f"l{i}_rnn_in_dgrad_rec_norm_bwd")
        stack("norm_mix", i, gn, depth)
    grad_x = dx.reshape(bsz, seq, d)

    lower = {n: grads[n][0][None] for n in LAYER_INDEXED}
    lower_parts, lower_info = _pack_grads(lower)
    from_sibling = _alone(sibling_rider(lower_parts), name="grads_to_sibling")[0]
    lower_sum, lower_sum_bf16 = add_sibling(lower_parts, from_sibling, name="grads_add_sibling")
    from_chips = _alone(chip_rider(lower_sum_bf16), name="grads_to_chips")[0]
    lower_mine = add_chips(lower_sum, from_chips, name="grads_add_chips")
    upper_local, upper_rep = _unpack_grads(upper_mine, upper_info)
    lower_local, lower_rep = _unpack_grads(lower_mine, lower_info)
    local_grads = {n: (jnp.concatenate([lower_local[n], upper_local[n]], axis=0) if n in upper_local
                       else lower_local[n]) if n in lower_local else upper_local[n]
                   for n in WEIGHTS if n in SHARD_AXIS}
    rep_all = all_gather(jnp.concatenate([upper_rep, lower_rep], axis=0), name="gather_replicated_grads")
    rep_vecs = {}
    for key, info, rows_at in (("upper", upper_info, 0), ("lower", lower_info, upper_rep.shape[0])):
        vec = rep_all[:, rows_at:rows_at + info[5]].reshape(-1)
        at = 0
        for n, shape in zip(info[3], info[4]):
            rep_vecs[key, n] = vec[at:at + math.prod(shape)].reshape(shape)
            at += math.prod(shape)
    for n in REPLICATED:
        both = [rep_vecs[k, n] for k in ("lower", "upper") if (k, n) in rep_vecs]
        local_grads[n] = both[0] if n not in LAYER_INDEXED else jnp.concatenate(both, axis=0)

    deltas, new_m, new_v = {}, {}, {}
    for n in WEIGHTS:
        deltas[n], new_m[n], new_v[n] = adamw(local[n], local_grads[n], given["m_" + n], given["v_" + n],
                                              name=f"adamw_{n}")
    return (loss, grad_x, *[local_grads[n] for n in WEIGHTS], *[deltas[n] for n in WEIGHTS],
            *[new_m[n] for n in WEIGHTS], *[new_v[n] for n in WEIGHTS])
```

```python
import functools
import math
from typing import Callable, NamedTuple

import jax
import jax.numpy as jnp
from jax import lax
from jax.experimental import pallas as pl
from jax.experimental.pallas import tpu as pltpu

F32 = jnp.float32
BF16 = jnp.bfloat16

EPS = 1e-6
HEAD_DIM = 64
RNN_HEADS = 16
LRU_C = 8.0
ADAM_LR = 0.001
ADAM_B1 = 0.9
ADAM_B2 = 0.999
ADAM_EPS = 1e-08
ADAM_WD = 0.01
ADAM_STEP = 10

N_DEV = 8
LANES = 128
SUBLANES = 8
VMEM_LIMIT = 56 * 1024 * 1024
MESH = pl.DeviceIdType.MESH
GRAD_ROWS_TILE = 2048
GELU_C = math.sqrt(2.0 / math.pi)
GELU_A = 0.044715

WEIGHTS = ['norm_mix', 'attn_w_qkv', 'attn_w_o', 'rnn_w_in', 'rnn_conv_w', 'rnn_conv_b', 'rnn_w_gate_a',
           'rnn_b_gate_a', 'rnn_w_gate_x', 'rnn_b_gate_x', 'rnn_lru_param', 'rnn_w_out', 'norm_ffn', 'ffn_w_up',
           'ffn_conv_w', 'ffn_conv_b', 'ffn_w_down', 'norm_ple', 'ple_w_gate', 'ple_w_proj', 'norm_final']
SHARD_AXIS = {'attn_w_qkv': 2, 'attn_w_o': 1, 'rnn_w_in': 2, 'rnn_conv_w': 2, 'rnn_conv_b': 1, 'rnn_b_gate_a': 1,
              'rnn_b_gate_x': 1, 'rnn_lru_param': 1, 'rnn_w_out': 1, 'ffn_w_up': 2, 'ffn_conv_w': 2,
              'ffn_w_down': 1, 'ple_w_gate': 1, 'ple_w_proj': 2}
MATMUL_WEIGHTS = ['attn_w_qkv', 'attn_w_o', 'rnn_w_in', 'rnn_w_out', 'ffn_w_up', 'ffn_w_down', 'ple_w_gate',
                  'ple_w_proj']
CHANNEL_WEIGHTS = ['rnn_conv_w', 'rnn_conv_b', 'rnn_b_gate_a', 'rnn_b_gate_x', 'rnn_lru_param', 'ffn_conv_w']
REPLICATED = [n for n in WEIGHTS if n not in SHARD_AXIS]


def _params(*sem):
    return pltpu.CompilerParams(dimension_semantics=sem, vmem_limit_bytes=VMEM_LIMIT)


def _tile(dim, pref, align=LANES):
    if dim <= pref:
        return dim
    t = (pref + pref // 2) // align * align
    while t >= align:
        if dim % t == 0:
            return t
        t -= align
    return dim


def _gelu(x):
    return 0.5 * x * (1.0 + jnp.tanh(GELU_C * (x + GELU_A * x * x * x)))


def _gelu_and_grad(x):
    t = jnp.tanh(GELU_C * (x + GELU_A * x * x * x))
    g = 0.5 * x * (1.0 + t)
    dg = 0.5 * (1.0 + t) + 0.5 * x * (1.0 - t * t) * GELU_C * (1.0 + 3.0 * GELU_A * x * x)
    return g, dg


def _log_sigmoid(x):
    return jnp.minimum(x, 0.0) - jnp.log(1.0 + jnp.exp(-jnp.abs(x)))


MM_VMEM_BUDGET = 36 * 1024 * 1024


def _mm_tiles(m, n, k, ta, a_item, b_item, out_item, n_extra):
    if ta:
        return _tile(m, 1024), _tile(n, 1024), _tile(k, 1024)
    row_bytes = k * a_item + n * (out_item + 4 * n_extra)
    w_bytes = k * n * b_item
    for tm in (1024, 512, 256, 128):
        if m % tm == 0 and 2 * tm * row_bytes + 2 * w_bytes + tm * n * 4 <= MM_VMEM_BUDGET:
            return tm, n, k
    return _tile(m, 512), _tile(n, 512), _tile(k, 1024)


def mm(a, b, *, name, ta=False, tb=False, out_dtype=F32, extras=(), epilogue=None):
    m, k = (a.shape[1], a.shape[0]) if ta else a.shape
    n = b.shape[0] if tb else b.shape[1]
    assert k == (b.shape[1] if tb else b.shape[0]), (a.shape, b.shape, ta, tb)
    tm, tn, tk = _mm_tiles(m, n, k, ta, a.dtype.itemsize, b.dtype.itemsize, jnp.dtype(out_dtype).itemsize,
                           len(extras))
    nk = k // tk
    n_extra = len(extras)
    dims = (((0 if ta else 1,), (1 if tb else 0,)), ((), ()))

    def body(a_ref, b_ref, *rest):
        extra_refs, o_ref = rest[:n_extra], rest[n_extra]

        def finish(acc):
            if epilogue is not None:
                acc = epilogue(acc, *[e[...] for e in extra_refs])
            o_ref[...] = acc.astype(o_ref.dtype)

        part = lax.dot_general(a_ref[...].astype(BF16), b_ref[...].astype(BF16), dims,
                               preferred_element_type=F32)
        if nk == 1:
            finish(part)
        else:
            acc_ref = rest[n_extra + 1]
            kk = pl.program_id(2)

            @pl.when(kk == 0)
            def _():
                acc_ref[...] = part

            @pl.when(kk > 0)
            def _():
                acc_ref[...] += part

            @pl.when(kk == nk - 1)
            def _():
                finish(acc_ref[...])

    a_spec = pl.BlockSpec((tk, tm), lambda i, j, kk: (kk, i)) if ta else pl.BlockSpec((tm, tk), lambda i, j, kk: (i, kk))
    b_spec = pl.BlockSpec((tn, tk), lambda i, j, kk: (j, kk)) if tb else pl.BlockSpec((tk, tn), lambda i, j, kk: (kk, j))
    o_spec = pl.BlockSpec((tm, tn), lambda i, j, kk: (i, j))
    return pl.pallas_call(
        body, name=name, grid=(m // tm, n // tn, nk),
        in_specs=[a_spec, b_spec] + [o_spec] * n_extra, out_specs=o_spec,
        out_shape=jax.ShapeDtypeStruct((m, n), out_dtype),
        scratch_shapes=[pltpu.VMEM((tm, tn), F32)] if nk > 1 else [],
        compiler_params=_params("parallel", "parallel", "arbitrary"),
    )(a, b, *extras)


def _add(acc, res):
    return acc + res


def mm_rms(x, g, w, *, name, out_dtype=F32):
    t, d = x.shape
    n = w.shape[1]
    out_item = jnp.dtype(out_dtype).itemsize
    row_bytes = d * 4 + d * 2 + n * out_item
    tr = next(tm for tm in (512, 256, 128, t)
              if t % tm == 0 and 2 * tm * row_bytes + 2 * w.size * w.dtype.itemsize + tm * n * 4 <= MM_VMEM_BUDGET)

    def body(x_ref, g_ref, w_ref, o_ref, hn_ref):
        xv = x_ref[...]
        r = lax.rsqrt(jnp.mean(xv * xv, axis=-1, keepdims=True) + EPS)
        hn = (xv * r * g_ref[...]).astype(BF16)
        hn_ref[...] = hn
        o_ref[...] = jnp.dot(hn, w_ref[...].astype(BF16), preferred_element_type=F32).astype(o_ref.dtype)

    row = pl.BlockSpec((tr, d), lambda i: (i, 0))
    return pl.pallas_call(
        body, name=name, grid=(t // tr,),
        in_specs=[row, pl.BlockSpec((1, d), lambda i: (0, 0)), pl.BlockSpec((d, n), lambda i: (0, 0))],
        out_specs=[pl.BlockSpec((tr, n), lambda i: (i, 0)), row],
        out_shape=[jax.ShapeDtypeStruct((t, n), out_dtype), jax.ShapeDtypeStruct((t, d), BF16)],
        compiler_params=_params("parallel"),
    )(x, g.reshape(1, d), w)


def mm_rms_bwd(pairs, x, g, dres, *, name):
    t, d = x.shape
    assert all(a.shape == (t, w.shape[1]) and w.shape[0] == d for a, w in pairs)
    row_bytes = sum(a.shape[1] * a.dtype.itemsize for a, _ in pairs) + d * 4 * 3
    w_bytes = sum(w.size * w.dtype.itemsize for _, w in pairs)
    tr = next(tm for tm in (512, 256, 128, t)
              if t % tm == 0 and 2 * tm * row_bytes + 2 * w_bytes + tm * d * 4 <= MM_VMEM_BUDGET)
    n_pairs = len(pairs)

    def body(*refs):
        x_ref, g_ref, dres_ref, dx_ref, dg_ref = refs[2 * n_pairs:]
        dhv = sum(lax.dot_general(refs[2 * p][...].astype(BF16), refs[2 * p + 1][...].astype(BF16), _NT,
                                  preferred_element_type=F32) for p in range(n_pairs))
        xv = x_ref[...]
        r = lax.rsqrt(jnp.mean(xv * xv, axis=-1, keepdims=True) + EPS)
        xh = xv * r
        u = dhv * g_ref[...]
        dx_ref[...] = dres_ref[...] + r * (u - xh * jnp.mean(u * xh, axis=-1, keepdims=True))
        part = jnp.sum(dhv * xh, axis=0, keepdims=True)

        @pl.when(pl.program_id(0) == 0)
        def _():
            dg_ref[...] = part

        @pl.when(pl.program_id(0) > 0)
        def _():
            dg_ref[...] += part

    row = pl.BlockSpec((tr, d), lambda i: (i, 0))
    vec = pl.BlockSpec((1, d), lambda i: (0, 0))
    pair_specs = [spec for a, w in pairs for spec in (pl.BlockSpec((tr, a.shape[1]), lambda i: (i, 0)),
                                                       pl.BlockSpec(w.shape, lambda i: (0, 0)))]
    dx, dg = pl.pallas_call(
        body, name=name, grid=(t // tr,), in_specs=pair_specs + [row, vec, row], out_specs=[row, vec],
        out_shape=[jax.ShapeDtypeStruct((t, d), F32), jax.ShapeDtypeStruct((1, d), F32)],
        compiler_params=_params("arbitrary"),
    )(*[m for pair in pairs for m in pair], x, g.reshape(1, d), dres)
    return dx, dg.reshape(d)


def final_loss(x, g, target, *, name):
    t, d = x.shape
    tr = _tile(t, 512, SUBLANES)

    def body(x_ref, g_ref, t_ref, dx_ref, dg_ref, loss_ref):
        xv = x_ref[...]
        gv = g_ref[...]
        r = lax.rsqrt(jnp.mean(xv * xv, axis=-1, keepdims=True) + EPS)
        xh = xv * r
        err = xh * gv - t_ref[...]
        dy = err * (1.0 / d)
        u = dy * gv
        dx_ref[...] = r * (u - xh * jnp.mean(u * xh, axis=-1, keepdims=True))
        dg_part = jnp.sum(dy * xh, axis=0, keepdims=True)
        loss_part = jnp.zeros((1, LANES), F32) + (0.5 / d) * jnp.sum(err * err)

        @pl.when(pl.program_id(0) == 0)
        def _():
            dg_ref[...] = dg_part
            loss_ref[...] = loss_part

        @pl.when(pl.program_id(0) > 0)
        def _():
            dg_ref[...] += dg_part
            loss_ref[...] += loss_part

    row = pl.BlockSpec((tr, d), lambda i: (i, 0))
    vec = pl.BlockSpec((1, d), lambda i: (0, 0))
    dx, dg, loss = pl.pallas_call(
        body, name=name, grid=(t // tr,), in_specs=[row, vec, row],
        out_specs=[row, vec, pl.BlockSpec((1, LANES), lambda i: (0, 0))],
        out_shape=[jax.ShapeDtypeStruct((t, d), F32), jax.ShapeDtypeStruct((1, d), F32),
                   jax.ShapeDtypeStruct((1, LANES), F32)],
        compiler_params=_params("arbitrary"),
    )(x, g.reshape(1, d), target)
    return dx, dg.reshape(d), loss[0, 0]


def _split_dot(x, mat, left):
    hi = x.astype(BF16)
    lo = (x - hi.astype(F32)).astype(BF16)
    if left:
        return (jnp.dot(mat, hi, preferred_element_type=F32) + jnp.dot(mat, lo, preferred_element_type=F32))
    return (jnp.dot(hi, mat, preferred_element_type=F32) + jnp.dot(lo, mat, preferred_element_type=F32))


_NT = (((1,), (1,)), ((), ()))
_TN = (((0,), (0,)), ((), ()))
HEADS_PER_STEP = LANES // HEAD_DIM


def attn_fwd(qkv, b, s, d, *, name, rider=None):
    t = b * s
    tq = min(256, s)
    nq = s // tq
    pairs = d // LANES
    scale = HEAD_DIM ** -0.5

    grid = (b, pairs, nq)
    n_ride = len(rider.inputs) if rider else 0

    def body(*refs):
        q_ref, k_ref, v_ref = refs[:3]
        o_ref, lt_ref = refs[3 + n_ride:5 + n_ride]
        _ride(rider, refs, 3, 2, grid)
        i = pl.program_id(2)
        row = lax.broadcasted_iota(jnp.int32, (tq, tq), 0)
        col = lax.broadcasted_iota(jnp.int32, (tq, tq), 1)
        later = (row > col).astype(BF16)
        causal = col < row
        lanes = [slice(HEAD_DIM * h, HEAD_DIM * (h + 1)) for h in range(HEADS_PER_STEP)]
        qs = [(q_ref[:, sl].astype(F32) * scale).astype(BF16) for sl in lanes]

        def block(js, carry, diag):
            starts = [pl.multiple_of(j * tq, tq) for j in js]
            hs = range(HEADS_PER_STEP)
            chains = [(n, h) for n in range(len(js)) for h in hs]
            masked = [c for c in chains if diag and c[0] == 0]
            kbs = {(n, h): k_ref[pl.ds(starts[n], tq), lanes[h]] for n, h in chains}
            vbs = {(n, h): v_ref[pl.ds(starts[n], tq), lanes[h]] for n, h in chains}
            zs = {c: lax.dot_general(qs[c[1]], kbs[c], _NT, preferred_element_type=F32) for c in chains}
            lss = {c: _log_sigmoid(zs[c]) for c in chains}
            lks = {c: lss[c] - zs[c] for c in chains}
            for c in masked:
                lks[c] = jnp.where(causal, lks[c], 0.0)
            sums = {c: _split_dot(lks[c], later, left=False) for c in chains}
            runs_in, runs = {}, []
            for h in hs:
                run = carry[h][0]
                for n in range(len(js)):
                    runs_in[n, h] = run
                    run = run + (sums[n, h][:, 0:1] + lks[n, h][:, 0:1])
                runs.append(run)
            ws = {c: jnp.exp(lss[c] + sums[c] + runs_in[c]) for c in chains}
            for c in masked:
                ws[c] = jnp.where(causal, ws[c], 0.0)
            pvs = {c: jnp.dot(ws[c].astype(BF16), vbs[c], preferred_element_type=F32) for c in chains}
            accs = [carry[h][1] + sum(pvs[n, h] for n in range(len(js))) for h in hs]
            return tuple(zip(runs, accs))

        zero = (jnp.zeros((tq, 1), F32), jnp.zeros((tq, HEAD_DIM), F32))
        odd = i % 2
        carry = lax.cond(odd == 1, lambda c: block([i, i - 1], c, True), lambda c: block([i], c, True),
                         (zero,) * HEADS_PER_STEP)
        near = i - 1 - odd
        carry = lax.fori_loop(0, i // 2, lambda n, c: block([near - 2 * n, near - 2 * n - 1], c, False), carry)
        eye = (row == col).astype(F32)
        for h, sl in enumerate(lanes):
            run, acc = carry[h]
            o_ref[:, sl] = acc.astype(o_ref.dtype)
            lt_ref[SUBLANES * h:SUBLANES * (h + 1), :] = lax.dot_general(
                jnp.broadcast_to(run, (tq, SUBLANES)), eye, _TN, precision=lax.Precision.HIGHEST,
                preferred_element_type=F32)

    q_spec = pl.BlockSpec((tq, LANES), lambda bb, p, i: (bb * nq + i, p))
    k_spec = pl.BlockSpec((s, LANES), lambda bb, p, i: (bb, pairs + p))
    v_spec = pl.BlockSpec((s, LANES), lambda bb, p, i: (bb, 2 * pairs + p))
    lt_spec = pl.BlockSpec((None, None, None, HEADS_PER_STEP * SUBLANES, tq), lambda bb, p, i: (bb, p, i, 0, 0))
    in_specs, out_specs, out_shape, scratch = _rider_specs(
        rider, [q_spec, k_spec, v_spec], [q_spec, lt_spec],
        [jax.ShapeDtypeStruct((t, d), BF16),
         jax.ShapeDtypeStruct((b, pairs, nq, HEADS_PER_STEP * SUBLANES, tq), F32)])
    order = ("arbitrary",) * 3 if rider else ("parallel", "parallel", "arbitrary")
    return pl.pallas_call(
        body, name=name, grid=grid, in_specs=in_specs, out_specs=out_specs, out_shape=out_shape,
        scratch_shapes=scratch, compiler_params=_params(*order),
    )(qkv, qkv, qkv, *(rider.inputs if rider else ()))


def attn_bwd(qkv, totals, do, b, s, d, *, name, rider=None):
    t = b * s
    tq = min(256, s)
    nq = s // tq
    pairs = d // LANES
    scale = HEAD_DIM ** -0.5

    grid = (b, pairs, nq)
    n_ride = len(rider.inputs) if rider else 0

    def body(*refs):
        q_ref, k_ref, v_ref, lt_ref, do_ref = refs[:5]
        dq_ref, dk_ref, dv_ref = refs[5 + n_ride:8 + n_ride]
        _ride(rider, refs, 5, 3, grid)
        i = pl.program_id(2)

        @pl.when(i == 0)
        def _():
            dk_ref[...] = jnp.zeros_like(dk_ref)
            dv_ref[...] = jnp.zeros_like(dv_ref)

        row = lax.broadcasted_iota(jnp.int32, (tq, tq), 0)
        col = lax.broadcasted_iota(jnp.int32, (tq, tq), 1)
        upto = (col <= row).astype(BF16)
        earlier = (col < row).astype(BF16)
        causal = row < col
        lanes = [slice(HEAD_DIM * h, HEAD_DIM * (h + 1)) for h in range(HEADS_PER_STEP)]
        qs = [(q_ref[:, sl].astype(F32) * scale).astype(BF16) for sl in lanes]
        dos = [do_ref[:, sl].astype(BF16) for sl in lanes]
        totals_h = [lt_ref[SUBLANES * h:SUBLANES * h + 1, :] for h in range(HEADS_PER_STEP)]

        def block(js, carry, diag):
            starts = [pl.multiple_of(j * tq, tq) for j in js]
            hs = range(HEADS_PER_STEP)
            ns = range(len(js))
            chains = [(n, h) for n in ns for h in hs]
            masked = [c for c in chains if diag and c[0] == len(js) - 1]
            kbs = {(n, h): k_ref[pl.ds(starts[n], tq), lanes[h]] for n, h in chains}
            vbs = {(n, h): v_ref[pl.ds(starts[n], tq), lanes[h]] for n, h in chains}
            zs = {c: lax.dot_general(kbs[c], qs[c[1]], _NT, preferred_element_type=F32) for c in chains}
            dws = {c: lax.dot_general(vbs[c], dos[c[1]], _NT, preferred_element_type=F32) for c in chains}
            lss = {c: _log_sigmoid(zs[c]) for c in chains}
            lks = {c: lss[c] - zs[c] for c in chains}
            for c in masked:
                lks[c] = jnp.where(causal, lks[c], 0.0)
            sums = {c: _split_dot(lks[c], upto, left=True) for c in chains}
            runs_in, runs = {}, []
            for h in hs:
                run = carry[h][0]
                for n in ns:
                    runs_in[n, h] = run
                    run = run + sums[n, h][tq - 1:tq, :]
                runs.append(run)
            ws = {c: jnp.exp(lss[c] + ((totals_h[c[1]] - runs_in[c]) - sums[c])) for c in chains}
            for c in masked:
                ws[c] = jnp.where(causal, ws[c], 0.0)
            gs = {c: dws[c] * ws[c] for c in chains}
            gsums = {c: _split_dot(gs[c], earlier, left=True) for c in chains}
            gruns_in, gruns = {}, []
            for h in hs:
                grun = carry[h][1]
                for n in ns:
                    gruns_in[n, h] = grun
                    grun = grun + (gsums[n, h][tq - 1:tq, :] + gs[n, h][tq - 1:tq, :])
                gruns.append(grun)
            dzs = {c: gs[c] - jnp.exp(lss[c]) * (gs[c] + (gruns_in[c] + gsums[c])) for c in chains}
            for c in masked:
                dzs[c] = jnp.where(causal, dzs[c], 0.0)
            dzbs = {c: dzs[c].astype(BF16) for c in chains}
            for n, h in chains:
                dv_ref[pl.ds(starts[n], tq), lanes[h]] += jnp.dot(ws[n, h].astype(BF16), dos[h],
                                                                  preferred_element_type=F32)
                dk_ref[pl.ds(starts[n], tq), lanes[h]] += jnp.dot(dzbs[n, h], qs[h], preferred_element_type=F32)
            dqs = [carry[h][2] + sum(lax.dot_general(dzbs[n, h], kbs[n, h], _TN, preferred_element_type=F32)
                                     for n in ns) for h in hs]
            return tuple(zip(runs, gruns, dqs))

        zero = (jnp.zeros((1, tq), F32), jnp.zeros((1, tq), F32), jnp.zeros((tq, HEAD_DIM), F32))
        carry = lax.fori_loop(0, i // 2, lambda n, c: block([2 * n, 2 * n + 1], c, False),
                              (zero,) * HEADS_PER_STEP)
        carry = lax.cond(i % 2 == 1, lambda c: block([i - 1, i], c, True), lambda c: block([i], c, True), carry)
        for h, sl in enumerate(lanes):
            dq_ref[:, sl] = carry[h][2] * scale

    q_spec = pl.BlockSpec((tq, LANES), lambda bb, p, i: (bb * nq + i, p))
    k_spec = pl.BlockSpec((s, LANES), lambda bb, p, i: (bb, pairs + p))
    v_spec = pl.BlockSpec((s, LANES), lambda bb, p, i: (bb, 2 * pairs + p))
    lt_spec = pl.BlockSpec((None, None, None) + totals.shape[3:], lambda bb, p, i: (bb, p, i, 0, 0))
    kv_out = pl.BlockSpec((s, LANES), lambda bb, p, i: (bb, p))
    out = jax.ShapeDtypeStruct((t, d), F32)
    in_specs, out_specs, out_shape, scratch = _rider_specs(
        rider, [q_spec, k_spec, v_spec, lt_spec, q_spec], [q_spec, kv_out, kv_out], [out, out, out])
    order = ("arbitrary",) * 3 if rider else ("parallel", "parallel", "arbitrary")
    return pl.pallas_call(
        body, name=name, grid=grid, in_specs=in_specs, out_specs=out_specs, out_shape=out_shape,
        scratch_shapes=scratch, compiler_params=_params(*order),
    )(qkv, qkv, qkv, totals, do, *(rider.inputs if rider else ()))


def _shift_down(cur, prev8, dist):
    ext = jnp.concatenate([prev8, cur], axis=0)
    return pltpu.roll(ext, dist, 0)[SUBLANES:]


def _shift_up(cur, next8, dist):
    ext = jnp.concatenate([cur, next8], axis=0)
    return pltpu.roll(ext, ext.shape[0] - dist, 0)[:cur.shape[0]]


def _delayed(cur, prev8, taps):
    return [cur if dist == 0 else _shift_down(cur, prev8, dist) for dist in range(taps - 1, -1, -1)]


def _causal_conv(cur, prev8, w_ref, b_ref, delayed=None):
    taps = w_ref.shape[0]
    delayed = _delayed(cur, prev8, taps) if delayed is None else delayed
    out = delayed[taps - 1] * w_ref[taps - 1:taps, :] + b_ref[...]
    for k in range(taps - 1):
        out = out + delayed[k] * w_ref[k:k + 1, :]
    return out


def _conv_specs(t, rows, tc, time_axis, dtype=F32):
    sub = SUBLANES * (4 // jnp.dtype(dtype).itemsize)
    per = rows // sub
    last = t // sub - 1

    def grid_ids(*ids):
        return ids[time_axis], ids[1 - time_axis]

    def cur(*ids):
        return grid_ids(*ids)

    def prev(*ids):
        i, j = grid_ids(*ids)
        return (jnp.maximum(i * per - 1, 0), j)

    def nxt(*ids):
        i, j = grid_ids(*ids)
        return (jnp.minimum((i + 1) * per, last), j)

    def chan(*ids):
        return (0, grid_ids(*ids)[1])

    return pl.BlockSpec((rows, tc), cur), pl.BlockSpec((sub, tc), prev), pl.BlockSpec((sub, tc), nxt), chan


def _rows_before(ref, keep):
    return ref[...].astype(F32)[-SUBLANES:] * keep


def _rows_after(ref, keep):
    return ref[...].astype(F32)[:SUBLANES] * keep


def _first_in_seq(i, rows, s):
    return (i % (s // rows)) == 0


def _last_in_seq(i, rows, s):
    return (i % (s // rows)) == (s // rows - 1)


FFN_ACT_ROWS, FFN_ACT_COLS = 256, 1024


def ffn_act_fwd(ug, uv, cwg, cwv, cbg, cbv, s, *, name):
    t, f = ug.shape
    rows, tc = _tile(s, FFN_ACT_ROWS, SUBLANES), _tile(f, FFN_ACT_COLS)
    cur, prev, _, chan = _conv_specs(t, rows, tc, 0, ug.dtype)
    taps = cwg.shape[0]

    def body(ug_ref, ugp_ref, uv_ref, uvp_ref, cwg_ref, cwv_ref, cbg_ref, cbv_ref, a_ref):
        keep = jnp.where(_first_in_seq(pl.program_id(0), rows, s), 0.0, 1.0)
        gate = _causal_conv(ug_ref[...].astype(F32), _rows_before(ugp_ref, keep), cwg_ref, cbg_ref)
        val = _causal_conv(uv_ref[...].astype(F32), _rows_before(uvp_ref, keep), cwv_ref, cbv_ref)
        a_ref[...] = (_gelu(gate) * val).astype(a_ref.dtype)

    wspec = pl.BlockSpec((taps, tc), chan)
    bspec = pl.BlockSpec((1, tc), chan)
    return pl.pallas_call(
        body, name=name, grid=(t // rows, f // tc), in_specs=[cur, prev, cur, prev, wspec, wspec, bspec, bspec],
        out_specs=cur, out_shape=jax.ShapeDtypeStruct((t, f), BF16), compiler_params=_params("parallel", "parallel"),
    )(ug, ug, uv, uv, cwg, cwv, cbg.reshape(1, f), cbv.reshape(1, f))


def _accumulate_rows(first, ref, rows):
    for k, r in enumerate(rows):
        @pl.when(first)
        def _(k=k, r=r):
            ref[k:k + 1, :] = r

        @pl.when(jnp.logical_not(first))
        def _(k=k, r=r):
            ref[k:k + 1, :] += r


def _conv_weight_grads(dc, delayed):
    out = [jnp.sum(dc * xs, axis=0, keepdims=True) for xs in delayed]
    out.append(jnp.sum(dc, axis=0, keepdims=True))
    return out


def _conv_transpose(dc_ext, rows, w_ref):
    taps = w_ref.shape[0]
    out = dc_ext[:rows] * w_ref[taps - 1:taps, :]
    for dist in range(1, taps):
        out = out + pltpu.roll(dc_ext, dc_ext.shape[0] - dist, 0)[:rows] * w_ref[taps - 1 - dist:taps - dist, :]
    return out


def ffn_act_bwd(ug, uv, cwg, cwv, cbg, cbv, da, s, *, name, rider=None):
    t, f = ug.shape
    rows, tc = _tile(s, FFN_ACT_ROWS, SUBLANES), _tile(f, FFN_ACT_COLS)
    cur, prev, nxt, chan = _conv_specs(t, rows, tc, 1, ug.dtype)
    taps = cwg.shape[0]

    grid = (f // tc, t // rows)
    n_ride = len(rider.inputs) if rider else 0

    def body(*refs):
        (ug_ref, ugp_ref, ugn_ref, uv_ref, uvp_ref, uvn_ref, cwg_ref, cwv_ref, cbg_ref, cbv_ref,
         da_ref, dan_ref) = refs[:12]
        dug_ref, duv_ref, wg_ref, wv_ref = refs[12 + n_ride:16 + n_ride]
        _ride(rider, refs, 12, 4, grid)
        i = pl.program_id(1)
        keep_before = jnp.where(_first_in_seq(i, rows, s), 0.0, 1.0)
        keep_after = jnp.where(_last_in_seq(i, rows, s), 0.0, 1.0)
        ugp, uvp = _rows_before(ugp_ref, keep_before), _rows_before(uvp_ref, keep_before)
        uge = jnp.concatenate([ug_ref[...].astype(F32), _rows_after(ugn_ref, 1.0)], axis=0)
        uve = jnp.concatenate([uv_ref[...].astype(F32), _rows_after(uvn_ref, 1.0)], axis=0)
        dae = jnp.concatenate([da_ref[...].astype(F32), _rows_after(dan_ref, keep_after)], axis=0)
        ug_delayed, uv_delayed = _delayed(uge, ugp, taps), _delayed(uve, uvp, taps)
        gate = _causal_conv(uge, ugp, cwg_ref, cbg_ref, ug_delayed)
        val = _causal_conv(uve, uvp, cwv_ref, cbv_ref, uv_delayed)
        act, dact = _gelu_and_grad(gate)
        dgate = dae * val * dact
        dval = dae * act
        dug_ref[...] = _conv_transpose(dgate, rows, cwg_ref).astype(dug_ref.dtype)
        duv_ref[...] = _conv_transpose(dval, rows, cwv_ref).astype(duv_ref.dtype)
        _accumulate_rows(i == 0, wg_ref, _conv_weight_grads(dgate[:rows], [x[:rows] for x in ug_delayed]))
        _accumulate_rows(i == 0, wv_ref, _conv_weight_grads(dval[:rows], [x[:rows] for x in uv_delayed]))

    wspec = pl.BlockSpec((taps, tc), chan)
    bspec = pl.BlockSpec((1, tc), chan)
    gspec = pl.BlockSpec((taps + 1, tc), chan)
    act_shape = jax.ShapeDtypeStruct((t, f), BF16)
    stat_shape = jax.ShapeDtypeStruct((taps + 1, f), F32)
    in_specs, out_specs, out_shape, scratch = _rider_specs(
        rider, [cur, prev, nxt, cur, prev, nxt, wspec, wspec, bspec, bspec, cur, nxt], [cur, cur, gspec, gspec],
        [act_shape, act_shape, stat_shape, stat_shape])
    order = ("arbitrary",) * 2 if rider else ("parallel", "arbitrary")
    return pl.pallas_call(
        body, name=name, grid=grid, in_specs=in_specs, out_specs=out_specs, out_shape=out_shape,
        scratch_shapes=scratch, compiler_params=_params(*order),
    )(ug, ug, ug, uv, uv, uv, cwg, cwv, cbg.reshape(1, f), cbv.reshape(1, f), da, da,
      *(rider.inputs if rider else ()))


def conv_input_grad(dc, cw, s, *, name, out_dtype):
    t, f = dc.shape
    rows, tc = _tile(s, 512, SUBLANES), _tile(f, 256)
    cur, _, nxt, chan = _conv_specs(t, rows, tc, 0, dc.dtype)
    taps = cw.shape[0]

    def body(dc_ref, dcn_ref, cw_ref, o_ref):
        keep = jnp.where(_last_in_seq(pl.program_id(0), rows, s), 0.0, 1.0)
        dcc = dc_ref[...].astype(F32)
        dcn = _rows_after(dcn_ref, keep)
        out = dcc * cw_ref[taps - 1:taps, :]
        for dist in range(1, taps):
            out = out + _shift_up(dcc, dcn, dist) * cw_ref[taps - 1 - dist:taps - dist, :]
        o_ref[...] = out.astype(o_ref.dtype)

    return pl.pallas_call(
        body, name=name, grid=(t // rows, f // tc), in_specs=[cur, nxt, pl.BlockSpec((taps, tc), chan)],
        out_specs=cur, out_shape=jax.ShapeDtypeStruct((t, f), out_dtype),
        compiler_params=_params("parallel", "parallel"),
    )(dc, dc, cw)


def rnn_conv_fwd(yr, cw, cb, s, *, name):
    t, w = yr.shape
    rows, tc = _tile(s, 512, SUBLANES), _tile(w, 256)
    cur, prev, _, chan = _conv_specs(t, rows, tc, 0, yr.dtype)
    taps = cw.shape[0]

    def body(y_ref, yp_ref, cw_ref, cb_ref, o_ref):
        keep = jnp.where(_first_in_seq(pl.program_id(0), rows, s), 0.0, 1.0)
        o_ref[...] = _causal_conv(y_ref[...].astype(F32), _rows_before(yp_ref, keep), cw_ref, cb_ref)

    return pl.pallas_call(
        body, name=name, grid=(t // rows, w // tc),
        in_specs=[cur, prev, pl.BlockSpec((taps, tc), chan), pl.BlockSpec((1, tc), chan)], out_specs=cur,
        out_shape=jax.ShapeDtypeStruct((t, w), F32), compiler_params=_params("parallel", "parallel"),
    )(yr, yr, cw, cb.reshape(1, w))


def rnn_conv_wgrad(dxr, yr, s, taps, *, name):
    t, w = yr.shape
    rows, tc = _tile(s, 512, SUBLANES), _tile(w, 256)
    cur, prev, _, chan = _conv_specs(t, rows, tc, 1, yr.dtype)

    def body(d_ref, y_ref, yp_ref, o_ref):
        i = pl.program_id(1)
        keep = jnp.where(_first_in_seq(i, rows, s), 0.0, 1.0)
        grads = _conv_weight_grads(d_ref[...], _delayed(y_ref[...].astype(F32), _rows_before(yp_ref, keep), taps))
        _accumulate_rows(i == 0, o_ref, grads)

    return pl.pallas_call(
        body, name=name, grid=(w // tc, t // rows), in_specs=[cur, cur, prev],
        out_specs=pl.BlockSpec((taps + 1, tc), chan), out_shape=jax.ShapeDtypeStruct((taps + 1, w), F32),
        compiler_params=_params("parallel", "arbitrary"),
    )(dxr, yr, yr)


SCAN_ROWS = 32


def _one_minus_exp(x):
    series = -x * (1.0 + x * (0.5 + x * (1.0 / 6.0)))
    return jnp.where(x > -0.01, series, 1.0 - jnp.exp(x))


def _gates(ga, gi, ba, bx, log_lam):
    ra = jax.nn.sigmoid(ga + ba)
    ri = jax.nn.sigmoid(gi + bx)
    log_a = LRU_C * ra * log_lam
    a = jnp.exp(log_a)
    mult = jnp.sqrt(_one_minus_exp(2.0 * log_a))
    return ra, ri, a, mult


def rnn_scan_fwd(ga, gi, xr, yg, ba, bx, lam, b, s, *, name):
    t, w = xr.shape
    tc = _tile(w, 256)
    rb = min(SCAN_ROWS, s)
    blocks = s // rb
    steps = [1 << e for e in range(rb.bit_length() - 1)]

    def body(ga_ref, gi_ref, xr_ref, yg_ref, ba_ref, bx_ref, lam_ref, h_ref, y_ref):
        log_lam = _log_sigmoid(lam_ref[...])
        ridx = lax.broadcasted_iota(jnp.int32, (rb, tc), 0)

        def step(n, carry):
            rs = pl.ds(pl.multiple_of(n * rb, rb), rb)
            xrv = xr_ref[rs, :]
            _, ri, a, mult = _gates(ga_ref[rs, :], gi_ref[rs, :], ba_ref[...], bx_ref[...], log_lam)
            u = mult * (ri * xrv)
            for dist in steps:
                a_sh = jnp.where(ridx >= dist, pltpu.roll(a, dist, 0), 1.0)
                u_sh = jnp.where(ridx >= dist, pltpu.roll(u, dist, 0), 0.0)
                u = a * u_sh + u
                a = a * a_sh
            hb = u + a * carry
            h_ref[rs, :] = hb
            y_ref[rs, :] = (_gelu(yg_ref[rs, :]) * hb).astype(y_ref.dtype)
            return hb[rb - 1:rb, :]

        lax.fori_loop(0, blocks, step, jnp.zeros((1, tc), F32))

    seq = pl.BlockSpec((s, tc), lambda bb, j: (bb, j))
    vec = pl.BlockSpec((1, tc), lambda bb, j: (0, j))
    return pl.pallas_call(
        body, name=name, grid=(b, w // tc), in_specs=[seq, seq, seq, seq, vec, vec, vec], out_specs=[seq, seq],
        out_shape=[jax.ShapeDtypeStruct((t, w), F32), jax.ShapeDtypeStruct((t, w), BF16)],
        compiler_params=_params("parallel", "parallel"),
    )(ga, gi, xr, yg, ba.reshape(1, w), bx.reshape(1, w), lam.reshape(1, w))


def rnn_scan_bwd(dy, ga, gi, xr, yg, h, ba, bx, lam, b, s, *, name):
    t, w = xr.shape
    tc = _tile(w, 256)
    rb = min(SCAN_ROWS, s)
    blocks = s // rb
    steps = [1 << e for e in range(rb.bit_length() - 1)]

    def body(dy_ref, ga_ref, gi_ref, xr_ref, yg_ref, h_ref, ba_ref, bx_ref, lam_ref,
             dyg_ref, dga_ref, dgi_ref, dxr_ref, stat_ref):
        lamv = lam_ref[...]
        log_lam = _log_sigmoid(lamv)
        dlog_lam = jax.nn.sigmoid(-lamv)
        ridx = lax.broadcasted_iota(jnp.int32, (rb, tc), 0)
        last = rb - 1

        def step(n, carry):
            lam_next, a_next, s_a, s_x, s_l = carry
            blk = blocks - 1 - n
            rs = pl.ds(pl.multiple_of(blk * rb, rb), rb)
            rp = pl.ds(pl.multiple_of(jnp.maximum(blk * rb - SUBLANES, 0), SUBLANES), SUBLANES)
            xrv = xr_ref[rs, :]
            hv = h_ref[rs, :]
            h_before = jnp.where(blk > 0, h_ref[rp, :][SUBLANES - 1:, :], 0.0)
            h_prev = jnp.where(ridx >= 1, pltpu.roll(hv, 1, 0), h_before)
            ra, ri, a, mult = _gates(ga_ref[rs, :], gi_ref[rs, :], ba_ref[...], bx_ref[...], log_lam)
            act, dact = _gelu_and_grad(yg_ref[rs, :])
            dyv = dy_ref[rs, :]
            dyg_ref[rs, :] = (dyv * hv * dact).astype(dyg_ref.dtype)
            v = dyv * act
            c = jnp.where(ridx < last, pltpu.roll(a, last, 0), a_next)
            for dist in steps:
                c_sh = jnp.where(ridx < rb - dist, pltpu.roll(c, rb - dist, 0), 1.0)
                v_sh = jnp.where(ridx < rb - dist, pltpu.roll(v, rb - dist, 0), 0.0)
                v = v + c * v_sh
                c = c * c_sh
            dh = v + c * lam_next
            du_ri_x = dh * xrv
            dmult = du_ri_x * ri
            dri = du_ri_x * mult
            dxr_ref[rs, :] = dh * mult * ri
            dlog_a = dh * h_prev * a - dmult * (a * a) / mult
            dra = dlog_a * (LRU_C * log_lam)
            dpa = dra * ra * (1.0 - ra)
            dpi = dri * ri * (1.0 - ri)
            dga_ref[rs, :] = dpa.astype(dga_ref.dtype)
            dgi_ref[rs, :] = dpi.astype(dgi_ref.dtype)
            s_a = s_a + jnp.sum(dpa, axis=0, keepdims=True)
            s_x = s_x + jnp.sum(dpi, axis=0, keepdims=True)
            s_l = s_l + jnp.sum(dlog_a * ra, axis=0, keepdims=True)
            return dh[0:1, :], a[0:1, :], s_a, s_x, s_l

        zero = jnp.zeros((1, tc), F32)
        _, _, s_a, s_x, s_l = lax.fori_loop(0, blocks, step, (zero, zero, zero, zero, zero))
        _accumulate_rows(pl.program_id(1) == 0, stat_ref, [s_a, s_x, s_l * (LRU_C * dlog_lam)])

    seq = pl.BlockSpec((s, tc), lambda j, bb: (bb, j))
    vec = pl.BlockSpec((1, tc), lambda j, bb: (0, j))
    half = jax.ShapeDtypeStruct((t, w), BF16)
    return pl.pallas_call(
        body, name=name, grid=(w // tc, b), in_specs=[seq, seq, seq, seq, seq, seq, vec, vec, vec],
        out_specs=[seq, seq, seq, seq, pl.BlockSpec((3, tc), lambda j, bb: (0, j))],
        out_shape=[half, half, half, jax.ShapeDtypeStruct((t, w), F32), jax.ShapeDtypeStruct((3, w), F32)],
        compiler_params=_params("parallel", "arbitrary"),
    )(dy, ga, gi, xr, yg, h, ba.reshape(1, w), bx.reshape(1, w), lam.reshape(1, w))


def _ple_mix(acc, x, gate):
    return x + jax.nn.sigmoid(gate) * acc


def ple_bwd(dx, gate, pin, w_proj, *, name):
    t, d = dx.shape
    k = pin.shape[1]
    tr = _tile(t, 512, SUBLANES)

    def body(dx_ref, g_ref, p_ref, w_ref, dg_ref, de_ref):
        emb = jnp.dot(p_ref[...].astype(BF16), w_ref[...], preferred_element_type=F32)
        sg = jax.nn.sigmoid(g_ref[...])
        dxv = dx_ref[...]
        de_ref[...] = (dxv * sg).astype(de_ref.dtype)
        dg_ref[...] = (dxv * emb * sg * (1.0 - sg)).astype(dg_ref.dtype)

    row = pl.BlockSpec((tr, d), lambda i: (i, 0))
    half = jax.ShapeDtypeStruct((t, d), BF16)
    return pl.pallas_call(
        body, name=name, grid=(t // tr,),
        in_specs=[row, row, pl.BlockSpec((tr, k), lambda i: (i, 0)), pl.BlockSpec((k, d), lambda i: (0, 0))],
        out_specs=[row, row], out_shape=[half, half], compiler_params=_params("parallel"))(dx, gate, pin, w_proj)


def adamw(w, g, m, v, *, name):
    shape = w.shape
    cols = shape[-1]
    rows = w.size // cols
    tr = _tile(rows, 1024, SUBLANES)
    bc1 = 1.0 / (1.0 - ADAM_B1 ** ADAM_STEP)
    bc2 = 1.0 / (1.0 - ADAM_B2 ** ADAM_STEP)

    def body(w_ref, g_ref, m_ref, v_ref, d_ref, nm_ref, nv_ref):
        gv = g_ref[...]
        nm = ADAM_B1 * m_ref[...] + (1.0 - ADAM_B1) * gv
        nv = ADAM_B2 * v_ref[...] + (1.0 - ADAM_B2) * (gv * gv)
        d_ref[...] = -ADAM_LR * ((nm * bc1) / (jnp.sqrt(nv * bc2) + ADAM_EPS) + ADAM_WD * w_ref[...])
        nm_ref[...] = nm
        nv_ref[...] = nv

    blk = pl.BlockSpec((tr, cols), lambda i: (i, 0))
    out = jax.ShapeDtypeStruct((rows, cols), F32)
    res = pl.pallas_call(body, name=name, grid=(rows // tr,), in_specs=[blk] * 4, out_specs=[blk] * 3,
                         out_shape=[out] * 3, compiler_params=_params("parallel"),
                         )(*[a.reshape(rows, cols) for a in (w, g, m, v)])
    return [r.reshape(shape) for r in res]


ANY = pl.BlockSpec(memory_space=pl.ANY)


def _place():
    return lax.axis_index("x"), lax.axis_index("y"), lax.axis_index("c")


class Rider(NamedTuple):
    inputs: tuple
    out_shapes: tuple
    scratch: tuple
    emit: Callable


def _when(cond):
    return (lambda fn: fn()) if cond is True else pl.when(cond)


def _rider_specs(rider, in_specs, out_specs, out_shape):
    if rider is None:
        return in_specs, out_specs, out_shape, []
    return (in_specs + [ANY] * len(rider.inputs), out_specs + [ANY] * len(rider.out_shapes),
            out_shape + list(rider.out_shapes), list(rider.scratch))


def _ride(rider, refs, n_in, n_out, grid):
    if rider is None:
        return
    ids = [pl.program_id(a) for a in range(len(grid))]
    first = functools.reduce(jnp.logical_and, [i == 0 for i in ids])
    last = functools.reduce(jnp.logical_and, [i == n - 1 for i, n in zip(ids, grid)])
    middle = functools.reduce(jnp.logical_and, [ids[0] == (3 * grid[0]) // 4] + [i == 0 for i in ids[1:]])
    r_in = refs[n_in:n_in + len(rider.inputs)]
    at = n_in + len(rider.inputs) + n_out
    r_out = refs[at:at + len(rider.out_shapes)]
    rider.emit(first, middle, last, r_in, r_out, refs[at + len(rider.out_shapes):])


def _alone(rider, *, name):
    n_in = len(rider.inputs)

    def body(*refs):
        rider.emit(True, True, True, refs[:n_in], refs[n_in:n_in + len(rider.out_shapes)],
                   refs[n_in + len(rider.out_shapes):])

    return pl.pallas_call(body, name=name, out_shape=list(rider.out_shapes), in_specs=[ANY] * n_in,
                          out_specs=[ANY] * len(rider.out_shapes), scratch_shapes=list(rider.scratch))(*rider.inputs)


def gather_rider(v):
    rows, cols = v.shape

    def emit(first, middle, last, ins, outs, sems):
        (v_ref,), (out_ref,), (send_sems, recv_sems, local_sem) = ins, outs, sems
        x, y, c = _place()
        me, sibling = (x, y, c), (x, y, 1 - c)
        chips = [(1 - x, y), (x, 1 - y), (1 - x, 1 - y)]

        def slot(px, py, pc):
            return out_ref.at[4 * px + 2 * py + pc]

        def copy(k, block, to, src=None):
            return pltpu.make_async_remote_copy(
                src_ref=slot(*block) if src is None else src, dst_ref=slot(*block),
                send_sem=send_sems.at[k], recv_sem=recv_sems.at[k], device_id=to, device_id_type=MESH)

        mine = pltpu.make_async_copy(v_ref, slot(*me), local_sem)
        own = [copy(0, me, sibling, src=v_ref)]
        own += [copy(1 + j, me, (*chip, c), src=v_ref) for j, chip in enumerate(chips)]
        passed = [copy(4 + j, (*chip, c), sibling) for j, chip in enumerate(chips)]

        @_when(first)
        def _():
            mine.start()
            for cp in own:
                cp.start()

        @_when(middle)
        def _():
            for j, chip in enumerate(chips):
                copy(1 + j, (*chip, c), me).wait_recv()
                passed[j].start()

        @_when(last)
        def _():
            copy(0, sibling, me).wait_recv()
            for j, chip in enumerate(chips):
                copy(4 + j, (*chip, 1 - c), me).wait_recv()
            for cp in own + passed:
                cp.wait_send()
            mine.wait()

    return Rider((v,), (jax.ShapeDtypeStruct((N_DEV, rows, cols), v.dtype),),
                 (pltpu.SemaphoreType.DMA((7,)), pltpu.SemaphoreType.DMA((7,)), pltpu.SemaphoreType.DMA(())), emit)


def all_gather(v, *, name):
    return _alone(gather_rider(v), name=name)[0]


def sibling_rider(parts):
    _, quads, rows, cols = parts.shape

    def emit(first, middle, last, ins, outs, sems):
        (p_ref,), (got_ref,), (send_sem, recv_sem) = ins, outs, sems
        x, y, c = _place()
        cp = pltpu.make_async_remote_copy(src_ref=p_ref.at[1 - c], dst_ref=got_ref, send_sem=send_sem,
                                          recv_sem=recv_sem, device_id=(x, y, 1 - c), device_id_type=MESH)
        _when(first)(cp.start)
        _when(last)(cp.wait)

    return Rider((parts,), (jax.ShapeDtypeStruct((quads, rows, cols), parts.dtype),),
                 (pltpu.SemaphoreType.DMA(()), pltpu.SemaphoreType.DMA(())), emit)


def chip_rider(parts):
    _, rows, cols = parts.shape

    def emit(first, middle, last, ins, outs, sems):
        (p_ref,), (got_ref,), (send_sems, recv_sems) = ins, outs, sems
        x, y, c = _place()
        chips = [(1 - x, y), (x, 1 - y), (1 - x, 1 - y)]
        copies = [pltpu.make_async_remote_copy(
            src_ref=p_ref.at[2 * cx + cy], dst_ref=got_ref.at[k], send_sem=send_sems.at[k],
            recv_sem=recv_sems.at[k], device_id=(cx, cy, c), device_id_type=MESH)
            for k, (cx, cy) in enumerate(chips)]

        @_when(first)
        def _():
            for cp in copies:
                cp.start()

        @_when(last)
        def _():
            for cp in copies:
                cp.wait()

    return Rider((parts,), (jax.ShapeDtypeStruct((3, rows, cols), parts.dtype),),
                 (pltpu.SemaphoreType.DMA((3,)), pltpu.SemaphoreType.DMA((3,))), emit)


def add_sibling(parts, got, *, name):
    _, quads, rows, cols = parts.shape
    tr = _tile(rows, GRAD_ROWS_TILE, SUBLANES)

    def body(c_ref, p_ref, g_ref, o_ref, ob_ref):
        total = p_ref[...] + g_ref[...]
        o_ref[...] = total
        ob_ref[...] = total.astype(ob_ref.dtype)

    c = lax.axis_index("c").astype(jnp.int32).reshape(1)
    quad = pl.BlockSpec((None, tr, cols), lambda q, i, c_ref: (q, i, 0))
    return pl.pallas_call(
        body, name=name,
        grid_spec=pltpu.PrefetchScalarGridSpec(
            num_scalar_prefetch=1, grid=(quads, rows // tr),
            in_specs=[pl.BlockSpec((None, None, tr, cols), lambda q, i, c_ref: (c_ref[0], q, i, 0)), quad],
            out_specs=[quad, quad]),
        out_shape=[jax.ShapeDtypeStruct((quads, rows, cols), parts.dtype),
                   jax.ShapeDtypeStruct((quads, rows, cols), BF16)],
        compiler_params=_params("parallel", "parallel"),
    )(c, parts, got)


def add_chips(parts, got, *, name):
    _, rows, cols = parts.shape
    tr = _tile(rows, GRAD_ROWS_TILE, SUBLANES)

    def body(q_ref, p_ref, g_ref, o_ref):
        o_ref[...] = ((p_ref[...] + g_ref[0].astype(F32)) + g_ref[1].astype(F32)) + g_ref[2].astype(F32)

    q = (2 * lax.axis_index("x") + lax.axis_index("y")).astype(jnp.int32).reshape(1)
    return pl.pallas_call(
        body, name=name,
        grid_spec=pltpu.PrefetchScalarGridSpec(
            num_scalar_prefetch=1, grid=(rows // tr,),
            in_specs=[pl.BlockSpec((None, tr, cols), lambda i, q_ref: (q_ref[0], i, 0)),
                      pl.BlockSpec((3, tr, cols), lambda i, q_ref: (0, i, 0))],
            out_specs=pl.BlockSpec((tr, cols), lambda i, q_ref: (i, 0))),
        out_shape=jax.ShapeDtypeStruct((rows, cols), parts.dtype), compiler_params=_params("parallel"),
    )(q, parts, got)


def _pack(arrays, dtype, row_align):
    pieces, spans, at = [], [], 0
    for a in arrays:
        flat = a.reshape(-1).astype(dtype)
        rows = -(-flat.size // (LANES * row_align)) * row_align
        pieces.append(jnp.pad(flat, (0, rows * LANES - flat.size)).reshape(rows, LANES))
        spans.append((at, rows))
        at += rows
    return jnp.concatenate(pieces, axis=0), spans


def _unpack(buf, spans, shapes, lead):
    out = []
    for (at, rows), shape in zip(spans, shapes):
        size = math.prod(shape)
        piece = buf[..., at:at + rows, :].reshape(*lead, rows * LANES)[..., :size]
        out.append(piece.reshape(*lead, *shape))
    return out


def _whole(gathered, axis):
    moved = jnp.moveaxis(gathered, 0, axis)
    shape = moved.shape
    return moved.reshape(*shape[:axis], shape[axis] * shape[axis + 1], *shape[axis + 2:])


def _blocks(whole, axis):
    shape = whole.shape
    cut = whole.reshape(*shape[:axis], N_DEV, shape[axis] // N_DEV, *shape[axis + 1:])
    return jnp.moveaxis(cut, axis, 0)


def _block_diag(w):
    heads, n, _ = w.shape
    eye = jnp.eye(heads, dtype=w.dtype)
    return (w[:, :, None, :] * eye[:, None, :, None]).reshape(heads * n, heads * n)


def _diag_blocks(full, heads):
    n = full.shape[0] // heads
    return jnp.stack([full[h * n:(h + 1) * n, h * n:(h + 1) * n] for h in range(heads)])


def _pack_grads(pieces):
    sharded = [n for n in pieces if n in SHARD_AXIS]
    replicated = [n for n in pieces if n not in SHARD_AXIS]
    cut = [_blocks(pieces[n], SHARD_AXIS[n]).reshape(N_DEV, -1) for n in sharded]
    rep = jnp.concatenate([pieces[n].reshape(-1) for n in replicated])
    rep_len = rep.size
    rep_rows = -(-rep_len // (N_DEV * LANES * SUBLANES)) * SUBLANES
    rep = jnp.pad(rep, (0, N_DEV * rep_rows * LANES - rep_len)).reshape(N_DEV, rep_rows * LANES)
    bufs, spans, at = [], [], 0
    for a in cut + [rep]:
        rows = -(-a.shape[1] // (LANES * SUBLANES)) * SUBLANES
        bufs.append(jnp.pad(a, ((0, 0), (0, rows * LANES - a.shape[1]))).reshape(N_DEV, rows, LANES))
        spans.append((at, rows))
        at += rows
    tail = -at % GRAD_ROWS_TILE
    bufs.append(jnp.zeros((N_DEV, tail, LANES), F32))
    at += tail
    parts = jnp.concatenate(bufs, axis=1)
    parts = parts.reshape(4, 2, at, LANES).transpose(1, 0, 2, 3)
    shapes = [_blocks(pieces[n], SHARD_AXIS[n]).shape[1:] for n in sharded]
    return parts, (sharded, shapes, spans, replicated, [pieces[n].shape for n in replicated], rep_rows)


def _unpack_grads(mine, info):
    sharded, shapes, spans, _, _, rep_rows = info
    rep_at = spans[-1][0]
    return dict(zip(sharded, _unpack(mine, spans[:-1], shapes, ()))), mine[rep_at:rep_at + rep_rows]


LAYER_INDEXED = ['norm_mix', 'attn_w_qkv', 'attn_w_o', 'norm_ffn', 'ffn_w_up', 'ffn_conv_w', 'ffn_conv_b',
                 'ffn_w_down', 'norm_ple', 'ple_w_gate', 'ple_w_proj']


def kernel(x, p, norm_mix, attn_w_qkv, attn_w_o, rnn_w_in, rnn_conv_w, rnn_conv_b, rnn_w_gate_a, rnn_b_gate_a, rnn_w_gate_x, rnn_b_gate_x, rnn_lru_param, rnn_w_out, norm_ffn, ffn_w_up, ffn_conv_w, ffn_conv_b, ffn_w_down, norm_ple, ple_w_gate, ple_w_proj, norm_final, loss_target, m_norm_mix, m_attn_w_qkv, m_attn_w_o, m_rnn_w_in, m_rnn_conv_w, m_rnn_conv_b, m_rnn_w_gate_a, m_rnn_b_gate_a, m_rnn_w_gate_x, m_rnn_b_gate_x, m_rnn_lru_param, m_rnn_w_out, m_norm_ffn, m_ffn_w_up, m_ffn_conv_w, m_ffn_conv_b, m_ffn_w_down, m_norm_ple, m_ple_w_gate, m_ple_w_proj, m_norm_final, v_norm_mix, v_attn_w_qkv, v_attn_w_o, v_rnn_w_in, v_rnn_conv_w, v_rnn_conv_b, v_rnn_w_gate_a, v_rnn_b_gate_a, v_rnn_w_gate_x, v_rnn_b_gate_x, v_rnn_lru_param, v_rnn_w_out, v_norm_ffn, v_ffn_w_up, v_ffn_conv_w, v_ffn_conv_b, v_ffn_w_down, v_norm_ple, v_ple_w_gate, v_ple_w_proj, v_norm_final):
    given = dict(locals())
    local = {n: given[n] for n in WEIGHTS}
    bsz, seq, d = x.shape
    t = bsz * seq
    depth = norm_mix.shape[0]
    width = rnn_w_out.shape[1] * N_DEV
    ffn = ffn_w_down.shape[1] * N_DEV

    assert depth >= 2
    now = [("attn_w_qkv", 0)]
    later = [(n, j) for n in MATMUL_WEIGHTS for j in range(local[n].shape[0]) if (n, j) not in now]
    full = {n: [None] * local[n].shape[0] for n in MATMUL_WEIGHTS}

    def packed(group):
        return _pack([local[n][j] for n, j in group], BF16, 2 * SUBLANES)

    def place(group, gathered, spans):
        got = _unpack(gathered, spans, [local[n][j].shape for n, j in group], (N_DEV,))
        for (n, j), g in zip(group, got):
            full[n][j] = _whole(g, SHARD_AXIS[n] - 1)

    buf, spans = packed(now)
    place(now, all_gather(buf, name="gather_layer0_weights"), spans)
    later_buf, later_spans = packed(later)
    buf, spans = _pack([local[n] for n in CHANNEL_WEIGHTS], F32, SUBLANES)
    got = _unpack(all_gather(buf, name="gather_channel_weights"), spans,
                  [local[n].shape for n in CHANNEL_WEIGHTS], (N_DEV,))
    full.update({n: _whole(g, SHARD_AXIS[n]) for n, g in zip(CHANNEL_WEIGHTS, got)})
    for n in REPLICATED:
        full[n] = local[n]

    grads = {}

    def stack(name, layer, value, count):
        grads.setdefault(name, [None] * count)[layer] = value

    saved = []
    h0 = x.reshape(t, d)
    for i in range(depth):
        slot = i // 2
        sv = {"x0": h0}
        if i % 2 == 0:
            qkv, hn = mm_rms(h0, full["norm_mix"][i], full["attn_w_qkv"][slot], out_dtype=BF16, name=f"l{i}_norm_qkv")
            if i == 0:
                o, totals, gathered = attn_fwd(qkv, bsz, seq, d, name=f"l{i}_attn", rider=gather_rider(later_buf))
                place(later, gathered, later_spans)
            else:
                o, totals = attn_fwd(qkv, bsz, seq, d, name=f"l{i}_attn")
            h1 = mm(o, full["attn_w_o"][slot], extras=(h0,), epilogue=_add, name=f"l{i}_attn_out")
            sv.update(qkv=qkv, o=o, totals=totals)
        else:
            w_in = full["rnn_w_in"][slot]
            yg, hn = mm_rms(h0, full["norm_mix"][i], w_in[:, :width], name=f"l{i}_norm_rnn_in_gate")
            yr = mm(hn, w_in[:, width:], name=f"l{i}_rnn_in_rec")
            xr = rnn_conv_fwd(yr, full["rnn_conv_w"][slot], full["rnn_conv_b"][slot], seq, name=f"l{i}_rnn_conv")
            wa = _block_diag(full["rnn_w_gate_a"][slot]).astype(BF16)
            wx = _block_diag(full["rnn_w_gate_x"][slot]).astype(BF16)
            ga = mm(xr, wa, name=f"l{i}_rnn_gate_a")
            gi = mm(xr, wx, name=f"l{i}_rnn_gate_x")
            hs, y = rnn_scan_fwd(ga, gi, xr, yg, full["rnn_b_gate_a"][slot], full["rnn_b_gate_x"][slot],
                                 full["rnn_lru_param"][slot], bsz, seq, name=f"l{i}_rnn_scan")
            h1 = mm(y, full["rnn_w_out"][slot], extras=(h0,), epilogue=_add, name=f"l{i}_rnn_out")
            sv.update(yg=yg, yr=yr, xr=xr, wa=wa, wx=wx, ga=ga, gi=gi, hs=hs, y=y)
        sv.update(hn=hn, x1=h1)
        w_up = full["ffn_w_up"][i]
        ug, hn2 = mm_rms(h1, full["norm_ffn"][i], w_up[:, :ffn], out_dtype=BF16, name=f"l{i}_norm_ffn_up_gate")
        uv = mm(hn2, w_up[:, ffn:], out_dtype=BF16, name=f"l{i}_ffn_up_val")
        cw, cb = full["ffn_conv_w"][i], full["ffn_conv_b"][i]
        act = ffn_act_fwd(ug, uv, cw[:, :ffn], cw[:, ffn:], cb[:ffn], cb[ffn:], seq, name=f"l{i}_ffn_act")
        h2 = mm(act, full["ffn_w_down"][i], extras=(h1,), epilogue=_add, name=f"l{i}_ffn_down")
        sv.update(hn2=hn2, ug=ug, uv=uv, act=act, x2=h2)
        pg, hn3 = mm_rms(h2, full["norm_ple"][i], full["ple_w_gate"][i], name=f"l{i}_norm_ple_gate")
        pin = p[i].reshape(t, p.shape[-1])
        h0 = mm(pin, full["ple_w_proj"][i], extras=(h2, pg), epilogue=_ple_mix, name=f"l{i}_ple_proj_mix")
        sv.update(hn3=hn3, pg=pg, pin=pin)
        saved.append(sv)

    dx, g_final, loss_part = final_loss(h0, full["norm_final"], loss_target.reshape(t, d), name="final_loss")
    grads["norm_final"] = g_final
    loss = lax.psum(loss_part, ("x", "y", "c"))

    def later_layers(name):
        return name not in LAYER_INDEXED or len(grads.get(name, ())) > 1

    for i in reversed(range(depth)):
        slot = i // 2
        sv = saved[i]
        if i == 0:
            upper = {n: (jnp.stack(grads[n][1:]) if n in LAYER_INDEXED else
                         jnp.stack(grads[n]) if isinstance(grads[n], list) else grads[n])
                     for n in WEIGHTS if later_layers(n)}
            upper_parts, upper_info = _pack_grads(upper)
        dpg, dpe = ple_bwd(dx, sv["pg"], sv["pin"], full["ple_w_proj"][i], name=f"l{i}_ple_mix_bwd")
        stack("ple_w_proj", i, mm(sv["pin"], dpe, ta=True, name=f"l{i}_ple_proj_wgrad"), depth)
        stack("ple_w_gate", i, mm(sv["hn3"], dpg, ta=True, name=f"l{i}_ple_gate_wgrad"), depth)
        dx, gn = mm_rms_bwd([(dpg, full["ple_w_gate"][i])], sv["x2"], full["norm_ple"][i], dx,
                            name=f"l{i}_ple_gate_dgrad_norm_bwd")
        stack("norm_ple", i, gn, depth)
        stack("ffn_w_down", i, mm(sv["act"], dx, ta=True, name=f"l{i}_ffn_down_wgrad"), depth)
        dact = mm(dx, full["ffn_w_down"][i], tb=True, out_dtype=BF16, name=f"l{i}_ffn_down_dgrad")
        cw, cb = full["ffn_conv_w"][i], full["ffn_conv_b"][i]
        taps = cw.shape[0]
        ride = sibling_rider(upper_parts) if i == 0 else None
        dug, duv, sg, svv, *rode = ffn_act_bwd(sv["ug"], sv["uv"], cw[:, :ffn], cw[:, ffn:], cb[:ffn], cb[ffn:], dact,
                                               seq, name=f"l{i}_ffn_act_bwd", rider=ride)
        if i == 0:
            upper_sum, upper_sum_bf16 = add_sibling(upper_parts, rode[0], name="upper_grads_add_sibling")
        stack("ffn_conv_w", i, jnp.concatenate([sg[:taps], svv[:taps]], axis=1), depth)
        stack("ffn_conv_b", i, jnp.concatenate([sg[taps], svv[taps]], axis=0), depth)
        stack("ffn_w_up", i, jnp.concatenate(
            [mm(sv["hn2"], dug, ta=True, name=f"l{i}_ffn_up_wgrad_gate"),
             mm(sv["hn2"], duv, ta=True, name=f"l{i}_ffn_up_wgrad_val")], axis=1), depth)
        w_up = full["ffn_w_up"][i]
        dx, gn = mm_rms_bwd([(dug, w_up[:, :ffn]), (duv, w_up[:, ffn:])], sv["x1"], full["norm_ffn"][i], dx,
                            name=f"l{i}_ffn_up_dgrad_norm_bwd")
        stack("norm_ffn", i, gn, depth)
        if i % 2 == 0:
            stack("attn_w_o", slot, mm(sv["o"], dx, ta=True, name=f"l{i}_attn_out_wgrad"), depth // 2)
            do = mm(dx, full["attn_w_o"][slot], tb=True, out_dtype=BF16, name=f"l{i}_attn_out_dgrad")
            ride = chip_rider(upper_sum_bf16) if i == 0 else None
            dq, dk, dv, *rode = attn_bwd(sv["qkv"], sv["totals"], do, bsz, seq, d, name=f"l{i}_attn_bwd", rider=ride)
            if i == 0:
                upper_mine = add_chips(upper_sum, rode[0], name="upper_grads_add_chips")
            dqkv = jnp.concatenate([dq, dk, dv], axis=1).astype(BF16)
            stack("attn_w_qkv", slot, mm(sv["hn"], dqkv, ta=True, name=f"l{i}_qkv_wgrad"), depth // 2)
            dx, gn = mm_rms_bwd([(dqkv, full["attn_w_qkv"][slot])], sv["x0"], full["norm_mix"][i], dx,
                                name=f"l{i}_qkv_dgrad_norm_bwd")
        else:
            nrnn = depth // 2
            stack("rnn_w_out", slot, mm(sv["y"], dx, ta=True, name=f"l{i}_rnn_out_wgrad"), nrnn)
            dy = mm(dx, full["rnn_w_out"][slot], tb=True, name=f"l{i}_rnn_out_dgrad")
            dyg, dga, dgi, dxr, stats = rnn_scan_bwd(
                dy, sv["ga"], sv["gi"], sv["xr"], sv["yg"], sv["hs"], full["rnn_b_gate_a"][slot],
                full["rnn_b_gate_x"][slot], full["rnn_lru_param"][slot], bsz, seq, name=f"l{i}_rnn_scan_bwd")
            stack("rnn_b_gate_a", slot, stats[0], nrnn)
            stack("rnn_b_gate_x", slot, stats[1], nrnn)
            stack("rnn_lru_param", slot, stats[2], nrnn)
            stack("rnn_w_gate_a", slot, _diag_blocks(mm(sv["xr"], dga, ta=True, name=f"l{i}_rnn_gate_a_wgrad"),
                                                     RNN_HEADS), nrnn)
            stack("rnn_w_gate_x", slot, _diag_blocks(mm(sv["xr"], dgi, ta=True, name=f"l{i}_rnn_gate_x_wgrad"),
                                                     RNN_HEADS), nrnn)
            dxr = mm(dga, sv["wa"], tb=True, extras=(dxr,), epilogue=_add, name=f"l{i}_rnn_gate_a_dgrad")
            dxr = mm(dgi, sv["wx"], tb=True, extras=(dxr,), epilogue=_add, name=f"l{i}_rnn_gate_x_dgrad")
            rcw = full["rnn_conv_w"][slot]
            rtaps = rcw.shape[0]
            cstats = rnn_conv_wgrad(dxr, sv["yr"], seq, rtaps, name=f"l{i}_rnn_conv_wgrad")
            stack("rnn_conv_w", slot, cstats[:rtaps], nrnn)
            stack("rnn_conv_b", slot, cstats[rtaps], nrnn)
            dyr = conv_input_grad(dxr, rcw, seq, out_dtype=BF16, name=f"l{i}_rnn_conv_bwd")
            stack("rnn_w_in", slot, jnp.concatenate(
                [mm(sv["hn"], dyg, ta=True, name=f"l{i}_rnn_in_wgrad_gate"),
                 mm(sv["hn"], dyr, ta=True, name=f"l{i}_rnn_in_wgrad_rec")], axis=1), nrnn)
            w_in = full["rnn_w_in"][slot]
            dx, gn = mm_rms_bwd([(dyg, w_in[:, :width]), (dyr, w_in[:, width:])], sv["x0"], full["norm_mix"][i], dx,
                                name=f"l{i}_rnn_in_dgrad_norm_bwd")
        stack("norm_mix", i, gn, depth)
    grad_x = dx.reshape(bsz, seq, d)

    lower = {n: grads[n][0][None] for n in LAYER_INDEXED}
    lower_parts, lower_info = _pack_grads(lower)
    from_sibling = _alone(sibling_rider(lower_parts), name="grads_to_sibling")[0]
    lower_sum, lower_sum_bf16 = add_sibling(lower_parts, from_sibling, name="grads_add_sibling")
    from_chips = _alone(chip_rider(lower_sum_bf16), name="grads_to_chips")[0]
    lower_mine = add_chips(lower_sum, from_chips, name="grads_add_chips")
    upper_local, upper_rep = _unpack_grads(upper_mine, upper_info)
    lower_local, lower_rep = _unpack_grads(lower_mine, lower_info)
    local_grads = {n: (jnp.concatenate([lower_local[n], upper_local[n]], axis=0) if n in upper_local
                       else lower_local[n]) if n in lower_local else upper_local[n]
                   for n in WEIGHTS if n in SHARD_AXIS}
    rep_all = all_gather(jnp.concatenate([upper_rep, lower_rep], axis=0), name="gather_replicated_grads")
    rep_vecs = {}
    for key, info, rows_at in (("upper", upper_info, 0), ("lower", lower_info, upper_rep.shape[0])):
        vec = rep_all[:, rows_at:rows_at + info[5]].reshape(-1)
        at = 0
        for n, shape in zip(info[3], info[4]):
            rep_vecs[key, n] = vec[at:at + math.prod(shape)].reshape(shape)
            at += math.prod(shape)
    for n in REPLICATED:
        both = [rep_vecs[k, n] for k in ("lower", "upper") if (k, n) in rep_vecs]
        local_grads[n] = both[0] if n not in LAYER_INDEXED else jnp.concatenate(both, axis=0)

    deltas, new_m, new_v = {}, {}, {}
    for n in WEIGHTS:
        deltas[n], new_m[n], new_v[n] = adamw(local[n], local_grads[n], given["m_" + n], given["v_" + n],
                                              name=f"adamw_{n}")
    return (loss, grad_x, *[local_grads[n] for n in WEIGHTS], *[deltas[n] for n in WEIGHTS],
            *[new_m[n] for n in WEIGHTS], *[new_v[n] for n in WEIGHTS])
```
